```python
import math
import jax, jax.numpy as jnp
from jax import lax
import numpy as np

D_MODEL = 2048
BATCH = 8
SEQ = 2048
DEPTH = 1

D_MIX = D_MODEL
POOL_WIDTH = D_MIX // 2
N_POOL_GROUPS = 4
POOL_GROUP_WIDTH = POOL_WIDTH // N_POOL_GROUPS
POOL_WINDOWS = (2, 4, 8, 16)
ATTN_WIDTH = D_MIX - POOL_WIDTH
HEAD_DIM = 128
N_HEADS = ATTN_WIDTH // HEAD_DIM
N_KV_HEADS = 2
HEAD_GROUP = N_HEADS // N_KV_HEADS
IDX_HEADS = 16
IDX_DIM = 64
TOPK_MAX = 256
ATTN_BLOCK = 128
N_BUCKETS = 32
MAX_DISTANCE = 128
N_GROUPS = 4
EXPERTS_PER_GROUP = 8
N_EXPERTS = N_GROUPS * EXPERTS_PER_GROUP
TOPK_EXPERTS = 2
D_EXPERT = D_MODEL // 4
MOE_BLOCK = 128
EPS = 1e-6
NEG_INF = -1e30
P_IN = (POOL_WIDTH + N_HEADS * HEAD_DIM + 2 * N_KV_HEADS * HEAD_DIM
        + IDX_HEADS * IDX_DIM + IDX_DIM + IDX_HEADS)

kernel_name = "hybrid_pool_dsa_hmoe_block"


def rmsnorm(x, g):
    xf = x.astype(jnp.float32)
    y = xf * lax.rsqrt(jnp.mean(xf * xf, axis=-1, keepdims=True) + EPS)
    return (y * g.astype(jnp.float32)).astype(x.dtype)


def modulate(h, shift, scale):
    return h * (1.0 + scale[:, None, :]) + shift[:, None, :]


def t5_bucket(rel):
    max_exact = N_BUCKETS // 2
    is_small = rel < max_exact
    relf = jnp.maximum(rel, 1).astype(jnp.float32)
    large = max_exact + (jnp.log(relf / max_exact) / math.log(MAX_DISTANCE / max_exact)
                         * (N_BUCKETS - max_exact)).astype(jnp.int32)
    large = jnp.minimum(large, N_BUCKETS - 1)
    return jnp.where(is_small, rel, large)


def pool_mixer(u, w_pool, pool_scale):
    b, s, _ = u.shape
    ug = u.astype(jnp.float32).reshape(b, s, N_POOL_GROUPS, POOL_GROUP_WIDTH)
    cs = jnp.cumsum(ug, axis=1)
    t = jnp.arange(s)
    outs = []
    for g, w in enumerate(POOL_WINDOWS):
        cp = jnp.pad(cs[:, :, g], ((0, 0), (w, 0), (0, 0)))
        win = cp[:, w:w + s] - cp[:, :s]
        cnt = jnp.minimum(t + 1, w).astype(jnp.float32)
        outs.append(win / cnt[None, :, None] - ug[:, :, g])
    p = jnp.stack(outs, axis=2).astype(u.dtype)
    y = jnp.einsum('bsgc,gcd->bsgd', p, w_pool)
    return y.reshape(b, s, POOL_WIDTH) * pool_scale


def dsa_mixer(q, k, v, q_idx, k_idx, w_idx, rel_bias):
    b, s = q.shape[0], q.shape[1]
    topk = min(TOPK_MAX, s // 4)
    nblk = s // ATTN_BLOCK

    def to_blocks(a):
        return a.reshape(b, nblk, ATTN_BLOCK, *a.shape[2:]).swapaxes(0, 1)

    qb, qib, wib = to_blocks(q), to_blocks(q_idx), to_blocks(w_idx)
    kif = k_idx.astype(jnp.float32)
    key_pos = jnp.arange(s)
    idx_scale = (IDX_DIM ** -0.5) * (IDX_HEADS ** -0.5)
    gather = jax.vmap(lambda kb, ib: kb[ib])

    def block_fn(args):
        blk, qblk, qiblk, wiblk = args
        qpos = blk * ATTN_BLOCK + jnp.arange(ATTN_BLOCK)
        dots = jnp.einsum('bqhd,bsd->bqhs', qiblk.astype(jnp.float32), kif)
        score = jnp.einsum('bqh,bqhs->bqs', wiblk.astype(jnp.float32), jax.nn.relu(dots)) * idx_scale
        causal = key_pos[None, :] <= qpos[:, None]
        score = jnp.where(causal[None], score, NEG_INF)
        _, sel = lax.top_k(score, topk)
        valid = sel <= qpos[None, :, None]
        k_sel = gather(k, sel)
        v_sel = gather(v, sel)
        qg = qblk.reshape(b, ATTN_BLOCK, N_KV_HEADS, HEAD_GROUP, HEAD_DIM)
        logits = jnp.einsum('bqhgd,bqjhd->bqhgj', qg, k_sel).astype(jnp.float32) * (HEAD_DIM ** -0.5)
        bucket = t5_bucket(jnp.maximum(qpos[None, :, None] - sel, 0))
        bias = rel_bias.astype(jnp.float32)[bucket]
        bias = bias.reshape(b, ATTN_BLOCK, topk, N_KV_HEADS, HEAD_GROUP).transpose(0, 1, 3, 4, 2)
        logits = jnp.where(valid[:, :, None, None, :], logits + bias, NEG_INF)
        probs = jax.nn.softmax(logits, axis=-1).astype(v_sel.dtype)
        o = jnp.einsum('bqhgj,bqjhd->bqhgd', probs, v_sel)
        return o.reshape(b, ATTN_BLOCK, N_HEADS * HEAD_DIM)

    out = lax.map(block_fn, (jnp.arange(nblk), qb, qib, wib))
    return out.swapaxes(0, 1).reshape(b, s, N_HEADS * HEAD_DIM)


def hier_moe(h, w_group, b_group, w_router, b_router, w_gate, w_up, w_down):
    b, s, d = h.shape
    n_tok = b * s
    ht = h.reshape(n_tok, d)
    glog = (ht @ w_group).astype(jnp.float32) + b_group
    gprob = jax.nn.softmax(glog, axis=-1)
    g_sel = jnp.argmax(glog, axis=-1)
    onehot = jax.nn.one_hot(g_sel, N_GROUPS, dtype=jnp.float32)
    p_g = jnp.sum(gprob * onehot, axis=-1)
    elog = ((ht @ w_router).astype(jnp.float32) + b_router).reshape(n_tok, N_GROUPS, EXPERTS_PER_GROUP)
    elog_sel = jnp.einsum('tg,tge->te', onehot, elog)
    eprob = jax.nn.softmax(elog_sel, axis=-1)
    top_p, top_i = lax.top_k(eprob, TOPK_EXPERTS)
    gates = p_g[:, None] * top_p / jnp.sum(top_p, axis=-1, keepdims=True)
    expert_ids = g_sel[:, None].astype(jnp.int32) * EXPERTS_PER_GROUP + top_i.astype(jnp.int32)

    n_asg = n_tok * TOPK_EXPERTS
    e_flat = expert_ids.reshape(n_asg)
    tok_flat = jnp.repeat(jnp.arange(n_tok, dtype=jnp.int32), TOPK_EXPERTS)
    gate_flat = gates.reshape(n_asg)
    order = jnp.argsort(e_flat)
    e_sorted = e_flat[order]
    counts = jnp.zeros((N_EXPERTS,), jnp.int32).at[e_flat].add(1)
    padded = (counts + MOE_BLOCK - 1) // MOE_BLOCK * MOE_BLOCK
    starts = jnp.cumsum(counts) - counts
    pends = jnp.cumsum(padded)
    pstarts = pends - padded
    dest = pstarts[e_sorted] + (jnp.arange(n_asg, dtype=jnp.int32) - starts[e_sorted])
    n_slots = (-(-n_asg // MOE_BLOCK) + N_EXPERTS) * MOE_BLOCK
    n_blocks = n_slots // MOE_BLOCK
    slot_tok = jnp.zeros((n_slots,), jnp.int32).at[dest].set(tok_flat[order])
    slot_gate = jnp.zeros((n_slots,), jnp.float32).at[dest].set(gate_flat[order])
    block_start = jnp.arange(n_blocks, dtype=jnp.int32) * MOE_BLOCK
    block_exp = jnp.minimum(jnp.searchsorted(pends, block_start, side='right'), N_EXPERTS - 1)
    xs = ht[slot_tok].reshape(n_blocks, MOE_BLOCK, d)

    def expert_block(args):
        xb, e = args
        a = jax.nn.silu(xb @ w_gate[e]) * (xb @ w_up[e])
        return a @ w_down[e]

    ys = lax.map(expert_block, (xs, block_exp)).reshape(n_slots, d)
    out = jnp.zeros((n_tok, d), jnp.float32).at[slot_tok].add(ys.astype(jnp.float32) * slot_gate[:, None])
    return out.astype(h.dtype).reshape(b, s, d)


def setup_inputs(seed: int = 0) -> dict:
    key = jax.random.key(seed)
    ks = jax.random.split(key, 20)
    f32 = jnp.float32
    nrm = lambda k, shape, sc: jax.random.normal(k, shape, f32) * sc
    return {
        "x": nrm(ks[0], (BATCH, SEQ, D_MODEL), 1.0),
        "c": nrm(ks[1], (BATCH, D_MODEL), 1.0),
        "w_ada": nrm(ks[2], (DEPTH, D_MODEL, 6 * D_MODEL), 0.5 * D_MODEL ** -0.5),
        "b_ada": nrm(ks[3], (DEPTH, 6 * D_MODEL), 0.02),
        "g_mix": 1.0 + nrm(ks[4], (DEPTH, D_MODEL), 0.05),
        "w_in": nrm(ks[5], (DEPTH, D_MODEL, P_IN), D_MODEL ** -0.5),
        "w_pool": nrm(ks[6], (DEPTH, N_POOL_GROUPS, POOL_GROUP_WIDTH, POOL_GROUP_WIDTH), POOL_GROUP_WIDTH ** -0.5),
        "pool_scale": 1.0 + nrm(ks[7], (DEPTH, POOL_WIDTH), 0.1),
        "rel_bias": nrm(ks[8], (N_BUCKETS, N_HEADS), 0.5),
        "w_out": nrm(ks[9], (DEPTH, D_MIX, D_MODEL), D_MIX ** -0.5),
        "g_ffn": 1.0 + nrm(ks[10], (DEPTH, D_MODEL), 0.05),
        "w_group": nrm(ks[11], (DEPTH, D_MODEL, N_GROUPS), D_MODEL ** -0.5),
        "b_group": nrm(ks[12], (DEPTH, N_GROUPS), 0.01),
        "w_router": nrm(ks[13], (DEPTH, D_MODEL, N_EXPERTS), D_MODEL ** -0.5),
        "b_router": nrm(ks[14], (DEPTH, N_EXPERTS), 0.01),
        "w_gate": nrm(ks[15], (DEPTH, N_EXPERTS, D_MODEL, D_EXPERT), D_MODEL ** -0.5),
        "w_up": nrm(ks[16], (DEPTH, N_EXPERTS, D_MODEL, D_EXPERT), D_MODEL ** -0.5),
        "w_down": nrm(ks[17], (DEPTH, N_EXPERTS, D_EXPERT, D_MODEL), D_EXPERT ** -0.5),
        "g_final": 1.0 + nrm(ks[18], (D_MODEL,), 0.05),
    }


def reference(x, c, w_ada, b_ada, g_mix, w_in, w_pool, pool_scale, rel_bias, w_out,
              g_ffn, w_group, b_group, w_router, b_router, w_gate, w_up, w_down, g_final):
    b, s, _ = x.shape
    sizes = [POOL_WIDTH, N_HEADS * HEAD_DIM, N_KV_HEADS * HEAD_DIM, N_KV_HEADS * HEAD_DIM,
             IDX_HEADS * IDX_DIM, IDX_DIM]
    cuts = [int(v) for v in np.cumsum(sizes)]
    for i in range(DEPTH):
        mod = jax.nn.silu(c) @ w_ada[i] + b_ada[i]
        sh1, sc1, gt1, sh2, sc2, gt2 = jnp.split(mod, 6, axis=-1)
        h = modulate(rmsnorm(x, g_mix[i]), sh1, sc1)
        proj = h @ w_in[i]
        u, q, k, v, qi, ki, wi = jnp.split(proj, cuts, axis=-1)
        pool_out = pool_mixer(u, w_pool[i], pool_scale[i])
        attn_out = dsa_mixer(q.reshape(b, s, N_HEADS, HEAD_DIM),
                             k.reshape(b, s, N_KV_HEADS, HEAD_DIM),
                             v.reshape(b, s, N_KV_HEADS, HEAD_DIM),
                             qi.reshape(b, s, IDX_HEADS, IDX_DIM), ki, wi, rel_bias)
        mixed = jnp.concatenate([pool_out, attn_out], axis=-1) @ w_out[i]
        x = x + gt1[:, None, :] * mixed
        h2 = modulate(rmsnorm(x, g_ffn[i]), sh2, sc2)
        x = x + gt2[:, None, :] * hier_moe(h2, w_group[i], b_group[i], w_router[i], b_router[i],
                                           w_gate[i], w_up[i], w_down[i])
    return rmsnorm(x, g_final)
```

```python
import functools
import math

import numpy as np
import jax
import jax.numpy as jnp
from jax import lax
from jax.experimental import pallas as pl
from jax.experimental.pallas import tpu as pltpu

F32 = jnp.float32
I32 = jnp.int32
_MXU_DTYPE = jnp.bfloat16

POOL_WINDOWS = (2, 4, 8, 16)
N_POOL_GROUPS = 4
HEAD_DIM = 128
N_HEADS = 8
N_KV_HEADS = 2
HEAD_GROUP = N_HEADS // N_KV_HEADS
IDX_HEADS = 16
IDX_DIM = 64
TOPK_MAX = 256
N_BUCKETS = 32
MAX_DISTANCE = 128
N_GROUPS = 4
EXPERTS_PER_GROUP = 8
N_EXPERTS = N_GROUPS * EXPERTS_PER_GROUP
EPS = 1e-6
NEG_INF = -1e30
BIG = 3e38

LANES = 128
SUBLANES = 8
VMEM_LIMIT_BYTES = 56 * 1024 * 1024

ADA_TN = 1024
PROJ_TM = 256
ATT_T = 256
SCORE_SUB = 64
COUNT_RB = 128
MIX_TM = 256
HALO = 16
ROUTE_TR = 256
DISP_TD = 256
MOE_BM = 256
COMB_TF = 128
SEL_MAX_ITERS = 80


def _cparams(sem):
    return pltpu.CompilerParams(dimension_semantics=sem, vmem_limit_bytes=VMEM_LIMIT_BYTES)


def _silu(x):
    return x * (1.0 / (1.0 + jnp.exp(-x)))


def _dot(a, b):
    return jnp.dot(a, b, preferred_element_type=F32)


def _dot_nt(a, b):
    return lax.dot_general(a, b, (((1,), (1,)), ((), ())), preferred_element_type=F32)


def _ada_body(c_ref, w_ref, b_ref, o_ref):
    s = _silu(c_ref[...])
    o_ref[...] = _dot(s.astype(_MXU_DTYPE), w_ref[...].astype(_MXU_DTYPE)) + b_ref[...]


def _ada(c, w, b):
    bsz, d = c.shape
    n = w.shape[1]
    return pl.pallas_call(
        _ada_body,
        grid=(n // ADA_TN,),
        in_specs=[pl.BlockSpec((bsz, d), lambda j: (0, 0)),
                  pl.BlockSpec((d, ADA_TN), lambda j: (0, j)),
                  pl.BlockSpec((1, ADA_TN), lambda j: (0, j))],
        out_specs=pl.BlockSpec((bsz, ADA_TN), lambda j: (0, j)),
        out_shape=jax.ShapeDtypeStruct((bsz, n), F32),
        compiler_params=_cparams(("arbitrary",)),
        name="ada",
    )(c, w, b)


def _rms_mod(x, g, shift, scale):
    ms = jnp.mean(x * x, axis=-1, keepdims=True)
    y = x * lax.rsqrt(ms + EPS) * g
    return y * (1.0 + scale) + shift


def _proj_body(x_ref, mod_ref, g_ref, w_ref, u_ref, q_ref, k_ref, vt_ref, qi_ref, kia_ref, kib_ref,
               wit_ref, *, cuts):
    h = _rms_mod(x_ref[0], g_ref[...], mod_ref[0, 0:1, :], mod_ref[0, 1:2, :])
    hb = h.astype(_MXU_DTYPE)

    def seg(name):
        lo, hi = cuts[name]
        return _dot(hb, w_ref[:, lo:hi])

    u_ref[0] = seg("u")
    q_ref[0] = seg("q").astype(q_ref.dtype)
    k_ref[0] = seg("k").astype(k_ref.dtype)
    vt_ref[0] = seg("v").T.astype(vt_ref.dtype)
    qi_ref[0] = seg("qi").astype(qi_ref.dtype)
    kia_ref[0] = seg("kia").astype(kia_ref.dtype)
    kib_ref[0] = seg("kib").astype(kib_ref.dtype)
    wit_ref[0] = seg("wi").T[:IDX_HEADS, :]


def _proj(x, mod, g_mix, w_in):
    bsz, s, d = x.shape
    pool_w = d // 2
    attn_w = N_HEADS * HEAD_DIM
    kv_w = N_KV_HEADS * HEAD_DIM
    qi_w = IDX_HEADS * IDX_DIM
    c = np.cumsum([0, pool_w, attn_w, kv_w, kv_w, qi_w, IDX_DIM, IDX_HEADS])
    w_ki = w_in[:, c[5]:c[6]]
    w_wi = w_in[:, c[6]:c[7]]
    zk = jnp.zeros((d, LANES - IDX_DIM), w_in.dtype)
    wp = jnp.concatenate(
        [w_in[:, :c[5]], w_ki, zk, zk, w_ki, w_wi, jnp.zeros((d, LANES - IDX_HEADS), w_in.dtype)],
        axis=1).astype(_MXU_DTYPE)
    base = int(c[5])
    cuts = {"u": (int(c[0]), int(c[1])), "q": (int(c[1]), int(c[2])), "k": (int(c[2]), int(c[3])),
            "v": (int(c[3]), int(c[4])), "qi": (int(c[4]), int(c[5])),
            "kia": (base, base + LANES), "kib": (base + LANES, base + 2 * LANES),
            "wi": (base + 2 * LANES, base + 3 * LANES)}
    tm = PROJ_TM
    pw = wp.shape[1]
    row = lambda w: pl.BlockSpec((1, tm, w), lambda b, i: (b, i, 0))
    out_shapes = [
        jax.ShapeDtypeStruct((bsz, s, pool_w), F32),
        jax.ShapeDtypeStruct((bsz, s, attn_w), _MXU_DTYPE),
        jax.ShapeDtypeStruct((bsz, s, kv_w), _MXU_DTYPE),
        jax.ShapeDtypeStruct((bsz, kv_w, s), _MXU_DTYPE),
        jax.ShapeDtypeStruct((bsz, s, qi_w), _MXU_DTYPE),
        jax.ShapeDtypeStruct((bsz, s, LANES), _MXU_DTYPE),
        jax.ShapeDtypeStruct((bsz, s, LANES), _MXU_DTYPE),
        jax.ShapeDtypeStruct((bsz, IDX_HEADS, s), F32),
    ]
    out_specs = [row(pool_w), row(attn_w), row(kv_w),
                 pl.BlockSpec((1, kv_w, tm), lambda b, i: (b, 0, i)),
                 row(qi_w), row(LANES), row(LANES),
                 pl.BlockSpec((1, IDX_HEADS, tm), lambda b, i: (b, 0, i))]
    return pl.pallas_call(
        functools.partial(_proj_body, cuts=cuts),
        grid=(bsz, s // tm),
        in_specs=[pl.BlockSpec((1, tm, d), lambda b, i: (b, i, 0)),
                  pl.BlockSpec((1, 6, d), lambda b, i: (b, 0, 0)),
                  pl.BlockSpec((1, d), lambda b, i: (0, 0)),
                  pl.BlockSpec((d, pw), lambda b, i: (0, 0))],
        out_specs=out_specs,
        out_shape=out_shapes,
        compiler_params=_cparams(("arbitrary", "arbitrary")),
        name="proj",
    )(x, mod, g_mix, wp)


def _bucket_starts():
    max_exact = N_BUCKETS // 2
    d = np.arange(1, 4 * MAX_DISTANCE, dtype=np.float32)
    large = max_exact + (np.log(d / np.float32(max_exact)) / np.float32(math.log(MAX_DISTANCE / max_exact))
                         * np.float32(N_BUCKETS - max_exact)).astype(np.int32)
    large = np.minimum(large, N_BUCKETS - 1)
    bucket = np.where(d < max_exact, d.astype(np.int32), large)
    bucket = np.concatenate([[0], bucket])
    starts = [int(np.argmax(bucket >= b)) for b in range(N_BUCKETS)]
    assert all(np.all(bucket[starts[b]:] >= b) for b in range(N_BUCKETS))
    assert starts[N_BUCKETS - 1] <= MAX_DISTANCE
    return starts


def _bias_body(rb_ref, o_ref, *, t, starts):
    diff = pl.program_id(0)
    s_l = lax.broadcasted_iota(I32, (t, t), 0)
    t_l = lax.broadcasted_iota(I32, (t, t), 1)
    dist = diff * t + t_l - s_l
    for h in range(N_HEADS):
        val = jnp.full((t, t), rb_ref[0, h], F32)
        for b in range(1, N_BUCKETS):
            val = jnp.where(dist >= starts[b], rb_ref[b, h], val)
        o_ref[0, h] = val


def _bias_tiles(rel_bias, t):
    assert t >= MAX_DISTANCE
    return pl.pallas_call(
        functools.partial(_bias_body, t=t, starts=_bucket_starts()),
        grid=(3,),
        in_specs=[pl.BlockSpec(memory_space=pltpu.SMEM)],
        out_specs=pl.BlockSpec((1, N_HEADS, t, t), lambda i: (i, 0, 0, 0)),
        out_shape=jax.ShapeDtypeStruct((3, N_HEADS, t, t), F32),
        compiler_params=_cparams(("arbitrary",)),
        name="bias_tiles",
    )(rel_bias)


def _dsa_body(q_ref, qi_ref, wit_ref, kia_ref, kib_ref, k_ref, vt_ref, bias_ref, o_ref,
              score_ref, mask_ref, *, t, topk):
    i = pl.program_id(1)
    nk = i + 1
    t_glob = i * t + lax.broadcasted_iota(I32, (1, t), 1)
    idx_scale = (IDX_DIM ** -0.5) * (IDX_HEADS ** -0.5)

    def score_chunk(c, carry):
        r0 = pl.multiple_of(c * SCORE_SUB, SCORE_SUB)
        ka = kia_ref[0, pl.ds(r0, SCORE_SUB), :]
        kb = kib_ref[0, pl.ds(r0, SCORE_SUB), :]
        acc = jnp.zeros((SCORE_SUB, t), F32)
        for j in range(IDX_HEADS // 2):
            qp = qi_ref[0, :, j * LANES:(j + 1) * LANES]
            da = _dot_nt(ka, qp)
            db = _dot_nt(kb, qp)
            acc = acc + jnp.maximum(da, 0.0) * wit_ref[0, 2 * j:2 * j + 1, :]
            acc = acc + jnp.maximum(db, 0.0) * wit_ref[0, 2 * j + 1:2 * j + 2, :]
        sc = acc * idx_scale
        s_glob = r0 + lax.broadcasted_iota(I32, (SCORE_SUB, 1), 0)
        score_ref[pl.ds(r0, SCORE_SUB), :] = jnp.where(s_glob <= t_glob, sc, NEG_INF)
        return carry

    lax.fori_loop(0, nk * (t // SCORE_SUB), score_chunk, 0)

    nblk = nk * (t // COUNT_RB)
    fold = lambda m: jnp.sum(m.reshape(COUNT_RB // SUBLANES, SUBLANES, t), axis=0)

    def blk(r):
        return score_ref[pl.ds(pl.multiple_of(r * COUNT_RB, COUNT_RB), COUNT_RB), :]

    def count_ge(c):
        def body(r, acc):
            return acc + fold((blk(r) >= c).astype(I32))
        acc = lax.fori_loop(0, nblk, body, jnp.zeros((SUBLANES, t), I32))
        return jnp.sum(acc, axis=0, keepdims=True)

    def stats_body(r, carry):
        cnt, mn, mx = carry
        b = blk(r)
        real = b > NEG_INF
        cnt = cnt + fold(real.astype(I32))
        mn = jnp.minimum(mn, jnp.min(jnp.where(real, b, BIG).reshape(COUNT_RB // SUBLANES, SUBLANES, t), axis=0))
        mx = jnp.maximum(mx, jnp.max(b.reshape(COUNT_RB // SUBLANES, SUBLANES, t), axis=0))
        return cnt, mn, mx

    cnt8, mn8, mx8 = lax.fori_loop(
        0, nblk, stats_body,
        (jnp.zeros((SUBLANES, t), I32), jnp.full((SUBLANES, t), BIG, F32), jnp.full((SUBLANES, t), -BIG, F32)))
    n_real = jnp.sum(cnt8, axis=0, keepdims=True)
    rmin = jnp.min(mn8, axis=0, keepdims=True)
    rmax = jnp.max(mx8, axis=0, keepdims=True)
    n_max = count_ge(rmax)
    degenerate = jnp.logical_and(n_real >= topk, n_max >= topk)
    lo0 = jnp.where(n_real < topk, NEG_INF, jnp.where(degenerate, rmax, rmin))
    hi0 = jnp.where(degenerate, BIG, rmax)
    done0 = jnp.where(n_real <= topk, 1, 0)

    def sel_cond(st):
        it, lo, hi, done, stuck = st
        return jnp.logical_and(it < SEL_MAX_ITERS, jnp.min(done + stuck) == 0)

    def sel_body(st):
        it, lo, hi, done, stuck = st
        c = lo + (hi - lo) * 0.5
        active = (done + stuck) == 0
        has_mid = jnp.logical_and(c > lo, c < hi)
        n = count_ge(c)
        upd = jnp.logical_and(active, has_mid)
        lo = jnp.where(jnp.logical_and(upd, n >= topk), c, lo)
        hi = jnp.where(jnp.logical_and(upd, n < topk), c, hi)
        done = jnp.where(jnp.logical_and(upd, n == topk), 1, done)
        stuck = jnp.where(jnp.logical_and(active, jnp.logical_not(has_mid)), 1, stuck)
        return it + 1, lo, hi, done, stuck

    _, lo, hi, done, _ = lax.while_loop(
        sel_cond, sel_body, (jnp.int32(0), lo0, hi0, done0, jnp.zeros((1, t), I32)))

    def write_mask(r, cut):
        r0 = pl.multiple_of(r * COUNT_RB, COUNT_RB)
        b = score_ref[pl.ds(r0, COUNT_RB), :]
        s_glob = r0 + lax.broadcasted_iota(I32, (COUNT_RB, 1), 0)
        sel = jnp.logical_and(b >= lo, s_glob <= t_glob)
        if cut is not None:
            sel = jnp.logical_and(sel, jnp.logical_or(b >= hi, s_glob < cut))
        mask_ref[pl.ds(r0, COUNT_RB), :] = jnp.where(sel, 0.0, NEG_INF)

    any_tied = jnp.min(done) == 0

    @pl.when(jnp.logical_not(any_tied))
    def _():
        def body(r, carry):
            write_mask(r, None)
            return carry
        lax.fori_loop(0, nblk, body, 0)

    @pl.when(any_tied)
    def _():
        need = topk - count_ge(hi)

        def count_tie_below(x):
            def body(r, acc):
                r0 = pl.multiple_of(r * COUNT_RB, COUNT_RB)
                b = score_ref[pl.ds(r0, COUNT_RB), :]
                s_glob = r0 + lax.broadcasted_iota(I32, (COUNT_RB, 1), 0)
                m = jnp.logical_and(jnp.logical_and(b >= lo, b < hi), s_glob < x)
                return acc + fold(m.astype(I32))
            acc = lax.fori_loop(0, nblk, body, jnp.zeros((SUBLANES, t), I32))
            return jnp.sum(acc, axis=0, keepdims=True)

        nbits = int(score_ref.shape[0]).bit_length()

        def bit_body(bi, x):
            c = x + jnp.left_shift(jnp.int32(1), nbits - 1 - bi)
            return jnp.where(count_tie_below(c) < need, c, x)

        x = lax.fori_loop(0, nbits, bit_body, jnp.zeros((1, t), I32))
        cut = jnp.where(done > 0, jnp.int32(2 ** 30), x + 1)

        def body(r, carry):
            write_mask(r, cut)
            return carry
        lax.fori_loop(0, nblk, body, 0)

    scale = HEAD_DIM ** -0.5
    for h in range(N_HEADS):
        g = h // HEAD_GROUP
        qh = q_ref[0, :, h * HEAD_DIM:(h + 1) * HEAD_DIM]

        def att_chunk(kc, carry, h=h, g=g, qh=qh):
            m, l, acc = carry
            r0 = pl.multiple_of(kc * t, t)
            kk = k_ref[0, pl.ds(r0, t), g * HEAD_DIM:(g + 1) * HEAD_DIM]
            s = _dot_nt(kk, qh) * scale
            s = s + bias_ref[jnp.minimum(i - kc, 2), h] + mask_ref[pl.ds(r0, t), :]
            m_new = jnp.maximum(m, jnp.max(s, axis=0, keepdims=True))
            alpha = jnp.exp(m - m_new)
            p = jnp.exp(s - m_new)
            l = alpha * l + jnp.sum(p, axis=0, keepdims=True)
            vt = vt_ref[0, g * HEAD_DIM:(g + 1) * HEAD_DIM, pl.ds(r0, t)]
            acc = alpha * acc + _dot(vt, p.astype(_MXU_DTYPE))
            return m_new, l, acc

        m, l, acc = lax.fori_loop(
            0, nk, att_chunk,
            (jnp.full((1, t), NEG_INF, F32), jnp.zeros((1, t), F32), jnp.zeros((HEAD_DIM, t), F32)))
        o_ref[0, :, h * HEAD_DIM:(h + 1) * HEAD_DIM] = (acc / l).T.astype(o_ref.dtype)


def _dsa(q, qi, wit, kia, kib, k, vt, bias, topk):
    bsz, s, aw = q.shape
    t = ATT_T
    assert s % t == 0 and topk <= t
    row = lambda w: pl.BlockSpec((1, t, w), lambda b, i: (b, i, 0))
    full = lambda a: pl.BlockSpec((1,) + a.shape[1:], lambda b, i: (b, 0, 0))
    return pl.pallas_call(
        functools.partial(_dsa_body, t=t, topk=topk),
        grid=(bsz, s // t),
        in_specs=[row(aw), row(qi.shape[2]),
                  pl.BlockSpec((1, IDX_HEADS, t), lambda b, i: (b, 0, i)),
                  full(kia), full(kib), full(k), full(vt),
                  pl.BlockSpec(bias.shape, lambda b, i: (0, 0, 0, 0))],
        out_specs=row(aw),
        out_shape=jax.ShapeDtypeStruct((bsz, s, aw), _MXU_DTYPE),
        scratch_shapes=[pltpu.VMEM((s, t), F32), pltpu.VMEM((s, t), F32)],
        compiler_params=_cparams(("arbitrary", "arbitrary")),
        name="dsa",
    )(q, qi, wit, kia, kib, k, vt, bias)


def _mix_body(u_ref, halo_ref, a_ref, x_ref, mod_ref, wpool_ref, ps_ref, wout_ref, g_ref, wr_ref, br_ref,
              x1_ref, h2_ref, lg_ref, *, seq, tm):
    t0 = (pl.program_id(0) * tm) % seq
    u = u_ref[...]
    halo = jnp.where(t0 == 0, 0.0, halo_ref[...])
    ext = jnp.concatenate([halo, u], axis=0)
    pos = t0 + lax.broadcasted_iota(I32, (tm, 1), 0)
    gw = u.shape[1] // N_POOL_GROUPS
    ys = []
    for g, w in enumerate(POOL_WINDOWS):
        a = ext[:, g * gw:(g + 1) * gw]
        sft = 1
        while sft < w:
            a = a + pltpu.roll(a, sft, 0)
            sft *= 2
        cnt = jnp.minimum(pos + 1, w).astype(F32)
        p = a[HALO:, :] / cnt - u[:, g * gw:(g + 1) * gw]
        ys.append(_dot(p.astype(_MXU_DTYPE), wpool_ref[g]) * ps_ref[:, g * gw:(g + 1) * gw])
    pool = jnp.concatenate(ys, axis=1).astype(_MXU_DTYPE)
    pw = pool.shape[1]
    mixed = _dot(pool, wout_ref[:pw, :]) + _dot(a_ref[...], wout_ref[pw:, :])
    x1 = x_ref[...] + mod_ref[0, 2:3, :] * mixed
    x1_ref[...] = x1
    h2 = _rms_mod(x1, g_ref[...], mod_ref[0, 3:4, :], mod_ref[0, 4:5, :])
    h2_ref[...] = h2
    lg_ref[...] = _dot(h2.astype(_MXU_DTYPE), wr_ref[...]) + br_ref[...]


def _mix(u, attn, x, mod, w_pool, pool_scale, w_out, g_ffn, w_r, b_r, seq):
    n_tok, d = x.shape
    tm = MIX_TM
    assert seq % tm == 0 and tm % HALO == 0
    pw = u.shape[1]
    aw = attn.shape[1]
    rows = lambda w: pl.BlockSpec((tm, w), lambda i: (i, 0))
    const = lambda a: pl.BlockSpec(a.shape, lambda i: (0,) * a.ndim)
    return pl.pallas_call(
        functools.partial(_mix_body, seq=seq, tm=tm),
        grid=(n_tok // tm,),
        in_specs=[rows(pw),
                  pl.BlockSpec((HALO, pw), lambda i: (jnp.maximum(i * (tm // HALO) - 1, 0), 0)),
                  rows(aw), rows(d),
                  pl.BlockSpec((1, 6, d), lambda i: ((i * tm) // seq, 0, 0)),
                  const(w_pool), const(pool_scale), const(w_out), const(g_ffn), const(w_r), const(b_r)],
        out_specs=[rows(d), rows(d), rows(LANES)],
        out_shape=[jax.ShapeDtypeStruct((n_tok, d), F32), jax.ShapeDtypeStruct((n_tok, d), F32),
                   jax.ShapeDtypeStruct((n_tok, LANES), F32)],
        compiler_params=_cparams(("arbitrary",)),
        name="mix",
    )(u, u, attn, x, mod, w_pool, pool_scale, w_out, g_ffn, w_r, b_r)


def _route_body(lg_ref, rt_ref, gate_ref, cnt_ref, run_ref):
    @pl.when(pl.program_id(0) == 0)
    def _():
        run_ref[...] = jnp.zeros_like(run_ref)

    lg = lg_ref[...]
    tr = lg.shape[0]
    lane = lax.broadcasted_iota(I32, lg.shape, 1)
    gmask = lane < N_GROUPS
    gl = jnp.where(gmask, lg, -BIG)
    gmax = jnp.max(gl, axis=-1, keepdims=True)
    gsel = jnp.min(jnp.where(jnp.logical_and(gmask, gl == gmax), lane, LANES), axis=-1, keepdims=True)
    gsum = jnp.sum(jnp.where(gmask, jnp.exp(gl - gmax), 0.0), axis=-1, keepdims=True)
    p_g = 1.0 / gsum
    in_e = jnp.logical_and(lane >= N_GROUPS, lane < N_GROUPS + N_EXPERTS)
    emask = jnp.logical_and(in_e, lax.shift_right_arithmetic(lane - N_GROUPS, EXPERTS_PER_GROUP.bit_length() - 1) == gsel)
    el = jnp.where(emask, lg, -BIG)
    emax = jnp.max(el, axis=-1, keepdims=True)
    ex = jnp.where(emask, jnp.exp(el - emax), 0.0)
    ep = ex / jnp.sum(ex, axis=-1, keepdims=True)
    p1 = jnp.max(jnp.where(emask, ep, -1.0), axis=-1, keepdims=True)
    i1 = jnp.min(jnp.where(jnp.logical_and(emask, ep == p1), lane, LANES), axis=-1, keepdims=True)
    m2 = jnp.logical_and(emask, lane != i1)
    p2 = jnp.max(jnp.where(m2, ep, -1.0), axis=-1, keepdims=True)
    i2 = jnp.min(jnp.where(jnp.logical_and(m2, ep == p2), lane, LANES), axis=-1, keepdims=True)
    den = p1 + p2
    g1 = p_g * p1 / den
    g2 = p_g * p2 / den
    e1 = i1 - N_GROUPS
    e2 = i2 - N_GROUPS

    oh = jnp.logical_or(lane == e1, lane == e2)
    ri = lax.broadcasted_iota(I32, (tr, tr), 0)
    ci = lax.broadcasted_iota(I32, (tr, tr), 1)
    ltri = (ci < ri).astype(_MXU_DTYPE)
    before = _dot(ltri, oh.astype(_MXU_DTYPE)) + run_ref[...]
    r1 = jnp.sum(jnp.where(lane == e1, before, 0.0), axis=-1, keepdims=True)
    r2 = jnp.sum(jnp.where(lane == e2, before, 0.0), axis=-1, keepdims=True)
    run = run_ref[...] + jnp.sum(oh.astype(F32), axis=0, keepdims=True)
    run_ref[...] = run
    cnt_ref[...] = jnp.broadcast_to(run, cnt_ref.shape)

    route = jnp.where(lane == 0, e1.astype(F32),
                      jnp.where(lane == 1, e2.astype(F32),
                                jnp.where(lane == 2, r1, jnp.where(lane == 3, r2, 0.0))))
    rt_ref[...] = route.T[:SUBLANES, :].astype(I32)
    gate_ref[...] = jnp.where(lane == 0, g1, jnp.where(lane == 1, g2, 0.0))


def _route(logits):
    n_tok = logits.shape[0]
    tr = ROUTE_TR
    return pl.pallas_call(
        _route_body,
        grid=(n_tok // tr,),
        in_specs=[pl.BlockSpec((tr, LANES), lambda i: (i, 0))],
        out_specs=[pl.BlockSpec((SUBLANES, tr), lambda i: (0, i)),
                   pl.BlockSpec((tr, LANES), lambda i: (i, 0)),
                   pl.BlockSpec((SUBLANES, LANES), lambda i: (0, 0))],
        out_shape=[jax.ShapeDtypeStruct((SUBLANES, n_tok), I32),
                   jax.ShapeDtypeStruct((n_tok, LANES), F32),
                   jax.ShapeDtypeStruct((SUBLANES, LANES), F32)],
        scratch_shapes=[pltpu.VMEM((1, LANES), F32)],
        compiler_params=_cparams(("arbitrary",)),
        name="route",
    )(logits)


def _row_copy(src_ref, src_row, dst_ref, dst_row, sem):
    return pltpu.make_async_copy(src_ref.at[pl.ds(src_row, 1), :], dst_ref.at[pl.ds(dst_row, 1), :], sem)


def _dispatch_body(pstart_ref, cnt_ref, nu_ref, h_ref, rt_ref, xs_ref, zero_ref, sem, *, td, bm):
    def issue(r, carry):
        for j in range(2):
            dst = pstart_ref[rt_ref[j, r]] + rt_ref[2 + j, r]
            _row_copy(h_ref, r, xs_ref, dst, sem).start()
        return carry

    lax.fori_loop(0, td, issue, 0)

    def drain(r, carry):
        _row_copy(h_ref, 0, xs_ref, 0, sem).wait()
        return carry

    lax.fori_loop(0, 2 * td, drain, 0)

    @pl.when(pl.program_id(0) == pl.num_programs(0) - 1)
    def _():
        zero_ref[...] = jnp.zeros_like(zero_ref)

        def per_expert(e, carry):
            c = cnt_ref[e]
            n_pad = (bm - c % bm) % bm
            base = pstart_ref[e] + c

            def start(r, cc):
                _row_copy(zero_ref, 0, xs_ref, base + r, sem).start()
                return cc

            def wait(r, cc):
                _row_copy(zero_ref, 0, xs_ref, 0, sem).wait()
                return cc

            lax.fori_loop(0, n_pad, start, 0)
            lax.fori_loop(0, n_pad, wait, 0)
            return carry

        lax.fori_loop(0, N_EXPERTS, per_expert, 0)

        first = nu_ref[0] * (bm // SUBLANES)
        n_tail = xs_ref.shape[0] // SUBLANES - first

        def tail_copy(r):
            return pltpu.make_async_copy(
                zero_ref, xs_ref.at[pl.ds(pl.multiple_of((first + r) * SUBLANES, SUBLANES), SUBLANES), :], sem)

        def tail_start(r, cc):
            tail_copy(r).start()
            return cc

        def tail_wait(r, cc):
            tail_copy(0).wait()
            return cc

        lax.fori_loop(0, n_tail, tail_start, 0)
        lax.fori_loop(0, n_tail, tail_wait, 0)


def _dispatch(h2, route_t, pstart, counts, n_used, n_slots):
    n_tok, d = h2.shape
    td = DISP_TD
    return pl.pallas_call(
        functools.partial(_dispatch_body, td=td, bm=MOE_BM),
        grid_spec=pltpu.PrefetchScalarGridSpec(
            num_scalar_prefetch=3,
            grid=(n_tok // td,),
            in_specs=[pl.BlockSpec((td, d), lambda i, ps, cn, nu: (i, 0)),
                      pl.BlockSpec((SUBLANES, td), lambda i, ps, cn, nu: (0, i), memory_space=pltpu.SMEM)],
            out_specs=pl.BlockSpec(memory_space=pl.ANY),
            scratch_shapes=[pltpu.VMEM((SUBLANES, d), F32), pltpu.SemaphoreType.DMA],
        ),
        out_shape=jax.ShapeDtypeStruct((n_slots, d), F32),
        compiler_params=_cparams(("arbitrary",)),
        name="dispatch",
    )(pstart, counts, n_used, h2, route_t)


def _experts_body(be_ref, nu_ref, x_ref, wg_ref, wu_ref, wd_ref, y_ref, wgb, wub, wdb):
    blk = pl.program_id(0)

    @pl.when(blk < nu_ref[0])
    def _():
        e = be_ref[blk]
        prev = be_ref[jnp.maximum(blk - 1, 0)]

        @pl.when(jnp.logical_or(blk == 0, e != prev))
        def _():
            wgb[...] = wg_ref[0].astype(wgb.dtype)
            wub[...] = wu_ref[0].astype(wub.dtype)
            wdb[...] = wd_ref[0].astype(wdb.dtype)

        xb = x_ref[...].astype(_MXU_DTYPE)
        a = _silu(_dot(xb, wgb[...])) * _dot(xb, wub[...])
        y_ref[...] = _dot(a.astype(_MXU_DTYPE), wdb[...])

    @pl.when(blk >= nu_ref[0])
    def _():
        y_ref[...] = jnp.zeros_like(y_ref)


def _experts(xs, block_exp, n_used, w_gate, w_up, w_down):
    n_slots, d = xs.shape
    bm = MOE_BM
    de = w_gate.shape[2]
    last = lambda i, nu: jnp.minimum(i, nu[0] - 1)
    return pl.pallas_call(
        _experts_body,
        grid_spec=pltpu.PrefetchScalarGridSpec(
            num_scalar_prefetch=2,
            grid=(n_slots // bm,),
            in_specs=[pl.BlockSpec((bm, d), lambda i, be, nu: (last(i, nu), 0)),
                      pl.BlockSpec((1, d, de), lambda i, be, nu: (be[last(i, nu)], 0, 0)),
                      pl.BlockSpec((1, d, de), lambda i, be, nu: (be[last(i, nu)], 0, 0)),
                      pl.BlockSpec((1, de, d), lambda i, be, nu: (be[last(i, nu)], 0, 0))],
            out_specs=pl.BlockSpec((bm, d), lambda i, be, nu: (i, 0)),
            scratch_shapes=[pltpu.VMEM((d, de), _MXU_DTYPE), pltpu.VMEM((d, de), _MXU_DTYPE),
                            pltpu.VMEM((de, d), _MXU_DTYPE)],
        ),
        out_shape=jax.ShapeDtypeStruct((n_slots, d), F32),
        compiler_params=_cparams(("arbitrary",)),
        name="experts",
    )(block_exp, n_used, xs, w_gate, w_up, w_down)


def _combine_body(pstart_ref, x1_ref, gate_ref, rt_ref, mod_ref, g_ref, ys_ref, o_ref, buf_ref, sem, *, tf):
    def issue(r, carry):
        for j in range(2):
            src = pstart_ref[rt_ref[j, r]] + rt_ref[2 + j, r]
            _row_copy(ys_ref, src, buf_ref.at[j], r, sem).start()
        return carry

    lax.fori_loop(0, tf, issue, 0)

    def drain(r, carry):
        _row_copy(ys_ref, 0, buf_ref.at[0], 0, sem).wait()
        return carry

    lax.fori_loop(0, 2 * tf, drain, 0)

    gates = gate_ref[...]
    moe = buf_ref[0] * gates[:, 0:1] + buf_ref[1] * gates[:, 1:2]
    x2 = x1_ref[...] + mod_ref[0, 5:6, :] * moe
    ms = jnp.mean(x2 * x2, axis=-1, keepdims=True)
    o_ref[...] = x2 * lax.rsqrt(ms + EPS) * g_ref[...]


def _combine(x1, gates, route_t, mod, g_final, ys, pstart, seq):
    n_tok, d = x1.shape
    tf = COMB_TF
    return pl.pallas_call(
        functools.partial(_combine_body, tf=tf),
        grid_spec=pltpu.PrefetchScalarGridSpec(
            num_scalar_prefetch=1,
            grid=(n_tok // tf,),
            in_specs=[pl.BlockSpec((tf, d), lambda i, ps: (i, 0)),
                      pl.BlockSpec((tf, LANES), lambda i, ps: (i, 0)),
                      pl.BlockSpec((SUBLANES, tf), lambda i, ps: (0, i), memory_space=pltpu.SMEM),
                      pl.BlockSpec((1, 6, d), lambda i, ps: ((i * tf) // seq, 0, 0)),
                      pl.BlockSpec((1, d), lambda i, ps: (0, 0)),
                      pl.BlockSpec(memory_space=pl.ANY)],
            out_specs=pl.BlockSpec((tf, d), lambda i, ps: (i, 0)),
            scratch_shapes=[pltpu.VMEM((2, tf, d), F32), pltpu.SemaphoreType.DMA],
        ),
        out_shape=jax.ShapeDtypeStruct((n_tok, d), F32),
        compiler_params=_cparams(("arbitrary",)),
        name="combine",
    )(pstart, x1, gates, route_t, mod, g_final, ys)


def kernel(x, c, w_ada, b_ada, g_mix, w_in, w_pool, pool_scale, rel_bias, w_out, g_ffn, w_group, b_group,
           w_router, b_router, w_gate, w_up, w_down, g_final):
    bsz, seq, d = x.shape
    n_tok = bsz * seq
    depth = w_ada.shape[0]
    assert depth == 1, "the final norm is fused into the only layer's combine"
    topk = min(TOPK_MAX, seq // 4)
    bias = _bias_tiles(rel_bias, ATT_T)
    bm = MOE_BM
    n_blocks = -(-(n_tok * 2) // bm) + N_EXPERTS
    n_slots = n_blocks * bm
    xt = x.reshape(n_tok, d)
    for i in range(depth):
        mod = _ada(c, w_ada[i], b_ada[i][None, :]).reshape(bsz, 6, d)
        u, q, k, vt, qi, kia, kib, wit = _proj(xt.reshape(bsz, seq, d), mod, g_mix[i][None, :], w_in[i])
        attn = _dsa(q, qi, wit, kia, kib, k, vt, bias, topk)
        w_r = jnp.concatenate(
            [w_group[i], w_router[i], jnp.zeros((d, LANES - N_GROUPS - N_EXPERTS), F32)], axis=1).astype(_MXU_DTYPE)
        b_r = jnp.concatenate(
            [b_group[i], b_router[i], jnp.zeros((LANES - N_GROUPS - N_EXPERTS,), F32)])[None, :]
        x1, h2, logits = _mix(u.reshape(n_tok, -1), attn.reshape(n_tok, -1), xt, mod,
                              w_pool[i].astype(_MXU_DTYPE), pool_scale[i][None, :],
                              w_out[i].astype(_MXU_DTYPE), g_ffn[i][None, :], w_r, b_r, seq)
        route_t, gates, cnt = _route(logits)
        counts = cnt[0, :N_EXPERTS].astype(I32)
        padded = (counts + bm - 1) // bm * bm
        pends = jnp.cumsum(padded)
        pstart = pends - padded
        n_used = (pends[-1:] // bm).astype(I32)
        block_exp = jnp.minimum(
            jnp.searchsorted(pends, jnp.arange(n_blocks, dtype=I32) * bm, side="right"), N_EXPERTS - 1).astype(I32)
        xs = _dispatch(h2, route_t, pstart, counts, n_used, n_slots)
        ys = _experts(xs, block_exp, n_used, w_gate[i], w_up[i], w_down[i])
        xt = _combine(x1, gates, route_t, mod, g_final[None, :], ys, pstart, seq)
    return xt.reshape(bsz, seq, d)
```

```python
import functools
import math

import numpy as np
import jax
import jax.numpy as jnp
from jax import lax
from jax.experimental import pallas as pl
from jax.experimental.pallas import tpu as pltpu

F32 = jnp.float32
I32 = jnp.int32
_MXU_DTYPE = jnp.bfloat16

POOL_WINDOWS = (2, 4, 8, 16)
N_POOL_GROUPS = 4
HEAD_DIM = 128
N_HEADS = 8
N_KV_HEADS = 2
HEAD_GROUP = N_HEADS // N_KV_HEADS
IDX_HEADS = 16
IDX_DIM = 64
TOPK_MAX = 256
N_BUCKETS = 32
MAX_DISTANCE = 128
N_GROUPS = 4
EXPERTS_PER_GROUP = 8
N_EXPERTS = N_GROUPS * EXPERTS_PER_GROUP
EPS = 1e-6
NEG_INF = -1e30
BIG = 3e38

LANES = 128
SUBLANES = 8
VMEM_LIMIT_BYTES = 56 * 1024 * 1024

ADA_TN = 1024
PROJ_TM = 256
ATT_T = 256
SCORE_SUB = 256
COUNT_RB = 128
MIX_TM = 256
HALO = 16
ROUTE_TR = 256
DISP_TD = 256
MOE_BM = 256
COMB_TF = 128
SEL_MAX_ITERS = 80


def _cparams(sem):
    return pltpu.CompilerParams(dimension_semantics=sem, vmem_limit_bytes=VMEM_LIMIT_BYTES)


def _silu(x):
    return x * (1.0 / (1.0 + jnp.exp(-x)))


def _dot(a, b):
    return jnp.dot(a, b, preferred_element_type=F32)


def _dot_nt(a, b):
    return lax.dot_general(a, b, (((1,), (1,)), ((), ())), preferred_element_type=F32)


def _ada_body(c_ref, w_ref, b_ref, o_ref):
    s = _silu(c_ref[...])
    o_ref[...] = _dot(s.astype(_MXU_DTYPE), w_ref[...].astype(_MXU_DTYPE)) + b_ref[...]


def _ada(c, w, b):
    bsz, d = c.shape
    n = w.shape[1]
    return pl.pallas_call(
        _ada_body,
        grid=(n // ADA_TN,),
        in_specs=[pl.BlockSpec((bsz, d), lambda j: (0, 0)),
                  pl.BlockSpec((d, ADA_TN), lambda j: (0, j)),
                  pl.BlockSpec((1, ADA_TN), lambda j: (0, j))],
        out_specs=pl.BlockSpec((bsz, ADA_TN), lambda j: (0, j)),
        out_shape=jax.ShapeDtypeStruct((bsz, n), F32),
        compiler_params=_cparams(("arbitrary",)),
        name="ada",
    )(c, w, b)


def _rms_mod(x, g, shift, scale):
    ms = jnp.mean(x * x, axis=-1, keepdims=True)
    y = x * lax.rsqrt(ms + EPS) * g
    return y * (1.0 + scale) + shift


def _proj_body(x_ref, mod_ref, g_ref, w_ref, u_ref, q_ref, k_ref, vt_ref, qi_ref, kia_ref, kib_ref,
               wit_ref, *, cuts):
    h = _rms_mod(x_ref[0], g_ref[...], mod_ref[0, 0:1, :], mod_ref[0, 1:2, :])
    hb = h.astype(_MXU_DTYPE)

    def seg(name):
        lo, hi = cuts[name]
        return _dot(hb, w_ref[:, lo:hi])

    u_ref[0] = seg("u")
    q_ref[0] = seg("q").astype(q_ref.dtype)
    k_ref[0] = seg("k").astype(k_ref.dtype)
    vt_ref[0] = seg("v").T.astype(vt_ref.dtype)
    qi_ref[0] = seg("qi").astype(qi_ref.dtype)
    kia_ref[0] = seg("kia").astype(kia_ref.dtype)
    kib_ref[0] = seg("kib").astype(kib_ref.dtype)
    wit_ref[0] = seg("wi").T[:IDX_HEADS, :]


def _proj(x, mod, g_mix, w_in):
    bsz, s, d = x.shape
    pool_w = d // 2
    attn_w = N_HEADS * HEAD_DIM
    kv_w = N_KV_HEADS * HEAD_DIM
    qi_w = IDX_HEADS * IDX_DIM
    c = np.cumsum([0, pool_w, attn_w, kv_w, kv_w, qi_w, IDX_DIM, IDX_HEADS])
    w_ki = w_in[:, c[5]:c[6]]
    w_wi = w_in[:, c[6]:c[7]]
    zk = jnp.zeros((d, LANES - IDX_DIM), w_in.dtype)
    wp = jnp.concatenate(
        [w_in[:, :c[5]], w_ki, zk, zk, w_ki, w_wi, jnp.zeros((d, LANES - IDX_HEADS), w_in.dtype)],
        axis=1).astype(_MXU_DTYPE)
    base = int(c[5])
    cuts = {"u": (int(c[0]), int(c[1])), "q": (int(c[1]), int(c[2])), "k": (int(c[2]), int(c[3])),
            "v": (int(c[3]), int(c[4])), "qi": (int(c[4]), int(c[5])),
            "kia": (base, base + LANES), "kib": (base + LANES, base + 2 * LANES),
            "wi": (base + 2 * LANES, base + 3 * LANES)}
    tm = PROJ_TM
    pw = wp.shape[1]
    row = lambda w: pl.BlockSpec((1, tm, w), lambda b, i: (b, i, 0))
    out_shapes = [
        jax.ShapeDtypeStruct((bsz, s, pool_w), F32),
        jax.ShapeDtypeStruct((bsz, s, attn_w), _MXU_DTYPE),
        jax.ShapeDtypeStruct((bsz, s, kv_w), _MXU_DTYPE),
        jax.ShapeDtypeStruct((bsz, kv_w, s), _MXU_DTYPE),
        jax.ShapeDtypeStruct((bsz, s, qi_w), _MXU_DTYPE),
        jax.ShapeDtypeStruct((bsz, s, LANES), _MXU_DTYPE),
        jax.ShapeDtypeStruct((bsz, s, LANES), _MXU_DTYPE),
        jax.ShapeDtypeStruct((bsz, IDX_HEADS, s), F32),
    ]
    out_specs = [row(pool_w), row(attn_w), row(kv_w),
                 pl.BlockSpec((1, kv_w, tm), lambda b, i: (b, 0, i)),
                 row(qi_w), row(LANES), row(LANES),
                 pl.BlockSpec((1, IDX_HEADS, tm), lambda b, i: (b, 0, i))]
    return pl.pallas_call(
        functools.partial(_proj_body, cuts=cuts),
        grid=(bsz, s // tm),
        in_specs=[pl.BlockSpec((1, tm, d), lambda b, i: (b, i, 0)),
                  pl.BlockSpec((1, 6, d), lambda b, i: (b, 0, 0)),
                  pl.BlockSpec((1, d), lambda b, i: (0, 0)),
                  pl.BlockSpec((d, pw), lambda b, i: (0, 0))],
        out_specs=out_specs,
        out_shape=out_shapes,
        compiler_params=_cparams(("arbitrary", "arbitrary")),
        name="proj",
    )(x, mod, g_mix, wp)


def _bucket_starts():
    max_exact = N_BUCKETS // 2
    d = np.arange(1, 4 * MAX_DISTANCE, dtype=np.float32)
    large = max_exact + (np.log(d / np.float32(max_exact)) / np.float32(math.log(MAX_DISTANCE / max_exact))
                         * np.float32(N_BUCKETS - max_exact)).astype(np.int32)
    large = np.minimum(large, N_BUCKETS - 1)
    bucket = np.where(d < max_exact, d.astype(np.int32), large)
    bucket = np.concatenate([[0], bucket])
    starts = [int(np.argmax(bucket >= b)) for b in range(N_BUCKETS)]
    assert all(np.all(bucket[starts[b]:] >= b) for b in range(N_BUCKETS))
    assert starts[N_BUCKETS - 1] <= MAX_DISTANCE
    return starts


def _bias_body(rb_ref, o_ref, *, t, starts):
    diff = pl.program_id(0)
    s_l = lax.broadcasted_iota(I32, (t, t), 0)
    t_l = lax.broadcasted_iota(I32, (t, t), 1)
    dist = diff * t + t_l - s_l
    for h in range(N_HEADS):
        val = jnp.full((t, t), rb_ref[0, h], F32)
        for b in range(1, N_BUCKETS):
            val = jnp.where(dist >= starts[b], rb_ref[b, h], val)
        hh = h % HEAD_GROUP
        o_ref[0, h // HEAD_GROUP, :, hh * t:(hh + 1) * t] = val


def _bias_tiles(rel_bias, t):
    assert t >= MAX_DISTANCE
    return pl.pallas_call(
        functools.partial(_bias_body, t=t, starts=_bucket_starts()),
        grid=(3,),
        in_specs=[pl.BlockSpec(memory_space=pltpu.SMEM)],
        out_specs=pl.BlockSpec((1, N_KV_HEADS, t, HEAD_GROUP * t), lambda i: (i, 0, 0, 0)),
        out_shape=jax.ShapeDtypeStruct((3, N_KV_HEADS, t, HEAD_GROUP * t), F32),
        compiler_params=_cparams(("arbitrary",)),
        name="bias_tiles",
    )(rel_bias)


def _dsa_body(q_ref, qi_ref, wit_ref, kia_ref, kib_ref, k_ref, vt_ref, bias_ref, o_ref,
              score_ref, mask_ref, *, t, topk):
    i = pl.program_id(1)
    nk = i + 1
    t_glob = i * t + lax.broadcasted_iota(I32, (1, t), 1)
    idx_scale = (IDX_DIM ** -0.5) * (IDX_HEADS ** -0.5)

    def score_chunk(c, carry):
        r0 = pl.multiple_of(c * SCORE_SUB, SCORE_SUB)
        ka = kia_ref[0, pl.ds(r0, SCORE_SUB), :]
        kb = kib_ref[0, pl.ds(r0, SCORE_SUB), :]
        acc = jnp.zeros((SCORE_SUB, t), F32)
        for j in range(IDX_HEADS // 2):
            qp = qi_ref[0, :, j * LANES:(j + 1) * LANES]
            da = _dot_nt(ka, qp)
            db = _dot_nt(kb, qp)
            acc = acc + jnp.maximum(da, 0.0) * wit_ref[0, 2 * j:2 * j + 1, :]
            acc = acc + jnp.maximum(db, 0.0) * wit_ref[0, 2 * j + 1:2 * j + 2, :]
        sc = acc * idx_scale
        s_glob = r0 + lax.broadcasted_iota(I32, (SCORE_SUB, 1), 0)
        score_ref[pl.ds(r0, SCORE_SUB), :] = jnp.where(s_glob <= t_glob, sc, NEG_INF)
        return carry

    lax.fori_loop(0, nk * (t // SCORE_SUB), score_chunk, 0)

    nblk = nk * (t // COUNT_RB)
    fold = lambda m: jnp.sum(m.reshape(COUNT_RB // SUBLANES, SUBLANES, t), axis=0)

    def blk(r):
        return score_ref[pl.ds(pl.multiple_of(r * COUNT_RB, COUNT_RB), COUNT_RB), :]

    def count_ge(c):
        def body(r, acc):
            return acc + fold((blk(r) >= c).astype(I32))
        acc = lax.fori_loop(0, nblk, body, jnp.zeros((SUBLANES, t), I32))
        return jnp.sum(acc, axis=0, keepdims=True)

    def stats_body(r, carry):
        cnt, mn, mx = carry
        b = blk(r)
        real = b > NEG_INF
        cnt = cnt + fold(real.astype(I32))
        mn = jnp.minimum(mn, jnp.min(jnp.where(real, b, BIG).reshape(COUNT_RB // SUBLANES, SUBLANES, t), axis=0))
        mx = jnp.maximum(mx, jnp.max(b.reshape(COUNT_RB // SUBLANES, SUBLANES, t), axis=0))
        return cnt, mn, mx

    cnt8, mn8, mx8 = lax.fori_loop(
        0, nblk, stats_body,
        (jnp.zeros((SUBLANES, t), I32), jnp.full((SUBLANES, t), BIG, F32), jnp.full((SUBLANES, t), -BIG, F32)))
    n_real = jnp.sum(cnt8, axis=0, keepdims=True)
    rmin = jnp.min(mn8, axis=0, keepdims=True)
    rmax = jnp.max(mx8, axis=0, keepdims=True)
    n_max = count_ge(rmax)
    degenerate = jnp.logical_and(n_real >= topk, n_max >= topk)
    lo0 = jnp.where(n_real < topk, NEG_INF, jnp.where(degenerate, rmax, rmin))
    hi0 = jnp.where(degenerate, BIG, rmax)
    done0 = jnp.where(n_real <= topk, 1, 0)

    def sel_cond(st):
        it, lo, hi, done, stuck = st
        return jnp.logical_and(it < SEL_MAX_ITERS, jnp.min(done + stuck) == 0)

    def sel_body(st):
        it, lo, hi, done, stuck = st
        c = lo + (hi - lo) * 0.5
        active = (done + stuck) == 0
        has_mid = jnp.logical_and(c > lo, c < hi)
        n = count_ge(c)
        upd = jnp.logical_and(active, has_mid)
        lo = jnp.where(jnp.logical_and(upd, n >= topk), c, lo)
        hi = jnp.where(jnp.logical_and(upd, n < topk), c, hi)
        done = jnp.where(jnp.logical_and(upd, n == topk), 1, done)
        stuck = jnp.where(jnp.logical_and(active, jnp.logical_not(has_mid)), 1, stuck)
        return it + 1, lo, hi, done, stuck

    _, lo, hi, done, _ = lax.while_loop(
        sel_cond, sel_body, (jnp.int32(0), lo0, hi0, done0, jnp.zeros((1, t), I32)))

    def write_mask(r, cut):
        r0 = pl.multiple_of(r * COUNT_RB, COUNT_RB)
        b = score_ref[pl.ds(r0, COUNT_RB), :]
        s_glob = r0 + lax.broadcasted_iota(I32, (COUNT_RB, 1), 0)
        sel = jnp.logical_and(b >= lo, s_glob <= t_glob)
        if cut is not None:
            sel = jnp.logical_and(sel, jnp.logical_or(b >= hi, s_glob < cut))
        mask_ref[pl.ds(r0, COUNT_RB), :] = jnp.where(sel, 0.0, NEG_INF)

    any_tied = jnp.min(done) == 0

    @pl.when(jnp.logical_not(any_tied))
    def _():
        def body(r, carry):
            write_mask(r, None)
            return carry
        lax.fori_loop(0, nblk, body, 0)

    @pl.when(any_tied)
    def _():
        need = topk - count_ge(hi)

        def count_tie_below(x):
            def body(r, acc):
                r0 = pl.multiple_of(r * COUNT_RB, COUNT_RB)
                b = score_ref[pl.ds(r0, COUNT_RB), :]
                s_glob = r0 + lax.broadcasted_iota(I32, (COUNT_RB, 1), 0)
                m = jnp.logical_and(jnp.logical_and(b >= lo, b < hi), s_glob < x)
                return acc + fold(m.astype(I32))
            acc = lax.fori_loop(0, nblk, body, jnp.zeros((SUBLANES, t), I32))
            return jnp.sum(acc, axis=0, keepdims=True)

        nbits = int(score_ref.shape[0]).bit_length()

        def bit_body(bi, x):
            c = x + jnp.left_shift(jnp.int32(1), nbits - 1 - bi)
            return jnp.where(count_tie_below(c) < need, c, x)

        x = lax.fori_loop(0, nbits, bit_body, jnp.zeros((1, t), I32))
        cut = jnp.where(done > 0, jnp.int32(2 ** 30), x + 1)

        def body(r, carry):
            write_mask(r, cut)
            return carry
        lax.fori_loop(0, nblk, body, 0)

    scale = HEAD_DIM ** -0.5
    gt = HEAD_GROUP * t
    for g in range(N_KV_HEADS):
        qg = jnp.concatenate(
            [q_ref[0, :, h * HEAD_DIM:(h + 1) * HEAD_DIM] for h in range(g * HEAD_GROUP, (g + 1) * HEAD_GROUP)],
            axis=0)

        def att_chunk(kc, carry, g=g, qg=qg):
            m, l, acc = carry
            r0 = pl.multiple_of(kc * t, t)
            kk = k_ref[0, pl.ds(r0, t), g * HEAD_DIM:(g + 1) * HEAD_DIM]
            mk = mask_ref[pl.ds(r0, t), :]
            s = _dot_nt(kk, qg) * scale
            s = s + bias_ref[jnp.minimum(i - kc, 2), g] + jnp.concatenate([mk] * HEAD_GROUP, axis=1)
            m_new = jnp.maximum(m, jnp.max(s, axis=0, keepdims=True))
            alpha = jnp.exp(m - m_new)
            p = jnp.exp(s - m_new)
            l = alpha * l + jnp.sum(p, axis=0, keepdims=True)
            vt = vt_ref[0, g * HEAD_DIM:(g + 1) * HEAD_DIM, pl.ds(r0, t)]
            acc = alpha * acc + _dot(vt, p.astype(_MXU_DTYPE))
            return m_new, l, acc

        m, l, acc = lax.fori_loop(
            0, nk, att_chunk,
            (jnp.full((1, gt), NEG_INF, F32), jnp.zeros((1, gt), F32), jnp.zeros((HEAD_DIM, gt), F32)))
        out = acc / l
        for hh in range(HEAD_GROUP):
            h = g * HEAD_GROUP + hh
            o_ref[0, :, h * HEAD_DIM:(h + 1) * HEAD_DIM] = out[:, hh * t:(hh + 1) * t].T.astype(o_ref.dtype)


def _dsa(q, qi, wit, kia, kib, k, vt, bias, topk):
    bsz, s, aw = q.shape
    t = ATT_T
    assert s % t == 0 and topk <= t
    row = lambda w: pl.BlockSpec((1, t, w), lambda b, i: (b, i, 0))
    full = lambda a: pl.BlockSpec((1,) + a.shape[1:], lambda b, i: (b, 0, 0))
    return pl.pallas_call(
        functools.partial(_dsa_body, t=t, topk=topk),
        grid=(bsz, s // t),
        in_specs=[row(aw), row(qi.shape[2]),
                  pl.BlockSpec((1, IDX_HEADS, t), lambda b, i: (b, 0, i)),
                  full(kia), full(kib), full(k), full(vt),
                  pl.BlockSpec(bias.shape, lambda b, i: (0, 0, 0, 0))],
        out_specs=row(aw),
        out_shape=jax.ShapeDtypeStruct((bsz, s, aw), _MXU_DTYPE),
        scratch_shapes=[pltpu.VMEM((s, t), F32), pltpu.VMEM((s, t), F32)],
        compiler_params=_cparams(("arbitrary", "arbitrary")),
        name="dsa",
    )(q, qi, wit, kia, kib, k, vt, bias)


def _mix_body(u_ref, halo_ref, a_ref, x_ref, mod_ref, wpool_ref, ps_ref, wout_ref, g_ref, wr_ref, br_ref,
              x1_ref, h2_ref, lg_ref, *, seq, tm):
    t0 = (pl.program_id(0) * tm) % seq
    u = u_ref[...]
    halo = jnp.where(t0 == 0, 0.0, halo_ref[...])
    ext = jnp.concatenate([halo, u], axis=0)
    pos = t0 + lax.broadcasted_iota(I32, (tm, 1), 0)
    gw = u.shape[1] // N_POOL_GROUPS
    ys = []
    for g, w in enumerate(POOL_WINDOWS):
        a = ext[:, g * gw:(g + 1) * gw]
        sft = 1
        while sft < w:
            a = a + pltpu.roll(a, sft, 0)
            sft *= 2
        cnt = jnp.minimum(pos + 1, w).astype(F32)
        p = a[HALO:, :] / cnt - u[:, g * gw:(g + 1) * gw]
        ys.append(_dot(p.astype(_MXU_DTYPE), wpool_ref[g]) * ps_ref[:, g * gw:(g + 1) * gw])
    pool = jnp.concatenate(ys, axis=1).astype(_MXU_DTYPE)
    pw = pool.shape[1]
    mixed = _dot(pool, wout_ref[:pw, :]) + _dot(a_ref[...], wout_ref[pw:, :])
    x1 = x_ref[...] + mod_ref[0, 2:3, :] * mixed
    x1_ref[...] = x1
    h2 = _rms_mod(x1, g_ref[...], mod_ref[0, 3:4, :], mod_ref[0, 4:5, :])
    h2_ref[...] = h2
    lg_ref[...] = _dot(h2.astype(_MXU_DTYPE), wr_ref[...]) + br_ref[...]


def _mix(u, attn, x, mod, w_pool, pool_scale, w_out, g_ffn, w_r, b_r, seq):
    n_tok, d = x.shape
    tm = MIX_TM
    assert seq % tm == 0 and tm % HALO == 0
    pw = u.shape[1]
    aw = attn.shape[1]
    rows = lambda w: pl.BlockSpec((tm, w), lambda i: (i, 0))
    const = lambda a: pl.BlockSpec(a.shape, lambda i: (0,) * a.ndim)
    return pl.pallas_call(
        functools.partial(_mix_body, seq=seq, tm=tm),
        grid=(n_tok // tm,),
        in_specs=[rows(pw),
                  pl.BlockSpec((HALO, pw), lambda i: (jnp.maximum(i * (tm // HALO) - 1, 0), 0)),
                  rows(aw), rows(d),
                  pl.BlockSpec((1, 6, d), lambda i: ((i * tm) // seq, 0, 0)),
                  const(w_pool), const(pool_scale), const(w_out), const(g_ffn), const(w_r), const(b_r)],
        out_specs=[rows(d), rows(d), rows(LANES)],
        out_shape=[jax.ShapeDtypeStruct((n_tok, d), F32), jax.ShapeDtypeStruct((n_tok, d), F32),
                   jax.ShapeDtypeStruct((n_tok, LANES), F32)],
        compiler_params=_cparams(("arbitrary",)),
        name="mix",
    )(u, u, attn, x, mod, w_pool, pool_scale, w_out, g_ffn, w_r, b_r)


def _route_body(lg_ref, rt_ref, gate_ref, cnt_ref, run_ref):
    @pl.when(pl.program_id(0) == 0)
    def _():
        run_ref[...] = jnp.zeros_like(run_ref)

    lg = lg_ref[...]
    tr = lg.shape[0]
    lane = lax.broadcasted_iota(I32, lg.shape, 1)
    gmask = lane < N_GROUPS
    gl = jnp.where(gmask, lg, -BIG)
    gmax = jnp.max(gl, axis=-1, keepdims=True)
    gsel = jnp.min(jnp.where(jnp.logical_and(gmask, gl == gmax), lane, LANES), axis=-1, keepdims=True)
    gsum = jnp.sum(jnp.where(gmask, jnp.exp(gl - gmax), 0.0), axis=-1, keepdims=True)
    p_g = 1.0 / gsum
    in_e = jnp.logical_and(lane >= N_GROUPS, lane < N_GROUPS + N_EXPERTS)
    emask = jnp.logical_and(in_e, lax.shift_right_arithmetic(lane - N_GROUPS, EXPERTS_PER_GROUP.bit_length() - 1) == gsel)
    el = jnp.where(emask, lg, -BIG)
    emax = jnp.max(el, axis=-1, keepdims=True)
    ex = jnp.where(emask, jnp.exp(el - emax), 0.0)
    ep = ex / jnp.sum(ex, axis=-1, keepdims=True)
    p1 = jnp.max(jnp.where(emask, ep, -1.0), axis=-1, keepdims=True)
    i1 = jnp.min(jnp.where(jnp.logical_and(emask, ep == p1), lane, LANES), axis=-1, keepdims=True)
    m2 = jnp.logical_and(emask, lane != i1)
    p2 = jnp.max(jnp.where(m2, ep, -1.0), axis=-1, keepdims=True)
    i2 = jnp.min(jnp.where(jnp.logical_and(m2, ep == p2), lane, LANES), axis=-1, keepdims=True)
    den = p1 + p2
    g1 = p_g * p1 / den
    g2 = p_g * p2 / den
    e1 = i1 - N_GROUPS
    e2 = i2 - N_GROUPS

    oh = jnp.logical_or(lane == e1, lane == e2)
    ri = lax.broadcasted_iota(I32, (tr, tr), 0)
    ci = lax.broadcasted_iota(I32, (tr, tr), 1)
    ltri = (ci < ri).astype(_MXU_DTYPE)
    before = _dot(ltri, oh.astype(_MXU_DTYPE)) + run_ref[...]
    r1 = jnp.sum(jnp.where(lane == e1, before, 0.0), axis=-1, keepdims=True)
    r2 = jnp.sum(jnp.where(lane == e2, before, 0.0), axis=-1, keepdims=True)
    run = run_ref[...] + jnp.sum(oh.astype(F32), axis=0, keepdims=True)
    run_ref[...] = run
    cnt_ref[...] = jnp.broadcast_to(run, cnt_ref.shape)

    route = jnp.where(lane == 0, e1.astype(F32),
                      jnp.where(lane == 1, e2.astype(F32),
                                jnp.where(lane == 2, r1, jnp.where(lane == 3, r2, 0.0))))
    rt_ref[...] = route.T[:SUBLANES, :].astype(I32)
    gate_ref[...] = jnp.where(lane == 0, g1, jnp.where(lane == 1, g2, 0.0))


def _route(logits):
    n_tok = logits.shape[0]
    tr = ROUTE_TR
    return pl.pallas_call(
        _route_body,
        grid=(n_tok // tr,),
        in_specs=[pl.BlockSpec((tr, LANES), lambda i: (i, 0))],
        out_specs=[pl.BlockSpec((SUBLANES, tr), lambda i: (0, i)),
                   pl.BlockSpec((tr, LANES), lambda i: (i, 0)),
                   pl.BlockSpec((SUBLANES, LANES), lambda i: (0, 0))],
        out_shape=[jax.ShapeDtypeStruct((SUBLANES, n_tok), I32),
                   jax.ShapeDtypeStruct((n_tok, LANES), F32),
                   jax.ShapeDtypeStruct((SUBLANES, LANES), F32)],
        scratch_shapes=[pltpu.VMEM((1, LANES), F32)],
        compiler_params=_cparams(("arbitrary",)),
        name="route",
    )(logits)


def _row_copy(src_ref, src_row, dst_ref, dst_row, sem):
    return pltpu.make_async_copy(src_ref.at[pl.ds(src_row, 1), :], dst_ref.at[pl.ds(dst_row, 1), :], sem)


def _dispatch_body(pstart_ref, cnt_ref, nu_ref, h_ref, rt_ref, xs_ref, zero_ref, sem, *, td, bm):
    def issue(r, carry):
        for j in range(2):
            dst = pstart_ref[rt_ref[j, r]] + rt_ref[2 + j, r]
            _row_copy(h_ref, r, xs_ref, dst, sem).start()
        return carry

    lax.fori_loop(0, td, issue, 0)

    def drain(r, carry):
        _row_copy(h_ref, 0, xs_ref, 0, sem).wait()
        return carry

    lax.fori_loop(0, 2 * td, drain, 0)

    @pl.when(pl.program_id(0) == pl.num_programs(0) - 1)
    def _():
        zero_ref[...] = jnp.zeros_like(zero_ref)

        def per_expert(e, carry):
            c = cnt_ref[e]
            n_pad = (bm - c % bm) % bm
            base = pstart_ref[e] + c

            def start(r, cc):
                _row_copy(zero_ref, 0, xs_ref, base + r, sem).start()
                return cc

            def wait(r, cc):
                _row_copy(zero_ref, 0, xs_ref, 0, sem).wait()
                return cc

            lax.fori_loop(0, n_pad, start, 0)
            lax.fori_loop(0, n_pad, wait, 0)
            return carry

        lax.fori_loop(0, N_EXPERTS, per_expert, 0)

        first = nu_ref[0] * (bm // SUBLANES)
        n_tail = xs_ref.shape[0] // SUBLANES - first

        def tail_copy(r):
            return pltpu.make_async_copy(
                zero_ref, xs_ref.at[pl.ds(pl.multiple_of((first + r) * SUBLANES, SUBLANES), SUBLANES), :], sem)

        def tail_start(r, cc):
            tail_copy(r).start()
            return cc

        def tail_wait(r, cc):
            tail_copy(0).wait()
            return cc

        lax.fori_loop(0, n_tail, tail_start, 0)
        lax.fori_loop(0, n_tail, tail_wait, 0)


def _dispatch(h2, route_t, pstart, counts, n_used, n_slots):
    n_tok, d = h2.shape
    td = DISP_TD
    return pl.pallas_call(
        functools.partial(_dispatch_body, td=td, bm=MOE_BM),
        grid_spec=pltpu.PrefetchScalarGridSpec(
            num_scalar_prefetch=3,
            grid=(n_tok // td,),
            in_specs=[pl.BlockSpec((td, d), lambda i, ps, cn, nu: (i, 0)),
                      pl.BlockSpec((SUBLANES, td), lambda i, ps, cn, nu: (0, i), memory_space=pltpu.SMEM)],
            out_specs=pl.BlockSpec(memory_space=pl.ANY),
            scratch_shapes=[pltpu.VMEM((SUBLANES, d), F32), pltpu.SemaphoreType.DMA],
        ),
        out_shape=jax.ShapeDtypeStruct((n_slots, d), F32),
        compiler_params=_cparams(("arbitrary",)),
        name="dispatch",
    )(pstart, counts, n_used, h2, route_t)


def _experts_body(be_ref, nu_ref, x_ref, wg_ref, wu_ref, wd_ref, y_ref, wgb, wub, wdb):
    blk = pl.program_id(0)

    @pl.when(blk < nu_ref[0])
    def _():
        e = be_ref[blk]
        prev = be_ref[jnp.maximum(blk - 1, 0)]

        @pl.when(jnp.logical_or(blk == 0, e != prev))
        def _():
            wgb[...] = wg_ref[0].astype(wgb.dtype)
            wub[...] = wu_ref[0].astype(wub.dtype)
            wdb[...] = wd_ref[0].astype(wdb.dtype)

        xb = x_ref[...].astype(_MXU_DTYPE)
        a = _silu(_dot(xb, wgb[...])) * _dot(xb, wub[...])
        y_ref[...] = _dot(a.astype(_MXU_DTYPE), wdb[...])

    @pl.when(blk >= nu_ref[0])
    def _():
        y_ref[...] = jnp.zeros_like(y_ref)


def _experts(xs, block_exp, n_used, w_gate, w_up, w_down):
    n_slots, d = xs.shape
    bm = MOE_BM
    de = w_gate.shape[2]
    last = lambda i, nu: jnp.minimum(i, nu[0] - 1)
    return pl.pallas_call(
        _experts_body,
        grid_spec=pltpu.PrefetchScalarGridSpec(
            num_scalar_prefetch=2,
            grid=(n_slots // bm,),
            in_specs=[pl.BlockSpec((bm, d), lambda i, be, nu: (last(i, nu), 0)),
                      pl.BlockSpec((1, d, de), lambda i, be, nu: (be[last(i, nu)], 0, 0)),
                      pl.BlockSpec((1, d, de), lambda i, be, nu: (be[last(i, nu)], 0, 0)),
                      pl.BlockSpec((1, de, d), lambda i, be, nu: (be[last(i, nu)], 0, 0))],
            out_specs=pl.BlockSpec((bm, d), lambda i, be, nu: (i, 0)),
            scratch_shapes=[pltpu.VMEM((d, de), _MXU_DTYPE), pltpu.VMEM((d, de), _MXU_DTYPE),
                            pltpu.VMEM((de, d), _MXU_DTYPE)],
        ),
        out_shape=jax.ShapeDtypeStruct((n_slots, d), F32),
        compiler_params=_cparams(("arbitrary",)),
        name="experts",
    )(block_exp, n_used, xs, w_gate, w_up, w_down)


def _combine_body(pstart_ref, x1_ref, gate_ref, rt_ref, mod_ref, g_ref, ys_ref, o_ref, buf_ref, sem, *, tf):
    def issue(r, carry):
        for j in range(2):
            src = pstart_ref[rt_ref[j, r]] + rt_ref[2 + j, r]
            _row_copy(ys_ref, src, buf_ref.at[j], r, sem).start()
        return carry

    lax.fori_loop(0, tf, issue, 0)

    def drain(r, carry):
        _row_copy(ys_ref, 0, buf_ref.at[0], 0, sem).wait()
        return carry

    lax.fori_loop(0, 2 * tf, drain, 0)

    gates = gate_ref[...]
    moe = buf_ref[0] * gates[:, 0:1] + buf_ref[1] * gates[:, 1:2]
    x2 = x1_ref[...] + mod_ref[0, 5:6, :] * moe
    ms = jnp.mean(x2 * x2, axis=-1, keepdims=True)
    o_ref[...] = x2 * lax.rsqrt(ms + EPS) * g_ref[...]


def _combine(x1, gates, route_t, mod, g_final, ys, pstart, seq):
    n_tok, d = x1.shape
    tf = COMB_TF
    return pl.pallas_call(
        functools.partial(_combine_body, tf=tf),
        grid_spec=pltpu.PrefetchScalarGridSpec(
            num_scalar_prefetch=1,
            grid=(n_tok // tf,),
            in_specs=[pl.BlockSpec((tf, d), lambda i, ps: (i, 0)),
                      pl.BlockSpec((tf, LANES), lambda i, ps: (i, 0)),
                      pl.BlockSpec((SUBLANES, tf), lambda i, ps: (0, i), memory_space=pltpu.SMEM),
                      pl.BlockSpec((1, 6, d), lambda i, ps: ((i * tf) // seq, 0, 0)),
                      pl.BlockSpec((1, d), lambda i, ps: (0, 0)),
                      pl.BlockSpec(memory_space=pl.ANY)],
            out_specs=pl.BlockSpec((tf, d), lambda i, ps: (i, 0)),
            scratch_shapes=[pltpu.VMEM((2, tf, d), F32), pltpu.SemaphoreType.DMA],
        ),
        out_shape=jax.ShapeDtypeStruct((n_tok, d), F32),
        compiler_params=_cparams(("arbitrary",)),
        name="combine",
    )(pstart, x1, gates, route_t, mod, g_final, ys)


def kernel(x, c, w_ada, b_ada, g_mix, w_in, w_pool, pool_scale, rel_bias, w_out, g_ffn, w_group, b_group,
           w_router, b_router, w_gate, w_up, w_down, g_final):
    bsz, seq, d = x.shape
    n_tok = bsz * seq
    depth = w_ada.shape[0]
    assert depth == 1, "the final norm is fused into the only layer's combine"
    topk = min(TOPK_MAX, seq // 4)
    bias = _bias_tiles(rel_bias, ATT_T)
    bm = MOE_BM
    n_blocks = -(-(n_tok * 2) // bm) + N_EXPERTS
    n_slots = n_blocks * bm
    xt = x.reshape(n_tok, d)
    for i in range(depth):
        mod = _ada(c, w_ada[i], b_ada[i][None, :]).reshape(bsz, 6, d)
        u, q, k, vt, qi, kia, kib, wit = _proj(xt.reshape(bsz, seq, d), mod, g_mix[i][None, :], w_in[i])
        attn = _dsa(q, qi, wit, kia, kib, k, vt, bias, topk)
        w_r = jnp.concatenate(
            [w_group[i], w_router[i], jnp.zeros((d, LANES - N_GROUPS - N_EXPERTS), F32)], axis=1).astype(_MXU_DTYPE)
        b_r = jnp.concatenate(
            [b_group[i], b_router[i], jnp.zeros((LANES - N_GROUPS - N_EXPERTS,), F32)])[None, :]
        x1, h2, logits = _mix(u.reshape(n_tok, -1), attn.reshape(n_tok, -1), xt, mod,
                              w_pool[i].astype(_MXU_DTYPE), pool_scale[i][None, :],
                              w_out[i].astype(_MXU_DTYPE), g_ffn[i][None, :], w_r, b_r, seq)
        route_t, gates, cnt = _route(logits)
        counts = cnt[0, :N_EXPERTS].astype(I32)
        padded = (counts + bm - 1) // bm * bm
        pends = jnp.cumsum(padded)
        pstart = pends - padded
        n_used = (pends[-1:] // bm).astype(I32)
        block_start = jnp.arange(n_blocks, dtype=I32) * bm
        block_exp = jnp.minimum(
            jnp.sum((pends[None, :] <= block_start[:, None]).astype(I32), axis=1), N_EXPERTS - 1)
        xs = _dispatch(h2, route_t, pstart, counts, n_used, n_slots)
        ys = _experts(xs, block_exp, n_used, w_gate[i], w_up[i], w_down[i])
        xt = _combine(x1, gates, route_t, mod, g_final[None, :], ys, pstart, seq)
    return xt.reshape(bsz, seq, d)
```

```python
import functools
import math

import numpy as np
import jax
import jax.numpy as jnp
from jax import lax
from jax.experimental import pallas as pl
from jax.experimental.pallas import tpu as pltpu

F32 = jnp.float32
I32 = jnp.int32
_MXU_DTYPE = jnp.bfloat16

POOL_WINDOWS = (2, 4, 8, 16)
N_POOL_GROUPS = 4
HEAD_DIM = 128
N_HEADS = 8
N_KV_HEADS = 2
HEAD_GROUP = N_HEADS // N_KV_HEADS
IDX_HEADS = 16
IDX_DIM = 64
TOPK_MAX = 256
N_BUCKETS = 32
MAX_DISTANCE = 128
N_GROUPS = 4
EXPERTS_PER_GROUP = 8
N_EXPERTS = N_GROUPS * EXPERTS_PER_GROUP
EPS = 1e-6
NEG_INF = -1e30
BIG = 3e38

LANES = 128
SUBLANES = 8
VMEM_LIMIT_BYTES = 56 * 1024 * 1024

ADA_TN = 1024
PROJ_TM = 256
ATT_T = 256
SCORE_SUB = 256
COUNT_RB = 128
MIX_TM = 256
HALO = 16
ROUTE_TR = 256
DISP_TD = 256
MOE_BM = 256
COMB_TF = 256
DMA_UNROLL = 8
SEL_MAX_ITERS = 80


def _cparams(sem):
    return pltpu.CompilerParams(dimension_semantics=sem, vmem_limit_bytes=VMEM_LIMIT_BYTES)


def _silu(x):
    return x * (1.0 / (1.0 + jnp.exp(-x)))


def _dot(a, b):
    return jnp.dot(a, b, preferred_element_type=F32)


def _dot_nt(a, b):
    return lax.dot_general(a, b, (((1,), (1,)), ((), ())), preferred_element_type=F32)


def _ada_body(c_ref, w_ref, b_ref, o_ref):
    s = _silu(c_ref[...])
    o_ref[...] = _dot(s.astype(_MXU_DTYPE), w_ref[...].astype(_MXU_DTYPE)) + b_ref[...]


def _ada(c, w, b):
    bsz, d = c.shape
    n = w.shape[1]
    return pl.pallas_call(
        _ada_body,
        grid=(n // ADA_TN,),
        in_specs=[pl.BlockSpec((bsz, d), lambda j: (0, 0)),
                  pl.BlockSpec((d, ADA_TN), lambda j: (0, j)),
                  pl.BlockSpec((1, ADA_TN), lambda j: (0, j))],
        out_specs=pl.BlockSpec((bsz, ADA_TN), lambda j: (0, j)),
        out_shape=jax.ShapeDtypeStruct((bsz, n), F32),
        compiler_params=_cparams(("arbitrary",)),
        name="ada",
    )(c, w, b)


def _rms_mod(x, g, shift, scale):
    ms = jnp.mean(x * x, axis=-1, keepdims=True)
    y = x * lax.rsqrt(ms + EPS) * g
    return y * (1.0 + scale) + shift


def _proj_body(x_ref, mod_ref, g_ref, w_ref, u_ref, q_ref, k_ref, vt_ref, qi_ref, kia_ref, kib_ref,
               wit_ref, *, cuts):
    h = _rms_mod(x_ref[0], g_ref[...], mod_ref[0, 0:1, :], mod_ref[0, 1:2, :])
    hb = h.astype(_MXU_DTYPE)

    def seg(name):
        lo, hi = cuts[name]
        return _dot(hb, w_ref[:, lo:hi])

    u_ref[0] = seg("u")
    q_ref[0] = seg("q").astype(q_ref.dtype)
    k_ref[0] = seg("k").astype(k_ref.dtype)
    vt_ref[0] = seg("v").T.astype(vt_ref.dtype)
    qi_ref[0] = seg("qi").astype(qi_ref.dtype)
    kia_ref[0] = seg("kia").astype(kia_ref.dtype)
    kib_ref[0] = seg("kib").astype(kib_ref.dtype)
    wit_ref[0] = seg("wi").T[:IDX_HEADS, :]


def _proj(x, mod, g_mix, w_in):
    bsz, s, d = x.shape
    pool_w = d // 2
    attn_w = N_HEADS * HEAD_DIM
    kv_w = N_KV_HEADS * HEAD_DIM
    qi_w = IDX_HEADS * IDX_DIM
    c = np.cumsum([0, pool_w, attn_w, kv_w, kv_w, qi_w, IDX_DIM, IDX_HEADS])
    w_ki = w_in[:, c[5]:c[6]]
    w_wi = w_in[:, c[6]:c[7]]
    zk = jnp.zeros((d, LANES - IDX_DIM), w_in.dtype)
    wp = jnp.concatenate(
        [w_in[:, :c[5]], w_ki, zk, zk, w_ki, w_wi, jnp.zeros((d, LANES - IDX_HEADS), w_in.dtype)],
        axis=1).astype(_MXU_DTYPE)
    base = int(c[5])
    cuts = {"u": (int(c[0]), int(c[1])), "q": (int(c[1]), int(c[2])), "k": (int(c[2]), int(c[3])),
            "v": (int(c[3]), int(c[4])), "qi": (int(c[4]), int(c[5])),
            "kia": (base, base + LANES), "kib": (base + LANES, base + 2 * LANES),
            "wi": (base + 2 * LANES, base + 3 * LANES)}
    tm = PROJ_TM
    pw = wp.shape[1]
    row = lambda w: pl.BlockSpec((1, tm, w), lambda b, i: (b, i, 0))
    out_shapes = [
        jax.ShapeDtypeStruct((bsz, s, pool_w), F32),
        jax.ShapeDtypeStruct((bsz, s, attn_w), _MXU_DTYPE),
        jax.ShapeDtypeStruct((bsz, s, kv_w), _MXU_DTYPE),
        jax.ShapeDtypeStruct((bsz, kv_w, s), _MXU_DTYPE),
        jax.ShapeDtypeStruct((bsz, s, qi_w), _MXU_DTYPE),
        jax.ShapeDtypeStruct((bsz, s, LANES), _MXU_DTYPE),
        jax.ShapeDtypeStruct((bsz, s, LANES), _MXU_DTYPE),
        jax.ShapeDtypeStruct((bsz, IDX_HEADS, s), F32),
    ]
    out_specs = [row(pool_w), row(attn_w), row(kv_w),
                 pl.BlockSpec((1, kv_w, tm), lambda b, i: (b, 0, i)),
                 row(qi_w), row(LANES), row(LANES),
                 pl.BlockSpec((1, IDX_HEADS, tm), lambda b, i: (b, 0, i))]
    return pl.pallas_call(
        functools.partial(_proj_body, cuts=cuts),
        grid=(bsz, s // tm),
        in_specs=[pl.BlockSpec((1, tm, d), lambda b, i: (b, i, 0)),
                  pl.BlockSpec((1, 6, d), lambda b, i: (b, 0, 0)),
                  pl.BlockSpec((1, d), lambda b, i: (0, 0)),
                  pl.BlockSpec((d, pw), lambda b, i: (0, 0))],
        out_specs=out_specs,
        out_shape=out_shapes,
        compiler_params=_cparams(("arbitrary", "arbitrary")),
        name="proj",
    )(x, mod, g_mix, wp)


def _bucket_starts():
    max_exact = N_BUCKETS // 2
    d = np.arange(1, 4 * MAX_DISTANCE, dtype=np.float32)
    large = max_exact + (np.log(d / np.float32(max_exact)) / np.float32(math.log(MAX_DISTANCE / max_exact))
                         * np.float32(N_BUCKETS - max_exact)).astype(np.int32)
    large = np.minimum(large, N_BUCKETS - 1)
    bucket = np.where(d < max_exact, d.astype(np.int32), large)
    bucket = np.concatenate([[0], bucket])
    starts = [int(np.argmax(bucket >= b)) for b in range(N_BUCKETS)]
    assert all(np.all(bucket[starts[b]:] >= b) for b in range(N_BUCKETS))
    assert starts[N_BUCKETS - 1] <= MAX_DISTANCE
    return starts


def _bias_body(rb_ref, o_ref, *, t, starts):
    diff = pl.program_id(0)
    s_l = lax.broadcasted_iota(I32, (t, t), 0)
    t_l = lax.broadcasted_iota(I32, (t, t), 1)
    dist = diff * t + t_l - s_l
    for h in range(N_HEADS):
        val = jnp.full((t, t), rb_ref[0, h], F32)
        for b in range(1, N_BUCKETS):
            val = jnp.where(dist >= starts[b], rb_ref[b, h], val)
        hh = h % HEAD_GROUP
        o_ref[0, h // HEAD_GROUP, :, hh * t:(hh + 1) * t] = val


def _bias_tiles(rel_bias, t):
    assert t >= MAX_DISTANCE
    return pl.pallas_call(
        functools.partial(_bias_body, t=t, starts=_bucket_starts()),
        grid=(3,),
        in_specs=[pl.BlockSpec(memory_space=pltpu.SMEM)],
        out_specs=pl.BlockSpec((1, N_KV_HEADS, t, HEAD_GROUP * t), lambda i: (i, 0, 0, 0)),
        out_shape=jax.ShapeDtypeStruct((3, N_KV_HEADS, t, HEAD_GROUP * t), F32),
        compiler_params=_cparams(("arbitrary",)),
        name="bias_tiles",
    )(rel_bias)


def _dsa_body(q_ref, qi_ref, wit_ref, kia_ref, kib_ref, k_ref, vt_ref, bias_ref, o_ref,
              score_ref, mask_ref, *, t, topk):
    i = pl.program_id(1)
    nk = i + 1
    t_glob = i * t + lax.broadcasted_iota(I32, (1, t), 1)
    idx_scale = (IDX_DIM ** -0.5) * (IDX_HEADS ** -0.5)

    def score_chunk(c, carry):
        r0 = pl.multiple_of(c * SCORE_SUB, SCORE_SUB)
        ka = kia_ref[0, pl.ds(r0, SCORE_SUB), :]
        kb = kib_ref[0, pl.ds(r0, SCORE_SUB), :]
        acc = jnp.zeros((SCORE_SUB, t), F32)
        for j in range(IDX_HEADS // 2):
            qp = qi_ref[0, :, j * LANES:(j + 1) * LANES]
            da = _dot_nt(ka, qp)
            db = _dot_nt(kb, qp)
            acc = acc + jnp.maximum(da, 0.0) * wit_ref[0, 2 * j:2 * j + 1, :]
            acc = acc + jnp.maximum(db, 0.0) * wit_ref[0, 2 * j + 1:2 * j + 2, :]
        sc = acc * idx_scale
        s_glob = r0 + lax.broadcasted_iota(I32, (SCORE_SUB, 1), 0)
        score_ref[pl.ds(r0, SCORE_SUB), :] = jnp.where(s_glob <= t_glob, sc, NEG_INF)
        return carry

    lax.fori_loop(0, nk * (t // SCORE_SUB), score_chunk, 0)

    nblk = nk * (t // COUNT_RB)
    fold = lambda m: jnp.sum(m.reshape(COUNT_RB // SUBLANES, SUBLANES, t), axis=0)

    def blk(r):
        return score_ref[pl.ds(pl.multiple_of(r * COUNT_RB, COUNT_RB), COUNT_RB), :]

    def count_ge(c):
        def body(r, acc):
            return acc + fold((blk(r) >= c).astype(I32))
        acc = lax.fori_loop(0, nblk, body, jnp.zeros((SUBLANES, t), I32))
        return jnp.sum(acc, axis=0, keepdims=True)

    def stats_body(r, carry):
        cnt, mn, mx = carry
        b = blk(r)
        real = b > NEG_INF
        cnt = cnt + fold(real.astype(I32))
        mn = jnp.minimum(mn, jnp.min(jnp.where(real, b, BIG).reshape(COUNT_RB // SUBLANES, SUBLANES, t), axis=0))
        mx = jnp.maximum(mx, jnp.max(b.reshape(COUNT_RB // SUBLANES, SUBLANES, t), axis=0))
        return cnt, mn, mx

    cnt8, mn8, mx8 = lax.fori_loop(
        0, nblk, stats_body,
        (jnp.zeros((SUBLANES, t), I32), jnp.full((SUBLANES, t), BIG, F32), jnp.full((SUBLANES, t), -BIG, F32)))
    n_real = jnp.sum(cnt8, axis=0, keepdims=True)
    rmin = jnp.min(mn8, axis=0, keepdims=True)
    rmax = jnp.max(mx8, axis=0, keepdims=True)
    n_max = count_ge(rmax)
    degenerate = jnp.logical_and(n_real >= topk, n_max >= topk)
    lo0 = jnp.where(n_real < topk, NEG_INF, jnp.where(degenerate, rmax, rmin))
    hi0 = jnp.where(degenerate, BIG, rmax)
    done0 = jnp.where(n_real <= topk, 1, 0)

    def sel_cond(st):
        it, lo, hi, done, stuck = st
        return jnp.logical_and(it < SEL_MAX_ITERS, jnp.min(done + stuck) == 0)

    def sel_body(st):
        it, lo, hi, done, stuck = st
        c = lo + (hi - lo) * 0.5
        active = (done + stuck) == 0
        has_mid = jnp.logical_and(c > lo, c < hi)
        n = count_ge(c)
        upd = jnp.logical_and(active, has_mid)
        lo = jnp.where(jnp.logical_and(upd, n >= topk), c, lo)
        hi = jnp.where(jnp.logical_and(upd, n < topk), c, hi)
        done = jnp.where(jnp.logical_and(upd, n == topk), 1, done)
        stuck = jnp.where(jnp.logical_and(active, jnp.logical_not(has_mid)), 1, stuck)
        return it + 1, lo, hi, done, stuck

    _, lo, hi, done, _ = lax.while_loop(
        sel_cond, sel_body, (jnp.int32(0), lo0, hi0, done0, jnp.zeros((1, t), I32)))

    def write_mask(r, cut):
        r0 = pl.multiple_of(r * COUNT_RB, COUNT_RB)
        b = score_ref[pl.ds(r0, COUNT_RB), :]
        s_glob = r0 + lax.broadcasted_iota(I32, (COUNT_RB, 1), 0)
        sel = jnp.logical_and(b >= lo, s_glob <= t_glob)
        if cut is not None:
            sel = jnp.logical_and(sel, jnp.logical_or(b >= hi, s_glob < cut))
        mask_ref[pl.ds(r0, COUNT_RB), :] = jnp.where(sel, 0.0, NEG_INF)

    any_tied = jnp.min(done) == 0

    @pl.when(jnp.logical_not(any_tied))
    def _():
        def body(r, carry):
            write_mask(r, None)
            return carry
        lax.fori_loop(0, nblk, body, 0)

    @pl.when(any_tied)
    def _():
        need = topk - count_ge(hi)

        def count_tie_below(x):
            def body(r, acc):
                r0 = pl.multiple_of(r * COUNT_RB, COUNT_RB)
                b = score_ref[pl.ds(r0, COUNT_RB), :]
                s_glob = r0 + lax.broadcasted_iota(I32, (COUNT_RB, 1), 0)
                m = jnp.logical_and(jnp.logical_and(b >= lo, b < hi), s_glob < x)
                return acc + fold(m.astype(I32))
            acc = lax.fori_loop(0, nblk, body, jnp.zeros((SUBLANES, t), I32))
            return jnp.sum(acc, axis=0, keepdims=True)

        nbits = int(score_ref.shape[0]).bit_length()

        def bit_body(bi, x):
            c = x + jnp.left_shift(jnp.int32(1), nbits - 1 - bi)
            return jnp.where(count_tie_below(c) < need, c, x)

        x = lax.fori_loop(0, nbits, bit_body, jnp.zeros((1, t), I32))
        cut = jnp.where(done > 0, jnp.int32(2 ** 30), x + 1)

        def body(r, carry):
            write_mask(r, cut)
            return carry
        lax.fori_loop(0, nblk, body, 0)

    scale = HEAD_DIM ** -0.5
    gt = HEAD_GROUP * t
    for g in range(N_KV_HEADS):
        qg = jnp.concatenate(
            [q_ref[0, :, h * HEAD_DIM:(h + 1) * HEAD_DIM] for h in range(g * HEAD_GROUP, (g + 1) * HEAD_GROUP)],
            axis=0)

        def att_chunk(kc, carry, g=g, qg=qg):
            m, l, acc = carry
            r0 = pl.multiple_of(kc * t, t)
            kk = k_ref[0, pl.ds(r0, t), g * HEAD_DIM:(g + 1) * HEAD_DIM]
            mk = mask_ref[pl.ds(r0, t), :]
            s = _dot_nt(kk, qg) * scale
            s = s + bias_ref[jnp.minimum(i - kc, 2), g] + jnp.concatenate([mk] * HEAD_GROUP, axis=1)
            m_new = jnp.maximum(m, jnp.max(s, axis=0, keepdims=True))
            alpha = jnp.exp(m - m_new)
            p = jnp.exp(s - m_new)
            l = alpha * l + jnp.sum(p, axis=0, keepdims=True)
            vt = vt_ref[0, g * HEAD_DIM:(g + 1) * HEAD_DIM, pl.ds(r0, t)]
            acc = alpha * acc + _dot(vt, p.astype(_MXU_DTYPE))
            return m_new, l, acc

        m, l, acc = lax.fori_loop(
            0, nk, att_chunk,
            (jnp.full((1, gt), NEG_INF, F32), jnp.zeros((1, gt), F32), jnp.zeros((HEAD_DIM, gt), F32)))
        out = acc / l
        for hh in range(HEAD_GROUP):
            h = g * HEAD_GROUP + hh
            o_ref[0, :, h * HEAD_DIM:(h + 1) * HEAD_DIM] = out[:, hh * t:(hh + 1) * t].T.astype(o_ref.dtype)


def _dsa(q, qi, wit, kia, kib, k, vt, bias, topk):
    bsz, s, aw = q.shape
    t = ATT_T
    assert s % t == 0 and topk <= t
    row = lambda w: pl.BlockSpec((1, t, w), lambda b, i: (b, i, 0))
    full = lambda a: pl.BlockSpec((1,) + a.shape[1:], lambda b, i: (b, 0, 0))
    return pl.pallas_call(
        functools.partial(_dsa_body, t=t, topk=topk),
        grid=(bsz, s // t),
        in_specs=[row(aw), row(qi.shape[2]),
                  pl.BlockSpec((1, IDX_HEADS, t), lambda b, i: (b, 0, i)),
                  full(kia), full(kib), full(k), full(vt),
                  pl.BlockSpec(bias.shape, lambda b, i: (0, 0, 0, 0))],
        out_specs=row(aw),
        out_shape=jax.ShapeDtypeStruct((bsz, s, aw), _MXU_DTYPE),
        scratch_shapes=[pltpu.VMEM((s, t), F32), pltpu.VMEM((s, t), F32)],
        compiler_params=_cparams(("arbitrary", "arbitrary")),
        name="dsa",
    )(q, qi, wit, kia, kib, k, vt, bias)


def _pack_halves(hb):
    assert jnp.dtype(hb.dtype).itemsize == 2
    n = hb.shape[1] // 2
    hi = lax.bitcast_convert_type(hb[:, :n].astype(F32), I32)
    lo = lax.bitcast_convert_type(hb[:, n:].astype(F32), I32)
    return jnp.bitwise_or(hi, lax.shift_right_logical(lo, jnp.full(lo.shape, 16, I32)))


def _unpack_halves(w, dtype):
    hi = lax.bitcast_convert_type(jnp.bitwise_and(w, jnp.int32(-65536)), F32).astype(dtype)
    lo = lax.bitcast_convert_type(jnp.left_shift(w, 16), F32).astype(dtype)
    return jnp.concatenate([hi, lo], axis=1)


def _mix_body(u_ref, halo_ref, a_ref, x_ref, mod_ref, wpool_ref, ps_ref, wout_ref, g_ref, wr_ref, br_ref,
              x1_ref, h2_ref, lg_ref, *, seq, tm):
    t0 = (pl.program_id(0) * tm) % seq
    u = u_ref[...]
    halo = jnp.where(t0 == 0, 0.0, halo_ref[...])
    ext = jnp.concatenate([halo, u], axis=0)
    pos = t0 + lax.broadcasted_iota(I32, (tm, 1), 0)
    gw = u.shape[1] // N_POOL_GROUPS
    ys = []
    for g, w in enumerate(POOL_WINDOWS):
        a = ext[:, g * gw:(g + 1) * gw]
        sft = 1
        while sft < w:
            a = a + pltpu.roll(a, sft, 0)
            sft *= 2
        cnt = jnp.minimum(pos + 1, w).astype(F32)
        p = a[HALO:, :] / cnt - u[:, g * gw:(g + 1) * gw]
        ys.append(_dot(p.astype(_MXU_DTYPE), wpool_ref[g]) * ps_ref[:, g * gw:(g + 1) * gw])
    pool = jnp.concatenate(ys, axis=1).astype(_MXU_DTYPE)
    pw = pool.shape[1]
    mixed = _dot(pool, wout_ref[:pw, :]) + _dot(a_ref[...], wout_ref[pw:, :])
    x1 = x_ref[...] + mod_ref[0, 2:3, :] * mixed
    x1_ref[...] = x1
    h2 = _rms_mod(x1, g_ref[...], mod_ref[0, 3:4, :], mod_ref[0, 4:5, :]).astype(_MXU_DTYPE)
    h2_ref[...] = _pack_halves(h2)
    lg_ref[...] = _dot(h2, wr_ref[...]) + br_ref[...]


def _mix(u, attn, x, mod, w_pool, pool_scale, w_out, g_ffn, w_r, b_r, seq):
    n_tok, d = x.shape
    tm = MIX_TM
    assert seq % tm == 0 and tm % HALO == 0
    pw = u.shape[1]
    aw = attn.shape[1]
    rows = lambda w: pl.BlockSpec((tm, w), lambda i: (i, 0))
    const = lambda a: pl.BlockSpec(a.shape, lambda i: (0,) * a.ndim)
    return pl.pallas_call(
        functools.partial(_mix_body, seq=seq, tm=tm),
        grid=(n_tok // tm,),
        in_specs=[rows(pw),
                  pl.BlockSpec((HALO, pw), lambda i: (jnp.maximum(i * (tm // HALO) - 1, 0), 0)),
                  rows(aw), rows(d),
                  pl.BlockSpec((1, 6, d), lambda i: ((i * tm) // seq, 0, 0)),
                  const(w_pool), const(pool_scale), const(w_out), const(g_ffn), const(w_r), const(b_r)],
        out_specs=[rows(d), rows(d // 2), rows(LANES)],
        out_shape=[jax.ShapeDtypeStruct((n_tok, d), F32), jax.ShapeDtypeStruct((n_tok, d // 2), I32),
                   jax.ShapeDtypeStruct((n_tok, LANES), F32)],
        compiler_params=_cparams(("arbitrary",)),
        name="mix",
    )(u, u, attn, x, mod, w_pool, pool_scale, w_out, g_ffn, w_r, b_r)


def _route_body(lg_ref, rt_ref, gate_ref, cnt_ref, run_ref):
    @pl.when(pl.program_id(0) == 0)
    def _():
        run_ref[...] = jnp.zeros_like(run_ref)

    lg = lg_ref[...]
    tr = lg.shape[0]
    lane = lax.broadcasted_iota(I32, lg.shape, 1)
    gmask = lane < N_GROUPS
    gl = jnp.where(gmask, lg, -BIG)
    gmax = jnp.max(gl, axis=-1, keepdims=True)
    gsel = jnp.min(jnp.where(jnp.logical_and(gmask, gl == gmax), lane, LANES), axis=-1, keepdims=True)
    gsum = jnp.sum(jnp.where(gmask, jnp.exp(gl - gmax), 0.0), axis=-1, keepdims=True)
    p_g = 1.0 / gsum
    in_e = jnp.logical_and(lane >= N_GROUPS, lane < N_GROUPS + N_EXPERTS)
    emask = jnp.logical_and(in_e, lax.shift_right_arithmetic(lane - N_GROUPS, EXPERTS_PER_GROUP.bit_length() - 1) == gsel)
    el = jnp.where(emask, lg, -BIG)
    emax = jnp.max(el, axis=-1, keepdims=True)
    ex = jnp.where(emask, jnp.exp(el - emax), 0.0)
    ep = ex / jnp.sum(ex, axis=-1, keepdims=True)
    p1 = jnp.max(jnp.where(emask, ep, -1.0), axis=-1, keepdims=True)
    i1 = jnp.min(jnp.where(jnp.logical_and(emask, ep == p1), lane, LANES), axis=-1, keepdims=True)
    m2 = jnp.logical_and(emask, lane != i1)
    p2 = jnp.max(jnp.where(m2, ep, -1.0), axis=-1, keepdims=True)
    i2 = jnp.min(jnp.where(jnp.logical_and(m2, ep == p2), lane, LANES), axis=-1, keepdims=True)
    den = p1 + p2
    g1 = p_g * p1 / den
    g2 = p_g * p2 / den
    e1 = i1 - N_GROUPS
    e2 = i2 - N_GROUPS

    oh = jnp.logical_or(lane == e1, lane == e2)
    ri = lax.broadcasted_iota(I32, (tr, tr), 0)
    ci = lax.broadcasted_iota(I32, (tr, tr), 1)
    ltri = (ci < ri).astype(_MXU_DTYPE)
    before = _dot(ltri, oh.astype(_MXU_DTYPE)) + run_ref[...]
    r1 = jnp.sum(jnp.where(lane == e1, before, 0.0), axis=-1, keepdims=True)
    r2 = jnp.sum(jnp.where(lane == e2, before, 0.0), axis=-1, keepdims=True)
    run = run_ref[...] + jnp.sum(oh.astype(F32), axis=0, keepdims=True)
    run_ref[...] = run
    cnt_ref[...] = jnp.broadcast_to(run, cnt_ref.shape)

    route = jnp.where(lane == 0, e1.astype(F32),
                      jnp.where(lane == 1, e2.astype(F32),
                                jnp.where(lane == 2, r1, jnp.where(lane == 3, r2, 0.0))))
    rt_ref[...] = route.T[:SUBLANES, :].astype(I32)
    gate_ref[...] = jnp.where(lane == 0, g1, jnp.where(lane == 1, g2, 0.0))


def _route(logits):
    n_tok = logits.shape[0]
    tr = ROUTE_TR
    return pl.pallas_call(
        _route_body,
        grid=(n_tok // tr,),
        in_specs=[pl.BlockSpec((tr, LANES), lambda i: (i, 0))],
        out_specs=[pl.BlockSpec((SUBLANES, tr), lambda i: (0, i)),
                   pl.BlockSpec((tr, LANES), lambda i: (i, 0)),
                   pl.BlockSpec((SUBLANES, LANES), lambda i: (0, 0))],
        out_shape=[jax.ShapeDtypeStruct((SUBLANES, n_tok), I32),
                   jax.ShapeDtypeStruct((n_tok, LANES), F32),
                   jax.ShapeDtypeStruct((SUBLANES, LANES), F32)],
        scratch_shapes=[pltpu.VMEM((1, LANES), F32)],
        compiler_params=_cparams(("arbitrary",)),
        name="route",
    )(logits)


def _dest_body(ps_ref, rt_ref, o_ref):
    r = rt_ref[...]
    start = jnp.zeros_like(r)
    for e in range(N_EXPERTS):
        start = jnp.where(r == e, ps_ref[e], start)
    o_ref[...] = start + pltpu.roll(r, SUBLANES - 2, 0)


def _dest(route_t, pstart):
    n_tok = route_t.shape[1]
    tb = min(n_tok, 2048)
    return pl.pallas_call(
        _dest_body,
        grid_spec=pltpu.PrefetchScalarGridSpec(
            num_scalar_prefetch=1,
            grid=(n_tok // tb,),
            in_specs=[pl.BlockSpec((SUBLANES, tb), lambda i, ps: (0, i))],
            out_specs=pl.BlockSpec((SUBLANES, tb), lambda i, ps: (0, i)),
        ),
        out_shape=jax.ShapeDtypeStruct((SUBLANES, n_tok), I32),
        compiler_params=_cparams(("arbitrary",)),
        name="dest",
    )(pstart, route_t)


def _row_copy(src_ref, src_row, dst_ref, dst_row, sem):
    return pltpu.make_async_copy(src_ref.at[pl.ds(src_row, 1), :], dst_ref.at[pl.ds(dst_row, 1), :], sem)


def _dispatch_body(pstart_ref, cnt_ref, nu_ref, h_ref, d0_ref, d1_ref, xs_ref, zero_ref, sem, *, td, bm):
    def issue(c, carry):
        for u in range(SUBLANES):
            r = c * SUBLANES + u
            for j, d_ref in enumerate((d0_ref, d1_ref)):
                pltpu.make_async_copy(h_ref.at[c, pl.ds(u, 1), :], xs_ref.at[pl.ds(d_ref[r], 1), :],
                                      sem).start(priority=j)
        return carry

    lax.fori_loop(0, td // SUBLANES, issue, 0)

    def drain(c, carry):
        for _ in range(2 * SUBLANES):
            pltpu.make_async_copy(h_ref.at[0, pl.ds(0, 1), :], xs_ref.at[pl.ds(0, 1), :], sem).wait()
        return carry

    lax.fori_loop(0, td // SUBLANES, drain, 0)

    @pl.when(pl.program_id(0) == pl.num_programs(0) - 1)
    def _():
        zero_ref[...] = jnp.zeros_like(zero_ref)

        def per_expert(e, carry):
            c = cnt_ref[e]
            n_pad = (bm - c % bm) % bm
            base = pstart_ref[e] + c

            def start(r, cc):
                _row_copy(zero_ref, 0, xs_ref, base + r, sem).start()
                return cc

            def wait(r, cc):
                _row_copy(zero_ref, 0, xs_ref, 0, sem).wait()
                return cc

            lax.fori_loop(0, n_pad, start, 0)
            lax.fori_loop(0, n_pad, wait, 0)
            return carry

        lax.fori_loop(0, N_EXPERTS, per_expert, 0)

        first = nu_ref[0] * (bm // SUBLANES)
        n_tail = xs_ref.shape[0] // SUBLANES - first

        def tail_copy(r):
            return pltpu.make_async_copy(
                zero_ref, xs_ref.at[pl.ds(pl.multiple_of((first + r) * SUBLANES, SUBLANES), SUBLANES), :], sem)

        def tail_start(r, cc):
            tail_copy(r).start()
            return cc

        def tail_wait(r, cc):
            tail_copy(0).wait()
            return cc

        lax.fori_loop(0, n_tail, tail_start, 0)
        lax.fori_loop(0, n_tail, tail_wait, 0)


def _dispatch(h2, dest0, dest1, pstart, counts, n_used, n_slots):
    n_tok, d = h2.shape
    td = DISP_TD
    tok = pl.BlockSpec((td,), lambda i, ps, cn, nu: (i,), memory_space=pltpu.SMEM)
    return pl.pallas_call(
        functools.partial(_dispatch_body, td=td, bm=MOE_BM),
        grid_spec=pltpu.PrefetchScalarGridSpec(
            num_scalar_prefetch=3,
            grid=(n_tok // td,),
            in_specs=[pl.BlockSpec((td // SUBLANES, SUBLANES, d), lambda i, ps, cn, nu: (i, 0, 0)), tok, tok],
            out_specs=pl.BlockSpec(memory_space=pl.ANY),
            scratch_shapes=[pltpu.VMEM((SUBLANES, d), h2.dtype), pltpu.SemaphoreType.DMA],
        ),
        out_shape=jax.ShapeDtypeStruct((n_slots, d), h2.dtype),
        compiler_params=_cparams(("arbitrary",)),
        name="dispatch",
    )(pstart, counts, n_used, h2.reshape(n_tok // SUBLANES, SUBLANES, d), dest0, dest1)


def _experts_body(be_ref, nu_ref, x_ref, wg_ref, wu_ref, wd_ref, y_ref, wgb, wub, wdb):
    blk = pl.program_id(0)

    @pl.when(blk < nu_ref[0])
    def _():
        e = be_ref[blk]
        prev = be_ref[jnp.maximum(blk - 1, 0)]

        @pl.when(jnp.logical_or(blk == 0, e != prev))
        def _():
            wgb[...] = wg_ref[0].astype(wgb.dtype)
            wub[...] = wu_ref[0].astype(wub.dtype)
            wdb[...] = wd_ref[0].astype(wdb.dtype)

        xb = _unpack_halves(x_ref[...], _MXU_DTYPE)
        a = _silu(_dot(xb, wgb[...])) * _dot(xb, wub[...])
        y_ref[...] = _dot(a.astype(_MXU_DTYPE), wdb[...])

    @pl.when(blk >= nu_ref[0])
    def _():
        y_ref[...] = jnp.zeros_like(y_ref)


def _experts(xs, block_exp, n_used, w_gate, w_up, w_down):
    n_slots, dpk = xs.shape
    bm = MOE_BM
    d, de = w_gate.shape[1:]
    last = lambda i, nu: jnp.minimum(i, nu[0] - 1)
    return pl.pallas_call(
        _experts_body,
        grid_spec=pltpu.PrefetchScalarGridSpec(
            num_scalar_prefetch=2,
            grid=(n_slots // bm,),
            in_specs=[pl.BlockSpec((bm, dpk), lambda i, be, nu: (last(i, nu), 0)),
                      pl.BlockSpec((1, d, de), lambda i, be, nu: (be[last(i, nu)], 0, 0)),
                      pl.BlockSpec((1, d, de), lambda i, be, nu: (be[last(i, nu)], 0, 0)),
                      pl.BlockSpec((1, de, d), lambda i, be, nu: (be[last(i, nu)], 0, 0))],
            out_specs=pl.BlockSpec((bm, d), lambda i, be, nu: (i, 0)),
            scratch_shapes=[pltpu.VMEM((d, de), _MXU_DTYPE), pltpu.VMEM((d, de), _MXU_DTYPE),
                            pltpu.VMEM((de, d), _MXU_DTYPE)],
        ),
        out_shape=jax.ShapeDtypeStruct((n_slots, d), F32),
        compiler_params=_cparams(("arbitrary",)),
        name="experts",
    )(block_exp, n_used, xs, w_gate, w_up, w_down)


def _combine_body(x1_ref, gate_ref, d0_ref, d1_ref, mod_ref, g_ref, ys_ref, o_ref, buf_ref, sem, *, tf):
    def issue(c, carry):
        for u in range(SUBLANES):
            r = c * SUBLANES + u
            for j, d_ref in enumerate((d0_ref, d1_ref)):
                pltpu.make_async_copy(ys_ref.at[pl.ds(d_ref[r], 1), :], buf_ref.at[j, c, pl.ds(u, 1), :],
                                      sem).start(priority=j)
        return carry

    lax.fori_loop(0, tf // SUBLANES, issue, 0)

    def drain(c, carry):
        for _ in range(2 * SUBLANES):
            pltpu.make_async_copy(ys_ref.at[pl.ds(0, 1), :], buf_ref.at[0, 0, pl.ds(0, 1), :], sem).wait()
        return carry

    lax.fori_loop(0, tf // SUBLANES, drain, 0)

    gates = gate_ref[...]
    d = o_ref.shape[1]
    moe = buf_ref[0].reshape(tf, d) * gates[:, 0:1] + buf_ref[1].reshape(tf, d) * gates[:, 1:2]
    x2 = x1_ref[...] + mod_ref[0, 5:6, :] * moe
    ms = jnp.mean(x2 * x2, axis=-1, keepdims=True)
    o_ref[...] = x2 * lax.rsqrt(ms + EPS) * g_ref[...]


def _combine(x1, gates, dest0, dest1, mod, g_final, ys, seq):
    n_tok, d = x1.shape
    tf = COMB_TF
    tok = pl.BlockSpec((tf,), lambda i: (i,), memory_space=pltpu.SMEM)
    return pl.pallas_call(
        functools.partial(_combine_body, tf=tf),
        grid=(n_tok // tf,),
        in_specs=[pl.BlockSpec((tf, d), lambda i: (i, 0)),
                  pl.BlockSpec((tf, LANES), lambda i: (i, 0)),
                  tok, tok,
                  pl.BlockSpec((1, 6, d), lambda i: ((i * tf) // seq, 0, 0)),
                  pl.BlockSpec((1, d), lambda i: (0, 0)),
                  pl.BlockSpec(memory_space=pl.ANY)],
        out_specs=pl.BlockSpec((tf, d), lambda i: (i, 0)),
        scratch_shapes=[pltpu.VMEM((2, tf // SUBLANES, SUBLANES, d), F32), pltpu.SemaphoreType.DMA],
        out_shape=jax.ShapeDtypeStruct((n_tok, d), F32),
        compiler_params=_cparams(("arbitrary",)),
        name="combine",
    )(x1, gates, dest0, dest1, mod, g_final, ys)


def kernel(x, c, w_ada, b_ada, g_mix, w_in, w_pool, pool_scale, rel_bias, w_out, g_ffn, w_group, b_group,
           w_router, b_router, w_gate, w_up, w_down, g_final):
    bsz, seq, d = x.shape
    n_tok = bsz * seq
    depth = w_ada.shape[0]
    assert depth == 1, "the final norm is fused into the only layer's combine"
    topk = min(TOPK_MAX, seq // 4)
    bias = _bias_tiles(rel_bias, ATT_T)
    bm = MOE_BM
    n_blocks = -(-(n_tok * 2) // bm) + N_EXPERTS
    n_slots = n_blocks * bm
    xt = x.reshape(n_tok, d)
    for i in range(depth):
        mod = _ada(c, w_ada[i], b_ada[i][None, :]).reshape(bsz, 6, d)
        u, q, k, vt, qi, kia, kib, wit = _proj(xt.reshape(bsz, seq, d), mod, g_mix[i][None, :], w_in[i])
        attn = _dsa(q, qi, wit, kia, kib, k, vt, bias, topk)
        w_r = jnp.concatenate(
            [w_group[i], w_router[i], jnp.zeros((d, LANES - N_GROUPS - N_EXPERTS), F32)], axis=1).astype(_MXU_DTYPE)
        b_r = jnp.concatenate(
            [b_group[i], b_router[i], jnp.zeros((LANES - N_GROUPS - N_EXPERTS,), F32)])[None, :]
        x1, h2, logits = _mix(u.reshape(n_tok, -1), attn.reshape(n_tok, -1), xt, mod,
                              w_pool[i].astype(_MXU_DTYPE), pool_scale[i][None, :],
                              w_out[i].astype(_MXU_DTYPE), g_ffn[i][None, :], w_r, b_r, seq)
        route_t, gates, cnt = _route(logits)
        counts = cnt[0, :N_EXPERTS].astype(I32)
        padded = (counts + bm - 1) // bm * bm
        pends = jnp.cumsum(padded)
        pstart = pends - padded
        n_used = (pends[-1:] // bm).astype(I32)
        block_start = jnp.arange(n_blocks, dtype=I32) * bm
        block_exp = jnp.minimum(
            jnp.sum((pends[None, :] <= block_start[:, None]).astype(I32), axis=1), N_EXPERTS - 1)
        dest = _dest(route_t, pstart)
        dest0, dest1 = dest[0], dest[1]
        xs = _dispatch(h2, dest0, dest1, pstart, counts, n_used, n_slots)
        ys = _experts(xs, block_exp, n_used, w_gate[i], w_up[i], w_down[i])
        xt = _combine(x1, gates, dest0, dest1, mod, g_final[None, :], ys, seq)
    return xt.reshape(bsz, seq, d)
```

```python
import functools
import math

import numpy as np
import jax
import jax.numpy as jnp
from jax import lax
from jax.experimental import pallas as pl
from jax.experimental.pallas import tpu as pltpu

F32 = jnp.float32
I32 = jnp.int32
_MXU_DTYPE = jnp.bfloat16

POOL_WINDOWS = (2, 4, 8, 16)
N_POOL_GROUPS = 4
HEAD_DIM = 128
N_HEADS = 8
N_KV_HEADS = 2
HEAD_GROUP = N_HEADS // N_KV_HEADS
IDX_HEADS = 16
IDX_DIM = 64
TOPK_MAX = 256
N_BUCKETS = 32
MAX_DISTANCE = 128
N_GROUPS = 4
EXPERTS_PER_GROUP = 8
N_EXPERTS = N_GROUPS * EXPERTS_PER_GROUP
EPS = 1e-6
NEG_INF = -1e30
BIG = 3e38
LOG2E = 1.4426950408889634

LANES = 128
SUBLANES = 8
VMEM_LIMIT_BYTES = 56 * 1024 * 1024

ADA_TN = 1024
PROJ_TM = 256
ATT_T = 256
SCORE_SUB = 256
COUNT_RB = 128
MIX_TM = 256
HALO = 16
ROUTE_TR = 256
DISP_TD = 256
MOE_BM = 256
COMB_TF = 256
DMA_UNROLL = 8
SEL_FIXED_ITERS = 18
SEL_MAX_ITERS = 80


def _cparams(sem):
    return pltpu.CompilerParams(dimension_semantics=sem, vmem_limit_bytes=VMEM_LIMIT_BYTES)


def _silu(x):
    return x * (1.0 / (1.0 + jnp.exp(-x)))


def _dot(a, b):
    return jnp.dot(a, b, preferred_element_type=F32)


def _dot_nt(a, b):
    return lax.dot_general(a, b, (((1,), (1,)), ((), ())), preferred_element_type=F32)


def _ada_body(c_ref, w_ref, b_ref, o_ref):
    s = _silu(c_ref[...])
    o_ref[...] = _dot(s.astype(_MXU_DTYPE), w_ref[...].astype(_MXU_DTYPE)) + b_ref[...]


def _ada(c, w, b):
    bsz, d = c.shape
    n = w.shape[1]
    return pl.pallas_call(
        _ada_body,
        grid=(n // ADA_TN,),
        in_specs=[pl.BlockSpec((bsz, d), lambda j: (0, 0)),
                  pl.BlockSpec((d, ADA_TN), lambda j: (0, j)),
                  pl.BlockSpec((1, ADA_TN), lambda j: (0, j))],
        out_specs=pl.BlockSpec((bsz, ADA_TN), lambda j: (0, j)),
        out_shape=jax.ShapeDtypeStruct((bsz, n), F32),
        compiler_params=_cparams(("arbitrary",)),
        name="ada",
    )(c, w, b)


def _rms_mod(x, g, shift, scale):
    ms = jnp.mean(x * x, axis=-1, keepdims=True)
    y = x * lax.rsqrt(ms + EPS) * g
    return y * (1.0 + scale) + shift


def _proj_body(x_ref, mod_ref, g_ref, w_ref, u_ref, q_ref, k_ref, vt_ref, qi_ref, kia_ref, kib_ref,
               wit_ref, *, cuts):
    h = _rms_mod(x_ref[0], g_ref[...], mod_ref[0, 0:1, :], mod_ref[0, 1:2, :])
    hb = h.astype(_MXU_DTYPE)

    def seg(name):
        lo, hi = cuts[name]
        return _dot(hb, w_ref[:, lo:hi])

    u_ref[0] = seg("u")
    q_ref[0] = seg("q").astype(q_ref.dtype)
    k_ref[0] = seg("k").astype(k_ref.dtype)
    vt_ref[0] = seg("v").T.astype(vt_ref.dtype)
    qi_ref[0] = seg("qi").astype(qi_ref.dtype)
    kia_ref[0] = seg("kia").astype(kia_ref.dtype)
    kib_ref[0] = seg("kib").astype(kib_ref.dtype)
    wit_ref[0] = seg("wi").T[:IDX_HEADS, :]


def _proj(x, mod, g_mix, w_in):
    bsz, s, d = x.shape
    pool_w = d // 2
    attn_w = N_HEADS * HEAD_DIM
    kv_w = N_KV_HEADS * HEAD_DIM
    qi_w = IDX_HEADS * IDX_DIM
    c = np.cumsum([0, pool_w, attn_w, kv_w, kv_w, qi_w, IDX_DIM, IDX_HEADS])
    w_ki = w_in[:, c[5]:c[6]]
    w_wi = w_in[:, c[6]:c[7]]
    zk = jnp.zeros((d, LANES - IDX_DIM), w_in.dtype)
    wp = jnp.concatenate(
        [w_in[:, :c[5]], w_ki, zk, zk, w_ki, w_wi, jnp.zeros((d, LANES - IDX_HEADS), w_in.dtype)],
        axis=1).astype(_MXU_DTYPE)
    base = int(c[5])
    cuts = {"u": (int(c[0]), int(c[1])), "q": (int(c[1]), int(c[2])), "k": (int(c[2]), int(c[3])),
            "v": (int(c[3]), int(c[4])), "qi": (int(c[4]), int(c[5])),
            "kia": (base, base + LANES), "kib": (base + LANES, base + 2 * LANES),
            "wi": (base + 2 * LANES, base + 3 * LANES)}
    tm = PROJ_TM
    pw = wp.shape[1]
    row = lambda w: pl.BlockSpec((1, tm, w), lambda b, i: (b, i, 0))
    out_shapes = [
        jax.ShapeDtypeStruct((bsz, s, pool_w), F32),
        jax.ShapeDtypeStruct((bsz, s, attn_w), _MXU_DTYPE),
        jax.ShapeDtypeStruct((bsz, s, kv_w), _MXU_DTYPE),
        jax.ShapeDtypeStruct((bsz, kv_w, s), _MXU_DTYPE),
        jax.ShapeDtypeStruct((bsz, s, qi_w), _MXU_DTYPE),
        jax.ShapeDtypeStruct((bsz, s, LANES), _MXU_DTYPE),
        jax.ShapeDtypeStruct((bsz, s, LANES), _MXU_DTYPE),
        jax.ShapeDtypeStruct((bsz, IDX_HEADS, s), F32),
    ]
    out_specs = [row(pool_w), row(attn_w), row(kv_w),
                 pl.BlockSpec((1, kv_w, tm), lambda b, i: (b, 0, i)),
                 row(qi_w), row(LANES), row(LANES),
                 pl.BlockSpec((1, IDX_HEADS, tm), lambda b, i: (b, 0, i))]
    return pl.pallas_call(
        functools.partial(_proj_body, cuts=cuts),
        grid=(bsz, s // tm),
        in_specs=[pl.BlockSpec((1, tm, d), lambda b, i: (b, i, 0)),
                  pl.BlockSpec((1, 6, d), lambda b, i: (b, 0, 0)),
                  pl.BlockSpec((1, d), lambda b, i: (0, 0)),
                  pl.BlockSpec((d, pw), lambda b, i: (0, 0))],
        out_specs=out_specs,
        out_shape=out_shapes,
        compiler_params=_cparams(("arbitrary", "arbitrary")),
        name="proj",
    )(x, mod, g_mix, wp)


def _bucket_starts():
    max_exact = N_BUCKETS // 2
    d = np.arange(1, 4 * MAX_DISTANCE, dtype=np.float32)
    large = max_exact + (np.log(d / np.float32(max_exact)) / np.float32(math.log(MAX_DISTANCE / max_exact))
                         * np.float32(N_BUCKETS - max_exact)).astype(np.int32)
    large = np.minimum(large, N_BUCKETS - 1)
    bucket = np.where(d < max_exact, d.astype(np.int32), large)
    bucket = np.concatenate([[0], bucket])
    starts = [int(np.argmax(bucket >= b)) for b in range(N_BUCKETS)]
    assert all(np.all(bucket[starts[b]:] >= b) for b in range(N_BUCKETS))
    assert starts[N_BUCKETS - 1] <= MAX_DISTANCE
    return starts


def _bias_body(rb_ref, o_ref, *, t, starts):
    diff = pl.program_id(0)
    s_l = lax.broadcasted_iota(I32, (t, t), 0)
    t_l = lax.broadcasted_iota(I32, (t, t), 1)
    dist = diff * t + t_l - s_l
    for h in range(N_HEADS):
        val = jnp.full((t, t), rb_ref[0, h], F32)
        for b in range(1, N_BUCKETS):
            val = jnp.where(dist >= starts[b], rb_ref[b, h], val)
        hh = h % HEAD_GROUP
        o_ref[0, h // HEAD_GROUP, :, hh * t:(hh + 1) * t] = val * LOG2E


def _bias_tiles(rel_bias, t):
    assert t >= MAX_DISTANCE
    return pl.pallas_call(
        functools.partial(_bias_body, t=t, starts=_bucket_starts()),
        grid=(3,),
        in_specs=[pl.BlockSpec(memory_space=pltpu.SMEM)],
        out_specs=pl.BlockSpec((1, N_KV_HEADS, t, HEAD_GROUP * t), lambda i: (i, 0, 0, 0)),
        out_shape=jax.ShapeDtypeStruct((3, N_KV_HEADS, t, HEAD_GROUP * t), F32),
        compiler_params=_cparams(("arbitrary",)),
        name="bias_tiles",
    )(rel_bias)


def _dsa_body(q_ref, qi_ref, wit_ref, kia_ref, kib_ref, k_ref, vt_ref, bias_ref, o_ref,
              score_ref, mask_ref, acc_ref, *, t, topk):
    i = pl.program_id(1)
    nk = i + 1
    t_glob = i * t + lax.broadcasted_iota(I32, (1, t), 1)
    idx_scale = (IDX_DIM ** -0.5) * (IDX_HEADS ** -0.5)

    def score_chunk(c, carry):
        r0 = pl.multiple_of(c * SCORE_SUB, SCORE_SUB)
        ka = kia_ref[0, pl.ds(r0, SCORE_SUB), :]
        kb = kib_ref[0, pl.ds(r0, SCORE_SUB), :]
        acc = jnp.zeros((SCORE_SUB, t), F32)
        for j in range(IDX_HEADS // 2):
            qp = qi_ref[0, :, j * LANES:(j + 1) * LANES]
            da = _dot_nt(ka, qp)
            db = _dot_nt(kb, qp)
            acc = acc + jnp.maximum(da, 0.0) * wit_ref[0, 2 * j:2 * j + 1, :]
            acc = acc + jnp.maximum(db, 0.0) * wit_ref[0, 2 * j + 1:2 * j + 2, :]
        sc = acc * idx_scale
        s_glob = r0 + lax.broadcasted_iota(I32, (SCORE_SUB, 1), 0)
        score_ref[pl.ds(r0, SCORE_SUB), :] = jnp.where(s_glob <= t_glob, sc, NEG_INF)
        return carry

    lax.fori_loop(0, nk * (t // SCORE_SUB), score_chunk, 0)

    nblk = nk * (t // COUNT_RB)
    fold = lambda m: jnp.sum(m.reshape(COUNT_RB // SUBLANES, SUBLANES, t), axis=0)

    def blk(r):
        return score_ref[pl.ds(pl.multiple_of(r * COUNT_RB, COUNT_RB), COUNT_RB), :]

    def count_ge(c):
        def body(r, acc):
            return acc + fold((blk(r) >= c).astype(I32))
        acc = lax.fori_loop(0, nblk, body, jnp.zeros((SUBLANES, t), I32))
        return jnp.sum(acc, axis=0, keepdims=True)

    def stats_body(r, carry):
        cnt, mn, mx = carry
        b = blk(r)
        real = b > NEG_INF
        cnt = cnt + fold(real.astype(I32))
        mn = jnp.minimum(mn, jnp.min(jnp.where(real, b, BIG).reshape(COUNT_RB // SUBLANES, SUBLANES, t), axis=0))
        mx = jnp.maximum(mx, jnp.max(b.reshape(COUNT_RB // SUBLANES, SUBLANES, t), axis=0))
        return cnt, mn, mx

    cnt8, mn8, mx8 = lax.fori_loop(
        0, nblk, stats_body,
        (jnp.zeros((SUBLANES, t), I32), jnp.full((SUBLANES, t), BIG, F32), jnp.full((SUBLANES, t), -BIG, F32)))
    n_real = jnp.sum(cnt8, axis=0, keepdims=True)
    rmin = jnp.min(mn8, axis=0, keepdims=True)
    rmax = jnp.max(mx8, axis=0, keepdims=True)
    n_max = count_ge(rmax)
    degenerate = jnp.logical_and(n_real >= topk, n_max >= topk)
    lo0 = jnp.where(n_real < topk, NEG_INF, jnp.where(degenerate, rmax, rmin))
    hi0 = jnp.where(degenerate, BIG, rmax)
    done0 = jnp.where(n_real <= topk, 1, 0)

    def bisect(st):
        lo, hi, done, stuck = st
        c = lo + (hi - lo) * 0.5
        active = (done + stuck) == 0
        has_mid = jnp.logical_and(c > lo, c < hi)
        n = count_ge(c)
        upd = jnp.logical_and(active, has_mid)
        lo = jnp.where(jnp.logical_and(upd, n >= topk), c, lo)
        hi = jnp.where(jnp.logical_and(upd, n < topk), c, hi)
        done = jnp.where(jnp.logical_and(upd, n == topk), 1, done)
        stuck = jnp.where(jnp.logical_and(active, jnp.logical_not(has_mid)), 1, stuck)
        return lo, hi, done, stuck

    st = lax.fori_loop(0, SEL_FIXED_ITERS, lambda _, s: bisect(s), (lo0, hi0, done0, jnp.zeros((1, t), I32)))

    def sel_cond(s):
        return jnp.logical_and(s[0] < SEL_MAX_ITERS, jnp.min(s[3] + s[4]) == 0)

    _, lo, hi, done, _ = lax.while_loop(sel_cond, lambda s: (s[0] + 1,) + bisect(s[1:]), (jnp.int32(0),) + st)

    def write_mask(r, cut):
        r0 = pl.multiple_of(r * COUNT_RB, COUNT_RB)
        b = score_ref[pl.ds(r0, COUNT_RB), :]
        s_glob = r0 + lax.broadcasted_iota(I32, (COUNT_RB, 1), 0)
        sel = jnp.logical_and(b >= lo, s_glob <= t_glob)
        if cut is not None:
            sel = jnp.logical_and(sel, jnp.logical_or(b >= hi, s_glob < cut))
        mask_ref[pl.ds(r0, COUNT_RB), :] = jnp.concatenate([jnp.where(sel, 0.0, NEG_INF)] * HEAD_GROUP, axis=1)

    any_tied = jnp.min(done) == 0

    @pl.when(jnp.logical_not(any_tied))
    def _():
        def body(r, carry):
            write_mask(r, None)
            return carry
        lax.fori_loop(0, nblk, body, 0)

    @pl.when(any_tied)
    def _():
        need = topk - count_ge(hi)

        def count_tie_below(x):
            def body(r, acc):
                r0 = pl.multiple_of(r * COUNT_RB, COUNT_RB)
                b = score_ref[pl.ds(r0, COUNT_RB), :]
                s_glob = r0 + lax.broadcasted_iota(I32, (COUNT_RB, 1), 0)
                m = jnp.logical_and(jnp.logical_and(b >= lo, b < hi), s_glob < x)
                return acc + fold(m.astype(I32))
            acc = lax.fori_loop(0, nblk, body, jnp.zeros((SUBLANES, t), I32))
            return jnp.sum(acc, axis=0, keepdims=True)

        nbits = int(score_ref.shape[0]).bit_length()

        def bit_body(bi, x):
            c = x + jnp.left_shift(jnp.int32(1), nbits - 1 - bi)
            return jnp.where(count_tie_below(c) < need, c, x)

        x = lax.fori_loop(0, nbits, bit_body, jnp.zeros((1, t), I32))
        cut = jnp.where(done > 0, jnp.int32(2 ** 30), x + 1)

        def body(r, carry):
            write_mask(r, cut)
            return carry
        lax.fori_loop(0, nblk, body, 0)

    scale = HEAD_DIM ** -0.5
    gt = HEAD_GROUP * t

    def col_reduce(op, a):
        part = op(a.reshape(a.shape[0] // SUBLANES, SUBLANES, a.shape[1]), axis=0)
        return op(part, axis=0, keepdims=True)

    qgs = [jnp.concatenate(
        [q_ref[0, :, h * HEAD_DIM:(h + 1) * HEAD_DIM] for h in range(g * HEAD_GROUP, (g + 1) * HEAD_GROUP)],
        axis=0) for g in range(N_KV_HEADS)]
    acc_ref[...] = jnp.zeros_like(acc_ref)

    def att_chunk(kc, carry):
        r0 = pl.multiple_of(kc * t, t)
        mk = mask_ref[pl.ds(r0, t), :]
        new = []
        for g in range(N_KV_HEADS):
            m, l = carry[g]
            kk = k_ref[0, pl.ds(r0, t), g * HEAD_DIM:(g + 1) * HEAD_DIM]
            s = _dot_nt(kk, qgs[g]) * (scale * LOG2E)
            s = s + bias_ref[jnp.minimum(i - kc, 2), g] + mk
            m_new = jnp.maximum(m, col_reduce(jnp.max, s))
            alpha = jnp.exp2(m - m_new)
            p = jnp.exp2(s - m_new)
            l = alpha * l + col_reduce(jnp.sum, p)
            vt = vt_ref[0, g * HEAD_DIM:(g + 1) * HEAD_DIM, pl.ds(r0, t)]
            acc_ref[g] = alpha * acc_ref[g] + _dot(vt, p.astype(_MXU_DTYPE))
            new.append((m_new, l))
        return tuple(new)

    init = (jnp.full((1, gt), NEG_INF, F32), jnp.zeros((1, gt), F32))
    fin = lax.fori_loop(0, nk, att_chunk, (init,) * N_KV_HEADS)
    for g in range(N_KV_HEADS):
        out = acc_ref[g] / fin[g][1]
        for hh in range(HEAD_GROUP):
            h = g * HEAD_GROUP + hh
            o_ref[0, :, h * HEAD_DIM:(h + 1) * HEAD_DIM] = out[:, hh * t:(hh + 1) * t].T.astype(o_ref.dtype)


def _dsa(q, qi, wit, kia, kib, k, vt, bias, topk):
    bsz, s, aw = q.shape
    t = ATT_T
    assert s % t == 0 and topk <= t
    row = lambda w: pl.BlockSpec((1, t, w), lambda b, i: (b, i, 0))
    full = lambda a: pl.BlockSpec((1,) + a.shape[1:], lambda b, i: (b, 0, 0))
    return pl.pallas_call(
        functools.partial(_dsa_body, t=t, topk=topk),
        grid=(bsz, s // t),
        in_specs=[row(aw), row(qi.shape[2]),
                  pl.BlockSpec((1, IDX_HEADS, t), lambda b, i: (b, 0, i)),
                  full(kia), full(kib), full(k), full(vt),
                  pl.BlockSpec(bias.shape, lambda b, i: (0, 0, 0, 0))],
        out_specs=row(aw),
        out_shape=jax.ShapeDtypeStruct((bsz, s, aw), _MXU_DTYPE),
        scratch_shapes=[pltpu.VMEM((s, t), F32), pltpu.VMEM((s, HEAD_GROUP * t), F32),
                        pltpu.VMEM((N_KV_HEADS, HEAD_DIM, HEAD_GROUP * t), F32)],
        compiler_params=_cparams(("arbitrary", "arbitrary")),
        name="dsa",
    )(q, qi, wit, kia, kib, k, vt, bias)


def _pack_halves(hb):
    assert jnp.dtype(hb.dtype).itemsize == 2
    n = hb.shape[1] // 2
    hi = lax.bitcast_convert_type(hb[:, :n].astype(F32), I32)
    lo = lax.bitcast_convert_type(hb[:, n:].astype(F32), I32)
    return jnp.bitwise_or(hi, lax.shift_right_logical(lo, jnp.full(lo.shape, 16, I32)))


def _unpack_halves(w, dtype):
    hi = lax.bitcast_convert_type(jnp.bitwise_and(w, jnp.int32(-65536)), F32).astype(dtype)
    lo = lax.bitcast_convert_type(jnp.left_shift(w, 16), F32).astype(dtype)
    return jnp.concatenate([hi, lo], axis=1)


def _mix_body(u_ref, halo_ref, a_ref, x_ref, mod_ref, wpool_ref, ps_ref, wout_ref, g_ref, wr_ref, br_ref,
              x1_ref, h2_ref, lg_ref, *, seq, tm):
    t0 = (pl.program_id(0) * tm) % seq
    u = u_ref[...]
    halo = jnp.where(t0 == 0, 0.0, halo_ref[...])
    ext = jnp.concatenate([halo, u], axis=0)
    pos = t0 + lax.broadcasted_iota(I32, (tm, 1), 0)
    gw = u.shape[1] // N_POOL_GROUPS
    ys = []
    for g, w in enumerate(POOL_WINDOWS):
        a = ext[:, g * gw:(g + 1) * gw]
        sft = 1
        while sft < w:
            a = a + pltpu.roll(a, sft, 0)
            sft *= 2
        cnt = jnp.minimum(pos + 1, w).astype(F32)
        p = a[HALO:, :] / cnt - u[:, g * gw:(g + 1) * gw]
        ys.append(_dot(p.astype(_MXU_DTYPE), wpool_ref[g]) * ps_ref[:, g * gw:(g + 1) * gw])
    pool = jnp.concatenate(ys, axis=1).astype(_MXU_DTYPE)
    pw = pool.shape[1]
    mixed = _dot(pool, wout_ref[:pw, :]) + _dot(a_ref[...], wout_ref[pw:, :])
    x1 = x_ref[...] + mod_ref[0, 2:3, :] * mixed
    x1_ref[...] = x1
    h2 = _rms_mod(x1, g_ref[...], mod_ref[0, 3:4, :], mod_ref[0, 4:5, :]).astype(_MXU_DTYPE)
    h2_ref[...] = _pack_halves(h2)
    lg_ref[...] = _dot(h2, wr_ref[...]) + br_ref[...]


def _mix(u, attn, x, mod, w_pool, pool_scale, w_out, g_ffn, w_r, b_r, seq):
    n_tok, d = x.shape
    tm = MIX_TM
    assert seq % tm == 0 and tm % HALO == 0
    pw = u.shape[1]
    aw = attn.shape[1]
    rows = lambda w: pl.BlockSpec((tm, w), lambda i: (i, 0))
    const = lambda a: pl.BlockSpec(a.shape, lambda i: (0,) * a.ndim)
    return pl.pallas_call(
        functools.partial(_mix_body, seq=seq, tm=tm),
        grid=(n_tok // tm,),
        in_specs=[rows(pw),
                  pl.BlockSpec((HALO, pw), lambda i: (jnp.maximum(i * (tm // HALO) - 1, 0), 0)),
                  rows(aw), rows(d),
                  pl.BlockSpec((1, 6, d), lambda i: ((i * tm) // seq, 0, 0)),
                  const(w_pool), const(pool_scale), const(w_out), const(g_ffn), const(w_r), const(b_r)],
        out_specs=[rows(d), rows(d // 2), rows(LANES)],
        out_shape=[jax.ShapeDtypeStruct((n_tok, d), F32), jax.ShapeDtypeStruct((n_tok, d // 2), I32),
                   jax.ShapeDtypeStruct((n_tok, LANES), F32)],
        compiler_params=_cparams(("arbitrary",)),
        name="mix",
    )(u, u, attn, x, mod, w_pool, pool_scale, w_out, g_ffn, w_r, b_r)


def _route_body(lg_ref, rt_ref, gate_ref, cnt_ref, run_ref):
    @pl.when(pl.program_id(0) == 0)
    def _():
        run_ref[...] = jnp.zeros_like(run_ref)

    lg = lg_ref[...]
    tr = lg.shape[0]
    lane = lax.broadcasted_iota(I32, lg.shape, 1)
    gmask = lane < N_GROUPS
    gl = jnp.where(gmask, lg, -BIG)
    gmax = jnp.max(gl, axis=-1, keepdims=True)
    gsel = jnp.min(jnp.where(jnp.logical_and(gmask, gl == gmax), lane, LANES), axis=-1, keepdims=True)
    gsum = jnp.sum(jnp.where(gmask, jnp.exp(gl - gmax), 0.0), axis=-1, keepdims=True)
    p_g = 1.0 / gsum
    in_e = jnp.logical_and(lane >= N_GROUPS, lane < N_GROUPS + N_EXPERTS)
    emask = jnp.logical_and(in_e, lax.shift_right_arithmetic(lane - N_GROUPS, EXPERTS_PER_GROUP.bit_length() - 1) == gsel)
    el = jnp.where(emask, lg, -BIG)
    emax = jnp.max(el, axis=-1, keepdims=True)
    ex = jnp.where(emask, jnp.exp(el - emax), 0.0)
    ep = ex / jnp.sum(ex, axis=-1, keepdims=True)
    p1 = jnp.max(jnp.where(emask, ep, -1.0), axis=-1, keepdims=True)
    i1 = jnp.min(jnp.where(jnp.logical_and(emask, ep == p1), lane, LANES), axis=-1, keepdims=True)
    m2 = jnp.logical_and(emask, lane != i1)
    p2 = jnp.max(jnp.where(m2, ep, -1.0), axis=-1, keepdims=True)
    i2 = jnp.min(jnp.where(jnp.logical_and(m2, ep == p2), lane, LANES), axis=-1, keepdims=True)
    den = p1 + p2
    g1 = p_g * p1 / den
    g2 = p_g * p2 / den
    e1 = i1 - N_GROUPS
    e2 = i2 - N_GROUPS

    oh = jnp.logical_or(lane == e1, lane == e2)
    ri = lax.broadcasted_iota(I32, (tr, tr), 0)
    ci = lax.broadcasted_iota(I32, (tr, tr), 1)
    ltri = (ci < ri).astype(_MXU_DTYPE)
    before = _dot(ltri, oh.astype(_MXU_DTYPE)) + run_ref[...]
    r1 = jnp.sum(jnp.where(lane == e1, before, 0.0), axis=-1, keepdims=True)
    r2 = jnp.sum(jnp.where(lane == e2, before, 0.0), axis=-1, keepdims=True)
    run = run_ref[...] + jnp.sum(oh.astype(F32), axis=0, keepdims=True)
    run_ref[...] = run
    cnt_ref[...] = jnp.broadcast_to(run, cnt_ref.shape)

    route = jnp.where(lane == 0, e1.astype(F32),
                      jnp.where(lane == 1, e2.astype(F32),
                                jnp.where(lane == 2, r1, jnp.where(lane == 3, r2, 0.0))))
    rt_ref[...] = route.T[:SUBLANES, :].astype(I32)
    gate_ref[...] = jnp.where(lane == 0, g1, jnp.where(lane == 1, g2, 0.0))


def _route(logits):
    n_tok = logits.shape[0]
    tr = ROUTE_TR
    return pl.pallas_call(
        _route_body,
        grid=(n_tok // tr,),
        in_specs=[pl.BlockSpec((tr, LANES), lambda i: (i, 0))],
        out_specs=[pl.BlockSpec((SUBLANES, tr), lambda i: (0, i)),
                   pl.BlockSpec((tr, LANES), lambda i: (i, 0)),
                   pl.BlockSpec((SUBLANES, LANES), lambda i: (0, 0))],
        out_shape=[jax.ShapeDtypeStruct((SUBLANES, n_tok), I32),
                   jax.ShapeDtypeStruct((n_tok, LANES), F32),
                   jax.ShapeDtypeStruct((SUBLANES, LANES), F32)],
        scratch_shapes=[pltpu.VMEM((1, LANES), F32)],
        compiler_params=_cparams(("arbitrary",)),
        name="route",
    )(logits)


def _dest_body(ps_ref, rt_ref, o_ref):
    r = rt_ref[...]
    start = jnp.zeros_like(r)
    for e in range(N_EXPERTS):
        start = jnp.where(r == e, ps_ref[e], start)
    o_ref[...] = start + pltpu.roll(r, SUBLANES - 2, 0)


def _dest(route_t, pstart):
    n_tok = route_t.shape[1]
    tb = min(n_tok, 2048)
    return pl.pallas_call(
        _dest_body,
        grid_spec=pltpu.PrefetchScalarGridSpec(
            num_scalar_prefetch=1,
            grid=(n_tok // tb,),
            in_specs=[pl.BlockSpec((SUBLANES, tb), lambda i, ps: (0, i))],
            out_specs=pl.BlockSpec((SUBLANES, tb), lambda i, ps: (0, i)),
        ),
        out_shape=jax.ShapeDtypeStruct((SUBLANES, n_tok), I32),
        compiler_params=_cparams(("arbitrary",)),
        name="dest",
    )(pstart, route_t)


def _row_copy(src_ref, src_row, dst_ref, dst_row, sem):
    return pltpu.make_async_copy(src_ref.at[pl.ds(src_row, 1), :], dst_ref.at[pl.ds(dst_row, 1), :], sem)


def _dispatch_body(pstart_ref, cnt_ref, nu_ref, h_ref, d0_ref, d1_ref, xs_ref, zero_ref, sem, *, td, bm):
    def issue(c, carry):
        for u in range(SUBLANES):
            r = c * SUBLANES + u
            for j, d_ref in enumerate((d0_ref, d1_ref)):
                pltpu.make_async_copy(h_ref.at[c, pl.ds(u, 1), :], xs_ref.at[pl.ds(d_ref[r], 1), :],
                                      sem).start(priority=j)
        return carry

    lax.fori_loop(0, td // SUBLANES, issue, 0)

    def drain(c, carry):
        for _ in range(2 * SUBLANES):
            pltpu.make_async_copy(h_ref.at[0, pl.ds(0, 1), :], xs_ref.at[pl.ds(0, 1), :], sem).wait()
        return carry

    lax.fori_loop(0, td // SUBLANES, drain, 0)

    @pl.when(pl.program_id(0) == pl.num_programs(0) - 1)
    def _():
        zero_ref[...] = jnp.zeros_like(zero_ref)

        def per_expert(e, carry):
            c = cnt_ref[e]
            n_pad = (bm - c % bm) % bm
            base = pstart_ref[e] + c

            def start(r, cc):
                _row_copy(zero_ref, 0, xs_ref, base + r, sem).start()
                return cc

            def wait(r, cc):
                _row_copy(zero_ref, 0, xs_ref, 0, sem).wait()
                return cc

            lax.fori_loop(0, n_pad, start, 0)
            lax.fori_loop(0, n_pad, wait, 0)
            return carry

        lax.fori_loop(0, N_EXPERTS, per_expert, 0)

        first = nu_ref[0] * (bm // SUBLANES)
        n_tail = xs_ref.shape[0] // SUBLANES - first

        def tail_copy(r):
            return pltpu.make_async_copy(
                zero_ref, xs_ref.at[pl.ds(pl.multiple_of((first + r) * SUBLANES, SUBLANES), SUBLANES), :], sem)

        def tail_start(r, cc):
            tail_copy(r).start()
            return cc

        def tail_wait(r, cc):
            tail_copy(0).wait()
            return cc

        lax.fori_loop(0, n_tail, tail_start, 0)
        lax.fori_loop(0, n_tail, tail_wait, 0)


def _dispatch(h2, dest0, dest1, pstart, counts, n_used, n_slots):
    n_tok, d = h2.shape
    td = DISP_TD
    tok = pl.BlockSpec((td,), lambda i, ps, cn, nu: (i,), memory_space=pltpu.SMEM)
    return pl.pallas_call(
        functools.partial(_dispatch_body, td=td, bm=MOE_BM),
        grid_spec=pltpu.PrefetchScalarGridSpec(
            num_scalar_prefetch=3,
            grid=(n_tok // td,),
            in_specs=[pl.BlockSpec((td // SUBLANES, SUBLANES, d), lambda i, ps, cn, nu: (i, 0, 0)), tok, tok],
            out_specs=pl.BlockSpec(memory_space=pl.ANY),
            scratch_shapes=[pltpu.VMEM((SUBLANES, d), h2.dtype), pltpu.SemaphoreType.DMA],
        ),
        out_shape=jax.ShapeDtypeStruct((n_slots, d), h2.dtype),
        compiler_params=_cparams(("arbitrary",)),
        name="dispatch",
    )(pstart, counts, n_used, h2.reshape(n_tok // SUBLANES, SUBLANES, d), dest0, dest1)


def _experts_body(be_ref, nu_ref, x_ref, wg_ref, wu_ref, wd_ref, y_ref, wgb, wub, wdb):
    blk = pl.program_id(0)

    @pl.when(blk < nu_ref[0])
    def _():
        e = be_ref[blk]
        prev = be_ref[jnp.maximum(blk - 1, 0)]

        @pl.when(jnp.logical_or(blk == 0, e != prev))
        def _():
            wgb[...] = wg_ref[0].astype(wgb.dtype)
            wub[...] = wu_ref[0].astype(wub.dtype)
            wdb[...] = wd_ref[0].astype(wdb.dtype)

        xb = _unpack_halves(x_ref[...], _MXU_DTYPE)
        a = _silu(_dot(xb, wgb[...])) * _dot(xb, wub[...])
        y_ref[...] = _dot(a.astype(_MXU_DTYPE), wdb[...])

    @pl.when(blk >= nu_ref[0])
    def _():
        y_ref[...] = jnp.zeros_like(y_ref)


def _experts(xs, block_exp, n_used, w_gate, w_up, w_down):
    n_slots, dpk = xs.shape
    bm = MOE_BM
    d, de = w_gate.shape[1:]
    last = lambda i, nu: jnp.minimum(i, nu[0] - 1)
    return pl.pallas_call(
        _experts_body,
        grid_spec=pltpu.PrefetchScalarGridSpec(
            num_scalar_prefetch=2,
            grid=(n_slots // bm,),
            in_specs=[pl.BlockSpec((bm, dpk), lambda i, be, nu: (last(i, nu), 0)),
                      pl.BlockSpec((1, d, de), lambda i, be, nu: (be[last(i, nu)], 0, 0)),
                      pl.BlockSpec((1, d, de), lambda i, be, nu: (be[last(i, nu)], 0, 0)),
                      pl.BlockSpec((1, de, d), lambda i, be, nu: (be[last(i, nu)], 0, 0))],
            out_specs=pl.BlockSpec((bm, d), lambda i, be, nu: (i, 0)),
            scratch_shapes=[pltpu.VMEM((d, de), _MXU_DTYPE), pltpu.VMEM((d, de), _MXU_DTYPE),
                            pltpu.VMEM((de, d), _MXU_DTYPE)],
        ),
        out_shape=jax.ShapeDtypeStruct((n_slots, d), F32),
        compiler_params=_cparams(("arbitrary",)),
        name="experts",
    )(block_exp, n_used, xs, w_gate, w_up, w_down)


def _combine_body(x1_ref, gate_ref, d0_ref, d1_ref, mod_ref, g_ref, ys_ref, o_ref, buf_ref, sem, *, tf):
    def issue(c, carry):
        for u in range(SUBLANES):
            r = c * SUBLANES + u
            for j, d_ref in enumerate((d0_ref, d1_ref)):
                pltpu.make_async_copy(ys_ref.at[pl.ds(d_ref[r], 1), :], buf_ref.at[j, c, pl.ds(u, 1), :],
                                      sem).start(priority=j)
        return carry

    lax.fori_loop(0, tf // SUBLANES, issue, 0)

    def drain(c, carry):
        for _ in range(2 * SUBLANES):
            pltpu.make_async_copy(ys_ref.at[pl.ds(0, 1), :], buf_ref.at[0, 0, pl.ds(0, 1), :], sem).wait()
        return carry

    lax.fori_loop(0, tf // SUBLANES, drain, 0)

    gates = gate_ref[...]
    d = o_ref.shape[1]
    moe = buf_ref[0].reshape(tf, d) * gates[:, 0:1] + buf_ref[1].reshape(tf, d) * gates[:, 1:2]
    x2 = x1_ref[...] + mod_ref[0, 5:6, :] * moe
    ms = jnp.mean(x2 * x2, axis=-1, keepdims=True)
    o_ref[...] = x2 * lax.rsqrt(ms + EPS) * g_ref[...]


def _combine(x1, gates, dest0, dest1, mod, g_final, ys, seq):
    n_tok, d = x1.shape
    tf = COMB_TF
    tok = pl.BlockSpec((tf,), lambda i: (i,), memory_space=pltpu.SMEM)
    return pl.pallas_call(
        functools.partial(_combine_body, tf=tf),
        grid=(n_tok // tf,),
        in_specs=[pl.BlockSpec((tf, d), lambda i: (i, 0)),
                  pl.BlockSpec((tf, LANES), lambda i: (i, 0)),
                  tok, tok,
                  pl.BlockSpec((1, 6, d), lambda i: ((i * tf) // seq, 0, 0)),
                  pl.BlockSpec((1, d), lambda i: (0, 0)),
                  pl.BlockSpec(memory_space=pl.ANY)],
        out_specs=pl.BlockSpec((tf, d), lambda i: (i, 0)),
        scratch_shapes=[pltpu.VMEM((2, tf // SUBLANES, SUBLANES, d), F32), pltpu.SemaphoreType.DMA],
        out_shape=jax.ShapeDtypeStruct((n_tok, d), F32),
        compiler_params=_cparams(("arbitrary",)),
        name="combine",
    )(x1, gates, dest0, dest1, mod, g_final, ys)


def kernel(x, c, w_ada, b_ada, g_mix, w_in, w_pool, pool_scale, rel_bias, w_out, g_ffn, w_group, b_group,
           w_router, b_router, w_gate, w_up, w_down, g_final):
    bsz, seq, d = x.shape
    n_tok = bsz * seq
    depth = w_ada.shape[0]
    assert depth == 1, "the final norm is fused into the only layer's combine"
    topk = min(TOPK_MAX, seq // 4)
    bias = _bias_tiles(rel_bias, ATT_T)
    bm = MOE_BM
    n_blocks = -(-(n_tok * 2) // bm) + N_EXPERTS
    n_slots = n_blocks * bm
    xt = x.reshape(n_tok, d)
    for i in range(depth):
        mod = _ada(c, w_ada[i], b_ada[i][None, :]).reshape(bsz, 6, d)
        u, q, k, vt, qi, kia, kib, wit = _proj(xt.reshape(bsz, seq, d), mod, g_mix[i][None, :], w_in[i])
        attn = _dsa(q, qi, wit, kia, kib, k, vt, bias, topk)
        w_r = jnp.concatenate(
            [w_group[i], w_router[i], jnp.zeros((d, LANES - N_GROUPS - N_EXPERTS), F32)], axis=1).astype(_MXU_DTYPE)
        b_r = jnp.concatenate(
            [b_group[i], b_router[i], jnp.zeros((LANES - N_GROUPS - N_EXPERTS,), F32)])[None, :]
        x1, h2, logits = _mix(u.reshape(n_tok, -1), attn.reshape(n_tok, -1), xt, mod,
                              w_pool[i].astype(_MXU_DTYPE), pool_scale[i][None, :],
                              w_out[i].astype(_MXU_DTYPE), g_ffn[i][None, :], w_r, b_r, seq)
        route_t, gates, cnt = _route(logits)
        counts = cnt[0, :N_EXPERTS].astype(I32)
        padded = (counts + bm - 1) // bm * bm
        pends = jnp.cumsum(padded)
        pstart = pends - padded
        n_used = (pends[-1:] // bm).astype(I32)
        block_start = jnp.arange(n_blocks, dtype=I32) * bm
        block_exp = jnp.minimum(
            jnp.sum((pends[None, :] <= block_start[:, None]).astype(I32), axis=1), N_EXPERTS - 1)
        dest = _dest(route_t, pstart)
        dest0, dest1 = dest[0], dest[1]
        xs = _dispatch(h2, dest0, dest1, pstart, counts, n_used, n_slots)
        ys = _experts(xs, block_exp, n_used, w_gate[i], w_up[i], w_down[i])
        xt = _combine(x1, gates, dest0, dest1, mod, g_final[None, :], ys, seq)
    return xt.reshape(bsz, seq, d)
```

```python
import functools
import math

import numpy as np
import jax
import jax.numpy as jnp
from jax import lax
from jax.experimental import pallas as pl
from jax.experimental.pallas import tpu as pltpu

F32 = jnp.float32
I32 = jnp.int32
_MXU_DTYPE = jnp.bfloat16

POOL_WINDOWS = (2, 4, 8, 16)
N_POOL_GROUPS = 4
HEAD_DIM = 128
N_HEADS = 8
N_KV_HEADS = 2
HEAD_GROUP = N_HEADS // N_KV_HEADS
IDX_HEADS = 16
IDX_DIM = 64
TOPK_MAX = 256
N_BUCKETS = 32
MAX_DISTANCE = 128
N_GROUPS = 4
EXPERTS_PER_GROUP = 8
N_EXPERTS = N_GROUPS * EXPERTS_PER_GROUP
EPS = 1e-6
NEG_INF = -1e30
BIG = 3e38
LOG2E = 1.4426950408889634

LANES = 128
SUBLANES = 8
VMEM_LIMIT_BYTES = 56 * 1024 * 1024

ADA_TN = 1024
PROJ_TM = 256
ATT_T = 256
SCORE_SUB = 256
COUNT_RB = 128
MIX_TM = 256
HALO = 16
ROUTE_TR = 256
DISP_TD = 256
MOE_BM = 256
COMB_TF = 256
DMA_UNROLL = 8
SEL_FIXED_ITERS = 18
SEL_MAX_ITERS = 80


def _cparams(sem):
    return pltpu.CompilerParams(dimension_semantics=sem, vmem_limit_bytes=VMEM_LIMIT_BYTES)


def _silu(x):
    return x * (1.0 / (1.0 + jnp.exp(-x)))


def _dot(a, b):
    return jnp.dot(a, b, preferred_element_type=F32)


def _dot_nt(a, b):
    return lax.dot_general(a, b, (((1,), (1,)), ((), ())), preferred_element_type=F32)


def _ada_body(c_ref, w_ref, b_ref, o_ref):
    s = _silu(c_ref[...])
    o_ref[...] = _dot(s.astype(_MXU_DTYPE), w_ref[...].astype(_MXU_DTYPE)) + b_ref[...]


def _ada(c, w, b):
    bsz, d = c.shape
    n = w.shape[1]
    return pl.pallas_call(
        _ada_body,
        grid=(n // ADA_TN,),
        in_specs=[pl.BlockSpec((bsz, d), lambda j: (0, 0)),
                  pl.BlockSpec((d, ADA_TN), lambda j: (0, j)),
                  pl.BlockSpec((1, ADA_TN), lambda j: (0, j))],
        out_specs=pl.BlockSpec((bsz, ADA_TN), lambda j: (0, j)),
        out_shape=jax.ShapeDtypeStruct((bsz, n), F32),
        compiler_params=_cparams(("arbitrary",)),
        name="ada",
    )(c, w, b)


def _rms_mod(x, g, shift, scale):
    ms = jnp.mean(x * x, axis=-1, keepdims=True)
    y = x * lax.rsqrt(ms + EPS) * g
    return y * (1.0 + scale) + shift


def _proj_body(x_ref, mod_ref, g_ref, w_ref, u_ref, q_ref, k_ref, vt_ref, qi_ref, kia_ref, kib_ref,
               wit_ref, *, cuts):
    h = _rms_mod(x_ref[0], g_ref[...], mod_ref[0, 0:1, :], mod_ref[0, 1:2, :])
    hb = h.astype(_MXU_DTYPE)

    def seg(name):
        lo, hi = cuts[name]
        return _dot(hb, w_ref[:, lo:hi])

    u_ref[0] = seg("u")
    q_ref[0] = seg("q").astype(q_ref.dtype)
    k_ref[0] = seg("k").astype(k_ref.dtype)
    vt_ref[0] = seg("v").T.astype(vt_ref.dtype)
    qi_ref[0] = seg("qi").astype(qi_ref.dtype)
    kia_ref[0] = seg("kia").astype(kia_ref.dtype)
    kib_ref[0] = seg("kib").astype(kib_ref.dtype)
    wit_ref[0] = seg("wi").T[:IDX_HEADS, :]


def _proj(x, mod, g_mix, w_in):
    bsz, s, d = x.shape
    pool_w = d // 2
    attn_w = N_HEADS * HEAD_DIM
    kv_w = N_KV_HEADS * HEAD_DIM
    qi_w = IDX_HEADS * IDX_DIM
    c = np.cumsum([0, pool_w, attn_w, kv_w, kv_w, qi_w, IDX_DIM, IDX_HEADS])
    w_ki = w_in[:, c[5]:c[6]]
    w_wi = w_in[:, c[6]:c[7]]
    zk = jnp.zeros((d, LANES - IDX_DIM), w_in.dtype)
    wp = jnp.concatenate(
        [w_in[:, :c[5]], w_ki, zk, zk, w_ki, w_wi, jnp.zeros((d, LANES - IDX_HEADS), w_in.dtype)],
        axis=1).astype(_MXU_DTYPE)
    base = int(c[5])
    cuts = {"u": (int(c[0]), int(c[1])), "q": (int(c[1]), int(c[2])), "k": (int(c[2]), int(c[3])),
            "v": (int(c[3]), int(c[4])), "qi": (int(c[4]), int(c[5])),
            "kia": (base, base + LANES), "kib": (base + LANES, base + 2 * LANES),
            "wi": (base + 2 * LANES, base + 3 * LANES)}
    tm = PROJ_TM
    pw = wp.shape[1]
    row = lambda w: pl.BlockSpec((1, tm, w), lambda b, i: (b, i, 0))
    out_shapes = [
        jax.ShapeDtypeStruct((bsz, s, pool_w), F32),
        jax.ShapeDtypeStruct((bsz, s, attn_w), _MXU_DTYPE),
        jax.ShapeDtypeStruct((bsz, s, kv_w), _MXU_DTYPE),
        jax.ShapeDtypeStruct((bsz, kv_w, s), _MXU_DTYPE),
        jax.ShapeDtypeStruct((bsz, s, qi_w), _MXU_DTYPE),
        jax.ShapeDtypeStruct((bsz, s, LANES), _MXU_DTYPE),
        jax.ShapeDtypeStruct((bsz, s, LANES), _MXU_DTYPE),
        jax.ShapeDtypeStruct((bsz, IDX_HEADS, s), F32),
    ]
    out_specs = [row(pool_w), row(attn_w), row(kv_w),
                 pl.BlockSpec((1, kv_w, tm), lambda b, i: (b, 0, i)),
                 row(qi_w), row(LANES), row(LANES),
                 pl.BlockSpec((1, IDX_HEADS, tm), lambda b, i: (b, 0, i))]
    return pl.pallas_call(
        functools.partial(_proj_body, cuts=cuts),
        grid=(bsz, s // tm),
        in_specs=[pl.BlockSpec((1, tm, d), lambda b, i: (b, i, 0)),
                  pl.BlockSpec((1, 6, d), lambda b, i: (b, 0, 0)),
                  pl.BlockSpec((1, d), lambda b, i: (0, 0)),
                  pl.BlockSpec((d, pw), lambda b, i: (0, 0))],
        out_specs=out_specs,
        out_shape=out_shapes,
        compiler_params=_cparams(("arbitrary", "arbitrary")),
        name="proj",
    )(x, mod, g_mix, wp)


def _bucket_starts():
    max_exact = N_BUCKETS // 2
    d = np.arange(1, 4 * MAX_DISTANCE, dtype=np.float32)
    large = max_exact + (np.log(d / np.float32(max_exact)) / np.float32(math.log(MAX_DISTANCE / max_exact))
                         * np.float32(N_BUCKETS - max_exact)).astype(np.int32)
    large = np.minimum(large, N_BUCKETS - 1)
    bucket = np.where(d < max_exact, d.astype(np.int32), large)
    bucket = np.concatenate([[0], bucket])
    starts = [int(np.argmax(bucket >= b)) for b in range(N_BUCKETS)]
    assert all(np.all(bucket[starts[b]:] >= b) for b in range(N_BUCKETS))
    assert starts[N_BUCKETS - 1] <= MAX_DISTANCE
    return starts


def _bias_body(rb_ref, o_ref, *, t, starts):
    diff = pl.program_id(0)
    s_l = lax.broadcasted_iota(I32, (t, t), 0)
    t_l = lax.broadcasted_iota(I32, (t, t), 1)
    dist = diff * t + t_l - s_l
    for h in range(N_HEADS):
        val = jnp.full((t, t), rb_ref[0, h], F32)
        for b in range(1, N_BUCKETS):
            val = jnp.where(dist >= starts[b], rb_ref[b, h], val)
        hh = h % HEAD_GROUP
        o_ref[0, h // HEAD_GROUP, :, hh * t:(hh + 1) * t] = val * LOG2E


def _bias_tiles(rel_bias, t):
    assert t >= MAX_DISTANCE
    return pl.pallas_call(
        functools.partial(_bias_body, t=t, starts=_bucket_starts()),
        grid=(3,),
        in_specs=[pl.BlockSpec(memory_space=pltpu.SMEM)],
        out_specs=pl.BlockSpec((1, N_KV_HEADS, t, HEAD_GROUP * t), lambda i: (i, 0, 0, 0)),
        out_shape=jax.ShapeDtypeStruct((3, N_KV_HEADS, t, HEAD_GROUP * t), F32),
        compiler_params=_cparams(("arbitrary",)),
        name="bias_tiles",
    )(rel_bias)


def _dsa_body(q_ref, qi_ref, wit_ref, kia_ref, kib_ref, k_ref, vt_ref, bias_ref, o_ref,
              score_ref, mask_ref, acc_ref, *, t, topk):
    i = pl.program_id(1)
    nk = i + 1
    t_glob = i * t + lax.broadcasted_iota(I32, (1, t), 1)
    idx_scale = (IDX_DIM ** -0.5) * (IDX_HEADS ** -0.5)

    def score_chunk(c, carry):
        r0 = pl.multiple_of(c * SCORE_SUB, SCORE_SUB)
        ka = kia_ref[0, pl.ds(r0, SCORE_SUB), :]
        kb = kib_ref[0, pl.ds(r0, SCORE_SUB), :]
        acc = jnp.zeros((SCORE_SUB, t), F32)
        for j in range(IDX_HEADS // 2):
            qp = qi_ref[0, :, j * LANES:(j + 1) * LANES]
            da = _dot_nt(ka, qp)
            db = _dot_nt(kb, qp)
            acc = acc + jnp.maximum(da, 0.0) * wit_ref[0, 2 * j:2 * j + 1, :]
            acc = acc + jnp.maximum(db, 0.0) * wit_ref[0, 2 * j + 1:2 * j + 2, :]
        sc = acc * idx_scale
        s_glob = r0 + lax.broadcasted_iota(I32, (SCORE_SUB, 1), 0)
        score_ref[pl.ds(r0, SCORE_SUB), :] = jnp.where(s_glob <= t_glob, sc, NEG_INF)
        return carry

    lax.fori_loop(0, nk * (t // SCORE_SUB), score_chunk, 0)

    nblk = nk * (t // COUNT_RB)
    fold = lambda m: jnp.sum(m.reshape(COUNT_RB // SUBLANES, SUBLANES, t), axis=0)

    def blk(r):
        return score_ref[pl.ds(pl.multiple_of(r * COUNT_RB, COUNT_RB), COUNT_RB), :]

    def count_ge(c):
        def body(r, acc):
            return acc + fold((blk(r) >= c).astype(I32))
        acc = lax.fori_loop(0, nblk, body, jnp.zeros((SUBLANES, t), I32))
        return jnp.sum(acc, axis=0, keepdims=True)

    def stats_body(r, carry):
        cnt, mn, mx = carry
        b = blk(r)
        real = b > NEG_INF
        cnt = cnt + fold(real.astype(I32))
        mn = jnp.minimum(mn, jnp.min(jnp.where(real, b, BIG).reshape(COUNT_RB // SUBLANES, SUBLANES, t), axis=0))
        mx = jnp.maximum(mx, jnp.max(b.reshape(COUNT_RB // SUBLANES, SUBLANES, t), axis=0))
        return cnt, mn, mx

    cnt8, mn8, mx8 = lax.fori_loop(
        0, nblk, stats_body,
        (jnp.zeros((SUBLANES, t), I32), jnp.full((SUBLANES, t), BIG, F32), jnp.full((SUBLANES, t), -BIG, F32)))
    n_real = jnp.sum(cnt8, axis=0, keepdims=True)
    rmin = jnp.min(mn8, axis=0, keepdims=True)
    rmax = jnp.max(mx8, axis=0, keepdims=True)
    n_max = count_ge(rmax)
    degenerate = jnp.logical_and(n_real >= topk, n_max >= topk)
    lo0 = jnp.where(n_real < topk, NEG_INF, jnp.where(degenerate, rmax, rmin))
    hi0 = jnp.where(degenerate, BIG, rmax)
    done0 = jnp.where(n_real <= topk, 1, 0)

    def bisect(st):
        lo, hi, done, stuck = st
        c = lo + (hi - lo) * 0.5
        active = (done + stuck) == 0
        has_mid = jnp.logical_and(c > lo, c < hi)
        n = count_ge(c)
        upd = jnp.logical_and(active, has_mid)
        lo = jnp.where(jnp.logical_and(upd, n >= topk), c, lo)
        hi = jnp.where(jnp.logical_and(upd, n < topk), c, hi)
        done = jnp.where(jnp.logical_and(upd, n == topk), 1, done)
        stuck = jnp.where(jnp.logical_and(active, jnp.logical_not(has_mid)), 1, stuck)
        return lo, hi, done, stuck

    st = lax.fori_loop(0, SEL_FIXED_ITERS, lambda _, s: bisect(s), (lo0, hi0, done0, jnp.zeros((1, t), I32)))

    def sel_cond(s):
        return jnp.logical_and(s[0] < SEL_MAX_ITERS, jnp.min(s[3] + s[4]) == 0)

    _, lo, hi, done, _ = lax.while_loop(sel_cond, lambda s: (s[0] + 1,) + bisect(s[1:]), (jnp.int32(0),) + st)

    def write_mask(r, cut):
        r0 = pl.multiple_of(r * COUNT_RB, COUNT_RB)
        b = score_ref[pl.ds(r0, COUNT_RB), :]
        s_glob = r0 + lax.broadcasted_iota(I32, (COUNT_RB, 1), 0)
        sel = jnp.logical_and(b >= lo, s_glob <= t_glob)
        if cut is not None:
            sel = jnp.logical_and(sel, jnp.logical_or(b >= hi, s_glob < cut))
        mask_ref[pl.ds(r0, COUNT_RB), :] = jnp.concatenate([jnp.where(sel, 0.0, NEG_INF)] * HEAD_GROUP, axis=1)

    any_tied = jnp.min(done) == 0

    @pl.when(jnp.logical_not(any_tied))
    def _():
        def body(r, carry):
            write_mask(r, None)
            return carry
        lax.fori_loop(0, nblk, body, 0)

    @pl.when(any_tied)
    def _():
        need = topk - count_ge(hi)

        def count_tie_below(x):
            def body(r, acc):
                r0 = pl.multiple_of(r * COUNT_RB, COUNT_RB)
                b = score_ref[pl.ds(r0, COUNT_RB), :]
                s_glob = r0 + lax.broadcasted_iota(I32, (COUNT_RB, 1), 0)
                m = jnp.logical_and(jnp.logical_and(b >= lo, b < hi), s_glob < x)
                return acc + fold(m.astype(I32))
            acc = lax.fori_loop(0, nblk, body, jnp.zeros((SUBLANES, t), I32))
            return jnp.sum(acc, axis=0, keepdims=True)

        nbits = int(score_ref.shape[0]).bit_length()

        def bit_body(bi, x):
            c = x + jnp.left_shift(jnp.int32(1), nbits - 1 - bi)
            return jnp.where(count_tie_below(c) < need, c, x)

        x = lax.fori_loop(0, nbits, bit_body, jnp.zeros((1, t), I32))
        cut = jnp.where(done > 0, jnp.int32(2 ** 30), x + 1)

        def body(r, carry):
            write_mask(r, cut)
            return carry
        lax.fori_loop(0, nblk, body, 0)

    scale = HEAD_DIM ** -0.5
    gt = HEAD_GROUP * t

    def col_reduce(op, a):
        part = op(a.reshape(a.shape[0] // SUBLANES, SUBLANES, a.shape[1]), axis=0)
        return op(part, axis=0, keepdims=True)

    qgs = [jnp.concatenate(
        [q_ref[0, :, h * HEAD_DIM:(h + 1) * HEAD_DIM] for h in range(g * HEAD_GROUP, (g + 1) * HEAD_GROUP)],
        axis=0) for g in range(N_KV_HEADS)]
    acc_ref[...] = jnp.zeros_like(acc_ref)

    def att_chunk(kc, carry):
        r0 = pl.multiple_of(kc * t, t)
        mk = mask_ref[pl.ds(r0, t), :]
        new = []
        for g in range(N_KV_HEADS):
            m, l = carry[g]
            kk = k_ref[0, pl.ds(r0, t), g * HEAD_DIM:(g + 1) * HEAD_DIM]
            s = _dot_nt(kk, qgs[g]) * (scale * LOG2E)
            s = s + bias_ref[jnp.minimum(i - kc, 2), g] + mk
            m_new = jnp.maximum(m, col_reduce(jnp.max, s))
            alpha = jnp.exp2(m - m_new)
            p = jnp.exp2(s - m_new)
            l = alpha * l + col_reduce(jnp.sum, p)
            vt = vt_ref[0, g * HEAD_DIM:(g + 1) * HEAD_DIM, pl.ds(r0, t)]
            acc_ref[g] = alpha * acc_ref[g] + _dot(vt, p.astype(_MXU_DTYPE))
            new.append((m_new, l))
        return tuple(new)

    init = (jnp.full((1, gt), NEG_INF, F32), jnp.zeros((1, gt), F32))
    fin = lax.fori_loop(0, nk, att_chunk, (init,) * N_KV_HEADS)
    for g in range(N_KV_HEADS):
        out = acc_ref[g] / fin[g][1]
        for hh in range(HEAD_GROUP):
            h = g * HEAD_GROUP + hh
            o_ref[0, :, h * HEAD_DIM:(h + 1) * HEAD_DIM] = out[:, hh * t:(hh + 1) * t].T.astype(o_ref.dtype)


def _dsa(q, qi, wit, kia, kib, k, vt, bias, topk):
    bsz, s, aw = q.shape
    t = ATT_T
    assert s % t == 0 and topk <= t
    row = lambda w: pl.BlockSpec((1, t, w), lambda b, i: (b, i, 0))
    full = lambda a: pl.BlockSpec((1,) + a.shape[1:], lambda b, i: (b, 0, 0))
    return pl.pallas_call(
        functools.partial(_dsa_body, t=t, topk=topk),
        grid=(bsz, s // t),
        in_specs=[row(aw), row(qi.shape[2]),
                  pl.BlockSpec((1, IDX_HEADS, t), lambda b, i: (b, 0, i)),
                  full(kia), full(kib), full(k), full(vt),
                  pl.BlockSpec(bias.shape, lambda b, i: (0, 0, 0, 0))],
        out_specs=row(aw),
        out_shape=jax.ShapeDtypeStruct((bsz, s, aw), _MXU_DTYPE),
        scratch_shapes=[pltpu.VMEM((s, t), F32), pltpu.VMEM((s, HEAD_GROUP * t), F32),
                        pltpu.VMEM((N_KV_HEADS, HEAD_DIM, HEAD_GROUP * t), F32)],
        compiler_params=_cparams(("arbitrary", "arbitrary")),
        name="dsa",
    )(q, qi, wit, kia, kib, k, vt, bias)


def _pack_halves(hb):
    assert jnp.dtype(hb.dtype).itemsize == 2
    n = hb.shape[1] // 2
    hi = lax.bitcast_convert_type(hb[:, :n].astype(F32), I32)
    lo = lax.bitcast_convert_type(hb[:, n:].astype(F32), I32)
    return jnp.bitwise_or(hi, lax.shift_right_logical(lo, jnp.full(lo.shape, 16, I32)))


def _unpack_halves(w, dtype):
    hi = lax.bitcast_convert_type(jnp.bitwise_and(w, jnp.int32(-65536)), F32).astype(dtype)
    lo = lax.bitcast_convert_type(jnp.left_shift(w, 16), F32).astype(dtype)
    return jnp.concatenate([hi, lo], axis=1)


def _mix_body(u_ref, halo_ref, a_ref, x_ref, mod_ref, wpool_ref, ps_ref, wout_ref, g_ref, wr_ref, br_ref,
              x1_ref, h2_ref, lg_ref, *, seq, tm):
    t0 = (pl.program_id(0) * tm) % seq
    u = u_ref[...]
    halo = jnp.where(t0 == 0, 0.0, halo_ref[...])
    ext = jnp.concatenate([halo, u], axis=0)
    pos = t0 + lax.broadcasted_iota(I32, (tm, 1), 0)
    gw = u.shape[1] // N_POOL_GROUPS
    ys = []
    for g, w in enumerate(POOL_WINDOWS):
        a = ext[:, g * gw:(g + 1) * gw]
        sft = 1
        while sft < w:
            a = a + pltpu.roll(a, sft, 0)
            sft *= 2
        cnt = jnp.minimum(pos + 1, w).astype(F32)
        p = a[HALO:, :] / cnt - u[:, g * gw:(g + 1) * gw]
        ys.append(_dot(p.astype(_MXU_DTYPE), wpool_ref[g]) * ps_ref[:, g * gw:(g + 1) * gw])
    pool = jnp.concatenate(ys, axis=1).astype(_MXU_DTYPE)
    pw = pool.shape[1]
    mixed = _dot(pool, wout_ref[:pw, :]) + _dot(a_ref[...], wout_ref[pw:, :])
    x1 = x_ref[...] + mod_ref[0, 2:3, :] * mixed
    x1_ref[...] = x1
    h2 = _rms_mod(x1, g_ref[...], mod_ref[0, 3:4, :], mod_ref[0, 4:5, :]).astype(_MXU_DTYPE)
    h2_ref[...] = _pack_halves(h2)
    lg_ref[...] = _dot(h2, wr_ref[...]) + br_ref[...]


def _mix(u, attn, x, mod, w_pool, pool_scale, w_out, g_ffn, w_r, b_r, seq):
    n_tok, d = x.shape
    tm = MIX_TM
    assert seq % tm == 0 and tm % HALO == 0
    pw = u.shape[1]
    aw = attn.shape[1]
    rows = lambda w: pl.BlockSpec((tm, w), lambda i: (i, 0))
    const = lambda a: pl.BlockSpec(a.shape, lambda i: (0,) * a.ndim)
    return pl.pallas_call(
        functools.partial(_mix_body, seq=seq, tm=tm),
        grid=(n_tok // tm,),
        in_specs=[rows(pw),
                  pl.BlockSpec((HALO, pw), lambda i: (jnp.maximum(i * (tm // HALO) - 1, 0), 0)),
                  rows(aw), rows(d),
                  pl.BlockSpec((1, 6, d), lambda i: ((i * tm) // seq, 0, 0)),
                  const(w_pool), const(pool_scale), const(w_out), const(g_ffn), const(w_r), const(b_r)],
        out_specs=[rows(d), rows(d // 2), rows(LANES)],
        out_shape=[jax.ShapeDtypeStruct((n_tok, d), F32), jax.ShapeDtypeStruct((n_tok, d // 2), I32),
                   jax.ShapeDtypeStruct((n_tok, LANES), F32)],
        compiler_params=_cparams(("arbitrary",)),
        name="mix",
    )(u, u, attn, x, mod, w_pool, pool_scale, w_out, g_ffn, w_r, b_r)


def _route_body(lg_ref, rt_ref, gate_ref, cnt_ref, run_ref):
    @pl.when(pl.program_id(0) == 0)
    def _():
        run_ref[...] = jnp.zeros_like(run_ref)

    lg = lg_ref[...]
    tr = lg.shape[0]
    lane = lax.broadcasted_iota(I32, lg.shape, 1)
    gmask = lane < N_GROUPS
    gl = jnp.where(gmask, lg, -BIG)
    gmax = jnp.max(gl, axis=-1, keepdims=True)
    gsel = jnp.min(jnp.where(jnp.logical_and(gmask, gl == gmax), lane, LANES), axis=-1, keepdims=True)
    gsum = jnp.sum(jnp.where(gmask, jnp.exp(gl - gmax), 0.0), axis=-1, keepdims=True)
    p_g = 1.0 / gsum
    in_e = jnp.logical_and(lane >= N_GROUPS, lane < N_GROUPS + N_EXPERTS)
    emask = jnp.logical_and(in_e, lax.shift_right_arithmetic(lane - N_GROUPS, EXPERTS_PER_GROUP.bit_length() - 1) == gsel)
    el = jnp.where(emask, lg, -BIG)
    emax = jnp.max(el, axis=-1, keepdims=True)
    ex = jnp.where(emask, jnp.exp(el - emax), 0.0)
    ep = ex / jnp.sum(ex, axis=-1, keepdims=True)
    p1 = jnp.max(jnp.where(emask, ep, -1.0), axis=-1, keepdims=True)
    i1 = jnp.min(jnp.where(jnp.logical_and(emask, ep == p1), lane, LANES), axis=-1, keepdims=True)
    m2 = jnp.logical_and(emask, lane != i1)
    p2 = jnp.max(jnp.where(m2, ep, -1.0), axis=-1, keepdims=True)
    i2 = jnp.min(jnp.where(jnp.logical_and(m2, ep == p2), lane, LANES), axis=-1, keepdims=True)
    den = p1 + p2
    g1 = p_g * p1 / den
    g2 = p_g * p2 / den
    e1 = i1 - N_GROUPS
    e2 = i2 - N_GROUPS

    oh = jnp.logical_or(lane == e1, lane == e2)
    ri = lax.broadcasted_iota(I32, (tr, tr), 0)
    ci = lax.broadcasted_iota(I32, (tr, tr), 1)
    ltri = (ci < ri).astype(_MXU_DTYPE)
    before = _dot(ltri, oh.astype(_MXU_DTYPE)) + run_ref[...]
    r1 = jnp.sum(jnp.where(lane == e1, before, 0.0), axis=-1, keepdims=True)
    r2 = jnp.sum(jnp.where(lane == e2, before, 0.0), axis=-1, keepdims=True)
    run = run_ref[...] + jnp.sum(oh.astype(F32), axis=0, keepdims=True)
    run_ref[...] = run
    cnt_ref[...] = jnp.broadcast_to(run, cnt_ref.shape)

    route = jnp.where(lane == 0, e1.astype(F32),
                      jnp.where(lane == 1, e2.astype(F32),
                                jnp.where(lane == 2, r1, jnp.where(lane == 3, r2, 0.0))))
    rt_ref[...] = route.T[:SUBLANES, :].astype(I32)
    gate_ref[...] = jnp.where(lane == 0, g1, jnp.where(lane == 1, g2, 0.0))


def _route(logits):
    n_tok = logits.shape[0]
    tr = ROUTE_TR
    return pl.pallas_call(
        _route_body,
        grid=(n_tok // tr,),
        in_specs=[pl.BlockSpec((tr, LANES), lambda i: (i, 0))],
        out_specs=[pl.BlockSpec((SUBLANES, tr), lambda i: (0, i)),
                   pl.BlockSpec((tr, LANES), lambda i: (i, 0)),
                   pl.BlockSpec((SUBLANES, LANES), lambda i: (0, 0))],
        out_shape=[jax.ShapeDtypeStruct((SUBLANES, n_tok), I32),
                   jax.ShapeDtypeStruct((n_tok, LANES), F32),
                   jax.ShapeDtypeStruct((SUBLANES, LANES), F32)],
        scratch_shapes=[pltpu.VMEM((1, LANES), F32)],
        compiler_params=_cparams(("arbitrary",)),
        name="route",
    )(logits)


def _dest_body(ps_ref, rt_ref, o_ref):
    r = rt_ref[...]
    start = jnp.zeros_like(r)
    for e in range(N_EXPERTS):
        start = jnp.where(r == e, ps_ref[e], start)
    o_ref[...] = start + pltpu.roll(r, SUBLANES - 2, 0)


def _dest(route_t, pstart):
    n_tok = route_t.shape[1]
    tb = min(n_tok, 2048)
    return pl.pallas_call(
        _dest_body,
        grid_spec=pltpu.PrefetchScalarGridSpec(
            num_scalar_prefetch=1,
            grid=(n_tok // tb,),
            in_specs=[pl.BlockSpec((SUBLANES, tb), lambda i, ps: (0, i))],
            out_specs=pl.BlockSpec((SUBLANES, tb), lambda i, ps: (0, i)),
        ),
        out_shape=jax.ShapeDtypeStruct((SUBLANES, n_tok), I32),
        compiler_params=_cparams(("arbitrary",)),
        name="dest",
    )(pstart, route_t)


def _row_copy(src_ref, src_row, dst_ref, dst_row, sem):
    return pltpu.make_async_copy(src_ref.at[pl.ds(src_row, 1), :], dst_ref.at[pl.ds(dst_row, 1), :], sem)


def _dispatch_body(pstart_ref, cnt_ref, nu_ref, h_ref, d0_ref, d1_ref, xs_ref, zero_ref, sem, *, td, bm):
    def issue(c, carry):
        for u in range(SUBLANES):
            r = c * SUBLANES + u
            for j, d_ref in enumerate((d0_ref, d1_ref)):
                pltpu.make_async_copy(h_ref.at[c, pl.ds(u, 1), :], xs_ref.at[pl.ds(d_ref[r], 1), :],
                                      sem).start(priority=j)
        return carry

    lax.fori_loop(0, td // SUBLANES, issue, 0)

    def drain(c, carry):
        for _ in range(2 * SUBLANES):
            pltpu.make_async_copy(h_ref.at[0, pl.ds(0, 1), :], xs_ref.at[pl.ds(0, 1), :], sem).wait()
        return carry

    lax.fori_loop(0, td // SUBLANES, drain, 0)

    @pl.when(pl.program_id(0) == pl.num_programs(0) - 1)
    def _():
        zero_ref[...] = jnp.zeros_like(zero_ref)

        def per_expert(e, carry):
            c = cnt_ref[e]
            n_pad = (bm - c % bm) % bm
            base = pstart_ref[e] + c

            def start(r, cc):
                _row_copy(zero_ref, 0, xs_ref, base + r, sem).start()
                return cc

            def wait(r, cc):
                _row_copy(zero_ref, 0, xs_ref, 0, sem).wait()
                return cc

            lax.fori_loop(0, n_pad, start, 0)
            lax.fori_loop(0, n_pad, wait, 0)
            return carry

        lax.fori_loop(0, N_EXPERTS, per_expert, 0)

        first = nu_ref[0] * (bm // SUBLANES)
        n_tail = xs_ref.shape[0] // SUBLANES - first

        def tail_copy(r):
            return pltpu.make_async_copy(
                zero_ref, xs_ref.at[pl.ds(pl.multiple_of((first + r) * SUBLANES, SUBLANES), SUBLANES), :], sem)

        def tail_start(r, cc):
            tail_copy(r).start()
            return cc

        def tail_wait(r, cc):
            tail_copy(0).wait()
            return cc

        lax.fori_loop(0, n_tail, tail_start, 0)
        lax.fori_loop(0, n_tail, tail_wait, 0)


def _dispatch(h2, dest0, dest1, pstart, counts, n_used, n_slots):
    n_tok, d = h2.shape
    td = DISP_TD
    tok = pl.BlockSpec((td,), lambda i, ps, cn, nu: (i,), memory_space=pltpu.SMEM)
    return pl.pallas_call(
        functools.partial(_dispatch_body, td=td, bm=MOE_BM),
        grid_spec=pltpu.PrefetchScalarGridSpec(
            num_scalar_prefetch=3,
            grid=(n_tok // td,),
            in_specs=[pl.BlockSpec((td // SUBLANES, SUBLANES, d), lambda i, ps, cn, nu: (i, 0, 0)), tok, tok],
            out_specs=pl.BlockSpec(memory_space=pl.ANY),
            scratch_shapes=[pltpu.VMEM((SUBLANES, d), h2.dtype), pltpu.SemaphoreType.DMA],
        ),
        out_shape=jax.ShapeDtypeStruct((n_slots, d), h2.dtype),
        compiler_params=_cparams(("arbitrary",)),
        name="dispatch",
    )(pstart, counts, n_used, h2.reshape(n_tok // SUBLANES, SUBLANES, d), dest0, dest1)


def _experts_body(widx_ref, slot_ref, cast_ref, nu_ref, x_ref, wg_ref, wu_ref, wd_ref, y_ref, wgb, wub, wdb):
    j = pl.program_id(0)
    blk = j - 1

    def cast_into(slot):
        wgb[slot] = wg_ref[0].astype(wgb.dtype)
        wub[slot] = wu_ref[0].astype(wub.dtype)
        wdb[slot] = wd_ref[0].astype(wdb.dtype)

    @pl.when(j == 0)
    def _():
        cast_into(0)

    @pl.when(jnp.logical_and(j > 0, blk < nu_ref[0]))
    def _():
        slot = slot_ref[blk]
        xb = _unpack_halves(x_ref[...], _MXU_DTYPE)
        a = _silu(_dot(xb, wgb[slot])) * _dot(xb, wub[slot])
        y_ref[...] = _dot(a.astype(_MXU_DTYPE), wdb[slot])

        @pl.when(cast_ref[blk] == 1)
        def _():
            cast_into(1 - slot)

    @pl.when(blk >= nu_ref[0])
    def _():
        y_ref[...] = jnp.zeros_like(y_ref)


def _experts(xs, block_exp, n_used, pends, w_gate, w_up, w_down):
    n_slots, dpk = xs.shape
    bm = MOE_BM
    n_blocks = n_slots // bm
    d, de = w_gate.shape[1:]
    blk = jnp.minimum(jnp.arange(n_blocks, dtype=I32), n_used[0] - 1)
    be = block_exp[blk]
    first_next = jnp.minimum(pends[be] // bm, n_used[0] - 1)
    nxt = block_exp[first_next]
    is_last = jnp.concatenate([be[1:] != be[:-1], jnp.ones((1,), jnp.bool_)])
    is_last = jnp.logical_or(is_last, jnp.arange(n_blocks) >= n_used[0] - 1)
    cast = jnp.logical_and(is_last, nxt != be).astype(I32)
    is_first = jnp.concatenate([jnp.ones((1,), jnp.bool_), be[1:] != be[:-1]])
    slot = ((jnp.cumsum(is_first.astype(I32)) - 1) % 2).astype(I32)
    widx = jnp.concatenate([block_exp[:1], nxt]).astype(I32)
    xblk = lambda j, wi, sl, ca, nu: jnp.minimum(jnp.maximum(j - 1, 0), nu[0] - 1)
    wspec = lambda shape: pl.BlockSpec((1,) + shape, lambda j, wi, sl, ca, nu: (wi[j], 0, 0))
    return pl.pallas_call(
        _experts_body,
        grid_spec=pltpu.PrefetchScalarGridSpec(
            num_scalar_prefetch=4,
            grid=(n_blocks + 1,),
            in_specs=[pl.BlockSpec((bm, dpk), lambda j, wi, sl, ca, nu: (xblk(j, wi, sl, ca, nu), 0)),
                      wspec((d, de)), wspec((d, de)), wspec((de, d))],
            out_specs=pl.BlockSpec((bm, d), lambda j, wi, sl, ca, nu: (jnp.maximum(j - 1, 0), 0)),
            scratch_shapes=[pltpu.VMEM((2, d, de), _MXU_DTYPE), pltpu.VMEM((2, d, de), _MXU_DTYPE),
                            pltpu.VMEM((2, de, d), _MXU_DTYPE)],
        ),
        out_shape=jax.ShapeDtypeStruct((n_slots, d), F32),
        compiler_params=_cparams(("arbitrary",)),
        name="experts",
    )(widx, slot, cast, n_used, xs, w_gate, w_up, w_down)


def _combine_body(x1_ref, gate_ref, d0_ref, d1_ref, n0_ref, n1_ref, mod_ref, g_ref, ys_ref, o_ref, buf_ref, sem,
                  *, tf):
    i = pl.program_id(0)
    cur = i % 2

    def gather(da_ref, db_ref, half):
        def issue(c, carry):
            for u in range(SUBLANES):
                r = c * SUBLANES + u
                for j, d_ref in enumerate((da_ref, db_ref)):
                    pltpu.make_async_copy(ys_ref.at[pl.ds(d_ref[r], 1), :],
                                          buf_ref.at[half, j, c, pl.ds(u, 1), :], sem.at[half]).start(priority=j)
            return carry

        lax.fori_loop(0, tf // SUBLANES, issue, 0)

    @pl.when(i == 0)
    def _():
        gather(d0_ref, d1_ref, 0)

    @pl.when(i + 1 < pl.num_programs(0))
    def _():
        gather(n0_ref, n1_ref, 1 - cur)

    def drain(c, carry):
        for _ in range(2 * SUBLANES):
            pltpu.make_async_copy(ys_ref.at[pl.ds(0, 1), :], buf_ref.at[cur, 0, 0, pl.ds(0, 1), :],
                                  sem.at[cur]).wait()
        return carry

    lax.fori_loop(0, tf // SUBLANES, drain, 0)

    gates = gate_ref[...]
    d = o_ref.shape[1]
    moe = buf_ref[cur, 0].reshape(tf, d) * gates[:, 0:1] + buf_ref[cur, 1].reshape(tf, d) * gates[:, 1:2]
    x2 = x1_ref[...] + mod_ref[0, 5:6, :] * moe
    ms = jnp.mean(x2 * x2, axis=-1, keepdims=True)
    o_ref[...] = x2 * lax.rsqrt(ms + EPS) * g_ref[...]


def _combine(x1, gates, dest0, dest1, mod, g_final, ys, seq):
    n_tok, d = x1.shape
    tf = COMB_TF
    n_tiles = n_tok // tf
    tok = pl.BlockSpec((tf,), lambda i: (i,), memory_space=pltpu.SMEM)
    nxt = pl.BlockSpec((tf,), lambda i: (jnp.minimum(i + 1, n_tiles - 1),), memory_space=pltpu.SMEM)
    return pl.pallas_call(
        functools.partial(_combine_body, tf=tf),
        grid=(n_tiles,),
        in_specs=[pl.BlockSpec((tf, d), lambda i: (i, 0)),
                  pl.BlockSpec((tf, LANES), lambda i: (i, 0)),
                  tok, tok, nxt, nxt,
                  pl.BlockSpec((1, 6, d), lambda i: ((i * tf) // seq, 0, 0)),
                  pl.BlockSpec((1, d), lambda i: (0, 0)),
                  pl.BlockSpec(memory_space=pl.ANY)],
        out_specs=pl.BlockSpec((tf, d), lambda i: (i, 0)),
        scratch_shapes=[pltpu.VMEM((2, 2, tf // SUBLANES, SUBLANES, d), F32), pltpu.SemaphoreType.DMA((2,))],
        out_shape=jax.ShapeDtypeStruct((n_tok, d), F32),
        compiler_params=_cparams(("arbitrary",)),
        name="combine",
    )(x1, gates, dest0, dest1, dest0, dest1, mod, g_final, ys)


def kernel(x, c, w_ada, b_ada, g_mix, w_in, w_pool, pool_scale, rel_bias, w_out, g_ffn, w_group, b_group,
           w_router, b_router, w_gate, w_up, w_down, g_final):
    bsz, seq, d = x.shape
    n_tok = bsz * seq
    depth = w_ada.shape[0]
    assert depth == 1, "the final norm is fused into the only layer's combine"
    topk = min(TOPK_MAX, seq // 4)
    bias = _bias_tiles(rel_bias, ATT_T)
    bm = MOE_BM
    n_blocks = -(-(n_tok * 2) // bm) + N_EXPERTS
    n_slots = n_blocks * bm
    xt = x.reshape(n_tok, d)
    for i in range(depth):
        mod = _ada(c, w_ada[i], b_ada[i][None, :]).reshape(bsz, 6, d)
        u, q, k, vt, qi, kia, kib, wit = _proj(xt.reshape(bsz, seq, d), mod, g_mix[i][None, :], w_in[i])
        attn = _dsa(q, qi, wit, kia, kib, k, vt, bias, topk)
        w_r = jnp.concatenate(
            [w_group[i], w_router[i], jnp.zeros((d, LANES - N_GROUPS - N_EXPERTS), F32)], axis=1).astype(_MXU_DTYPE)
        b_r = jnp.concatenate(
            [b_group[i], b_router[i], jnp.zeros((LANES - N_GROUPS - N_EXPERTS,), F32)])[None, :]
        x1, h2, logits = _mix(u.reshape(n_tok, -1), attn.reshape(n_tok, -1), xt, mod,
                              w_pool[i].astype(_MXU_DTYPE), pool_scale[i][None, :],
                              w_out[i].astype(_MXU_DTYPE), g_ffn[i][None, :], w_r, b_r, seq)
        route_t, gates, cnt = _route(logits)
        counts = cnt[0, :N_EXPERTS].astype(I32)
        padded = (counts + bm - 1) // bm * bm
        pends = jnp.cumsum(padded)
        pstart = pends - padded
        n_used = (pends[-1:] // bm).astype(I32)
        block_start = jnp.arange(n_blocks, dtype=I32) * bm
        block_exp = jnp.minimum(
            jnp.sum((pends[None, :] <= block_start[:, None]).astype(I32), axis=1), N_EXPERTS - 1)
        dest = _dest(route_t, pstart)
        dest0, dest1 = dest[0], dest[1]
        xs = _dispatch(h2, dest0, dest1, pstart, counts, n_used, n_slots)
        ys = _experts(xs, block_exp, n_used, pends, w_gate[i], w_up[i], w_down[i])
        xt = _combine(x1, gates, dest0, dest1, mod, g_final[None, :], ys, seq)
    return xt.reshape(bsz, seq, d)
```

```python
import functools
import math

import numpy as np
import jax
import jax.numpy as jnp
from jax import lax
from jax.experimental import pallas as pl
from jax.experimental.pallas import tpu as pltpu

F32 = jnp.float32
I32 = jnp.int32
_MXU_DTYPE = jnp.bfloat16

POOL_WINDOWS = (2, 4, 8, 16)
N_POOL_GROUPS = 4
HEAD_DIM = 128
N_HEADS = 8
N_KV_HEADS = 2
HEAD_GROUP = N_HEADS // N_KV_HEADS
IDX_HEADS = 16
IDX_DIM = 64
TOPK_MAX = 256
N_BUCKETS = 32
MAX_DISTANCE = 128
N_GROUPS = 4
EXPERTS_PER_GROUP = 8
N_EXPERTS = N_GROUPS * EXPERTS_PER_GROUP
EPS = 1e-6
NEG_INF = -1e30
BIG = 3e38
LOG2E = 1.4426950408889634

LANES = 128
SUBLANES = 8
VMEM_LIMIT_BYTES = 56 * 1024 * 1024

ADA_TN = 1024
PROJ_TM = 256
ATT_T = 256
SCORE_SUB = 256
COUNT_RB = 128
MIX_TM = 256
HALO = 16
ROUTE_TR = 256
DISP_TD = 256
MOE_BM = 256
COMB_TF = 256
DMA_UNROLL = 8
SEL_FIXED_ITERS = 18
SEL_MAX_ITERS = 80


def _cparams(sem):
    return pltpu.CompilerParams(dimension_semantics=sem, vmem_limit_bytes=VMEM_LIMIT_BYTES)


def _silu(x):
    return x * (1.0 / (1.0 + jnp.exp(-x)))


def _dot(a, b):
    return jnp.dot(a, b, preferred_element_type=F32)


def _dot_nt(a, b):
    return lax.dot_general(a, b, (((1,), (1,)), ((), ())), preferred_element_type=F32)


def _ada_body(c_ref, w_ref, b_ref, o_ref):
    s = _silu(c_ref[...])
    o_ref[...] = _dot(s.astype(_MXU_DTYPE), w_ref[...].astype(_MXU_DTYPE)) + b_ref[...]


def _ada(c, w, b):
    bsz, d = c.shape
    n = w.shape[1]
    return pl.pallas_call(
        _ada_body,
        grid=(n // ADA_TN,),
        in_specs=[pl.BlockSpec((bsz, d), lambda j: (0, 0)),
                  pl.BlockSpec((d, ADA_TN), lambda j: (0, j)),
                  pl.BlockSpec((1, ADA_TN), lambda j: (0, j))],
        out_specs=pl.BlockSpec((bsz, ADA_TN), lambda j: (0, j)),
        out_shape=jax.ShapeDtypeStruct((bsz, n), F32),
        compiler_params=_cparams(("arbitrary",)),
        name="ada",
    )(c, w, b)


def _rms_mod(x, g, shift, scale):
    ms = jnp.mean(x * x, axis=-1, keepdims=True)
    y = x * lax.rsqrt(ms + EPS) * g
    return y * (1.0 + scale) + shift


def _proj_body(x_ref, mod_ref, g_ref, w_ref, u_ref, q_ref, k_ref, vt_ref, qi_ref, kia_ref, kib_ref,
               wit_ref, *, cuts):
    h = _rms_mod(x_ref[0], g_ref[...], mod_ref[0, 0:1, :], mod_ref[0, 1:2, :])
    hb = h.astype(_MXU_DTYPE)

    def seg(name):
        lo, hi = cuts[name]
        return _dot(hb, w_ref[:, lo:hi])

    u_ref[0] = seg("u")
    q_ref[0] = seg("q").astype(q_ref.dtype)
    k_ref[0] = seg("k").astype(k_ref.dtype)
    vt_ref[0] = seg("v").T.astype(vt_ref.dtype)
    qi_ref[0] = seg("qi").astype(qi_ref.dtype)
    kia_ref[0] = seg("kia").astype(kia_ref.dtype)
    kib_ref[0] = seg("kib").astype(kib_ref.dtype)
    wit_ref[0] = seg("wi").T[:IDX_HEADS, :]


def _proj(x, mod, g_mix, w_in):
    bsz, s, d = x.shape
    pool_w = d // 2
    attn_w = N_HEADS * HEAD_DIM
    kv_w = N_KV_HEADS * HEAD_DIM
    qi_w = IDX_HEADS * IDX_DIM
    c = np.cumsum([0, pool_w, attn_w, kv_w, kv_w, qi_w, IDX_DIM, IDX_HEADS])
    w_ki = w_in[:, c[5]:c[6]]
    w_wi = w_in[:, c[6]:c[7]]
    zk = jnp.zeros((d, LANES - IDX_DIM), w_in.dtype)
    wp = jnp.concatenate(
        [w_in[:, :c[5]], w_ki, zk, zk, w_ki, w_wi, jnp.zeros((d, LANES - IDX_HEADS), w_in.dtype)],
        axis=1).astype(_MXU_DTYPE)
    base = int(c[5])
    cuts = {"u": (int(c[0]), int(c[1])), "q": (int(c[1]), int(c[2])), "k": (int(c[2]), int(c[3])),
            "v": (int(c[3]), int(c[4])), "qi": (int(c[4]), int(c[5])),
            "kia": (base, base + LANES), "kib": (base + LANES, base + 2 * LANES),
            "wi": (base + 2 * LANES, base + 3 * LANES)}
    tm = PROJ_TM
    pw = wp.shape[1]
    row = lambda w: pl.BlockSpec((1, tm, w), lambda b, i: (b, i, 0))
    out_shapes = [
        jax.ShapeDtypeStruct((bsz, s, pool_w), F32),
        jax.ShapeDtypeStruct((bsz, s, attn_w), _MXU_DTYPE),
        jax.ShapeDtypeStruct((bsz, s, kv_w), _MXU_DTYPE),
        jax.ShapeDtypeStruct((bsz, kv_w, s), _MXU_DTYPE),
        jax.ShapeDtypeStruct((bsz, s, qi_w), _MXU_DTYPE),
        jax.ShapeDtypeStruct((bsz, s, LANES), _MXU_DTYPE),
        jax.ShapeDtypeStruct((bsz, s, LANES), _MXU_DTYPE),
        jax.ShapeDtypeStruct((bsz, IDX_HEADS, s), F32),
    ]
    out_specs = [row(pool_w), row(attn_w), row(kv_w),
                 pl.BlockSpec((1, kv_w, tm), lambda b, i: (b, 0, i)),
                 row(qi_w), row(LANES), row(LANES),
                 pl.BlockSpec((1, IDX_HEADS, tm), lambda b, i: (b, 0, i))]
    return pl.pallas_call(
        functools.partial(_proj_body, cuts=cuts),
        grid=(bsz, s // tm),
        in_specs=[pl.BlockSpec((1, tm, d), lambda b, i: (b, i, 0)),
                  pl.BlockSpec((1, 6, d), lambda b, i: (b, 0, 0)),
                  pl.BlockSpec((1, d), lambda b, i: (0, 0)),
                  pl.BlockSpec((d, pw), lambda b, i: (0, 0))],
        out_specs=out_specs,
        out_shape=out_shapes,
        compiler_params=_cparams(("arbitrary", "arbitrary")),
        name="proj",
    )(x, mod, g_mix, wp)


def _bucket_starts():
    max_exact = N_BUCKETS // 2
    d = np.arange(1, 4 * MAX_DISTANCE, dtype=np.float32)
    large = max_exact + (np.log(d / np.float32(max_exact)) / np.float32(math.log(MAX_DISTANCE / max_exact))
                         * np.float32(N_BUCKETS - max_exact)).astype(np.int32)
    large = np.minimum(large, N_BUCKETS - 1)
    bucket = np.where(d < max_exact, d.astype(np.int32), large)
    bucket = np.concatenate([[0], bucket])
    starts = [int(np.argmax(bucket >= b)) for b in range(N_BUCKETS)]
    assert all(np.all(bucket[starts[b]:] >= b) for b in range(N_BUCKETS))
    assert starts[N_BUCKETS - 1] <= MAX_DISTANCE
    return starts


def _bias_body(rb_ref, o_ref, *, t, starts):
    diff = pl.program_id(0)
    s_l = lax.broadcasted_iota(I32, (t, t), 0)
    t_l = lax.broadcasted_iota(I32, (t, t), 1)
    dist = diff * t + t_l - s_l
    for h in range(N_HEADS):
        val = jnp.full((t, t), rb_ref[0, h], F32)
        for b in range(1, N_BUCKETS):
            val = jnp.where(dist >= starts[b], rb_ref[b, h], val)
        hh = h % HEAD_GROUP
        o_ref[0, h // HEAD_GROUP, :, hh * t:(hh + 1) * t] = val * LOG2E


def _bias_tiles(rel_bias, t):
    assert t >= MAX_DISTANCE
    return pl.pallas_call(
        functools.partial(_bias_body, t=t, starts=_bucket_starts()),
        grid=(3,),
        in_specs=[pl.BlockSpec(memory_space=pltpu.SMEM)],
        out_specs=pl.BlockSpec((1, N_KV_HEADS, t, HEAD_GROUP * t), lambda i: (i, 0, 0, 0)),
        out_shape=jax.ShapeDtypeStruct((3, N_KV_HEADS, t, HEAD_GROUP * t), F32),
        compiler_params=_cparams(("arbitrary",)),
        name="bias_tiles",
    )(rel_bias)


def _dsa_body(q_ref, qi_ref, wit_ref, kia_ref, kib_ref, k_ref, vt_ref, bias_ref, o_ref,
              score_ref, mask_ref, acc_ref, *, t, topk):
    i = pl.program_id(1)
    nk = i + 1
    t_glob = i * t + lax.broadcasted_iota(I32, (1, t), 1)
    idx_scale = (IDX_DIM ** -0.5) * (IDX_HEADS ** -0.5)

    def score_chunk(c, carry):
        r0 = pl.multiple_of(c * SCORE_SUB, SCORE_SUB)
        ka = kia_ref[0, pl.ds(r0, SCORE_SUB), :]
        kb = kib_ref[0, pl.ds(r0, SCORE_SUB), :]
        acc = jnp.zeros((SCORE_SUB, t), F32)
        for j in range(IDX_HEADS // 2):
            qp = qi_ref[0, :, j * LANES:(j + 1) * LANES]
            da = _dot_nt(ka, qp)
            db = _dot_nt(kb, qp)
            acc = acc + jnp.maximum(da, 0.0) * wit_ref[0, 2 * j:2 * j + 1, :]
            acc = acc + jnp.maximum(db, 0.0) * wit_ref[0, 2 * j + 1:2 * j + 2, :]
        sc = acc * idx_scale
        s_glob = r0 + lax.broadcasted_iota(I32, (SCORE_SUB, 1), 0)
        score_ref[pl.ds(r0, SCORE_SUB), :] = jnp.where(s_glob <= t_glob, sc, NEG_INF)
        return carry

    lax.fori_loop(0, nk * (t // SCORE_SUB), score_chunk, 0)

    nblk = nk * (t // COUNT_RB)
    fold = lambda m: jnp.sum(m.reshape(COUNT_RB // SUBLANES, SUBLANES, t), axis=0)

    def blk(r):
        return score_ref[pl.ds(pl.multiple_of(r * COUNT_RB, COUNT_RB), COUNT_RB), :]

    def count_ge(c):
        def body(r, acc):
            return acc + fold((blk(r) >= c).astype(I32))
        acc = lax.fori_loop(0, nblk, body, jnp.zeros((SUBLANES, t), I32))
        return jnp.sum(acc, axis=0, keepdims=True)

    def stats_body(r, carry):
        cnt, mn, mx = carry
        b = blk(r)
        real = b > NEG_INF
        cnt = cnt + fold(real.astype(I32))
        mn = jnp.minimum(mn, jnp.min(jnp.where(real, b, BIG).reshape(COUNT_RB // SUBLANES, SUBLANES, t), axis=0))
        mx = jnp.maximum(mx, jnp.max(b.reshape(COUNT_RB // SUBLANES, SUBLANES, t), axis=0))
        return cnt, mn, mx

    cnt8, mn8, mx8 = lax.fori_loop(
        0, nblk, stats_body,
        (jnp.zeros((SUBLANES, t), I32), jnp.full((SUBLANES, t), BIG, F32), jnp.full((SUBLANES, t), -BIG, F32)))
    n_real = jnp.sum(cnt8, axis=0, keepdims=True)
    rmin = jnp.min(mn8, axis=0, keepdims=True)
    rmax = jnp.max(mx8, axis=0, keepdims=True)
    n_max = count_ge(rmax)
    degenerate = jnp.logical_and(n_real >= topk, n_max >= topk)
    lo0 = jnp.where(n_real < topk, NEG_INF, jnp.where(degenerate, rmax, rmin))
    hi0 = jnp.where(degenerate, BIG, rmax)
    done0 = jnp.where(n_real <= topk, 1, 0)

    def bisect(st):
        lo, hi, done, stuck = st
        c = lo + (hi - lo) * 0.5
        active = (done + stuck) == 0
        has_mid = jnp.logical_and(c > lo, c < hi)
        n = count_ge(c)
        upd = jnp.logical_and(active, has_mid)
        lo = jnp.where(jnp.logical_and(upd, n >= topk), c, lo)
        hi = jnp.where(jnp.logical_and(upd, n < topk), c, hi)
        done = jnp.where(jnp.logical_and(upd, n == topk), 1, done)
        stuck = jnp.where(jnp.logical_and(active, jnp.logical_not(has_mid)), 1, stuck)
        return lo, hi, done, stuck

    st = lax.fori_loop(0, SEL_FIXED_ITERS, lambda _, s: bisect(s), (lo0, hi0, done0, jnp.zeros((1, t), I32)))

    def sel_cond(s):
        return jnp.logical_and(s[0] < SEL_MAX_ITERS, jnp.min(s[3] + s[4]) == 0)

    _, lo, hi, done, _ = lax.while_loop(sel_cond, lambda s: (s[0] + 1,) + bisect(s[1:]), (jnp.int32(0),) + st)

    def write_mask(r, cut):
        r0 = pl.multiple_of(r * COUNT_RB, COUNT_RB)
        b = score_ref[pl.ds(r0, COUNT_RB), :]
        s_glob = r0 + lax.broadcasted_iota(I32, (COUNT_RB, 1), 0)
        sel = jnp.logical_and(b >= lo, s_glob <= t_glob)
        if cut is not None:
            sel = jnp.logical_and(sel, jnp.logical_or(b >= hi, s_glob < cut))
        mask_ref[pl.ds(r0, COUNT_RB), :] = jnp.concatenate([jnp.where(sel, 0.0, NEG_INF)] * HEAD_GROUP, axis=1)

    any_tied = jnp.min(done) == 0

    @pl.when(jnp.logical_not(any_tied))
    def _():
        def body(r, carry):
            write_mask(r, None)
            return carry
        lax.fori_loop(0, nblk, body, 0)

    @pl.when(any_tied)
    def _():
        need = topk - count_ge(hi)

        def count_tie_below(x):
            def body(r, acc):
                r0 = pl.multiple_of(r * COUNT_RB, COUNT_RB)
                b = score_ref[pl.ds(r0, COUNT_RB), :]
                s_glob = r0 + lax.broadcasted_iota(I32, (COUNT_RB, 1), 0)
                m = jnp.logical_and(jnp.logical_and(b >= lo, b < hi), s_glob < x)
                return acc + fold(m.astype(I32))
            acc = lax.fori_loop(0, nblk, body, jnp.zeros((SUBLANES, t), I32))
            return jnp.sum(acc, axis=0, keepdims=True)

        nbits = int(score_ref.shape[0]).bit_length()

        def bit_body(bi, x):
            c = x + jnp.left_shift(jnp.int32(1), nbits - 1 - bi)
            return jnp.where(count_tie_below(c) < need, c, x)

        x = lax.fori_loop(0, nbits, bit_body, jnp.zeros((1, t), I32))
        cut = jnp.where(done > 0, jnp.int32(2 ** 30), x + 1)

        def body(r, carry):
            write_mask(r, cut)
            return carry
        lax.fori_loop(0, nblk, body, 0)

    scale = HEAD_DIM ** -0.5
    gt = HEAD_GROUP * t

    def col_reduce(op, a):
        part = op(a.reshape(a.shape[0] // SUBLANES, SUBLANES, a.shape[1]), axis=0)
        return op(part, axis=0, keepdims=True)

    qgs = [jnp.concatenate(
        [q_ref[0, :, h * HEAD_DIM:(h + 1) * HEAD_DIM] for h in range(g * HEAD_GROUP, (g + 1) * HEAD_GROUP)],
        axis=0) for g in range(N_KV_HEADS)]
    acc_ref[...] = jnp.zeros_like(acc_ref)

    def att_chunk(kc, carry):
        r0 = pl.multiple_of(kc * t, t)
        mk = mask_ref[pl.ds(r0, t), :]
        new = []
        for g in range(N_KV_HEADS):
            m, l = carry[g]
            kk = k_ref[0, pl.ds(r0, t), g * HEAD_DIM:(g + 1) * HEAD_DIM]
            s = _dot_nt(kk, qgs[g]) * (scale * LOG2E)
            s = s + bias_ref[jnp.minimum(i - kc, 2), g] + mk
            m_new = jnp.maximum(m, col_reduce(jnp.max, s))
            alpha = jnp.exp2(m - m_new)
            p = jnp.exp2(s - m_new)
            l = alpha * l + col_reduce(jnp.sum, p)
            vt = vt_ref[0, g * HEAD_DIM:(g + 1) * HEAD_DIM, pl.ds(r0, t)]
            acc_ref[g] = alpha * acc_ref[g] + _dot(vt, p.astype(_MXU_DTYPE))
            new.append((m_new, l))
        return tuple(new)

    init = (jnp.full((1, gt), NEG_INF, F32), jnp.zeros((1, gt), F32))
    fin = lax.fori_loop(0, nk, att_chunk, (init,) * N_KV_HEADS)
    for g in range(N_KV_HEADS):
        out = acc_ref[g] / fin[g][1]
        for hh in range(HEAD_GROUP):
            h = g * HEAD_GROUP + hh
            o_ref[0, :, h * HEAD_DIM:(h + 1) * HEAD_DIM] = out[:, hh * t:(hh + 1) * t].T.astype(o_ref.dtype)


def _dsa(q, qi, wit, kia, kib, k, vt, bias, topk):
    bsz, s, aw = q.shape
    t = ATT_T
    assert s % t == 0 and topk <= t
    row = lambda w: pl.BlockSpec((1, t, w), lambda b, i: (b, i, 0))
    full = lambda a: pl.BlockSpec((1,) + a.shape[1:], lambda b, i: (b, 0, 0))
    return pl.pallas_call(
        functools.partial(_dsa_body, t=t, topk=topk),
        grid=(bsz, s // t),
        in_specs=[row(aw), row(qi.shape[2]),
                  pl.BlockSpec((1, IDX_HEADS, t), lambda b, i: (b, 0, i)),
                  full(kia), full(kib), full(k), full(vt),
                  pl.BlockSpec(bias.shape, lambda b, i: (0, 0, 0, 0))],
        out_specs=row(aw),
        out_shape=jax.ShapeDtypeStruct((bsz, s, aw), _MXU_DTYPE),
        scratch_shapes=[pltpu.VMEM((s, t), F32), pltpu.VMEM((s, HEAD_GROUP * t), F32),
                        pltpu.VMEM((N_KV_HEADS, HEAD_DIM, HEAD_GROUP * t), F32)],
        compiler_params=_cparams(("arbitrary", "arbitrary")),
        name="dsa",
    )(q, qi, wit, kia, kib, k, vt, bias)


def _pack_halves(hb):
    assert jnp.dtype(hb.dtype).itemsize == 2
    n = hb.shape[1] // 2
    hi = lax.bitcast_convert_type(hb[:, :n].astype(F32), I32)
    lo = lax.bitcast_convert_type(hb[:, n:].astype(F32), I32)
    return jnp.bitwise_or(hi, lax.shift_right_logical(lo, jnp.full(lo.shape, 16, I32)))


def _unpack_halves(w, dtype):
    hi = lax.bitcast_convert_type(jnp.bitwise_and(w, jnp.int32(-65536)), F32).astype(dtype)
    lo = lax.bitcast_convert_type(jnp.left_shift(w, 16), F32).astype(dtype)
    return jnp.concatenate([hi, lo], axis=1)


def _mix_body(u_ref, halo_ref, a_ref, x_ref, mod_ref, wpool_ref, ps_ref, wout_ref, g_ref, wr_ref, br_ref,
              x1_ref, h2_ref, lg_ref, *, seq, tm):
    t0 = (pl.program_id(0) * tm) % seq
    u = u_ref[...]
    halo = jnp.where(t0 == 0, 0.0, halo_ref[...])
    ext = jnp.concatenate([halo, u], axis=0)
    pos = t0 + lax.broadcasted_iota(I32, (tm, 1), 0)
    gw = u.shape[1] // N_POOL_GROUPS
    ys = []
    for g, w in enumerate(POOL_WINDOWS):
        a = ext[:, g * gw:(g + 1) * gw]
        sft = 1
        while sft < w:
            a = a + pltpu.roll(a, sft, 0)
            sft *= 2
        cnt = jnp.minimum(pos + 1, w).astype(F32)
        p = a[HALO:, :] / cnt - u[:, g * gw:(g + 1) * gw]
        ys.append(_dot(p.astype(_MXU_DTYPE), wpool_ref[g]) * ps_ref[:, g * gw:(g + 1) * gw])
    pool = jnp.concatenate(ys, axis=1).astype(_MXU_DTYPE)
    pw = pool.shape[1]
    mixed = _dot(pool, wout_ref[:pw, :]) + _dot(a_ref[...], wout_ref[pw:, :])
    x1 = x_ref[...] + mod_ref[0, 2:3, :] * mixed
    x1_ref[...] = x1
    h2 = _rms_mod(x1, g_ref[...], mod_ref[0, 3:4, :], mod_ref[0, 4:5, :]).astype(_MXU_DTYPE)
    h2_ref[...] = _pack_halves(h2)
    lg_ref[...] = _dot(h2, wr_ref[...]) + br_ref[...]


def _mix(u, attn, x, mod, w_pool, pool_scale, w_out, g_ffn, w_r, b_r, seq):
    n_tok, d = x.shape
    tm = MIX_TM
    assert seq % tm == 0 and tm % HALO == 0
    pw = u.shape[1]
    aw = attn.shape[1]
    rows = lambda w: pl.BlockSpec((tm, w), lambda i: (i, 0))
    const = lambda a: pl.BlockSpec(a.shape, lambda i: (0,) * a.ndim)
    return pl.pallas_call(
        functools.partial(_mix_body, seq=seq, tm=tm),
        grid=(n_tok // tm,),
        in_specs=[rows(pw),
                  pl.BlockSpec((HALO, pw), lambda i: (jnp.maximum(i * (tm // HALO) - 1, 0), 0)),
                  rows(aw), rows(d),
                  pl.BlockSpec((1, 6, d), lambda i: ((i * tm) // seq, 0, 0)),
                  const(w_pool), const(pool_scale), const(w_out), const(g_ffn), const(w_r), const(b_r)],
        out_specs=[rows(d), rows(d // 2), rows(LANES)],
        out_shape=[jax.ShapeDtypeStruct((n_tok, d), F32), jax.ShapeDtypeStruct((n_tok, d // 2), I32),
                   jax.ShapeDtypeStruct((n_tok, LANES), F32)],
        compiler_params=_cparams(("arbitrary",)),
        name="mix",
    )(u, u, attn, x, mod, w_pool, pool_scale, w_out, g_ffn, w_r, b_r)


def _route_body(lg_ref, rt_ref, gate_ref, cnt_ref, run_ref):
    @pl.when(pl.program_id(0) == 0)
    def _():
        run_ref[...] = jnp.zeros_like(run_ref)

    lg = lg_ref[...]
    tr = lg.shape[0]
    lane = lax.broadcasted_iota(I32, lg.shape, 1)
    gmask = lane < N_GROUPS
    gl = jnp.where(gmask, lg, -BIG)
    gmax = jnp.max(gl, axis=-1, keepdims=True)
    gsel = jnp.min(jnp.where(jnp.logical_and(gmask, gl == gmax), lane, LANES), axis=-1, keepdims=True)
    gsum = jnp.sum(jnp.where(gmask, jnp.exp(gl - gmax), 0.0), axis=-1, keepdims=True)
    p_g = 1.0 / gsum
    in_e = jnp.logical_and(lane >= N_GROUPS, lane < N_GROUPS + N_EXPERTS)
    emask = jnp.logical_and(in_e, lax.shift_right_arithmetic(lane - N_GROUPS, EXPERTS_PER_GROUP.bit_length() - 1) == gsel)
    el = jnp.where(emask, lg, -BIG)
    emax = jnp.max(el, axis=-1, keepdims=True)
    ex = jnp.where(emask, jnp.exp(el - emax), 0.0)
    ep = ex / jnp.sum(ex, axis=-1, keepdims=True)
    p1 = jnp.max(jnp.where(emask, ep, -1.0), axis=-1, keepdims=True)
    i1 = jnp.min(jnp.where(jnp.logical_and(emask, ep == p1), lane, LANES), axis=-1, keepdims=True)
    m2 = jnp.logical_and(emask, lane != i1)
    p2 = jnp.max(jnp.where(m2, ep, -1.0), axis=-1, keepdims=True)
    i2 = jnp.min(jnp.where(jnp.logical_and(m2, ep == p2), lane, LANES), axis=-1, keepdims=True)
    den = p1 + p2
    g1 = p_g * p1 / den
    g2 = p_g * p2 / den
    e1 = i1 - N_GROUPS
    e2 = i2 - N_GROUPS

    oh = jnp.logical_or(lane == e1, lane == e2)
    ri = lax.broadcasted_iota(I32, (tr, tr), 0)
    ci = lax.broadcasted_iota(I32, (tr, tr), 1)
    ltri = (ci < ri).astype(_MXU_DTYPE)
    before = _dot(ltri, oh.astype(_MXU_DTYPE)) + run_ref[...]
    r1 = jnp.sum(jnp.where(lane == e1, before, 0.0), axis=-1, keepdims=True)
    r2 = jnp.sum(jnp.where(lane == e2, before, 0.0), axis=-1, keepdims=True)
    run = run_ref[...] + jnp.sum(oh.astype(F32), axis=0, keepdims=True)
    run_ref[...] = run
    cnt_ref[...] = jnp.broadcast_to(run, cnt_ref.shape)

    route = jnp.where(lane == 0, e1.astype(F32),
                      jnp.where(lane == 1, e2.astype(F32),
                                jnp.where(lane == 2, r1, jnp.where(lane == 3, r2, 0.0))))
    rt_ref[...] = route.T[:SUBLANES, :].astype(I32)
    gate_ref[...] = jnp.where(lane == 0, g1, jnp.where(lane == 1, g2, 0.0))


def _route(logits):
    n_tok = logits.shape[0]
    tr = ROUTE_TR
    return pl.pallas_call(
        _route_body,
        grid=(n_tok // tr,),
        in_specs=[pl.BlockSpec((tr, LANES), lambda i: (i, 0))],
        out_specs=[pl.BlockSpec((SUBLANES, tr), lambda i: (0, i)),
                   pl.BlockSpec((tr, LANES), lambda i: (i, 0)),
                   pl.BlockSpec((SUBLANES, LANES), lambda i: (0, 0))],
        out_shape=[jax.ShapeDtypeStruct((SUBLANES, n_tok), I32),
                   jax.ShapeDtypeStruct((n_tok, LANES), F32),
                   jax.ShapeDtypeStruct((SUBLANES, LANES), F32)],
        scratch_shapes=[pltpu.VMEM((1, LANES), F32)],
        compiler_params=_cparams(("arbitrary",)),
        name="route",
    )(logits)


def _dest_body(ps_ref, rt_ref, o_ref):
    r = rt_ref[...]
    start = jnp.zeros_like(r)
    for e in range(N_EXPERTS):
        start = jnp.where(r == e, ps_ref[e], start)
    o_ref[...] = start + pltpu.roll(r, SUBLANES - 2, 0)


def _dest(route_t, pstart):
    n_tok = route_t.shape[1]
    tb = min(n_tok, 2048)
    return pl.pallas_call(
        _dest_body,
        grid_spec=pltpu.PrefetchScalarGridSpec(
            num_scalar_prefetch=1,
            grid=(n_tok // tb,),
            in_specs=[pl.BlockSpec((SUBLANES, tb), lambda i, ps: (0, i))],
            out_specs=pl.BlockSpec((SUBLANES, tb), lambda i, ps: (0, i)),
        ),
        out_shape=jax.ShapeDtypeStruct((SUBLANES, n_tok), I32),
        compiler_params=_cparams(("arbitrary",)),
        name="dest",
    )(pstart, route_t)


def _row_copy(src_ref, src_row, dst_ref, dst_row, sem):
    return pltpu.make_async_copy(src_ref.at[pl.ds(src_row, 1), :], dst_ref.at[pl.ds(dst_row, 1), :], sem)


def _dispatch_body(pstart_ref, cnt_ref, nu_ref, h_ref, d0_ref, d1_ref, xs_ref, zero_ref, sem, *, td, bm):
    def issue(c, carry):
        for u in range(SUBLANES):
            r = c * SUBLANES + u
            for j, d_ref in enumerate((d0_ref, d1_ref)):
                pltpu.make_async_copy(h_ref.at[c, pl.ds(u, 1), :], xs_ref.at[pl.ds(d_ref[r], 1), :],
                                      sem).start(priority=j)
        return carry

    lax.fori_loop(0, td // SUBLANES, issue, 0)

    def drain(c, carry):
        for _ in range(2 * SUBLANES):
            pltpu.make_async_copy(h_ref.at[0, pl.ds(0, 1), :], xs_ref.at[pl.ds(0, 1), :], sem).wait()
        return carry

    lax.fori_loop(0, td // SUBLANES, drain, 0)

    @pl.when(pl.program_id(0) == pl.num_programs(0) - 1)
    def _():
        zero_ref[...] = jnp.zeros_like(zero_ref)

        def per_expert(e, carry):
            c = cnt_ref[e]
            n_pad = (bm - c % bm) % bm
            base = pstart_ref[e] + c

            def start(r, cc):
                _row_copy(zero_ref, 0, xs_ref, base + r, sem).start()
                return cc

            def wait(r, cc):
                _row_copy(zero_ref, 0, xs_ref, 0, sem).wait()
                return cc

            lax.fori_loop(0, n_pad, start, 0)
            lax.fori_loop(0, n_pad, wait, 0)
            return carry

        lax.fori_loop(0, N_EXPERTS, per_expert, 0)

        first = nu_ref[0] * (bm // SUBLANES)
        n_tail = xs_ref.shape[0] // SUBLANES - first

        def tail_copy(r):
            return pltpu.make_async_copy(
                zero_ref, xs_ref.at[pl.ds(pl.multiple_of((first + r) * SUBLANES, SUBLANES), SUBLANES), :], sem)

        def tail_start(r, cc):
            tail_copy(r).start()
            return cc

        def tail_wait(r, cc):
            tail_copy(0).wait()
            return cc

        lax.fori_loop(0, n_tail, tail_start, 0)
        lax.fori_loop(0, n_tail, tail_wait, 0)


def _dispatch(h2, dest0, dest1, pstart, counts, n_used, n_slots):
    n_tok, d = h2.shape
    td = DISP_TD
    tok = pl.BlockSpec((td,), lambda i, ps, cn, nu: (i,), memory_space=pltpu.SMEM)
    return pl.pallas_call(
        functools.partial(_dispatch_body, td=td, bm=MOE_BM),
        grid_spec=pltpu.PrefetchScalarGridSpec(
            num_scalar_prefetch=3,
            grid=(n_tok // td,),
            in_specs=[pl.BlockSpec((td // SUBLANES, SUBLANES, d), lambda i, ps, cn, nu: (i, 0, 0)), tok, tok],
            out_specs=pl.BlockSpec(memory_space=pl.ANY),
            scratch_shapes=[pltpu.VMEM((SUBLANES, d), h2.dtype), pltpu.SemaphoreType.DMA],
        ),
        out_shape=jax.ShapeDtypeStruct((n_slots, d), h2.dtype),
        compiler_params=_cparams(("arbitrary",)),
        name="dispatch",
    )(pstart, counts, n_used, h2.reshape(n_tok // SUBLANES, SUBLANES, d), dest0, dest1)


def _experts_body(slot_ref, cast_ref, pre_ref, misc_ref, x_ref, wg_hbm, wu_hbm, wd_hbm, y_ref,
                  sg, su, sd, wgb, wub, wdb, sem):
    blk = pl.program_id(0)
    e0, pre0, n_used = misc_ref[0], misc_ref[1], misc_ref[2]

    def copies(e):
        return (pltpu.make_async_copy(wg_hbm.at[e], sg, sem.at[0]),
                pltpu.make_async_copy(wu_hbm.at[e], su, sem.at[1]),
                pltpu.make_async_copy(wd_hbm.at[e], sd, sem.at[2]))

    def fetch(e):
        for c in copies(e):
            c.start(priority=1)

    def land(slot):
        for c in copies(0):
            c.wait()
        wgb[slot] = sg[...].astype(wgb.dtype)
        wub[slot] = su[...].astype(wub.dtype)
        wdb[slot] = sd[...].astype(wdb.dtype)

    @pl.when(blk == 0)
    def _():
        fetch(e0)
        land(0)

        @pl.when(pre0 >= 0)
        def _():
            fetch(pre0)

    @pl.when(blk < n_used)
    def _():
        slot = slot_ref[blk]
        xb = _unpack_halves(x_ref[...], _MXU_DTYPE)
        a = _silu(_dot(xb, wgb[slot])) * _dot(xb, wub[slot])
        y_ref[...] = _dot(a.astype(_MXU_DTYPE), wdb[slot])

        @pl.when(cast_ref[blk] == 1)
        def _():
            land(1 - slot)

            @pl.when(pre_ref[blk] >= 0)
            def _():
                fetch(pre_ref[blk])

    @pl.when(blk >= n_used)
    def _():
        y_ref[...] = jnp.zeros_like(y_ref)


def _experts(xs, block_exp, n_used, pends, w_gate, w_up, w_down):
    n_slots, dpk = xs.shape
    bm = MOE_BM
    n_blocks = n_slots // bm
    d, de = w_gate.shape[1:]
    last_used = n_used[0] - 1
    blk = jnp.minimum(jnp.arange(n_blocks, dtype=I32), last_used)
    be = block_exp[blk]
    first_next = jnp.minimum(pends[be] // bm, last_used)
    nxt = block_exp[first_next]
    has_next = nxt != be
    is_last = jnp.concatenate([be[1:] != be[:-1], jnp.ones((1,), jnp.bool_)])
    is_last = jnp.logical_or(is_last, jnp.arange(n_blocks) >= last_used)
    cast = jnp.logical_and(is_last, has_next).astype(I32)
    nxt2 = jnp.where(has_next[first_next], nxt[first_next], -1)
    pre = jnp.where(cast == 1, nxt2, -1).astype(I32)
    is_first = jnp.concatenate([jnp.ones((1,), jnp.bool_), be[1:] != be[:-1]])
    slot = ((jnp.cumsum(is_first.astype(I32)) - 1) % 2).astype(I32)
    misc = jnp.stack([be[0], jnp.where(has_next[0], nxt[0], -1), n_used[0]]).astype(I32)
    hbm = pl.BlockSpec(memory_space=pl.ANY)
    return pl.pallas_call(
        _experts_body,
        grid_spec=pltpu.PrefetchScalarGridSpec(
            num_scalar_prefetch=4,
            grid=(n_blocks,),
            in_specs=[pl.BlockSpec((bm, dpk), lambda j, sl, ca, pr, mi: (jnp.minimum(j, mi[2] - 1), 0)),
                      hbm, hbm, hbm],
            out_specs=pl.BlockSpec((bm, d), lambda j, sl, ca, pr, mi: (j, 0)),
            scratch_shapes=[pltpu.VMEM((d, de), w_gate.dtype), pltpu.VMEM((d, de), w_up.dtype),
                            pltpu.VMEM((de, d), w_down.dtype),
                            pltpu.VMEM((2, d, de), _MXU_DTYPE), pltpu.VMEM((2, d, de), _MXU_DTYPE),
                            pltpu.VMEM((2, de, d), _MXU_DTYPE), pltpu.SemaphoreType.DMA((3,))],
        ),
        out_shape=jax.ShapeDtypeStruct((n_slots, d), F32),
        compiler_params=_cparams(("arbitrary",)),
        name="experts",
    )(slot, cast, pre, misc, xs, w_gate, w_up, w_down)


def _combine_body(x1_ref, gate_ref, d0_ref, d1_ref, n0_ref, n1_ref, mod_ref, g_ref, ys_ref, o_ref, buf_ref, sem,
                  *, tf):
    i = pl.program_id(0)
    cur = i % 2

    def gather(da_ref, db_ref, half):
        def issue(c, carry):
            for u in range(SUBLANES):
                r = c * SUBLANES + u
                for j, d_ref in enumerate((da_ref, db_ref)):
                    pltpu.make_async_copy(ys_ref.at[pl.ds(d_ref[r], 1), :],
                                          buf_ref.at[half, j, c, pl.ds(u, 1), :], sem.at[half]).start(priority=j)
            return carry

        lax.fori_loop(0, tf // SUBLANES, issue, 0)

    @pl.when(i == 0)
    def _():
        gather(d0_ref, d1_ref, 0)

    @pl.when(i + 1 < pl.num_programs(0))
    def _():
        gather(n0_ref, n1_ref, 1 - cur)

    def drain(c, carry):
        for _ in range(2 * SUBLANES):
            pltpu.make_async_copy(ys_ref.at[pl.ds(0, 1), :], buf_ref.at[cur, 0, 0, pl.ds(0, 1), :],
                                  sem.at[cur]).wait()
        return carry

    lax.fori_loop(0, tf // SUBLANES, drain, 0)

    gates = gate_ref[...]
    d = o_ref.shape[1]
    moe = buf_ref[cur, 0].reshape(tf, d) * gates[:, 0:1] + buf_ref[cur, 1].reshape(tf, d) * gates[:, 1:2]
    x2 = x1_ref[...] + mod_ref[0, 5:6, :] * moe
    ms = jnp.mean(x2 * x2, axis=-1, keepdims=True)
    o_ref[...] = x2 * lax.rsqrt(ms + EPS) * g_ref[...]


def _combine(x1, gates, dest0, dest1, mod, g_final, ys, seq):
    n_tok, d = x1.shape
    tf = COMB_TF
    n_tiles = n_tok // tf
    tok = pl.BlockSpec((tf,), lambda i: (i,), memory_space=pltpu.SMEM)
    nxt = pl.BlockSpec((tf,), lambda i: (jnp.minimum(i + 1, n_tiles - 1),), memory_space=pltpu.SMEM)
    return pl.pallas_call(
        functools.partial(_combine_body, tf=tf),
        grid=(n_tiles,),
        in_specs=[pl.BlockSpec((tf, d), lambda i: (i, 0)),
                  pl.BlockSpec((tf, LANES), lambda i: (i, 0)),
                  tok, tok, nxt, nxt,
                  pl.BlockSpec((1, 6, d), lambda i: ((i * tf) // seq, 0, 0)),
                  pl.BlockSpec((1, d), lambda i: (0, 0)),
                  pl.BlockSpec(memory_space=pl.ANY)],
        out_specs=pl.BlockSpec((tf, d), lambda i: (i, 0)),
        scratch_shapes=[pltpu.VMEM((2, 2, tf // SUBLANES, SUBLANES, d), F32), pltpu.SemaphoreType.DMA((2,))],
        out_shape=jax.ShapeDtypeStruct((n_tok, d), F32),
        compiler_params=_cparams(("arbitrary",)),
        name="combine",
    )(x1, gates, dest0, dest1, dest0, dest1, mod, g_final, ys)


def kernel(x, c, w_ada, b_ada, g_mix, w_in, w_pool, pool_scale, rel_bias, w_out, g_ffn, w_group, b_group,
           w_router, b_router, w_gate, w_up, w_down, g_final):
    bsz, seq, d = x.shape
    n_tok = bsz * seq
    depth = w_ada.shape[0]
    assert depth == 1, "the final norm is fused into the only layer's combine"
    topk = min(TOPK_MAX, seq // 4)
    bias = _bias_tiles(rel_bias, ATT_T)
    bm = MOE_BM
    n_blocks = -(-(n_tok * 2) // bm) + N_EXPERTS
    n_slots = n_blocks * bm
    xt = x.reshape(n_tok, d)
    for i in range(depth):
        mod = _ada(c, w_ada[i], b_ada[i][None, :]).reshape(bsz, 6, d)
        u, q, k, vt, qi, kia, kib, wit = _proj(xt.reshape(bsz, seq, d), mod, g_mix[i][None, :], w_in[i])
        attn = _dsa(q, qi, wit, kia, kib, k, vt, bias, topk)
        w_r = jnp.concatenate(
            [w_group[i], w_router[i], jnp.zeros((d, LANES - N_GROUPS - N_EXPERTS), F32)], axis=1).astype(_MXU_DTYPE)
        b_r = jnp.concatenate(
            [b_group[i], b_router[i], jnp.zeros((LANES - N_GROUPS - N_EXPERTS,), F32)])[None, :]
        x1, h2, logits = _mix(u.reshape(n_tok, -1), attn.reshape(n_tok, -1), xt, mod,
                              w_pool[i].astype(_MXU_DTYPE), pool_scale[i][None, :],
                              w_out[i].astype(_MXU_DTYPE), g_ffn[i][None, :], w_r, b_r, seq)
        route_t, gates, cnt = _route(logits)
        counts = cnt[0, :N_EXPERTS].astype(I32)
        padded = (counts + bm - 1) // bm * bm
        pends = jnp.cumsum(padded)
        pstart = pends - padded
        n_used = (pends[-1:] // bm).astype(I32)
        block_start = jnp.arange(n_blocks, dtype=I32) * bm
        block_exp = jnp.minimum(
            jnp.sum((pends[None, :] <= block_start[:, None]).astype(I32), axis=1), N_EXPERTS - 1)
        dest = _dest(route_t, pstart)
        dest0, dest1 = dest[0], dest[1]
        xs = _dispatch(h2, dest0, dest1, pstart, counts, n_used, n_slots)
        ys = _experts(xs, block_exp, n_used, pends, w_gate[i], w_up[i], w_down[i])
        xt = _combine(x1, gates, dest0, dest1, mod, g_final[None, :], ys, seq)
    return xt.reshape(bsz, seq, d)
```

```python
import functools
import math

import numpy as np
import jax
import jax.numpy as jnp
from jax import lax
from jax.experimental import pallas as pl
from jax.experimental.pallas import tpu as pltpu

F32 = jnp.float32
I32 = jnp.int32
_MXU_DTYPE = jnp.bfloat16

POOL_WINDOWS = (2, 4, 8, 16)
N_POOL_GROUPS = 4
HEAD_DIM = 128
N_HEADS = 8
N_KV_HEADS = 2
HEAD_GROUP = N_HEADS // N_KV_HEADS
IDX_HEADS = 16
IDX_DIM = 64
TOPK_MAX = 256
N_BUCKETS = 32
MAX_DISTANCE = 128
N_GROUPS = 4
EXPERTS_PER_GROUP = 8
N_EXPERTS = N_GROUPS * EXPERTS_PER_GROUP
EPS = 1e-6
NEG_INF = -1e30
BIG = 3e38
LOG2E = 1.4426950408889634

LANES = 128
SUBLANES = 8
VMEM_LIMIT_BYTES = 56 * 1024 * 1024

ADA_TN = 1024
PROJ_TM = 256
ATT_T = 256
SCORE_SUB = 256
COUNT_RB = 128
MIX_TM = 256
HALO = 16
ROUTE_TR = 1024
DISP_TD = 256
MOE_BM = 256
COMB_TF = 256
DMA_UNROLL = 8
SEL_FIXED_ITERS = 18
SEL_MAX_ITERS = 80


def _cparams(sem):
    return pltpu.CompilerParams(dimension_semantics=sem, vmem_limit_bytes=VMEM_LIMIT_BYTES)


def _silu(x):
    return x * (1.0 / (1.0 + jnp.exp(-x)))


def _dot(a, b):
    return jnp.dot(a, b, preferred_element_type=F32)


def _dot_nt(a, b):
    return lax.dot_general(a, b, (((1,), (1,)), ((), ())), preferred_element_type=F32)


def _ada_body(c_ref, w_ref, b_ref, o_ref):
    s = _silu(c_ref[...])
    o_ref[...] = _dot(s.astype(_MXU_DTYPE), w_ref[...].astype(_MXU_DTYPE)) + b_ref[...]


def _ada(c, w, b):
    bsz, d = c.shape
    n = w.shape[1]
    return pl.pallas_call(
        _ada_body,
        grid=(n // ADA_TN,),
        in_specs=[pl.BlockSpec((bsz, d), lambda j: (0, 0)),
                  pl.BlockSpec((d, ADA_TN), lambda j: (0, j)),
                  pl.BlockSpec((1, ADA_TN), lambda j: (0, j))],
        out_specs=pl.BlockSpec((bsz, ADA_TN), lambda j: (0, j)),
        out_shape=jax.ShapeDtypeStruct((bsz, n), F32),
        compiler_params=_cparams(("arbitrary",)),
        name="ada",
    )(c, w, b)


def _rms_mod(x, g, shift, scale):
    ms = jnp.mean(x * x, axis=-1, keepdims=True)
    y = x * lax.rsqrt(ms + EPS) * g
    return y * (1.0 + scale) + shift


def _proj_body(x_ref, mod_ref, g_ref, w_ref, u_ref, q_ref, k_ref, vt_ref, qi_ref, kia_ref, kib_ref,
               wit_ref, *, cuts):
    h = _rms_mod(x_ref[0], g_ref[...], mod_ref[0, 0:1, :], mod_ref[0, 1:2, :])
    hb = h.astype(_MXU_DTYPE)

    def seg(name):
        lo, hi = cuts[name]
        return _dot(hb, w_ref[:, lo:hi])

    u_ref[0] = seg("u")
    q_ref[0] = seg("q").astype(q_ref.dtype)
    k_ref[0] = seg("k").astype(k_ref.dtype)
    vt_ref[0] = seg("v").T.astype(vt_ref.dtype)
    qi_ref[0] = seg("qi").astype(qi_ref.dtype)
    tail = seg("tail")
    tail = jnp.concatenate([tail, jnp.zeros((tail.shape[0], LANES - tail.shape[1]), F32)], axis=1)
    swapped = pltpu.roll(tail, LANES - IDX_DIM, 1)
    lane = lax.broadcasted_iota(I32, tail.shape, 1)
    kia_ref[0] = jnp.where(lane < IDX_DIM, tail, 0.0).astype(kia_ref.dtype)
    kib_ref[0] = jnp.where(lane >= LANES - IDX_DIM, swapped, 0.0).astype(kib_ref.dtype)
    wit_ref[0] = swapped.T[:IDX_HEADS, :]


def _proj(x, mod, g_mix, w_in):
    bsz, s, d = x.shape
    pool_w = d // 2
    attn_w = N_HEADS * HEAD_DIM
    kv_w = N_KV_HEADS * HEAD_DIM
    qi_w = IDX_HEADS * IDX_DIM
    c = np.cumsum([0, pool_w, attn_w, kv_w, kv_w, qi_w, IDX_DIM, IDX_HEADS])
    assert IDX_DIM + IDX_HEADS <= LANES and w_in.shape[1] == c[7]
    wp = w_in.astype(_MXU_DTYPE)
    cuts = {"u": (int(c[0]), int(c[1])), "q": (int(c[1]), int(c[2])), "k": (int(c[2]), int(c[3])),
            "v": (int(c[3]), int(c[4])), "qi": (int(c[4]), int(c[5])), "tail": (int(c[5]), int(c[7]))}
    tm = PROJ_TM
    pw = wp.shape[1]
    row = lambda w: pl.BlockSpec((1, tm, w), lambda b, i: (b, i, 0))
    out_shapes = [
        jax.ShapeDtypeStruct((bsz, s, pool_w), F32),
        jax.ShapeDtypeStruct((bsz, s, attn_w), _MXU_DTYPE),
        jax.ShapeDtypeStruct((bsz, s, kv_w), _MXU_DTYPE),
        jax.ShapeDtypeStruct((bsz, kv_w, s), _MXU_DTYPE),
        jax.ShapeDtypeStruct((bsz, s, qi_w), _MXU_DTYPE),
        jax.ShapeDtypeStruct((bsz, s, LANES), _MXU_DTYPE),
        jax.ShapeDtypeStruct((bsz, s, LANES), _MXU_DTYPE),
        jax.ShapeDtypeStruct((bsz, IDX_HEADS, s), F32),
    ]
    out_specs = [row(pool_w), row(attn_w), row(kv_w),
                 pl.BlockSpec((1, kv_w, tm), lambda b, i: (b, 0, i)),
                 row(qi_w), row(LANES), row(LANES),
                 pl.BlockSpec((1, IDX_HEADS, tm), lambda b, i: (b, 0, i))]
    return pl.pallas_call(
        functools.partial(_proj_body, cuts=cuts),
        grid=(bsz, s // tm),
        in_specs=[pl.BlockSpec((1, tm, d), lambda b, i: (b, i, 0)),
                  pl.BlockSpec((1, 6, d), lambda b, i: (b, 0, 0)),
                  pl.BlockSpec((1, d), lambda b, i: (0, 0)),
                  pl.BlockSpec((d, pw), lambda b, i: (0, 0))],
        out_specs=out_specs,
        out_shape=out_shapes,
        compiler_params=_cparams(("arbitrary", "arbitrary")),
        name="proj",
    )(x, mod, g_mix, wp)


def _bucket_starts():
    max_exact = N_BUCKETS // 2
    d = np.arange(1, 4 * MAX_DISTANCE, dtype=np.float32)
    large = max_exact + (np.log(d / np.float32(max_exact)) / np.float32(math.log(MAX_DISTANCE / max_exact))
                         * np.float32(N_BUCKETS - max_exact)).astype(np.int32)
    large = np.minimum(large, N_BUCKETS - 1)
    bucket = np.where(d < max_exact, d.astype(np.int32), large)
    bucket = np.concatenate([[0], bucket])
    starts = [int(np.argmax(bucket >= b)) for b in range(N_BUCKETS)]
    assert all(np.all(bucket[starts[b]:] >= b) for b in range(N_BUCKETS))
    assert starts[N_BUCKETS - 1] <= MAX_DISTANCE
    return starts


def _bias_body(rb_ref, o_ref, *, t, starts):
    diff = pl.program_id(0)
    s_l = lax.broadcasted_iota(I32, (t, t), 0)
    t_l = lax.broadcasted_iota(I32, (t, t), 1)
    dist = diff * t + t_l - s_l
    for h in range(N_HEADS):
        val = jnp.full((t, t), rb_ref[0, h], F32)
        for b in range(1, N_BUCKETS):
            val = jnp.where(dist >= starts[b], rb_ref[b, h], val)
        hh = h % HEAD_GROUP
        o_ref[0, h // HEAD_GROUP, :, hh * t:(hh + 1) * t] = val * LOG2E


def _bias_tiles(rel_bias, t):
    assert t >= MAX_DISTANCE
    return pl.pallas_call(
        functools.partial(_bias_body, t=t, starts=_bucket_starts()),
        grid=(3,),
        in_specs=[pl.BlockSpec(memory_space=pltpu.SMEM)],
        out_specs=pl.BlockSpec((1, N_KV_HEADS, t, HEAD_GROUP * t), lambda i: (i, 0, 0, 0)),
        out_shape=jax.ShapeDtypeStruct((3, N_KV_HEADS, t, HEAD_GROUP * t), F32),
        compiler_params=_cparams(("arbitrary",)),
        name="bias_tiles",
    )(rel_bias)


def _dsa_body(q_ref, qi_ref, wit_ref, kia_ref, kib_ref, k_ref, vt_ref, bias_ref, o_ref,
              score_ref, mask_ref, acc_ref, *, t, topk):
    i = pl.program_id(1)
    nk = i + 1
    t_glob = i * t + lax.broadcasted_iota(I32, (1, t), 1)
    idx_scale = (IDX_DIM ** -0.5) * (IDX_HEADS ** -0.5)

    def score_chunk(c, carry):
        r0 = pl.multiple_of(c * SCORE_SUB, SCORE_SUB)
        ka = kia_ref[0, pl.ds(r0, SCORE_SUB), :]
        kb = kib_ref[0, pl.ds(r0, SCORE_SUB), :]
        acc = jnp.zeros((SCORE_SUB, t), F32)
        for j in range(IDX_HEADS // 2):
            qp = qi_ref[0, :, j * LANES:(j + 1) * LANES]
            da = _dot_nt(ka, qp)
            db = _dot_nt(kb, qp)
            acc = acc + jnp.maximum(da, 0.0) * wit_ref[0, 2 * j:2 * j + 1, :]
            acc = acc + jnp.maximum(db, 0.0) * wit_ref[0, 2 * j + 1:2 * j + 2, :]
        sc = acc * idx_scale
        s_glob = r0 + lax.broadcasted_iota(I32, (SCORE_SUB, 1), 0)
        score_ref[pl.ds(r0, SCORE_SUB), :] = jnp.where(s_glob <= t_glob, sc, NEG_INF)
        return carry

    lax.fori_loop(0, nk * (t // SCORE_SUB), score_chunk, 0)

    nblk = nk * (t // COUNT_RB)
    fold = lambda m: jnp.sum(m.reshape(COUNT_RB // SUBLANES, SUBLANES, t), axis=0)

    def blk(r):
        return score_ref[pl.ds(pl.multiple_of(r * COUNT_RB, COUNT_RB), COUNT_RB), :]

    def count_ge(c):
        def body(r, acc):
            return acc + fold((blk(r) >= c).astype(I32))
        acc = lax.fori_loop(0, nblk, body, jnp.zeros((SUBLANES, t), I32))
        return jnp.sum(acc, axis=0, keepdims=True)

    def stats_body(r, carry):
        cnt, mn, mx = carry
        b = blk(r)
        real = b > NEG_INF
        cnt = cnt + fold(real.astype(I32))
        mn = jnp.minimum(mn, jnp.min(jnp.where(real, b, BIG).reshape(COUNT_RB // SUBLANES, SUBLANES, t), axis=0))
        mx = jnp.maximum(mx, jnp.max(b.reshape(COUNT_RB // SUBLANES, SUBLANES, t), axis=0))
        return cnt, mn, mx

    cnt8, mn8, mx8 = lax.fori_loop(
        0, nblk, stats_body,
        (jnp.zeros((SUBLANES, t), I32), jnp.full((SUBLANES, t), BIG, F32), jnp.full((SUBLANES, t), -BIG, F32)))
    n_real = jnp.sum(cnt8, axis=0, keepdims=True)
    rmin = jnp.min(mn8, axis=0, keepdims=True)
    rmax = jnp.max(mx8, axis=0, keepdims=True)
    n_max = count_ge(rmax)
    degenerate = jnp.logical_and(n_real >= topk, n_max >= topk)
    lo0 = jnp.where(n_real < topk, NEG_INF, jnp.where(degenerate, rmax, rmin))
    hi0 = jnp.where(degenerate, BIG, rmax)
    done0 = jnp.where(n_real <= topk, 1, 0)

    def bisect(st):
        lo, hi, done, stuck = st
        c = lo + (hi - lo) * 0.5
        active = (done + stuck) == 0
        has_mid = jnp.logical_and(c > lo, c < hi)
        n = count_ge(c)
        upd = jnp.logical_and(active, has_mid)
        lo = jnp.where(jnp.logical_and(upd, n >= topk), c, lo)
        hi = jnp.where(jnp.logical_and(upd, n < topk), c, hi)
        done = jnp.where(jnp.logical_and(upd, n == topk), 1, done)
        stuck = jnp.where(jnp.logical_and(active, jnp.logical_not(has_mid)), 1, stuck)
        return lo, hi, done, stuck

    st = lax.fori_loop(0, SEL_FIXED_ITERS, lambda _, s: bisect(s), (lo0, hi0, done0, jnp.zeros((1, t), I32)))

    def sel_cond(s):
        return jnp.logical_and(s[0] < SEL_MAX_ITERS, jnp.min(s[3] + s[4]) == 0)

    _, lo, hi, done, _ = lax.while_loop(sel_cond, lambda s: (s[0] + 1,) + bisect(s[1:]), (jnp.int32(0),) + st)

    def write_mask(r, cut):
        r0 = pl.multiple_of(r * COUNT_RB, COUNT_RB)
        b = score_ref[pl.ds(r0, COUNT_RB), :]
        s_glob = r0 + lax.broadcasted_iota(I32, (COUNT_RB, 1), 0)
        sel = jnp.logical_and(b >= lo, s_glob <= t_glob)
        if cut is not None:
            sel = jnp.logical_and(sel, jnp.logical_or(b >= hi, s_glob < cut))
        mask_ref[pl.ds(r0, COUNT_RB), :] = jnp.concatenate([jnp.where(sel, 0.0, NEG_INF)] * HEAD_GROUP, axis=1)

    any_tied = jnp.min(done) == 0

    @pl.when(jnp.logical_not(any_tied))
    def _():
        def body(r, carry):
            write_mask(r, None)
            return carry
        lax.fori_loop(0, nblk, body, 0)

    @pl.when(any_tied)
    def _():
        need = topk - count_ge(hi)

        def count_tie_below(x):
            def body(r, acc):
                r0 = pl.multiple_of(r * COUNT_RB, COUNT_RB)
                b = score_ref[pl.ds(r0, COUNT_RB), :]
                s_glob = r0 + lax.broadcasted_iota(I32, (COUNT_RB, 1), 0)
                m = jnp.logical_and(jnp.logical_and(b >= lo, b < hi), s_glob < x)
                return acc + fold(m.astype(I32))
            acc = lax.fori_loop(0, nblk, body, jnp.zeros((SUBLANES, t), I32))
            return jnp.sum(acc, axis=0, keepdims=True)

        nbits = int(score_ref.shape[0]).bit_length()

        def bit_body(bi, x):
            c = x + jnp.left_shift(jnp.int32(1), nbits - 1 - bi)
            return jnp.where(count_tie_below(c) < need, c, x)

        x = lax.fori_loop(0, nbits, bit_body, jnp.zeros((1, t), I32))
        cut = jnp.where(done > 0, jnp.int32(2 ** 30), x + 1)

        def body(r, carry):
            write_mask(r, cut)
            return carry
        lax.fori_loop(0, nblk, body, 0)

    scale = HEAD_DIM ** -0.5
    gt = HEAD_GROUP * t

    def col_reduce(op, a):
        part = op(a.reshape(a.shape[0] // SUBLANES, SUBLANES, a.shape[1]), axis=0)
        return op(part, axis=0, keepdims=True)

    qgs = [jnp.concatenate(
        [q_ref[0, :, h * HEAD_DIM:(h + 1) * HEAD_DIM] for h in range(g * HEAD_GROUP, (g + 1) * HEAD_GROUP)],
        axis=0) for g in range(N_KV_HEADS)]
    acc_ref[...] = jnp.zeros_like(acc_ref)

    def att_chunk(kc, carry):
        r0 = pl.multiple_of(kc * t, t)
        mk = mask_ref[pl.ds(r0, t), :]
        new = []
        for g in range(N_KV_HEADS):
            m, l = carry[g]
            kk = k_ref[0, pl.ds(r0, t), g * HEAD_DIM:(g + 1) * HEAD_DIM]
            s = _dot_nt(kk, qgs[g]) * (scale * LOG2E)
            s = s + bias_ref[jnp.minimum(i - kc, 2), g] + mk
            m_new = jnp.maximum(m, col_reduce(jnp.max, s))
            alpha = jnp.exp2(m - m_new)
            p = jnp.exp2(s - m_new)
            l = alpha * l + col_reduce(jnp.sum, p)
            vt = vt_ref[0, g * HEAD_DIM:(g + 1) * HEAD_DIM, pl.ds(r0, t)]
            acc_ref[g] = alpha * acc_ref[g] + _dot(vt, p.astype(_MXU_DTYPE))
            new.append((m_new, l))
        return tuple(new)

    init = (jnp.full((1, gt), NEG_INF, F32), jnp.zeros((1, gt), F32))
    fin = lax.fori_loop(0, nk, att_chunk, (init,) * N_KV_HEADS)
    for g in range(N_KV_HEADS):
        out = acc_ref[g] / fin[g][1]
        for hh in range(HEAD_GROUP):
            h = g * HEAD_GROUP + hh
            o_ref[0, :, h * HEAD_DIM:(h + 1) * HEAD_DIM] = out[:, hh * t:(hh + 1) * t].T.astype(o_ref.dtype)


def _dsa(q, qi, wit, kia, kib, k, vt, bias, topk):
    bsz, s, aw = q.shape
    t = ATT_T
    assert s % t == 0 and topk <= t
    row = lambda w: pl.BlockSpec((1, t, w), lambda b, i: (b, i, 0))
    full = lambda a: pl.BlockSpec((1,) + a.shape[1:], lambda b, i: (b, 0, 0))
    return pl.pallas_call(
        functools.partial(_dsa_body, t=t, topk=topk),
        grid=(bsz, s // t),
        in_specs=[row(aw), row(qi.shape[2]),
                  pl.BlockSpec((1, IDX_HEADS, t), lambda b, i: (b, 0, i)),
                  full(kia), full(kib), full(k), full(vt),
                  pl.BlockSpec(bias.shape, lambda b, i: (0, 0, 0, 0))],
        out_specs=row(aw),
        out_shape=jax.ShapeDtypeStruct((bsz, s, aw), _MXU_DTYPE),
        scratch_shapes=[pltpu.VMEM((s, t), F32), pltpu.VMEM((s, HEAD_GROUP * t), F32),
                        pltpu.VMEM((N_KV_HEADS, HEAD_DIM, HEAD_GROUP * t), F32)],
        compiler_params=_cparams(("arbitrary", "arbitrary")),
        name="dsa",
    )(q, qi, wit, kia, kib, k, vt, bias)


def _pack_halves(hb):
    assert jnp.dtype(hb.dtype).itemsize == 2
    n = hb.shape[1] // 2
    hi = lax.bitcast_convert_type(hb[:, :n].astype(F32), I32)
    lo = lax.bitcast_convert_type(hb[:, n:].astype(F32), I32)
    return jnp.bitwise_or(hi, lax.shift_right_logical(lo, jnp.full(lo.shape, 16, I32)))


def _unpack_halves(w, dtype):
    hi = lax.bitcast_convert_type(jnp.bitwise_and(w, jnp.int32(-65536)), F32).astype(dtype)
    lo = lax.bitcast_convert_type(jnp.left_shift(w, 16), F32).astype(dtype)
    return jnp.concatenate([hi, lo], axis=1)


def _mix_body(u_ref, halo_ref, a_ref, x_ref, mod_ref, wpool_ref, ps_ref, wout_ref, g_ref, wr_ref, br_ref,
              x1_ref, h2_ref, lg_ref, *, seq, tm):
    t0 = (pl.program_id(0) * tm) % seq
    u = u_ref[...]
    halo = jnp.where(t0 == 0, 0.0, halo_ref[...])
    ext = jnp.concatenate([halo, u], axis=0)
    pos = t0 + lax.broadcasted_iota(I32, (tm, 1), 0)
    gw = u.shape[1] // N_POOL_GROUPS
    ys = []
    for g, w in enumerate(POOL_WINDOWS):
        a = ext[:, g * gw:(g + 1) * gw]
        sft = 1
        while sft < w:
            a = a + pltpu.roll(a, sft, 0)
            sft *= 2
        cnt = jnp.minimum(pos + 1, w).astype(F32)
        p = a[HALO:, :] / cnt - u[:, g * gw:(g + 1) * gw]
        ys.append(_dot(p.astype(_MXU_DTYPE), wpool_ref[g]) * ps_ref[:, g * gw:(g + 1) * gw])
    pool = jnp.concatenate(ys, axis=1).astype(_MXU_DTYPE)
    pw = pool.shape[1]
    mixed = _dot(pool, wout_ref[:pw, :]) + _dot(a_ref[...], wout_ref[pw:, :])
    x1 = x_ref[...] + mod_ref[0, 2:3, :] * mixed
    x1_ref[...] = x1
    h2 = _rms_mod(x1, g_ref[...], mod_ref[0, 3:4, :], mod_ref[0, 4:5, :]).astype(_MXU_DTYPE)
    h2_ref[...] = _pack_halves(h2)
    lg_ref[...] = _dot(h2, wr_ref[...]) + br_ref[...]


def _mix(u, attn, x, mod, w_pool, pool_scale, w_out, g_ffn, w_r, b_r, seq):
    n_tok, d = x.shape
    tm = MIX_TM
    assert seq % tm == 0 and tm % HALO == 0
    pw = u.shape[1]
    aw = attn.shape[1]
    rows = lambda w: pl.BlockSpec((tm, w), lambda i: (i, 0))
    const = lambda a: pl.BlockSpec(a.shape, lambda i: (0,) * a.ndim)
    return pl.pallas_call(
        functools.partial(_mix_body, seq=seq, tm=tm),
        grid=(n_tok // tm,),
        in_specs=[rows(pw),
                  pl.BlockSpec((HALO, pw), lambda i: (jnp.maximum(i * (tm // HALO) - 1, 0), 0)),
                  rows(aw), rows(d),
                  pl.BlockSpec((1, 6, d), lambda i: ((i * tm) // seq, 0, 0)),
                  const(w_pool), const(pool_scale), const(w_out), const(g_ffn), const(w_r), const(b_r)],
        out_specs=[rows(d), rows(d // 2), rows(LANES)],
        out_shape=[jax.ShapeDtypeStruct((n_tok, d), F32), jax.ShapeDtypeStruct((n_tok, d // 2), I32),
                   jax.ShapeDtypeStruct((n_tok, LANES), F32)],
        compiler_params=_cparams(("arbitrary",)),
        name="mix",
    )(u, u, attn, x, mod, w_pool, pool_scale, w_out, g_ffn, w_r, b_r)


def _route_body(lg_ref, rt_ref, gate_ref, cnt_ref, run_ref):
    @pl.when(pl.program_id(0) == 0)
    def _():
        run_ref[...] = jnp.zeros_like(run_ref)

    lg = lg_ref[...]
    tr = lg.shape[0]
    lane = lax.broadcasted_iota(I32, lg.shape, 1)
    gmask = lane < N_GROUPS
    gl = jnp.where(gmask, lg, -BIG)
    gmax = jnp.max(gl, axis=-1, keepdims=True)
    gsel = jnp.min(jnp.where(jnp.logical_and(gmask, gl == gmax), lane, LANES), axis=-1, keepdims=True)
    gsum = jnp.sum(jnp.where(gmask, jnp.exp(gl - gmax), 0.0), axis=-1, keepdims=True)
    p_g = 1.0 / gsum
    in_e = jnp.logical_and(lane >= N_GROUPS, lane < N_GROUPS + N_EXPERTS)
    emask = jnp.logical_and(in_e, lax.shift_right_arithmetic(lane - N_GROUPS, EXPERTS_PER_GROUP.bit_length() - 1) == gsel)
    el = jnp.where(emask, lg, -BIG)
    emax = jnp.max(el, axis=-1, keepdims=True)
    ex = jnp.where(emask, jnp.exp(el - emax), 0.0)
    ep = ex / jnp.sum(ex, axis=-1, keepdims=True)
    p1 = jnp.max(jnp.where(emask, ep, -1.0), axis=-1, keepdims=True)
    i1 = jnp.min(jnp.where(jnp.logical_and(emask, ep == p1), lane, LANES), axis=-1, keepdims=True)
    m2 = jnp.logical_and(emask, lane != i1)
    p2 = jnp.max(jnp.where(m2, ep, -1.0), axis=-1, keepdims=True)
    i2 = jnp.min(jnp.where(jnp.logical_and(m2, ep == p2), lane, LANES), axis=-1, keepdims=True)
    den = p1 + p2
    g1 = p_g * p1 / den
    g2 = p_g * p2 / den
    e1 = i1 - N_GROUPS
    e2 = i2 - N_GROUPS

    oh = jnp.logical_or(lane == e1, lane == e2)
    ri = lax.broadcasted_iota(I32, (tr, tr), 0)
    ci = lax.broadcasted_iota(I32, (tr, tr), 1)
    ltri = (ci < ri).astype(_MXU_DTYPE)
    before = _dot(ltri, oh.astype(_MXU_DTYPE)) + run_ref[...]
    r1 = jnp.sum(jnp.where(lane == e1, before, 0.0), axis=-1, keepdims=True)
    r2 = jnp.sum(jnp.where(lane == e2, before, 0.0), axis=-1, keepdims=True)
    run = run_ref[...] + jnp.sum(oh.astype(F32), axis=0, keepdims=True)
    run_ref[...] = run
    cnt_ref[...] = jnp.broadcast_to(run, cnt_ref.shape)

    route = jnp.where(lane == 0, e1.astype(F32),
                      jnp.where(lane == 1, e2.astype(F32),
                                jnp.where(lane == 2, r1, jnp.where(lane == 3, r2, 0.0))))
    rt_ref[...] = route.T[:SUBLANES, :].astype(I32)
    gate_ref[...] = jnp.where(lane == 0, g1, jnp.where(lane == 1, g2, 0.0))


def _route(logits):
    n_tok = logits.shape[0]
    tr = min(ROUTE_TR, n_tok)
    return pl.pallas_call(
        _route_body,
        grid=(n_tok // tr,),
        in_specs=[pl.BlockSpec((tr, LANES), lambda i: (i, 0))],
        out_specs=[pl.BlockSpec((SUBLANES, tr), lambda i: (0, i)),
                   pl.BlockSpec((tr, LANES), lambda i: (i, 0)),
                   pl.BlockSpec((SUBLANES, LANES), lambda i: (0, 0))],
        out_shape=[jax.ShapeDtypeStruct((SUBLANES, n_tok), I32),
                   jax.ShapeDtypeStruct((n_tok, LANES), F32),
                   jax.ShapeDtypeStruct((SUBLANES, LANES), F32)],
        scratch_shapes=[pltpu.VMEM((1, LANES), F32)],
        compiler_params=_cparams(("arbitrary",)),
        name="route",
    )(logits)


def _dest_body(ps_ref, rt_ref, o_ref):
    r = rt_ref[...]
    start = jnp.zeros_like(r)
    for e in range(N_EXPERTS):
        start = jnp.where(r == e, ps_ref[e], start)
    o_ref[...] = start + pltpu.roll(r, SUBLANES - 2, 0)


def _dest(route_t, pstart):
    n_tok = route_t.shape[1]
    tb = min(n_tok, 2048)
    return pl.pallas_call(
        _dest_body,
        grid_spec=pltpu.PrefetchScalarGridSpec(
            num_scalar_prefetch=1,
            grid=(n_tok // tb,),
            in_specs=[pl.BlockSpec((SUBLANES, tb), lambda i, ps: (0, i))],
            out_specs=pl.BlockSpec((SUBLANES, tb), lambda i, ps: (0, i)),
        ),
        out_shape=jax.ShapeDtypeStruct((SUBLANES, n_tok), I32),
        compiler_params=_cparams(("arbitrary",)),
        name="dest",
    )(pstart, route_t)


def _row_copy(src_ref, src_row, dst_ref, dst_row, sem):
    return pltpu.make_async_copy(src_ref.at[pl.ds(src_row, 1), :], dst_ref.at[pl.ds(dst_row, 1), :], sem)


def _dispatch_body(pstart_ref, cnt_ref, nu_ref, h_ref, d0_ref, d1_ref, xs_ref, zero_ref, sem, *, td, bm):
    def issue(c, carry):
        for u in range(SUBLANES):
            r = c * SUBLANES + u
            for j, d_ref in enumerate((d0_ref, d1_ref)):
                pltpu.make_async_copy(h_ref.at[c, pl.ds(u, 1), :], xs_ref.at[pl.ds(d_ref[r], 1), :],
                                      sem).start(priority=j)
        return carry

    lax.fori_loop(0, td // SUBLANES, issue, 0)

    for _ in range(2):
        pltpu.make_async_copy(xs_ref.at[pl.ds(0, td), :], xs_ref.at[pl.ds(0, td), :], sem).wait()

    @pl.when(pl.program_id(0) == pl.num_programs(0) - 1)
    def _():
        zero_ref[...] = jnp.zeros_like(zero_ref)

        def per_expert(e, carry):
            c = cnt_ref[e]
            n_pad = (bm - c % bm) % bm
            base = pstart_ref[e] + c

            def start(r, cc):
                _row_copy(zero_ref, 0, xs_ref, base + r, sem).start()
                return cc

            def wait(r, cc):
                _row_copy(zero_ref, 0, xs_ref, 0, sem).wait()
                return cc

            lax.fori_loop(0, n_pad, start, 0)
            lax.fori_loop(0, n_pad, wait, 0)
            return carry

        lax.fori_loop(0, N_EXPERTS, per_expert, 0)

        first = nu_ref[0] * (bm // SUBLANES)
        n_tail = xs_ref.shape[0] // SUBLANES - first

        def tail_copy(r):
            return pltpu.make_async_copy(
                zero_ref, xs_ref.at[pl.ds(pl.multiple_of((first + r) * SUBLANES, SUBLANES), SUBLANES), :], sem)

        def tail_start(r, cc):
            tail_copy(r).start()
            return cc

        def tail_wait(r, cc):
            tail_copy(0).wait()
            return cc

        lax.fori_loop(0, n_tail, tail_start, 0)
        lax.fori_loop(0, n_tail, tail_wait, 0)


def _dispatch(h2, dest0, dest1, pstart, counts, n_used, n_slots):
    n_tok, d = h2.shape
    td = DISP_TD
    tok = pl.BlockSpec((td,), lambda i, ps, cn, nu: (i,), memory_space=pltpu.SMEM)
    return pl.pallas_call(
        functools.partial(_dispatch_body, td=td, bm=MOE_BM),
        grid_spec=pltpu.PrefetchScalarGridSpec(
            num_scalar_prefetch=3,
            grid=(n_tok // td,),
            in_specs=[pl.BlockSpec((td // SUBLANES, SUBLANES, d), lambda i, ps, cn, nu: (i, 0, 0)), tok, tok],
            out_specs=pl.BlockSpec(memory_space=pl.ANY),
            scratch_shapes=[pltpu.VMEM((SUBLANES, d), h2.dtype), pltpu.SemaphoreType.DMA],
        ),
        out_shape=jax.ShapeDtypeStruct((n_slots, d), h2.dtype),
        compiler_params=_cparams(("arbitrary",)),
        name="dispatch",
    )(pstart, counts, n_used, h2.reshape(n_tok // SUBLANES, SUBLANES, d), dest0, dest1)


def _experts_body(slot_ref, cast_ref, pre_ref, misc_ref, x_ref, wg_hbm, wu_hbm, wd_hbm, y_ref,
                  sg, su, sd, wgb, wub, wdb, sem):
    blk = pl.program_id(0)
    e0, pre0, n_used = misc_ref[0], misc_ref[1], misc_ref[2]

    def copies(e):
        return (pltpu.make_async_copy(wg_hbm.at[e], sg, sem.at[0]),
                pltpu.make_async_copy(wu_hbm.at[e], su, sem.at[1]),
                pltpu.make_async_copy(wd_hbm.at[e], sd, sem.at[2]))

    def fetch(e):
        for c in copies(e):
            c.start(priority=1)

    def land(slot):
        for c in copies(0):
            c.wait()
        wgb[slot] = sg[...].astype(wgb.dtype)
        wub[slot] = su[...].astype(wub.dtype)
        wdb[slot] = sd[...].astype(wdb.dtype)

    @pl.when(blk == 0)
    def _():
        fetch(e0)
        land(0)

        @pl.when(pre0 >= 0)
        def _():
            fetch(pre0)

    @pl.when(blk < n_used)
    def _():
        slot = slot_ref[blk]
        xb = _unpack_halves(x_ref[...], _MXU_DTYPE)
        a = _silu(_dot(xb, wgb[slot])) * _dot(xb, wub[slot])
        y_ref[...] = _dot(a.astype(_MXU_DTYPE), wdb[slot])

        @pl.when(cast_ref[blk] == 1)
        def _():
            land(1 - slot)

            @pl.when(pre_ref[blk] >= 0)
            def _():
                fetch(pre_ref[blk])

    @pl.when(blk >= n_used)
    def _():
        y_ref[...] = jnp.zeros_like(y_ref)


def _experts(xs, n_used, pends, w_gate, w_up, w_down):
    n_slots, dpk = xs.shape
    bm = MOE_BM
    n_blocks = n_slots // bm
    d, de = w_gate.shape[1:]
    last_used = n_used[0] - 1
    total = pends[-1]
    eid = jnp.arange(N_EXPERTS, dtype=I32)[None, :]
    owner = lambda start: jnp.minimum(jnp.sum((pends[None, :] <= start[:, None]).astype(I32), axis=1), N_EXPERTS - 1)
    end_of = lambda e: jnp.sum(jnp.where(eid == e[:, None], pends[None, :], 0), axis=1)
    be = owner(jnp.minimum(jnp.arange(n_blocks, dtype=I32), last_used) * bm)
    end_e = end_of(be)
    has_next = end_e < total
    nxt = owner(end_e)
    end_n = end_of(nxt)
    nxt2 = jnp.where(jnp.logical_and(has_next, end_n < total), owner(end_n), -1)
    is_last = jnp.concatenate([be[1:] != be[:-1], jnp.ones((1,), jnp.bool_)])
    is_last = jnp.logical_or(is_last, jnp.arange(n_blocks) >= last_used)
    cast = jnp.logical_and(is_last, has_next).astype(I32)
    pre = jnp.where(cast == 1, nxt2, -1).astype(I32)
    is_first = jnp.concatenate([jnp.ones((1,), jnp.bool_), be[1:] != be[:-1]])
    slot = ((jnp.cumsum(is_first.astype(I32)) - 1) % 2).astype(I32)
    misc = jnp.stack([be[0], jnp.where(has_next[0], nxt[0], -1), n_used[0]]).astype(I32)
    hbm = pl.BlockSpec(memory_space=pl.ANY)
    return pl.pallas_call(
        _experts_body,
        grid_spec=pltpu.PrefetchScalarGridSpec(
            num_scalar_prefetch=4,
            grid=(n_blocks,),
            in_specs=[pl.BlockSpec((bm, dpk), lambda j, sl, ca, pr, mi: (jnp.minimum(j, mi[2] - 1), 0)),
                      hbm, hbm, hbm],
            out_specs=pl.BlockSpec((bm, d), lambda j, sl, ca, pr, mi: (j, 0)),
            scratch_shapes=[pltpu.VMEM((d, de), w_gate.dtype), pltpu.VMEM((d, de), w_up.dtype),
                            pltpu.VMEM((de, d), w_down.dtype),
                            pltpu.VMEM((2, d, de), _MXU_DTYPE), pltpu.VMEM((2, d, de), _MXU_DTYPE),
                            pltpu.VMEM((2, de, d), _MXU_DTYPE), pltpu.SemaphoreType.DMA((3,))],
        ),
        out_shape=jax.ShapeDtypeStruct((n_slots, d), F32),
        compiler_params=_cparams(("arbitrary",)),
        name="experts",
    )(slot, cast, pre, misc, xs, w_gate, w_up, w_down)


def _combine_body(x1_ref, gate_ref, d0_ref, d1_ref, n0_ref, n1_ref, mod_ref, g_ref, ys_ref, o_ref, buf_ref, sem,
                  *, tf):
    i = pl.program_id(0)
    cur = i % 2

    def gather(da_ref, db_ref, half):
        def issue(c, carry):
            for u in range(SUBLANES):
                r = c * SUBLANES + u
                for j, d_ref in enumerate((da_ref, db_ref)):
                    pltpu.make_async_copy(ys_ref.at[pl.ds(d_ref[r], 1), :],
                                          buf_ref.at[half, j, c, pl.ds(u, 1), :], sem.at[half]).start(priority=j)
            return carry

        lax.fori_loop(0, tf // SUBLANES, issue, 0)

    @pl.when(i == 0)
    def _():
        gather(d0_ref, d1_ref, 0)

    @pl.when(i + 1 < pl.num_programs(0))
    def _():
        gather(n0_ref, n1_ref, 1 - cur)

    for _ in range(2):
        pltpu.make_async_copy(ys_ref.at[pl.ds(0, tf), :], ys_ref.at[pl.ds(0, tf), :], sem.at[cur]).wait()

    gates = gate_ref[...]
    d = o_ref.shape[1]
    moe = buf_ref[cur, 0].reshape(tf, d) * gates[:, 0:1] + buf_ref[cur, 1].reshape(tf, d) * gates[:, 1:2]
    x2 = x1_ref[...] + mod_ref[0, 5:6, :] * moe
    ms = jnp.mean(x2 * x2, axis=-1, keepdims=True)
    o_ref[...] = x2 * lax.rsqrt(ms + EPS) * g_ref[...]


def _combine(x1, gates, dest0, dest1, mod, g_final, ys, seq):
    n_tok, d = x1.shape
    tf = COMB_TF
    n_tiles = n_tok // tf
    tok = pl.BlockSpec((tf,), lambda i: (i,), memory_space=pltpu.SMEM)
    nxt = pl.BlockSpec((tf,), lambda i: (jnp.minimum(i + 1, n_tiles - 1),), memory_space=pltpu.SMEM)
    return pl.pallas_call(
        functools.partial(_combine_body, tf=tf),
        grid=(n_tiles,),
        in_specs=[pl.BlockSpec((tf, d), lambda i: (i, 0)),
                  pl.BlockSpec((tf, LANES), lambda i: (i, 0)),
                  tok, tok, nxt, nxt,
                  pl.BlockSpec((1, 6, d), lambda i: ((i * tf) // seq, 0, 0)),
                  pl.BlockSpec((1, d), lambda i: (0, 0)),
                  pl.BlockSpec(memory_space=pl.ANY)],
        out_specs=pl.BlockSpec((tf, d), lambda i: (i, 0)),
        scratch_shapes=[pltpu.VMEM((2, 2, tf // SUBLANES, SUBLANES, d), F32), pltpu.SemaphoreType.DMA((2,))],
        out_shape=jax.ShapeDtypeStruct((n_tok, d), F32),
        compiler_params=_cparams(("arbitrary",)),
        name="combine",
    )(x1, gates, dest0, dest1, dest0, dest1, mod, g_final, ys)


def kernel(x, c, w_ada, b_ada, g_mix, w_in, w_pool, pool_scale, rel_bias, w_out, g_ffn, w_group, b_group,
           w_router, b_router, w_gate, w_up, w_down, g_final):
    bsz, seq, d = x.shape
    n_tok = bsz * seq
    depth = w_ada.shape[0]
    assert depth == 1, "the final norm is fused into the only layer's combine"
    topk = min(TOPK_MAX, seq // 4)
    bias = _bias_tiles(rel_bias, ATT_T)
    bm = MOE_BM
    n_blocks = -(-(n_tok * 2) // bm) + N_EXPERTS
    n_slots = n_blocks * bm
    xt = x.reshape(n_tok, d)
    for i in range(depth):
        mod = _ada(c, w_ada[i], b_ada[i][None, :]).reshape(bsz, 6, d)
        u, q, k, vt, qi, kia, kib, wit = _proj(xt.reshape(bsz, seq, d), mod, g_mix[i][None, :], w_in[i])
        attn = _dsa(q, qi, wit, kia, kib, k, vt, bias, topk)
        w_r = jnp.concatenate(
            [w_group[i], w_router[i], jnp.zeros((d, LANES - N_GROUPS - N_EXPERTS), F32)], axis=1).astype(_MXU_DTYPE)
        b_r = jnp.concatenate(
            [b_group[i], b_router[i], jnp.zeros((LANES - N_GROUPS - N_EXPERTS,), F32)])[None, :]
        x1, h2, logits = _mix(u.reshape(n_tok, -1), attn.reshape(n_tok, -1), xt, mod,
                              w_pool[i].astype(_MXU_DTYPE), pool_scale[i][None, :],
                              w_out[i].astype(_MXU_DTYPE), g_ffn[i][None, :], w_r, b_r, seq)
        route_t, gates, cnt = _route(logits)
        counts = cnt[0, :N_EXPERTS].astype(I32)
        padded = (counts + bm - 1) // bm * bm
        pends = jnp.cumsum(padded)
        pstart = pends - padded
        n_used = (pends[-1:] // bm).astype(I32)
        dest = _dest(route_t, pstart)
        dest0, dest1 = dest[0], dest[1]
        xs = _dispatch(h2, dest0, dest1, pstart, counts, n_used, n_slots)
        ys = _experts(xs, n_used, pends, w_gate[i], w_up[i], w_down[i])
        xt = _combine(x1, gates, dest0, dest1, mod, g_final[None, :], ys, seq)
    return xt.reshape(bsz, seq, d)
```

```python
import functools
import math

import numpy as np
import jax
import jax.numpy as jnp
from jax import lax
from jax.experimental import pallas as pl
from jax.experimental.pallas import tpu as pltpu

F32 = jnp.float32
I32 = jnp.int32
_MXU_DTYPE = jnp.bfloat16

POOL_WINDOWS = (2, 4, 8, 16)
N_POOL_GROUPS = 4
HEAD_DIM = 128
N_HEADS = 8
N_KV_HEADS = 2
HEAD_GROUP = N_HEADS // N_KV_HEADS
IDX_HEADS = 16
IDX_DIM = 64
TOPK_MAX = 256
N_BUCKETS = 32
MAX_DISTANCE = 128
N_GROUPS = 4
EXPERTS_PER_GROUP = 8
N_EXPERTS = N_GROUPS * EXPERTS_PER_GROUP
EPS = 1e-6
NEG_INF = -1e30
BIG = 3e38
LOG2E = 1.4426950408889634

LANES = 128
SUBLANES = 8
VMEM_LIMIT_BYTES = 56 * 1024 * 1024

ADA_TN = 1024
PROJ_TM = 256
ATT_T = 256
SCORE_SUB = 256
COUNT_RB = 256
MIX_TM = 256
HALO = 16
ROUTE_TR = 1024
DISP_TD = 256
MOE_BM = 256
COMB_TF = 256
DMA_UNROLL = 8
SEL_FIXED_ITERS = 18
SEL_MAX_ITERS = 80


def _cparams(sem):
    return pltpu.CompilerParams(dimension_semantics=sem, vmem_limit_bytes=VMEM_LIMIT_BYTES)


def _silu(x):
    return x * (1.0 / (1.0 + jnp.exp(-x)))


def _dot(a, b):
    return jnp.dot(a, b, preferred_element_type=F32)


def _dot_nt(a, b):
    return lax.dot_general(a, b, (((1,), (1,)), ((), ())), preferred_element_type=F32)


def _ada_body(c_ref, w_ref, b_ref, o_ref):
    s = _silu(c_ref[...])
    o_ref[...] = _dot(s.astype(_MXU_DTYPE), w_ref[...].astype(_MXU_DTYPE)) + b_ref[...]


def _ada(c, w, b):
    bsz, d = c.shape
    n = w.shape[1]
    return pl.pallas_call(
        _ada_body,
        grid=(n // ADA_TN,),
        in_specs=[pl.BlockSpec((bsz, d), lambda j: (0, 0)),
                  pl.BlockSpec((d, ADA_TN), lambda j: (0, j)),
                  pl.BlockSpec((1, ADA_TN), lambda j: (0, j))],
        out_specs=pl.BlockSpec((bsz, ADA_TN), lambda j: (0, j)),
        out_shape=jax.ShapeDtypeStruct((bsz, n), F32),
        compiler_params=_cparams(("arbitrary",)),
        name="ada",
    )(c, w, b)


def _rms_mod(x, g, shift, scale):
    ms = jnp.mean(x * x, axis=-1, keepdims=True)
    y = x * lax.rsqrt(ms + EPS) * g
    return y * (1.0 + scale) + shift


def _proj_body(x_ref, mod_ref, g_ref, w_ref, u_ref, q_ref, k_ref, vt_ref, qi_ref, kia_ref, kib_ref,
               wit_ref, *, cuts):
    h = _rms_mod(x_ref[0], g_ref[...], mod_ref[0, 0:1, :], mod_ref[0, 1:2, :])
    hb = h.astype(_MXU_DTYPE)

    def seg(name):
        lo, hi = cuts[name]
        return _dot(hb, w_ref[:, lo:hi])

    u_ref[0] = seg("u")
    q_ref[0] = seg("q").astype(q_ref.dtype)
    k_ref[0] = seg("k").astype(k_ref.dtype)
    vt_ref[0] = seg("v").T.astype(vt_ref.dtype)
    qi_ref[0] = seg("qi").astype(qi_ref.dtype)
    tail = seg("tail")
    tail = jnp.concatenate([tail, jnp.zeros((tail.shape[0], LANES - tail.shape[1]), F32)], axis=1)
    swapped = pltpu.roll(tail, LANES - IDX_DIM, 1)
    lane = lax.broadcasted_iota(I32, tail.shape, 1)
    kia_ref[0] = jnp.where(lane < IDX_DIM, tail, 0.0).astype(kia_ref.dtype)
    kib_ref[0] = jnp.where(lane >= LANES - IDX_DIM, swapped, 0.0).astype(kib_ref.dtype)
    wit_ref[0] = swapped.T[:IDX_HEADS, :]


def _proj(x, mod, g_mix, w_in):
    bsz, s, d = x.shape
    pool_w = d // 2
    attn_w = N_HEADS * HEAD_DIM
    kv_w = N_KV_HEADS * HEAD_DIM
    qi_w = IDX_HEADS * IDX_DIM
    c = np.cumsum([0, pool_w, attn_w, kv_w, kv_w, qi_w, IDX_DIM, IDX_HEADS])
    assert IDX_DIM + IDX_HEADS <= LANES and w_in.shape[1] == c[7]
    wp = w_in.astype(_MXU_DTYPE)
    cuts = {"u": (int(c[0]), int(c[1])), "q": (int(c[1]), int(c[2])), "k": (int(c[2]), int(c[3])),
            "v": (int(c[3]), int(c[4])), "qi": (int(c[4]), int(c[5])), "tail": (int(c[5]), int(c[7]))}
    tm = PROJ_TM
    pw = wp.shape[1]
    row = lambda w: pl.BlockSpec((1, tm, w), lambda b, i: (b, i, 0))
    out_shapes = [
        jax.ShapeDtypeStruct((bsz, s, pool_w), F32),
        jax.ShapeDtypeStruct((bsz, s, attn_w), _MXU_DTYPE),
        jax.ShapeDtypeStruct((bsz, s, kv_w), _MXU_DTYPE),
        jax.ShapeDtypeStruct((bsz, kv_w, s), _MXU_DTYPE),
        jax.ShapeDtypeStruct((bsz, s, qi_w), _MXU_DTYPE),
        jax.ShapeDtypeStruct((bsz, s, LANES), _MXU_DTYPE),
        jax.ShapeDtypeStruct((bsz, s, LANES), _MXU_DTYPE),
        jax.ShapeDtypeStruct((bsz, IDX_HEADS, s), F32),
    ]
    out_specs = [row(pool_w), row(attn_w), row(kv_w),
                 pl.BlockSpec((1, kv_w, tm), lambda b, i: (b, 0, i)),
                 row(qi_w), row(LANES), row(LANES),
                 pl.BlockSpec((1, IDX_HEADS, tm), lambda b, i: (b, 0, i))]
    return pl.pallas_call(
        functools.partial(_proj_body, cuts=cuts),
        grid=(bsz, s // tm),
        in_specs=[pl.BlockSpec((1, tm, d), lambda b, i: (b, i, 0)),
                  pl.BlockSpec((1, 6, d), lambda b, i: (b, 0, 0)),
                  pl.BlockSpec((1, d), lambda b, i: (0, 0)),
                  pl.BlockSpec((d, pw), lambda b, i: (0, 0))],
        out_specs=out_specs,
        out_shape=out_shapes,
        compiler_params=_cparams(("arbitrary", "arbitrary")),
        name="proj",
    )(x, mod, g_mix, wp)


def _bucket_starts():
    max_exact = N_BUCKETS // 2
    d = np.arange(1, 4 * MAX_DISTANCE, dtype=np.float32)
    large = max_exact + (np.log(d / np.float32(max_exact)) / np.float32(math.log(MAX_DISTANCE / max_exact))
                         * np.float32(N_BUCKETS - max_exact)).astype(np.int32)
    large = np.minimum(large, N_BUCKETS - 1)
    bucket = np.where(d < max_exact, d.astype(np.int32), large)
    bucket = np.concatenate([[0], bucket])
    starts = [int(np.argmax(bucket >= b)) for b in range(N_BUCKETS)]
    assert all(np.all(bucket[starts[b]:] >= b) for b in range(N_BUCKETS))
    assert starts[N_BUCKETS - 1] <= MAX_DISTANCE
    return starts


def _bias_body(rb_ref, o_ref, *, t, starts):
    diff = pl.program_id(0)
    s_l = lax.broadcasted_iota(I32, (t, t), 0)
    t_l = lax.broadcasted_iota(I32, (t, t), 1)
    dist = diff * t + t_l - s_l
    for h in range(N_HEADS):
        val = jnp.full((t, t), rb_ref[0, h], F32)
        for b in range(1, N_BUCKETS):
            val = jnp.where(dist >= starts[b], rb_ref[b, h], val)
        hh = h % HEAD_GROUP
        o_ref[0, h // HEAD_GROUP, :, hh * t:(hh + 1) * t] = val * LOG2E


def _bias_tiles(rel_bias, t):
    assert t >= MAX_DISTANCE
    return pl.pallas_call(
        functools.partial(_bias_body, t=t, starts=_bucket_starts()),
        grid=(3,),
        in_specs=[pl.BlockSpec(memory_space=pltpu.SMEM)],
        out_specs=pl.BlockSpec((1, N_KV_HEADS, t, HEAD_GROUP * t), lambda i: (i, 0, 0, 0)),
        out_shape=jax.ShapeDtypeStruct((3, N_KV_HEADS, t, HEAD_GROUP * t), F32),
        compiler_params=_cparams(("arbitrary",)),
        name="bias_tiles",
    )(rel_bias)


def _dsa_body(q_ref, qi_ref, wit_ref, kia_ref, kib_ref, k_ref, vt_ref, bias_ref, o_ref,
              score_ref, mask_ref, acc_ref, *, t, topk):
    i = pl.program_id(1)
    nk = i + 1
    t_glob = i * t + lax.broadcasted_iota(I32, (1, t), 1)
    idx_scale = (IDX_DIM ** -0.5) * (IDX_HEADS ** -0.5)

    def score_chunk(c, carry):
        r0 = pl.multiple_of(c * SCORE_SUB, SCORE_SUB)
        ka = kia_ref[0, pl.ds(r0, SCORE_SUB), :]
        kb = kib_ref[0, pl.ds(r0, SCORE_SUB), :]
        acc = jnp.zeros((SCORE_SUB, t), F32)
        for j in range(IDX_HEADS // 2):
            qp = qi_ref[0, :, j * LANES:(j + 1) * LANES]
            da = _dot_nt(ka, qp)
            db = _dot_nt(kb, qp)
            acc = acc + jnp.maximum(da, 0.0) * wit_ref[0, 2 * j:2 * j + 1, :]
            acc = acc + jnp.maximum(db, 0.0) * wit_ref[0, 2 * j + 1:2 * j + 2, :]
        sc = acc * idx_scale
        s_glob = r0 + lax.broadcasted_iota(I32, (SCORE_SUB, 1), 0)
        score_ref[pl.ds(r0, SCORE_SUB), :] = jnp.where(s_glob <= t_glob, sc, NEG_INF)
        return carry

    lax.fori_loop(0, nk * (t // SCORE_SUB), score_chunk, 0)

    nblk = nk * (t // COUNT_RB)
    fold = lambda m: jnp.sum(m.reshape(COUNT_RB // SUBLANES, SUBLANES, t), axis=0)

    def blk(r):
        return score_ref[pl.ds(pl.multiple_of(r * COUNT_RB, COUNT_RB), COUNT_RB), :]

    def count_ge(c):
        def body(r, acc):
            return acc + fold((blk(r) >= c).astype(I32))
        acc = lax.fori_loop(0, nblk, body, jnp.zeros((SUBLANES, t), I32))
        return jnp.sum(acc, axis=0, keepdims=True)

    def stats_body(r, carry):
        cnt, mn, mx = carry
        b = blk(r)
        real = b > NEG_INF
        cnt = cnt + fold(real.astype(I32))
        mn = jnp.minimum(mn, jnp.min(jnp.where(real, b, BIG).reshape(COUNT_RB // SUBLANES, SUBLANES, t), axis=0))
        mx = jnp.maximum(mx, jnp.max(b.reshape(COUNT_RB // SUBLANES, SUBLANES, t), axis=0))
        return cnt, mn, mx

    cnt8, mn8, mx8 = lax.fori_loop(
        0, nblk, stats_body,
        (jnp.zeros((SUBLANES, t), I32), jnp.full((SUBLANES, t), BIG, F32), jnp.full((SUBLANES, t), -BIG, F32)))
    n_real = jnp.sum(cnt8, axis=0, keepdims=True)
    rmin = jnp.min(mn8, axis=0, keepdims=True)
    rmax = jnp.max(mx8, axis=0, keepdims=True)
    n_max = count_ge(rmax)
    degenerate = jnp.logical_and(n_real >= topk, n_max >= topk)
    lo0 = jnp.where(n_real < topk, NEG_INF, jnp.where(degenerate, rmax, rmin))
    hi0 = jnp.where(degenerate, BIG, rmax)
    done0 = jnp.where(n_real <= topk, 1, 0)

    def bisect(st):
        lo, hi, done, stuck = st
        c = lo + (hi - lo) * 0.5
        active = (done + stuck) == 0
        has_mid = jnp.logical_and(c > lo, c < hi)
        n = count_ge(c)
        upd = jnp.logical_and(active, has_mid)
        lo = jnp.where(jnp.logical_and(upd, n >= topk), c, lo)
        hi = jnp.where(jnp.logical_and(upd, n < topk), c, hi)
        done = jnp.where(jnp.logical_and(upd, n == topk), 1, done)
        stuck = jnp.where(jnp.logical_and(active, jnp.logical_not(has_mid)), 1, stuck)
        return lo, hi, done, stuck

    st = lax.fori_loop(0, SEL_FIXED_ITERS, lambda _, s: bisect(s), (lo0, hi0, done0, jnp.zeros((1, t), I32)))

    def sel_cond(s):
        return jnp.logical_and(s[0] < SEL_MAX_ITERS, jnp.min(s[3] + s[4]) == 0)

    _, lo, hi, done, _ = lax.while_loop(sel_cond, lambda s: (s[0] + 1,) + bisect(s[1:]), (jnp.int32(0),) + st)

    def write_mask(r, cut):
        r0 = pl.multiple_of(r * COUNT_RB, COUNT_RB)
        b = score_ref[pl.ds(r0, COUNT_RB), :]
        s_glob = r0 + lax.broadcasted_iota(I32, (COUNT_RB, 1), 0)
        sel = jnp.logical_and(b >= lo, s_glob <= t_glob)
        if cut is not None:
            sel = jnp.logical_and(sel, jnp.logical_or(b >= hi, s_glob < cut))
        mask_ref[pl.ds(r0, COUNT_RB), :] = jnp.concatenate([jnp.where(sel, 0.0, NEG_INF)] * HEAD_GROUP, axis=1)

    any_tied = jnp.min(done) == 0

    @pl.when(jnp.logical_not(any_tied))
    def _():
        def body(r, carry):
            write_mask(r, None)
            return carry
        lax.fori_loop(0, nblk, body, 0)

    @pl.when(any_tied)
    def _():
        need = topk - count_ge(hi)

        def count_tie_below(x):
            def body(r, acc):
                r0 = pl.multiple_of(r * COUNT_RB, COUNT_RB)
                b = score_ref[pl.ds(r0, COUNT_RB), :]
                s_glob = r0 + lax.broadcasted_iota(I32, (COUNT_RB, 1), 0)
                m = jnp.logical_and(jnp.logical_and(b >= lo, b < hi), s_glob < x)
                return acc + fold(m.astype(I32))
            acc = lax.fori_loop(0, nblk, body, jnp.zeros((SUBLANES, t), I32))
            return jnp.sum(acc, axis=0, keepdims=True)

        nbits = int(score_ref.shape[0]).bit_length()

        def bit_body(bi, x):
            c = x + jnp.left_shift(jnp.int32(1), nbits - 1 - bi)
            return jnp.where(count_tie_below(c) < need, c, x)

        x = lax.fori_loop(0, nbits, bit_body, jnp.zeros((1, t), I32))
        cut = jnp.where(done > 0, jnp.int32(2 ** 30), x + 1)

        def body(r, carry):
            write_mask(r, cut)
            return carry
        lax.fori_loop(0, nblk, body, 0)

    scale = HEAD_DIM ** -0.5
    gt = HEAD_GROUP * t

    def col_reduce(op, a):
        part = op(a.reshape(a.shape[0] // SUBLANES, SUBLANES, a.shape[1]), axis=0)
        return op(part, axis=0, keepdims=True)

    qgs = [jnp.concatenate(
        [q_ref[0, :, h * HEAD_DIM:(h + 1) * HEAD_DIM] for h in range(g * HEAD_GROUP, (g + 1) * HEAD_GROUP)],
        axis=0) for g in range(N_KV_HEADS)]
    acc_ref[...] = jnp.zeros_like(acc_ref)

    def att_chunk(kc, carry):
        r0 = pl.multiple_of(kc * t, t)
        mk = mask_ref[pl.ds(r0, t), :]
        new = []
        for g in range(N_KV_HEADS):
            m, l = carry[g]
            kk = k_ref[0, pl.ds(r0, t), g * HEAD_DIM:(g + 1) * HEAD_DIM]
            s = _dot_nt(kk, qgs[g]) * (scale * LOG2E)
            s = s + bias_ref[jnp.minimum(i - kc, 2), g] + mk
            m_new = jnp.maximum(m, col_reduce(jnp.max, s))
            alpha = jnp.exp2(m - m_new)
            p = jnp.exp2(s - m_new)
            l = alpha * l + col_reduce(jnp.sum, p)
            vt = vt_ref[0, g * HEAD_DIM:(g + 1) * HEAD_DIM, pl.ds(r0, t)]
            acc_ref[g] = alpha * acc_ref[g] + _dot(vt, p.astype(_MXU_DTYPE))
            new.append((m_new, l))
        return tuple(new)

    init = (jnp.full((1, gt), NEG_INF, F32), jnp.zeros((1, gt), F32))
    fin = lax.fori_loop(0, nk, att_chunk, (init,) * N_KV_HEADS)
    for g in range(N_KV_HEADS):
        out = acc_ref[g] / fin[g][1]
        for hh in range(HEAD_GROUP):
            h = g * HEAD_GROUP + hh
            o_ref[0, :, h * HEAD_DIM:(h + 1) * HEAD_DIM] = out[:, hh * t:(hh + 1) * t].T.astype(o_ref.dtype)


def _dsa(q, qi, wit, kia, kib, k, vt, bias, topk):
    bsz, s, aw = q.shape
    t = ATT_T
    assert s % t == 0 and topk <= t
    row = lambda w: pl.BlockSpec((1, t, w), lambda b, i: (b, i, 0))
    full = lambda a: pl.BlockSpec((1,) + a.shape[1:], lambda b, i: (b, 0, 0))
    return pl.pallas_call(
        functools.partial(_dsa_body, t=t, topk=topk),
        grid=(bsz, s // t),
        in_specs=[row(aw), row(qi.shape[2]),
                  pl.BlockSpec((1, IDX_HEADS, t), lambda b, i: (b, 0, i)),
                  full(kia), full(kib), full(k), full(vt),
                  pl.BlockSpec(bias.shape, lambda b, i: (0, 0, 0, 0))],
        out_specs=row(aw),
        out_shape=jax.ShapeDtypeStruct((bsz, s, aw), _MXU_DTYPE),
        scratch_shapes=[pltpu.VMEM((s, t), F32), pltpu.VMEM((s, HEAD_GROUP * t), F32),
                        pltpu.VMEM((N_KV_HEADS, HEAD_DIM, HEAD_GROUP * t), F32)],
        compiler_params=_cparams(("arbitrary", "arbitrary")),
        name="dsa",
    )(q, qi, wit, kia, kib, k, vt, bias)


def _pack_halves(hb):
    assert jnp.dtype(hb.dtype).itemsize == 2
    n = hb.shape[1] // 2
    hi = lax.bitcast_convert_type(hb[:, :n].astype(F32), I32)
    lo = lax.bitcast_convert_type(hb[:, n:].astype(F32), I32)
    return jnp.bitwise_or(hi, lax.shift_right_logical(lo, jnp.full(lo.shape, 16, I32)))


def _unpack_halves(w, dtype):
    hi = lax.bitcast_convert_type(jnp.bitwise_and(w, jnp.int32(-65536)), F32).astype(dtype)
    lo = lax.bitcast_convert_type(jnp.left_shift(w, 16), F32).astype(dtype)
    return jnp.concatenate([hi, lo], axis=1)


def _mix_body(u_ref, halo_ref, a_ref, x_ref, mod_ref, wpool_ref, ps_ref, wout_ref, g_ref, wr_ref, br_ref,
              x1_ref, h2_ref, lg_ref, *, seq, tm):
    t0 = (pl.program_id(0) * tm) % seq
    u = u_ref[...]
    halo = jnp.where(t0 == 0, 0.0, halo_ref[...])
    ext = jnp.concatenate([halo, u], axis=0)
    pos = t0 + lax.broadcasted_iota(I32, (tm, 1), 0)
    gw = u.shape[1] // N_POOL_GROUPS
    ys = []
    for g, w in enumerate(POOL_WINDOWS):
        a = ext[:, g * gw:(g + 1) * gw]
        sft = 1
        while sft < w:
            a = a + pltpu.roll(a, sft, 0)
            sft *= 2
        cnt = jnp.minimum(pos + 1, w).astype(F32)
        p = a[HALO:, :] / cnt - u[:, g * gw:(g + 1) * gw]
        ys.append(_dot(p.astype(_MXU_DTYPE), wpool_ref[g]) * ps_ref[:, g * gw:(g + 1) * gw])
    pool = jnp.concatenate(ys, axis=1).astype(_MXU_DTYPE)
    pw = pool.shape[1]
    mixed = _dot(pool, wout_ref[:pw, :]) + _dot(a_ref[...], wout_ref[pw:, :])
    x1 = x_ref[...] + mod_ref[0, 2:3, :] * mixed
    x1_ref[...] = x1
    h2 = _rms_mod(x1, g_ref[...], mod_ref[0, 3:4, :], mod_ref[0, 4:5, :]).astype(_MXU_DTYPE)
    h2_ref[...] = _pack_halves(h2)
    lg_ref[...] = _dot(h2, wr_ref[...]) + br_ref[...]


def _mix(u, attn, x, mod, w_pool, pool_scale, w_out, g_ffn, w_r, b_r, seq):
    n_tok, d = x.shape
    tm = MIX_TM
    assert seq % tm == 0 and tm % HALO == 0
    pw = u.shape[1]
    aw = attn.shape[1]
    rows = lambda w: pl.BlockSpec((tm, w), lambda i: (i, 0))
    const = lambda a: pl.BlockSpec(a.shape, lambda i: (0,) * a.ndim)
    return pl.pallas_call(
        functools.partial(_mix_body, seq=seq, tm=tm),
        grid=(n_tok // tm,),
        in_specs=[rows(pw),
                  pl.BlockSpec((HALO, pw), lambda i: (jnp.maximum(i * (tm // HALO) - 1, 0), 0)),
                  rows(aw), rows(d),
                  pl.BlockSpec((1, 6, d), lambda i: ((i * tm) // seq, 0, 0)),
                  const(w_pool), const(pool_scale), const(w_out), const(g_ffn), const(w_r), const(b_r)],
        out_specs=[rows(d), rows(d // 2), rows(LANES)],
        out_shape=[jax.ShapeDtypeStruct((n_tok, d), F32), jax.ShapeDtypeStruct((n_tok, d // 2), I32),
                   jax.ShapeDtypeStruct((n_tok, LANES), F32)],
        compiler_params=_cparams(("arbitrary",)),
        name="mix",
    )(u, u, attn, x, mod, w_pool, pool_scale, w_out, g_ffn, w_r, b_r)


def _route_body(lg_ref, rt_ref, gate_ref, cnt_ref, run_ref):
    @pl.when(pl.program_id(0) == 0)
    def _():
        run_ref[...] = jnp.zeros_like(run_ref)

    lg = lg_ref[...]
    tr = lg.shape[0]
    lane = lax.broadcasted_iota(I32, lg.shape, 1)
    gmask = lane < N_GROUPS
    gl = jnp.where(gmask, lg, -BIG)
    gmax = jnp.max(gl, axis=-1, keepdims=True)
    gsel = jnp.min(jnp.where(jnp.logical_and(gmask, gl == gmax), lane, LANES), axis=-1, keepdims=True)
    gsum = jnp.sum(jnp.where(gmask, jnp.exp(gl - gmax), 0.0), axis=-1, keepdims=True)
    p_g = 1.0 / gsum
    in_e = jnp.logical_and(lane >= N_GROUPS, lane < N_GROUPS + N_EXPERTS)
    emask = jnp.logical_and(in_e, lax.shift_right_arithmetic(lane - N_GROUPS, EXPERTS_PER_GROUP.bit_length() - 1) == gsel)
    el = jnp.where(emask, lg, -BIG)
    emax = jnp.max(el, axis=-1, keepdims=True)
    ex = jnp.where(emask, jnp.exp(el - emax), 0.0)
    ep = ex / jnp.sum(ex, axis=-1, keepdims=True)
    p1 = jnp.max(jnp.where(emask, ep, -1.0), axis=-1, keepdims=True)
    i1 = jnp.min(jnp.where(jnp.logical_and(emask, ep == p1), lane, LANES), axis=-1, keepdims=True)
    m2 = jnp.logical_and(emask, lane != i1)
    p2 = jnp.max(jnp.where(m2, ep, -1.0), axis=-1, keepdims=True)
    i2 = jnp.min(jnp.where(jnp.logical_and(m2, ep == p2), lane, LANES), axis=-1, keepdims=True)
    den = p1 + p2
    g1 = p_g * p1 / den
    g2 = p_g * p2 / den
    e1 = i1 - N_GROUPS
    e2 = i2 - N_GROUPS

    oh = jnp.logical_or(lane == e1, lane == e2)
    ri = lax.broadcasted_iota(I32, (tr, tr), 0)
    ci = lax.broadcasted_iota(I32, (tr, tr), 1)
    ltri = (ci < ri).astype(_MXU_DTYPE)
    before = _dot(ltri, oh.astype(_MXU_DTYPE)) + run_ref[...]
    r1 = jnp.sum(jnp.where(lane == e1, before, 0.0), axis=-1, keepdims=True)
    r2 = jnp.sum(jnp.where(lane == e2, before, 0.0), axis=-1, keepdims=True)
    run = run_ref[...] + jnp.sum(oh.astype(F32), axis=0, keepdims=True)
    run_ref[...] = run
    cnt_ref[...] = jnp.broadcast_to(run, cnt_ref.shape)

    route = jnp.where(lane == 0, e1.astype(F32),
                      jnp.where(lane == 1, e2.astype(F32),
                                jnp.where(lane == 2, r1, jnp.where(lane == 3, r2, 0.0))))
    rt_ref[...] = route.T[:SUBLANES, :].astype(I32)
    gate_ref[...] = jnp.where(lane == 0, g1, jnp.where(lane == 1, g2, 0.0))


def _route(logits):
    n_tok = logits.shape[0]
    tr = min(ROUTE_TR, n_tok)
    return pl.pallas_call(
        _route_body,
        grid=(n_tok // tr,),
        in_specs=[pl.BlockSpec((tr, LANES), lambda i: (i, 0))],
        out_specs=[pl.BlockSpec((SUBLANES, tr), lambda i: (0, i)),
                   pl.BlockSpec((tr, LANES), lambda i: (i, 0)),
                   pl.BlockSpec((SUBLANES, LANES), lambda i: (0, 0))],
        out_shape=[jax.ShapeDtypeStruct((SUBLANES, n_tok), I32),
                   jax.ShapeDtypeStruct((n_tok, LANES), F32),
                   jax.ShapeDtypeStruct((SUBLANES, LANES), F32)],
        scratch_shapes=[pltpu.VMEM((1, LANES), F32)],
        compiler_params=_cparams(("arbitrary",)),
        name="route",
    )(logits)


def _dest_body(ps_ref, rt_ref, o_ref):
    r = rt_ref[...]
    start = jnp.zeros_like(r)
    for e in range(N_EXPERTS):
        start = jnp.where(r == e, ps_ref[e], start)
    o_ref[...] = start + pltpu.roll(r, SUBLANES - 2, 0)


def _dest(route_t, pstart):
    n_tok = route_t.shape[1]
    tb = min(n_tok, 2048)
    return pl.pallas_call(
        _dest_body,
        grid_spec=pltpu.PrefetchScalarGridSpec(
            num_scalar_prefetch=1,
            grid=(n_tok // tb,),
            in_specs=[pl.BlockSpec((SUBLANES, tb), lambda i, ps: (0, i))],
            out_specs=pl.BlockSpec((SUBLANES, tb), lambda i, ps: (0, i)),
        ),
        out_shape=jax.ShapeDtypeStruct((SUBLANES, n_tok), I32),
        compiler_params=_cparams(("arbitrary",)),
        name="dest",
    )(pstart, route_t)


def _row_copy(src_ref, src_row, dst_ref, dst_row, sem):
    return pltpu.make_async_copy(src_ref.at[pl.ds(src_row, 1), :], dst_ref.at[pl.ds(dst_row, 1), :], sem)


def _dispatch_body(pstart_ref, cnt_ref, nu_ref, h_hbm, d0_ref, d1_ref, xs_ref, hbuf, zero_ref, lsem, sems,
                   *, td, bm):
    i = pl.program_id(0)
    last = pl.num_programs(0) - 1
    g = td // SUBLANES
    cur = i % 3

    def load(tile, slot):
        return pltpu.make_async_copy(h_hbm.at[pl.ds(tile * g, g)], hbuf.at[slot], lsem.at[slot])

    @pl.when(i == 0)
    def _():
        load(0, 0).start()

    @pl.when(i < last)
    def _():
        load(i + 1, (i + 1) % 3).start()

    load(i, cur).wait()

    def issue(c, carry):
        for u in range(SUBLANES):
            r = c * SUBLANES + u
            for j, d_ref in enumerate((d0_ref, d1_ref)):
                pltpu.make_async_copy(hbuf.at[cur, c, pl.ds(u, 1), :], xs_ref.at[pl.ds(d_ref[r], 1), :],
                                      sems.at[cur]).start(priority=j)
        return carry

    lax.fori_loop(0, g, issue, 0)

    def wait_step(s):
        for _ in range(2):
            pltpu.make_async_copy(xs_ref.at[pl.ds(0, td), :], xs_ref.at[pl.ds(0, td), :], s).wait()

    @pl.when(i > 0)
    def _():
        wait_step(sems.at[(i + 2) % 3])

    @pl.when(i == last)
    def _():
        wait_step(sems.at[cur])
        zsem = sems.at[3]
        zero_ref[...] = jnp.zeros_like(zero_ref)

        def per_expert(e, carry):
            c = cnt_ref[e]
            n_pad = (bm - c % bm) % bm
            base = pstart_ref[e] + c

            def start(r, cc):
                _row_copy(zero_ref, 0, xs_ref, base + r, zsem).start()
                return cc

            def wait(r, cc):
                _row_copy(zero_ref, 0, xs_ref, 0, zsem).wait()
                return cc

            lax.fori_loop(0, n_pad, start, 0)
            lax.fori_loop(0, n_pad, wait, 0)
            return carry

        lax.fori_loop(0, N_EXPERTS, per_expert, 0)

        first = nu_ref[0] * (bm // SUBLANES)
        n_tail = xs_ref.shape[0] // SUBLANES - first

        def tail_copy(r):
            return pltpu.make_async_copy(
                zero_ref, xs_ref.at[pl.ds(pl.multiple_of((first + r) * SUBLANES, SUBLANES), SUBLANES), :], zsem)

        def tail_start(r, cc):
            tail_copy(r).start()
            return cc

        def tail_wait(r, cc):
            tail_copy(0).wait()
            return cc

        lax.fori_loop(0, n_tail, tail_start, 0)
        lax.fori_loop(0, n_tail, tail_wait, 0)


def _dispatch(h2, dest0, dest1, pstart, counts, n_used, n_slots):
    n_tok, d = h2.shape
    td = DISP_TD
    tok = pl.BlockSpec((td,), lambda i, ps, cn, nu: (i,), memory_space=pltpu.SMEM)
    return pl.pallas_call(
        functools.partial(_dispatch_body, td=td, bm=MOE_BM),
        grid_spec=pltpu.PrefetchScalarGridSpec(
            num_scalar_prefetch=3,
            grid=(n_tok // td,),
            in_specs=[pl.BlockSpec(memory_space=pl.ANY), tok, tok],
            out_specs=pl.BlockSpec(memory_space=pl.ANY),
            scratch_shapes=[pltpu.VMEM((3, td // SUBLANES, SUBLANES, d), h2.dtype),
                            pltpu.VMEM((SUBLANES, d), h2.dtype),
                            pltpu.SemaphoreType.DMA((3,)), pltpu.SemaphoreType.DMA((4,))],
        ),
        out_shape=jax.ShapeDtypeStruct((n_slots, d), h2.dtype),
        compiler_params=_cparams(("arbitrary",)),
        name="dispatch",
    )(pstart, counts, n_used, h2.reshape(n_tok // SUBLANES, SUBLANES, d), dest0, dest1)


def _experts_body(slot_ref, cast_ref, pre_ref, misc_ref, x_ref, wg_hbm, wu_hbm, wd_hbm, y_ref,
                  sg, su, sd, wgb, wub, wdb, sem):
    blk = pl.program_id(0)
    e0, pre0, n_used = misc_ref[0], misc_ref[1], misc_ref[2]

    def copies(e):
        return (pltpu.make_async_copy(wg_hbm.at[e], sg, sem.at[0]),
                pltpu.make_async_copy(wu_hbm.at[e], su, sem.at[1]),
                pltpu.make_async_copy(wd_hbm.at[e], sd, sem.at[2]))

    def fetch(e):
        for c in copies(e):
            c.start(priority=1)

    def land(slot):
        for c in copies(0):
            c.wait()
        wgb[slot] = sg[...].astype(wgb.dtype)
        wub[slot] = su[...].astype(wub.dtype)
        wdb[slot] = sd[...].astype(wdb.dtype)

    @pl.when(blk == 0)
    def _():
        fetch(e0)
        land(0)

        @pl.when(pre0 >= 0)
        def _():
            fetch(pre0)

    @pl.when(blk < n_used)
    def _():
        slot = slot_ref[blk]
        xb = _unpack_halves(x_ref[...], _MXU_DTYPE)
        a = _silu(_dot(xb, wgb[slot])) * _dot(xb, wub[slot])
        y_ref[...] = _dot(a.astype(_MXU_DTYPE), wdb[slot])

        @pl.when(cast_ref[blk] == 1)
        def _():
            land(1 - slot)

            @pl.when(pre_ref[blk] >= 0)
            def _():
                fetch(pre_ref[blk])

    @pl.when(blk >= n_used)
    def _():
        y_ref[...] = jnp.zeros_like(y_ref)


def _experts(xs, n_used, pends, w_gate, w_up, w_down):
    n_slots, dpk = xs.shape
    bm = MOE_BM
    n_blocks = n_slots // bm
    d, de = w_gate.shape[1:]
    last_used = n_used[0] - 1
    total = pends[-1]
    eid = jnp.arange(N_EXPERTS, dtype=I32)[None, :]
    owner = lambda start: jnp.minimum(jnp.sum((pends[None, :] <= start[:, None]).astype(I32), axis=1), N_EXPERTS - 1)
    end_of = lambda e: jnp.sum(jnp.where(eid == e[:, None], pends[None, :], 0), axis=1)
    be = owner(jnp.minimum(jnp.arange(n_blocks, dtype=I32), last_used) * bm)
    end_e = end_of(be)
    has_next = end_e < total
    nxt = owner(end_e)
    end_n = end_of(nxt)
    nxt2 = jnp.where(jnp.logical_and(has_next, end_n < total), owner(end_n), -1)
    is_last = jnp.concatenate([be[1:] != be[:-1], jnp.ones((1,), jnp.bool_)])
    is_last = jnp.logical_or(is_last, jnp.arange(n_blocks) >= last_used)
    cast = jnp.logical_and(is_last, has_next).astype(I32)
    pre = jnp.where(cast == 1, nxt2, -1).astype(I32)
    is_first = jnp.concatenate([jnp.ones((1,), jnp.bool_), be[1:] != be[:-1]])
    slot = ((jnp.cumsum(is_first.astype(I32)) - 1) % 2).astype(I32)
    misc = jnp.stack([be[0], jnp.where(has_next[0], nxt[0], -1), n_used[0]]).astype(I32)
    hbm = pl.BlockSpec(memory_space=pl.ANY)
    return pl.pallas_call(
        _experts_body,
        grid_spec=pltpu.PrefetchScalarGridSpec(
            num_scalar_prefetch=4,
            grid=(n_blocks,),
            in_specs=[pl.BlockSpec((bm, dpk), lambda j, sl, ca, pr, mi: (jnp.minimum(j, mi[2] - 1), 0)),
                      hbm, hbm, hbm],
            out_specs=pl.BlockSpec((bm, d), lambda j, sl, ca, pr, mi: (j, 0)),
            scratch_shapes=[pltpu.VMEM((d, de), w_gate.dtype), pltpu.VMEM((d, de), w_up.dtype),
                            pltpu.VMEM((de, d), w_down.dtype),
                            pltpu.VMEM((2, d, de), _MXU_DTYPE), pltpu.VMEM((2, d, de), _MXU_DTYPE),
                            pltpu.VMEM((2, de, d), _MXU_DTYPE), pltpu.SemaphoreType.DMA((3,))],
        ),
        out_shape=jax.ShapeDtypeStruct((n_slots, d), F32),
        compiler_params=_cparams(("arbitrary",)),
        name="experts",
    )(slot, cast, pre, misc, xs, w_gate, w_up, w_down)


def _combine_body(x1_ref, gate_ref, d0_ref, d1_ref, n0_ref, n1_ref, mod_ref, g_ref, ys_ref, o_ref, buf_ref, sem,
                  *, tf):
    i = pl.program_id(0)
    cur = i % 2

    def gather(da_ref, db_ref, half):
        def issue(c, carry):
            for u in range(SUBLANES):
                r = c * SUBLANES + u
                for j, d_ref in enumerate((da_ref, db_ref)):
                    pltpu.make_async_copy(ys_ref.at[pl.ds(d_ref[r], 1), :],
                                          buf_ref.at[half, j, c, pl.ds(u, 1), :], sem.at[half]).start(priority=j)
            return carry

        lax.fori_loop(0, tf // SUBLANES, issue, 0)

    @pl.when(i == 0)
    def _():
        gather(d0_ref, d1_ref, 0)

    @pl.when(i + 1 < pl.num_programs(0))
    def _():
        gather(n0_ref, n1_ref, 1 - cur)

    for _ in range(2):
        pltpu.make_async_copy(ys_ref.at[pl.ds(0, tf), :], ys_ref.at[pl.ds(0, tf), :], sem.at[cur]).wait()

    gates = gate_ref[...]
    d = o_ref.shape[1]
    moe = buf_ref[cur, 0].reshape(tf, d) * gates[:, 0:1] + buf_ref[cur, 1].reshape(tf, d) * gates[:, 1:2]
    x2 = x1_ref[...] + mod_ref[0, 5:6, :] * moe
    ms = jnp.mean(x2 * x2, axis=-1, keepdims=True)
    o_ref[...] = x2 * lax.rsqrt(ms + EPS) * g_ref[...]


def _combine(x1, gates, dest0, dest1, mod, g_final, ys, seq):
    n_tok, d = x1.shape
    tf = COMB_TF
    n_tiles = n_tok // tf
    tok = pl.BlockSpec((tf,), lambda i: (i,), memory_space=pltpu.SMEM)
    nxt = pl.BlockSpec((tf,), lambda i: (jnp.minimum(i + 1, n_tiles - 1),), memory_space=pltpu.SMEM)
    return pl.pallas_call(
        functools.partial(_combine_body, tf=tf),
        grid=(n_tiles,),
        in_specs=[pl.BlockSpec((tf, d), lambda i: (i, 0)),
                  pl.BlockSpec((tf, LANES), lambda i: (i, 0)),
                  tok, tok, nxt, nxt,
                  pl.BlockSpec((1, 6, d), lambda i: ((i * tf) // seq, 0, 0)),
                  pl.BlockSpec((1, d), lambda i: (0, 0)),
                  pl.BlockSpec(memory_space=pl.ANY)],
        out_specs=pl.BlockSpec((tf, d), lambda i: (i, 0)),
        scratch_shapes=[pltpu.VMEM((2, 2, tf // SUBLANES, SUBLANES, d), F32), pltpu.SemaphoreType.DMA((2,))],
        out_shape=jax.ShapeDtypeStruct((n_tok, d), F32),
        compiler_params=_cparams(("arbitrary",)),
        name="combine",
    )(x1, gates, dest0, dest1, dest0, dest1, mod, g_final, ys)


def kernel(x, c, w_ada, b_ada, g_mix, w_in, w_pool, pool_scale, rel_bias, w_out, g_ffn, w_group, b_group,
           w_router, b_router, w_gate, w_up, w_down, g_final):
    bsz, seq, d = x.shape
    n_tok = bsz * seq
    depth = w_ada.shape[0]
    assert depth == 1, "the final norm is fused into the only layer's combine"
    topk = min(TOPK_MAX, seq // 4)
    bias = _bias_tiles(rel_bias, ATT_T)
    bm = MOE_BM
    n_blocks = -(-(n_tok * 2) // bm) + N_EXPERTS
    n_slots = n_blocks * bm
    xt = x.reshape(n_tok, d)
    for i in range(depth):
        mod = _ada(c, w_ada[i], b_ada[i][None, :]).reshape(bsz, 6, d)
        u, q, k, vt, qi, kia, kib, wit = _proj(xt.reshape(bsz, seq, d), mod, g_mix[i][None, :], w_in[i])
        attn = _dsa(q, qi, wit, kia, kib, k, vt, bias, topk)
        w_r = jnp.concatenate(
            [w_group[i], w_router[i], jnp.zeros((d, LANES - N_GROUPS - N_EXPERTS), F32)], axis=1).astype(_MXU_DTYPE)
        b_r = jnp.concatenate(
            [b_group[i], b_router[i], jnp.zeros((LANES - N_GROUPS - N_EXPERTS,), F32)])[None, :]
        x1, h2, logits = _mix(u.reshape(n_tok, -1), attn.reshape(n_tok, -1), xt, mod,
                              w_pool[i].astype(_MXU_DTYPE), pool_scale[i][None, :],
                              w_out[i].astype(_MXU_DTYPE), g_ffn[i][None, :], w_r, b_r, seq)
        route_t, gates, cnt = _route(logits)
        counts = cnt[0, :N_EXPERTS].astype(I32)
        padded = (counts + bm - 1) // bm * bm
        pends = jnp.cumsum(padded)
        pstart = pends - padded
        n_used = (pends[-1:] // bm).astype(I32)
        dest = _dest(route_t, pstart)
        dest0, dest1 = dest[0], dest[1]
        xs = _dispatch(h2, dest0, dest1, pstart, counts, n_used, n_slots)
        ys = _experts(xs, n_used, pends, w_gate[i], w_up[i], w_down[i])
        xt = _combine(x1, gates, dest0, dest1, mod, g_final[None, :], ys, seq)
    return xt.reshape(bsz, seq, d)
```

```python
import functools
import math

import numpy as np
import jax
import jax.numpy as jnp
from jax import lax
from jax.experimental import pallas as pl
from jax.experimental.pallas import tpu as pltpu

F32 = jnp.float32
I32 = jnp.int32
_MXU_DTYPE = jnp.bfloat16

POOL_WINDOWS = (2, 4, 8, 16)
N_POOL_GROUPS = 4
HEAD_DIM = 128
N_HEADS = 8
N_KV_HEADS = 2
HEAD_GROUP = N_HEADS // N_KV_HEADS
IDX_HEADS = 16
IDX_DIM = 64
TOPK_MAX = 256
N_BUCKETS = 32
MAX_DISTANCE = 128
N_GROUPS = 4
EXPERTS_PER_GROUP = 8
N_EXPERTS = N_GROUPS * EXPERTS_PER_GROUP
EPS = 1e-6
NEG_INF = -1e30
BIG = 3e38
LOG2E = 1.4426950408889634

LANES = 128
SUBLANES = 8
VMEM_LIMIT_BYTES = 56 * 1024 * 1024

ADA_TN = 1024
PROJ_TM = 256
ATT_T = 256
SCORE_SUB = 256
COUNT_RB = 256
MIX_TM = 256
HALO = 16
ROUTE_TR = 1024
DISP_TD = 256
MOE_BM = 256
COMB_TF = 256
DMA_UNROLL = 8
SEL_FIXED_ITERS = 18
SEL_MAX_ITERS = 80


def _cparams(sem):
    return pltpu.CompilerParams(dimension_semantics=sem, vmem_limit_bytes=VMEM_LIMIT_BYTES)


def _silu(x):
    return x * (1.0 / (1.0 + jnp.exp(-x)))


def _dot(a, b):
    return jnp.dot(a, b, preferred_element_type=F32)


def _dot_nt(a, b):
    return lax.dot_general(a, b, (((1,), (1,)), ((), ())), preferred_element_type=F32)


def _ada_body(c_ref, w_ref, b_ref, o_ref):
    s = _silu(c_ref[...])
    o_ref[...] = _dot(s.astype(_MXU_DTYPE), w_ref[...].astype(_MXU_DTYPE)) + b_ref[...]


def _ada(c, w, b):
    bsz, d = c.shape
    n = w.shape[1]
    return pl.pallas_call(
        _ada_body,
        grid=(n // ADA_TN,),
        in_specs=[pl.BlockSpec((bsz, d), lambda j: (0, 0)),
                  pl.BlockSpec((d, ADA_TN), lambda j: (0, j)),
                  pl.BlockSpec((1, ADA_TN), lambda j: (0, j))],
        out_specs=pl.BlockSpec((bsz, ADA_TN), lambda j: (0, j)),
        out_shape=jax.ShapeDtypeStruct((bsz, n), F32),
        compiler_params=_cparams(("arbitrary",)),
        name="ada",
    )(c, w, b)


def _rms_mod(x, g, shift, scale):
    ms = jnp.mean(x * x, axis=-1, keepdims=True)
    y = x * lax.rsqrt(ms + EPS) * g
    return y * (1.0 + scale) + shift


def _proj_body(x_ref, mod_ref, g_ref, w_ref, u_ref, q_ref, k_ref, vt_ref, qi_ref, kia_ref, kib_ref,
               wit_ref, *, cuts):
    h = _rms_mod(x_ref[0], g_ref[...], mod_ref[0, 0:1, :], mod_ref[0, 1:2, :])
    hb = h.astype(_MXU_DTYPE)

    def seg(name):
        lo, hi = cuts[name]
        return _dot(hb, w_ref[:, lo:hi])

    u_ref[0] = seg("u")
    q_ref[0] = seg("q").astype(q_ref.dtype)
    k_ref[0] = seg("k").astype(k_ref.dtype)
    vt_ref[0] = seg("v").T.astype(vt_ref.dtype)
    qi_ref[0] = seg("qi").astype(qi_ref.dtype)
    tail = seg("tail")
    tail = jnp.concatenate([tail, jnp.zeros((tail.shape[0], LANES - tail.shape[1]), F32)], axis=1)
    swapped = pltpu.roll(tail, LANES - IDX_DIM, 1)
    lane = lax.broadcasted_iota(I32, tail.shape, 1)
    kia_ref[0] = jnp.where(lane < IDX_DIM, tail, 0.0).astype(kia_ref.dtype)
    kib_ref[0] = jnp.where(lane >= LANES - IDX_DIM, swapped, 0.0).astype(kib_ref.dtype)
    wit_ref[0] = swapped.T[:IDX_HEADS, :]


def _proj(x, mod, g_mix, w_in):
    bsz, s, d = x.shape
    pool_w = d // 2
    attn_w = N_HEADS * HEAD_DIM
    kv_w = N_KV_HEADS * HEAD_DIM
    qi_w = IDX_HEADS * IDX_DIM
    c = np.cumsum([0, pool_w, attn_w, kv_w, kv_w, qi_w, IDX_DIM, IDX_HEADS])
    assert IDX_DIM + IDX_HEADS <= LANES and w_in.shape[1] == c[7]
    wp = w_in.astype(_MXU_DTYPE)
    cuts = {"u": (int(c[0]), int(c[1])), "q": (int(c[1]), int(c[2])), "k": (int(c[2]), int(c[3])),
            "v": (int(c[3]), int(c[4])), "qi": (int(c[4]), int(c[5])), "tail": (int(c[5]), int(c[7]))}
    tm = PROJ_TM
    pw = wp.shape[1]
    row = lambda w: pl.BlockSpec((1, tm, w), lambda b, i: (b, i, 0))
    out_shapes = [
        jax.ShapeDtypeStruct((bsz, s, pool_w), F32),
        jax.ShapeDtypeStruct((bsz, s, attn_w), _MXU_DTYPE),
        jax.ShapeDtypeStruct((bsz, s, kv_w), _MXU_DTYPE),
        jax.ShapeDtypeStruct((bsz, kv_w, s), _MXU_DTYPE),
        jax.ShapeDtypeStruct((bsz, s, qi_w), _MXU_DTYPE),
        jax.ShapeDtypeStruct((bsz, s, LANES), _MXU_DTYPE),
        jax.ShapeDtypeStruct((bsz, s, LANES), _MXU_DTYPE),
        jax.ShapeDtypeStruct((bsz, IDX_HEADS, s), F32),
    ]
    out_specs = [row(pool_w), row(attn_w), row(kv_w),
                 pl.BlockSpec((1, kv_w, tm), lambda b, i: (b, 0, i)),
                 row(qi_w), row(LANES), row(LANES),
                 pl.BlockSpec((1, IDX_HEADS, tm), lambda b, i: (b, 0, i))]
    return pl.pallas_call(
        functools.partial(_proj_body, cuts=cuts),
        grid=(bsz, s // tm),
        in_specs=[pl.BlockSpec((1, tm, d), lambda b, i: (b, i, 0)),
                  pl.BlockSpec((1, 6, d), lambda b, i: (b, 0, 0)),
                  pl.BlockSpec((1, d), lambda b, i: (0, 0)),
                  pl.BlockSpec((d, pw), lambda b, i: (0, 0))],
        out_specs=out_specs,
        out_shape=out_shapes,
        compiler_params=_cparams(("arbitrary", "arbitrary")),
        name="proj",
    )(x, mod, g_mix, wp)


def _bucket_starts():
    max_exact = N_BUCKETS // 2
    d = np.arange(1, 4 * MAX_DISTANCE, dtype=np.float32)
    large = max_exact + (np.log(d / np.float32(max_exact)) / np.float32(math.log(MAX_DISTANCE / max_exact))
                         * np.float32(N_BUCKETS - max_exact)).astype(np.int32)
    large = np.minimum(large, N_BUCKETS - 1)
    bucket = np.where(d < max_exact, d.astype(np.int32), large)
    bucket = np.concatenate([[0], bucket])
    starts = [int(np.argmax(bucket >= b)) for b in range(N_BUCKETS)]
    assert all(np.all(bucket[starts[b]:] >= b) for b in range(N_BUCKETS))
    assert starts[N_BUCKETS - 1] <= MAX_DISTANCE
    return starts


def _bias_body(rb_ref, o_ref, *, t, starts):
    diff = pl.program_id(0)
    s_l = lax.broadcasted_iota(I32, (t, t), 0)
    t_l = lax.broadcasted_iota(I32, (t, t), 1)
    dist = diff * t + t_l - s_l
    for h in range(N_HEADS):
        val = jnp.full((t, t), rb_ref[0, h], F32)
        for b in range(1, N_BUCKETS):
            val = jnp.where(dist >= starts[b], rb_ref[b, h], val)
        hh = h % HEAD_GROUP
        o_ref[0, h // HEAD_GROUP, :, hh * t:(hh + 1) * t] = (val - rb_ref[N_BUCKETS - 1, h]) * LOG2E


def _bias_tiles(rel_bias, t):
    assert t >= MAX_DISTANCE
    return pl.pallas_call(
        functools.partial(_bias_body, t=t, starts=_bucket_starts()),
        grid=(2,),
        in_specs=[pl.BlockSpec(memory_space=pltpu.SMEM)],
        out_specs=pl.BlockSpec((1, N_KV_HEADS, t, HEAD_GROUP * t), lambda i: (i, 0, 0, 0)),
        out_shape=jax.ShapeDtypeStruct((2, N_KV_HEADS, t, HEAD_GROUP * t), F32),
        compiler_params=_cparams(("arbitrary",)),
        name="bias_tiles",
    )(rel_bias)


def _dsa_body(q_ref, qi_ref, wit_ref, kia_ref, kib_ref, k_ref, vt_ref, bias_ref, o_ref,
              score_ref, mask_ref, acc_ref, *, t, topk):
    i = pl.program_id(1)
    nk = i + 1
    t_glob = i * t + lax.broadcasted_iota(I32, (1, t), 1)
    idx_scale = (IDX_DIM ** -0.5) * (IDX_HEADS ** -0.5)

    def score_chunk(c, carry):
        r0 = pl.multiple_of(c * SCORE_SUB, SCORE_SUB)
        ka = kia_ref[0, pl.ds(r0, SCORE_SUB), :]
        kb = kib_ref[0, pl.ds(r0, SCORE_SUB), :]
        acc = jnp.zeros((SCORE_SUB, t), F32)
        for j in range(IDX_HEADS // 2):
            qp = qi_ref[0, :, j * LANES:(j + 1) * LANES]
            da = _dot_nt(ka, qp)
            db = _dot_nt(kb, qp)
            acc = acc + jnp.maximum(da, 0.0) * wit_ref[0, 2 * j:2 * j + 1, :]
            acc = acc + jnp.maximum(db, 0.0) * wit_ref[0, 2 * j + 1:2 * j + 2, :]
        sc = acc * idx_scale
        s_glob = r0 + lax.broadcasted_iota(I32, (SCORE_SUB, 1), 0)
        score_ref[pl.ds(r0, SCORE_SUB), :] = jnp.where(s_glob <= t_glob, sc, NEG_INF)
        return carry

    lax.fori_loop(0, nk * (t // SCORE_SUB), score_chunk, 0)

    nblk = nk * (t // COUNT_RB)
    fold = lambda m: jnp.sum(m.reshape(COUNT_RB // SUBLANES, SUBLANES, t), axis=0)

    def blk(r):
        return score_ref[pl.ds(pl.multiple_of(r * COUNT_RB, COUNT_RB), COUNT_RB), :]

    def count_ge(c):
        def body(r, acc):
            return acc + fold((blk(r) >= c).astype(I32))
        acc = lax.fori_loop(0, nblk, body, jnp.zeros((SUBLANES, t), I32))
        return jnp.sum(acc, axis=0, keepdims=True)

    def stats_body(r, carry):
        cnt, mn, mx = carry
        b = blk(r)
        real = b > NEG_INF
        cnt = cnt + fold(real.astype(I32))
        mn = jnp.minimum(mn, jnp.min(jnp.where(real, b, BIG).reshape(COUNT_RB // SUBLANES, SUBLANES, t), axis=0))
        mx = jnp.maximum(mx, jnp.max(b.reshape(COUNT_RB // SUBLANES, SUBLANES, t), axis=0))
        return cnt, mn, mx

    cnt8, mn8, mx8 = lax.fori_loop(
        0, nblk, stats_body,
        (jnp.zeros((SUBLANES, t), I32), jnp.full((SUBLANES, t), BIG, F32), jnp.full((SUBLANES, t), -BIG, F32)))
    n_real = jnp.sum(cnt8, axis=0, keepdims=True)
    rmin = jnp.min(mn8, axis=0, keepdims=True)
    rmax = jnp.max(mx8, axis=0, keepdims=True)
    n_max = count_ge(rmax)
    degenerate = jnp.logical_and(n_real >= topk, n_max >= topk)
    lo0 = jnp.where(n_real < topk, NEG_INF, jnp.where(degenerate, rmax, rmin))
    hi0 = jnp.where(degenerate, BIG, rmax)
    done0 = jnp.where(n_real <= topk, 1, 0)

    def bisect(st):
        lo, hi, done, stuck = st
        c = lo + (hi - lo) * 0.5
        active = (done + stuck) == 0
        has_mid = jnp.logical_and(c > lo, c < hi)
        n = count_ge(c)
        upd = jnp.logical_and(active, has_mid)
        lo = jnp.where(jnp.logical_and(upd, n >= topk), c, lo)
        hi = jnp.where(jnp.logical_and(upd, n < topk), c, hi)
        done = jnp.where(jnp.logical_and(upd, n == topk), 1, done)
        stuck = jnp.where(jnp.logical_and(active, jnp.logical_not(has_mid)), 1, stuck)
        return lo, hi, done, stuck

    st = lax.fori_loop(0, SEL_FIXED_ITERS, lambda _, s: bisect(s), (lo0, hi0, done0, jnp.zeros((1, t), I32)))

    def sel_cond(s):
        return jnp.logical_and(s[0] < SEL_MAX_ITERS, jnp.min(s[3] + s[4]) == 0)

    _, lo, hi, done, _ = lax.while_loop(sel_cond, lambda s: (s[0] + 1,) + bisect(s[1:]), (jnp.int32(0),) + st)

    def write_mask(r, cut):
        r0 = pl.multiple_of(r * COUNT_RB, COUNT_RB)
        b = score_ref[pl.ds(r0, COUNT_RB), :]
        s_glob = r0 + lax.broadcasted_iota(I32, (COUNT_RB, 1), 0)
        sel = jnp.logical_and(b >= lo, s_glob <= t_glob)
        if cut is not None:
            sel = jnp.logical_and(sel, jnp.logical_or(b >= hi, s_glob < cut))
        mask_ref[pl.ds(r0, COUNT_RB), :] = jnp.concatenate([jnp.where(sel, 0.0, NEG_INF)] * HEAD_GROUP, axis=1)

    any_tied = jnp.min(done) == 0

    @pl.when(jnp.logical_not(any_tied))
    def _():
        def body(r, carry):
            write_mask(r, None)
            return carry
        lax.fori_loop(0, nblk, body, 0)

    @pl.when(any_tied)
    def _():
        need = topk - count_ge(hi)

        def count_tie_below(x):
            def body(r, acc):
                r0 = pl.multiple_of(r * COUNT_RB, COUNT_RB)
                b = score_ref[pl.ds(r0, COUNT_RB), :]
                s_glob = r0 + lax.broadcasted_iota(I32, (COUNT_RB, 1), 0)
                m = jnp.logical_and(jnp.logical_and(b >= lo, b < hi), s_glob < x)
                return acc + fold(m.astype(I32))
            acc = lax.fori_loop(0, nblk, body, jnp.zeros((SUBLANES, t), I32))
            return jnp.sum(acc, axis=0, keepdims=True)

        nbits = int(score_ref.shape[0]).bit_length()

        def bit_body(bi, x):
            c = x + jnp.left_shift(jnp.int32(1), nbits - 1 - bi)
            return jnp.where(count_tie_below(c) < need, c, x)

        x = lax.fori_loop(0, nbits, bit_body, jnp.zeros((1, t), I32))
        cut = jnp.where(done > 0, jnp.int32(2 ** 30), x + 1)

        def body(r, carry):
            write_mask(r, cut)
            return carry
        lax.fori_loop(0, nblk, body, 0)

    scale = HEAD_DIM ** -0.5
    gt = HEAD_GROUP * t

    def col_reduce(op, a):
        part = op(a.reshape(a.shape[0] // SUBLANES, SUBLANES, a.shape[1]), axis=0)
        return op(part, axis=0, keepdims=True)

    qgs = [jnp.concatenate(
        [q_ref[0, :, h * HEAD_DIM:(h + 1) * HEAD_DIM] for h in range(g * HEAD_GROUP, (g + 1) * HEAD_GROUP)],
        axis=0) for g in range(N_KV_HEADS)]
    acc_ref[...] = jnp.zeros_like(acc_ref)

    def att_chunk(kc, carry, near):
        r0 = pl.multiple_of(kc * t, t)
        mk = mask_ref[pl.ds(r0, t), :]
        new = []
        for g in range(N_KV_HEADS):
            m, l = carry[g]
            kk = k_ref[0, pl.ds(r0, t), g * HEAD_DIM:(g + 1) * HEAD_DIM]
            s = _dot_nt(kk, qgs[g]) * (scale * LOG2E) + mk
            if near:
                s = s + bias_ref[i - kc, g]
            m_new = jnp.maximum(m, col_reduce(jnp.max, s))
            alpha = jnp.exp2(m - m_new)
            p = jnp.exp2(s - m_new)
            l = alpha * l + col_reduce(jnp.sum, p)
            vt = vt_ref[0, g * HEAD_DIM:(g + 1) * HEAD_DIM, pl.ds(r0, t)]
            acc_ref[g] = alpha * acc_ref[g] + _dot(vt, p.astype(_MXU_DTYPE))
            new.append((m_new, l))
        return tuple(new)

    init = (jnp.full((1, gt), NEG_INF, F32), jnp.zeros((1, gt), F32))
    n_far = jnp.maximum(i - 1, 0)
    far = lax.fori_loop(0, n_far, functools.partial(att_chunk, near=False), (init,) * N_KV_HEADS)
    fin = lax.fori_loop(n_far, nk, functools.partial(att_chunk, near=True), far)
    for g in range(N_KV_HEADS):
        out = acc_ref[g] / fin[g][1]
        for hh in range(HEAD_GROUP):
            h = g * HEAD_GROUP + hh
            o_ref[0, :, h * HEAD_DIM:(h + 1) * HEAD_DIM] = out[:, hh * t:(hh + 1) * t].T.astype(o_ref.dtype)


def _dsa(q, qi, wit, kia, kib, k, vt, bias, topk):
    bsz, s, aw = q.shape
    t = ATT_T
    assert s % t == 0 and topk <= t
    row = lambda w: pl.BlockSpec((1, t, w), lambda b, i: (b, i, 0))
    full = lambda a: pl.BlockSpec((1,) + a.shape[1:], lambda b, i: (b, 0, 0))
    return pl.pallas_call(
        functools.partial(_dsa_body, t=t, topk=topk),
        grid=(bsz, s // t),
        in_specs=[row(aw), row(qi.shape[2]),
                  pl.BlockSpec((1, IDX_HEADS, t), lambda b, i: (b, 0, i)),
                  full(kia), full(kib), full(k), full(vt),
                  pl.BlockSpec(bias.shape, lambda b, i: (0, 0, 0, 0))],
        out_specs=row(aw),
        out_shape=jax.ShapeDtypeStruct((bsz, s, aw), _MXU_DTYPE),
        scratch_shapes=[pltpu.VMEM((s, t), F32), pltpu.VMEM((s, HEAD_GROUP * t), F32),
                        pltpu.VMEM((N_KV_HEADS, HEAD_DIM, HEAD_GROUP * t), F32)],
        compiler_params=_cparams(("arbitrary", "arbitrary")),
        name="dsa",
    )(q, qi, wit, kia, kib, k, vt, bias)


def _pack_halves(hb):
    assert jnp.dtype(hb.dtype).itemsize == 2
    n = hb.shape[1] // 2
    hi = lax.bitcast_convert_type(hb[:, :n].astype(F32), I32)
    lo = lax.bitcast_convert_type(hb[:, n:].astype(F32), I32)
    return jnp.bitwise_or(hi, lax.shift_right_logical(lo, jnp.full(lo.shape, 16, I32)))


def _unpack_halves(w, dtype):
    hi = lax.bitcast_convert_type(jnp.bitwise_and(w, jnp.int32(-65536)), F32).astype(dtype)
    lo = lax.bitcast_convert_type(jnp.left_shift(w, 16), F32).astype(dtype)
    return jnp.concatenate([hi, lo], axis=1)


def _mix_body(u_ref, halo_ref, a_ref, x_ref, mod_ref, wpool_ref, ps_ref, wout_ref, g_ref, wr_ref, br_ref,
              x1_ref, h2_ref, lg_ref, *, seq, tm):
    t0 = (pl.program_id(0) * tm) % seq
    u = u_ref[...]
    halo = jnp.where(t0 == 0, 0.0, halo_ref[...])
    ext = jnp.concatenate([halo, u], axis=0)
    pos = t0 + lax.broadcasted_iota(I32, (tm, 1), 0)
    gw = u.shape[1] // N_POOL_GROUPS
    ys = []
    for g, w in enumerate(POOL_WINDOWS):
        a = ext[:, g * gw:(g + 1) * gw]
        sft = 1
        while sft < w:
            a = a + pltpu.roll(a, sft, 0)
            sft *= 2
        cnt = jnp.minimum(pos + 1, w).astype(F32)
        p = a[HALO:, :] / cnt - u[:, g * gw:(g + 1) * gw]
        ys.append(_dot(p.astype(_MXU_DTYPE), wpool_ref[g]) * ps_ref[:, g * gw:(g + 1) * gw])
    pool = jnp.concatenate(ys, axis=1).astype(_MXU_DTYPE)
    pw = pool.shape[1]
    mixed = _dot(pool, wout_ref[:pw, :]) + _dot(a_ref[...], wout_ref[pw:, :])
    x1 = x_ref[...] + mod_ref[0, 2:3, :] * mixed
    x1_ref[...] = x1
    h2 = _rms_mod(x1, g_ref[...], mod_ref[0, 3:4, :], mod_ref[0, 4:5, :]).astype(_MXU_DTYPE)
    h2_ref[...] = _pack_halves(h2)
    lg_ref[...] = _dot(h2, wr_ref[...]) + br_ref[...]


def _mix(u, attn, x, mod, w_pool, pool_scale, w_out, g_ffn, w_r, b_r, seq):
    n_tok, d = x.shape
    tm = MIX_TM
    assert seq % tm == 0 and tm % HALO == 0
    pw = u.shape[1]
    aw = attn.shape[1]
    rows = lambda w: pl.BlockSpec((tm, w), lambda i: (i, 0))
    const = lambda a: pl.BlockSpec(a.shape, lambda i: (0,) * a.ndim)
    return pl.pallas_call(
        functools.partial(_mix_body, seq=seq, tm=tm),
        grid=(n_tok // tm,),
        in_specs=[rows(pw),
                  pl.BlockSpec((HALO, pw), lambda i: (jnp.maximum(i * (tm // HALO) - 1, 0), 0)),
                  rows(aw), rows(d),
                  pl.BlockSpec((1, 6, d), lambda i: ((i * tm) // seq, 0, 0)),
                  const(w_pool), const(pool_scale), const(w_out), const(g_ffn), const(w_r), const(b_r)],
        out_specs=[rows(d), rows(d // 2), rows(LANES)],
        out_shape=[jax.ShapeDtypeStruct((n_tok, d), F32), jax.ShapeDtypeStruct((n_tok, d // 2), I32),
                   jax.ShapeDtypeStruct((n_tok, LANES), F32)],
        compiler_params=_cparams(("arbitrary",)),
        name="mix",
    )(u, u, attn, x, mod, w_pool, pool_scale, w_out, g_ffn, w_r, b_r)


def _route_body(lg_ref, rt_ref, gate_ref, cnt_ref, run_ref):
    @pl.when(pl.program_id(0) == 0)
    def _():
        run_ref[...] = jnp.zeros_like(run_ref)

    lg = lg_ref[...]
    tr = lg.shape[0]
    lane = lax.broadcasted_iota(I32, lg.shape, 1)
    gmask = lane < N_GROUPS
    gl = jnp.where(gmask, lg, -BIG)
    gmax = jnp.max(gl, axis=-1, keepdims=True)
    gsel = jnp.min(jnp.where(jnp.logical_and(gmask, gl == gmax), lane, LANES), axis=-1, keepdims=True)
    gsum = jnp.sum(jnp.where(gmask, jnp.exp(gl - gmax), 0.0), axis=-1, keepdims=True)
    p_g = 1.0 / gsum
    in_e = jnp.logical_and(lane >= N_GROUPS, lane < N_GROUPS + N_EXPERTS)
    emask = jnp.logical_and(in_e, lax.shift_right_arithmetic(lane - N_GROUPS, EXPERTS_PER_GROUP.bit_length() - 1) == gsel)
    el = jnp.where(emask, lg, -BIG)
    emax = jnp.max(el, axis=-1, keepdims=True)
    ex = jnp.where(emask, jnp.exp(el - emax), 0.0)
    ep = ex / jnp.sum(ex, axis=-1, keepdims=True)
    p1 = jnp.max(jnp.where(emask, ep, -1.0), axis=-1, keepdims=True)
    i1 = jnp.min(jnp.where(jnp.logical_and(emask, ep == p1), lane, LANES), axis=-1, keepdims=True)
    m2 = jnp.logical_and(emask, lane != i1)
    p2 = jnp.max(jnp.where(m2, ep, -1.0), axis=-1, keepdims=True)
    i2 = jnp.min(jnp.where(jnp.logical_and(m2, ep == p2), lane, LANES), axis=-1, keepdims=True)
    den = p1 + p2
    g1 = p_g * p1 / den
    g2 = p_g * p2 / den
    e1 = i1 - N_GROUPS
    e2 = i2 - N_GROUPS

    oh = jnp.logical_or(lane == e1, lane == e2)
    ri = lax.broadcasted_iota(I32, (tr, tr), 0)
    ci = lax.broadcasted_iota(I32, (tr, tr), 1)
    ltri = (ci < ri).astype(_MXU_DTYPE)
    before = _dot(ltri, oh.astype(_MXU_DTYPE)) + run_ref[...]
    r1 = jnp.sum(jnp.where(lane == e1, before, 0.0), axis=-1, keepdims=True)
    r2 = jnp.sum(jnp.where(lane == e2, before, 0.0), axis=-1, keepdims=True)
    run = run_ref[...] + jnp.sum(oh.astype(F32), axis=0, keepdims=True)
    run_ref[...] = run
    cnt_ref[...] = jnp.broadcast_to(run, cnt_ref.shape)

    route = jnp.where(lane == 0, e1.astype(F32),
                      jnp.where(lane == 1, e2.astype(F32),
                                jnp.where(lane == 2, r1, jnp.where(lane == 3, r2, 0.0))))
    rt_ref[...] = route.T[:SUBLANES, :].astype(I32)
    gate_ref[...] = jnp.where(lane == 0, g1, jnp.where(lane == 1, g2, 0.0))


def _route(logits):
    n_tok = logits.shape[0]
    tr = min(ROUTE_TR, n_tok)
    return pl.pallas_call(
        _route_body,
        grid=(n_tok // tr,),
        in_specs=[pl.BlockSpec((tr, LANES), lambda i: (i, 0))],
        out_specs=[pl.BlockSpec((SUBLANES, tr), lambda i: (0, i)),
                   pl.BlockSpec((tr, LANES), lambda i: (i, 0)),
                   pl.BlockSpec((SUBLANES, LANES), lambda i: (0, 0))],
        out_shape=[jax.ShapeDtypeStruct((SUBLANES, n_tok), I32),
                   jax.ShapeDtypeStruct((n_tok, LANES), F32),
                   jax.ShapeDtypeStruct((SUBLANES, LANES), F32)],
        scratch_shapes=[pltpu.VMEM((1, LANES), F32)],
        compiler_params=_cparams(("arbitrary",)),
        name="route",
    )(logits)


def _dest_body(ps_ref, rt_ref, o_ref):
    r = rt_ref[...]
    start = jnp.zeros_like(r)
    for e in range(N_EXPERTS):
        start = jnp.where(r == e, ps_ref[e], start)
    o_ref[...] = start + pltpu.roll(r, SUBLANES - 2, 0)


def _dest(route_t, pstart):
    n_tok = route_t.shape[1]
    tb = min(n_tok, 2048)
    return pl.pallas_call(
        _dest_body,
        grid_spec=pltpu.PrefetchScalarGridSpec(
            num_scalar_prefetch=1,
            grid=(n_tok // tb,),
            in_specs=[pl.BlockSpec((SUBLANES, tb), lambda i, ps: (0, i))],
            out_specs=pl.BlockSpec((SUBLANES, tb), lambda i, ps: (0, i)),
        ),
        out_shape=jax.ShapeDtypeStruct((SUBLANES, n_tok), I32),
        compiler_params=_cparams(("arbitrary",)),
        name="dest",
    )(pstart, route_t)


def _row_copy(src_ref, src_row, dst_ref, dst_row, sem):
    return pltpu.make_async_copy(src_ref.at[pl.ds(src_row, 1), :], dst_ref.at[pl.ds(dst_row, 1), :], sem)


def _dispatch_body(pstart_ref, cnt_ref, nu_ref, h_hbm, d0_ref, d1_ref, xs_ref, hbuf, zero_ref, lsem, sems,
                   *, td, bm):
    i = pl.program_id(0)
    last = pl.num_programs(0) - 1
    g = td // SUBLANES
    cur = i % 3

    def load(tile, slot):
        return pltpu.make_async_copy(h_hbm.at[pl.ds(tile * g, g)], hbuf.at[slot], lsem.at[slot])

    @pl.when(i == 0)
    def _():
        load(0, 0).start()

    @pl.when(i < last)
    def _():
        load(i + 1, (i + 1) % 3).start()

    load(i, cur).wait()

    def issue(c, carry):
        for u in range(SUBLANES):
            r = c * SUBLANES + u
            for j, d_ref in enumerate((d0_ref, d1_ref)):
                pltpu.make_async_copy(hbuf.at[cur, c, pl.ds(u, 1), :], xs_ref.at[pl.ds(d_ref[r], 1), :],
                                      sems.at[cur]).start(priority=j)
        return carry

    lax.fori_loop(0, g, issue, 0)

    def wait_step(s):
        for _ in range(2):
            pltpu.make_async_copy(xs_ref.at[pl.ds(0, td), :], xs_ref.at[pl.ds(0, td), :], s).wait()

    @pl.when(i > 0)
    def _():
        wait_step(sems.at[(i + 2) % 3])

    @pl.when(i == last)
    def _():
        wait_step(sems.at[cur])
        zsem = sems.at[3]
        zero_ref[...] = jnp.zeros_like(zero_ref)

        def per_expert(e, carry):
            c = cnt_ref[e]
            n_pad = (bm - c % bm) % bm
            base = pstart_ref[e] + c

            def start(r, cc):
                _row_copy(zero_ref, 0, xs_ref, base + r, zsem).start()
                return cc

            def wait(r, cc):
                _row_copy(zero_ref, 0, xs_ref, 0, zsem).wait()
                return cc

            lax.fori_loop(0, n_pad, start, 0)
            lax.fori_loop(0, n_pad, wait, 0)
            return carry

        lax.fori_loop(0, N_EXPERTS, per_expert, 0)

        first = nu_ref[0] * (bm // SUBLANES)
        n_tail = xs_ref.shape[0] // SUBLANES - first

        def tail_copy(r):
            return pltpu.make_async_copy(
                zero_ref, xs_ref.at[pl.ds(pl.multiple_of((first + r) * SUBLANES, SUBLANES), SUBLANES), :], zsem)

        def tail_start(r, cc):
            tail_copy(r).start()
            return cc

        def tail_wait(r, cc):
            tail_copy(0).wait()
            return cc

        lax.fori_loop(0, n_tail, tail_start, 0)
        lax.fori_loop(0, n_tail, tail_wait, 0)


def _dispatch(h2, dest0, dest1, pstart, counts, n_used, n_slots):
    n_tok, d = h2.shape
    td = DISP_TD
    tok = pl.BlockSpec((td,), lambda i, ps, cn, nu: (i,), memory_space=pltpu.SMEM)
    return pl.pallas_call(
        functools.partial(_dispatch_body, td=td, bm=MOE_BM),
        grid_spec=pltpu.PrefetchScalarGridSpec(
            num_scalar_prefetch=3,
            grid=(n_tok // td,),
            in_specs=[pl.BlockSpec(memory_space=pl.ANY), tok, tok],
            out_specs=pl.BlockSpec(memory_space=pl.ANY),
            scratch_shapes=[pltpu.VMEM((3, td // SUBLANES, SUBLANES, d), h2.dtype),
                            pltpu.VMEM((SUBLANES, d), h2.dtype),
                            pltpu.SemaphoreType.DMA((3,)), pltpu.SemaphoreType.DMA((4,))],
        ),
        out_shape=jax.ShapeDtypeStruct((n_slots, d), h2.dtype),
        compiler_params=_cparams(("arbitrary",)),
        name="dispatch",
    )(pstart, counts, n_used, h2.reshape(n_tok // SUBLANES, SUBLANES, d), dest0, dest1)


def _experts_body(slot_ref, cast_ref, pre_ref, misc_ref, x_ref, wg_hbm, wu_hbm, wd_hbm, y_ref,
                  sg, su, sd, wgb, wub, wdb, sem):
    blk = pl.program_id(0)
    e0, pre0, n_used = misc_ref[0], misc_ref[1], misc_ref[2]

    def copies(e):
        return (pltpu.make_async_copy(wg_hbm.at[e], sg, sem.at[0]),
                pltpu.make_async_copy(wu_hbm.at[e], su, sem.at[1]),
                pltpu.make_async_copy(wd_hbm.at[e], sd, sem.at[2]))

    def fetch(e):
        for c in copies(e):
            c.start(priority=1)

    def land(slot):
        for c in copies(0):
            c.wait()
        wgb[slot] = sg[...].astype(wgb.dtype)
        wub[slot] = su[...].astype(wub.dtype)
        wdb[slot] = sd[...].astype(wdb.dtype)

    @pl.when(blk == 0)
    def _():
        fetch(e0)
        land(0)

        @pl.when(pre0 >= 0)
        def _():
            fetch(pre0)

    @pl.when(blk < n_used)
    def _():
        slot = slot_ref[blk]
        xb = _unpack_halves(x_ref[...], _MXU_DTYPE)
        a = _silu(_dot(xb, wgb[slot])) * _dot(xb, wub[slot])
        y_ref[...] = _dot(a.astype(_MXU_DTYPE), wdb[slot])

        @pl.when(cast_ref[blk] == 1)
        def _():
            land(1 - slot)

            @pl.when(pre_ref[blk] >= 0)
            def _():
                fetch(pre_ref[blk])

    @pl.when(blk >= n_used)
    def _():
        y_ref[...] = jnp.zeros_like(y_ref)


def _experts(xs, n_used, pends, w_gate, w_up, w_down):
    n_slots, dpk = xs.shape
    bm = MOE_BM
    n_blocks = n_slots // bm
    d, de = w_gate.shape[1:]
    last_used = n_used[0] - 1
    total = pends[-1]
    eid = jnp.arange(N_EXPERTS, dtype=I32)[None, :]
    owner = lambda start: jnp.minimum(jnp.sum((pends[None, :] <= start[:, None]).astype(I32), axis=1), N_EXPERTS - 1)
    end_of = lambda e: jnp.sum(jnp.where(eid == e[:, None], pends[None, :], 0), axis=1)
    be = owner(jnp.minimum(jnp.arange(n_blocks, dtype=I32), last_used) * bm)
    end_e = end_of(be)
    has_next = end_e < total
    nxt = owner(end_e)
    end_n = end_of(nxt)
    nxt2 = jnp.where(jnp.logical_and(has_next, end_n < total), owner(end_n), -1)
    is_last = jnp.concatenate([be[1:] != be[:-1], jnp.ones((1,), jnp.bool_)])
    is_last = jnp.logical_or(is_last, jnp.arange(n_blocks) >= last_used)
    cast = jnp.logical_and(is_last, has_next).astype(I32)
    pre = jnp.where(cast == 1, nxt2, -1).astype(I32)
    is_first = jnp.concatenate([jnp.ones((1,), jnp.bool_), be[1:] != be[:-1]])
    slot = ((jnp.cumsum(is_first.astype(I32)) - 1) % 2).astype(I32)
    misc = jnp.stack([be[0], jnp.where(has_next[0], nxt[0], -1), n_used[0]]).astype(I32)
    hbm = pl.BlockSpec(memory_space=pl.ANY)
    return pl.pallas_call(
        _experts_body,
        grid_spec=pltpu.PrefetchScalarGridSpec(
            num_scalar_prefetch=4,
            grid=(n_blocks,),
            in_specs=[pl.BlockSpec((bm, dpk), lambda j, sl, ca, pr, mi: (jnp.minimum(j, mi[2] - 1), 0)),
                      hbm, hbm, hbm],
            out_specs=pl.BlockSpec((bm, d), lambda j, sl, ca, pr, mi: (j, 0)),
            scratch_shapes=[pltpu.VMEM((d, de), w_gate.dtype), pltpu.VMEM((d, de), w_up.dtype),
                            pltpu.VMEM((de, d), w_down.dtype),
                            pltpu.VMEM((2, d, de), _MXU_DTYPE), pltpu.VMEM((2, d, de), _MXU_DTYPE),
                            pltpu.VMEM((2, de, d), _MXU_DTYPE), pltpu.SemaphoreType.DMA((3,))],
        ),
        out_shape=jax.ShapeDtypeStruct((n_slots, d), F32),
        compiler_params=_cparams(("arbitrary",)),
        name="experts",
    )(slot, cast, pre, misc, xs, w_gate, w_up, w_down)


def _combine_body(x1_ref, gate_ref, d0_ref, d1_ref, n0_ref, n1_ref, mod_ref, g_ref, ys_ref, o_ref, buf_ref, sem,
                  *, tf):
    i = pl.program_id(0)
    cur = i % 2

    def gather(da_ref, db_ref, half):
        def issue(c, carry):
            for u in range(SUBLANES):
                r = c * SUBLANES + u
                for j, d_ref in enumerate((da_ref, db_ref)):
                    pltpu.make_async_copy(ys_ref.at[pl.ds(d_ref[r], 1), :],
                                          buf_ref.at[half, j, c, pl.ds(u, 1), :], sem.at[half]).start(priority=j)
            return carry

        lax.fori_loop(0, tf // SUBLANES, issue, 0)

    @pl.when(i == 0)
    def _():
        gather(d0_ref, d1_ref, 0)

    @pl.when(i + 1 < pl.num_programs(0))
    def _():
        gather(n0_ref, n1_ref, 1 - cur)

    for _ in range(2):
        pltpu.make_async_copy(ys_ref.at[pl.ds(0, tf), :], ys_ref.at[pl.ds(0, tf), :], sem.at[cur]).wait()

    gates = gate_ref[...]
    d = o_ref.shape[1]
    moe = buf_ref[cur, 0].reshape(tf, d) * gates[:, 0:1] + buf_ref[cur, 1].reshape(tf, d) * gates[:, 1:2]
    x2 = x1_ref[...] + mod_ref[0, 5:6, :] * moe
    ms = jnp.mean(x2 * x2, axis=-1, keepdims=True)
    o_ref[...] = x2 * lax.rsqrt(ms + EPS) * g_ref[...]


def _combine(x1, gates, dest0, dest1, mod, g_final, ys, seq):
    n_tok, d = x1.shape
    tf = COMB_TF
    n_tiles = n_tok // tf
    tok = pl.BlockSpec((tf,), lambda i: (i,), memory_space=pltpu.SMEM)
    nxt = pl.BlockSpec((tf,), lambda i: (jnp.minimum(i + 1, n_tiles - 1),), memory_space=pltpu.SMEM)
    return pl.pallas_call(
        functools.partial(_combine_body, tf=tf),
        grid=(n_tiles,),
        in_specs=[pl.BlockSpec((tf, d), lambda i: (i, 0)),
                  pl.BlockSpec((tf, LANES), lambda i: (i, 0)),
                  tok, tok, nxt, nxt,
                  pl.BlockSpec((1, 6, d), lambda i: ((i * tf) // seq, 0, 0)),
                  pl.BlockSpec((1, d), lambda i: (0, 0)),
                  pl.BlockSpec(memory_space=pl.ANY)],
        out_specs=pl.BlockSpec((tf, d), lambda i: (i, 0)),
        scratch_shapes=[pltpu.VMEM((2, 2, tf // SUBLANES, SUBLANES, d), F32), pltpu.SemaphoreType.DMA((2,))],
        out_shape=jax.ShapeDtypeStruct((n_tok, d), F32),
        compiler_params=_cparams(("arbitrary",)),
        name="combine",
    )(x1, gates, dest0, dest1, dest0, dest1, mod, g_final, ys)


def kernel(x, c, w_ada, b_ada, g_mix, w_in, w_pool, pool_scale, rel_bias, w_out, g_ffn, w_group, b_group,
           w_router, b_router, w_gate, w_up, w_down, g_final):
    bsz, seq, d = x.shape
    n_tok = bsz * seq
    depth = w_ada.shape[0]
    assert depth == 1, "the final norm is fused into the only layer's combine"
    topk = min(TOPK_MAX, seq // 4)
    bias = _bias_tiles(rel_bias, ATT_T)
    bm = MOE_BM
    n_blocks = -(-(n_tok * 2) // bm) + N_EXPERTS
    n_slots = n_blocks * bm
    xt = x.reshape(n_tok, d)
    for i in range(depth):
        mod = _ada(c, w_ada[i], b_ada[i][None, :]).reshape(bsz, 6, d)
        u, q, k, vt, qi, kia, kib, wit = _proj(xt.reshape(bsz, seq, d), mod, g_mix[i][None, :], w_in[i])
        attn = _dsa(q, qi, wit, kia, kib, k, vt, bias, topk)
        w_r = jnp.concatenate(
            [w_group[i], w_router[i], jnp.zeros((d, LANES - N_GROUPS - N_EXPERTS), F32)], axis=1).astype(_MXU_DTYPE)
        b_r = jnp.concatenate(
            [b_group[i], b_router[i], jnp.zeros((LANES - N_GROUPS - N_EXPERTS,), F32)])[None, :]
        x1, h2, logits = _mix(u.reshape(n_tok, -1), attn.reshape(n_tok, -1), xt, mod,
                              w_pool[i].astype(_MXU_DTYPE), pool_scale[i][None, :],
                              w_out[i].astype(_MXU_DTYPE), g_ffn[i][None, :], w_r, b_r, seq)
        route_t, gates, cnt = _route(logits)
        counts = cnt[0, :N_EXPERTS].astype(I32)
        padded = (counts + bm - 1) // bm * bm
        pends = jnp.cumsum(padded)
        pstart = pends - padded
        n_used = (pends[-1:] // bm).astype(I32)
        dest = _dest(route_t, pstart)
        dest0, dest1 = dest[0], dest[1]
        xs = _dispatch(h2, dest0, dest1, pstart, counts, n_used, n_slots)
        ys = _experts(xs, n_used, pends, w_gate[i], w_up[i], w_down[i])
        xt = _combine(x1, gates, dest0, dest1, mod, g_final[None, :], ys, seq)
    return xt.reshape(bsz, seq, d)
```

```python
import functools
import math

import numpy as np
import jax
import jax.numpy as jnp
from jax import lax
from jax.experimental import pallas as pl
from jax.experimental.pallas import tpu as pltpu
from jax.experimental.pallas import tpu_sc as plsc

F32 = jnp.float32
I32 = jnp.int32
_MXU_DTYPE = jnp.bfloat16

POOL_WINDOWS = (2, 4, 8, 16)
N_POOL_GROUPS = 4
HEAD_DIM = 128
N_HEADS = 8
N_KV_HEADS = 2
HEAD_GROUP = N_HEADS // N_KV_HEADS
IDX_HEADS = 16
IDX_DIM = 64
TOPK_MAX = 256
N_BUCKETS = 32
MAX_DISTANCE = 128
N_GROUPS = 4
EXPERTS_PER_GROUP = 8
N_EXPERTS = N_GROUPS * EXPERTS_PER_GROUP
EPS = 1e-6
NEG_INF = -1e30
BIG = 3e38
LOG2E = 1.4426950408889634

LANES = 128
SUBLANES = 8
VMEM_LIMIT_BYTES = 56 * 1024 * 1024

ADA_TN = 1024
PROJ_TM = 256
ATT_T = 256
SCORE_SUB = 256
COUNT_RB = 256
MIX_TM = 256
HALO = 16
ROUTE_TR = 1024
DISP_TD = 256
MOE_BM = 256
COMB_TF = 256
DMA_UNROLL = 8
SEL_FIXED_ITERS = 18
SEL_MAX_ITERS = 80


def _cparams(sem):
    return pltpu.CompilerParams(dimension_semantics=sem, vmem_limit_bytes=VMEM_LIMIT_BYTES)


def _silu(x):
    return x * (1.0 / (1.0 + jnp.exp(-x)))


def _dot(a, b):
    return jnp.dot(a, b, preferred_element_type=F32)


def _dot_nt(a, b):
    return lax.dot_general(a, b, (((1,), (1,)), ((), ())), preferred_element_type=F32)


def _ada_body(c_ref, w_ref, b_ref, o_ref):
    s = _silu(c_ref[...])
    o_ref[...] = _dot(s.astype(_MXU_DTYPE), w_ref[...].astype(_MXU_DTYPE)) + b_ref[...]


def _ada(c, w, b):
    bsz, d = c.shape
    n = w.shape[1]
    return pl.pallas_call(
        _ada_body,
        grid=(n // ADA_TN,),
        in_specs=[pl.BlockSpec((bsz, d), lambda j: (0, 0)),
                  pl.BlockSpec((d, ADA_TN), lambda j: (0, j)),
                  pl.BlockSpec((1, ADA_TN), lambda j: (0, j))],
        out_specs=pl.BlockSpec((bsz, ADA_TN), lambda j: (0, j)),
        out_shape=jax.ShapeDtypeStruct((bsz, n), F32),
        compiler_params=_cparams(("arbitrary",)),
        name="ada",
    )(c, w, b)


def _rms_mod(x, g, shift, scale):
    ms = jnp.mean(x * x, axis=-1, keepdims=True)
    y = x * lax.rsqrt(ms + EPS) * g
    return y * (1.0 + scale) + shift


def _proj_body(x_ref, mod_ref, g_ref, w_ref, u_ref, q_ref, k_ref, vt_ref, qi_ref, kia_ref, kib_ref,
               wit_ref, *, cuts):
    h = _rms_mod(x_ref[0], g_ref[...], mod_ref[0, 0:1, :], mod_ref[0, 1:2, :])
    hb = h.astype(_MXU_DTYPE)

    def seg(name):
        lo, hi = cuts[name]
        return _dot(hb, w_ref[:, lo:hi])

    u_ref[0] = seg("u")
    q_ref[0] = seg("q").astype(q_ref.dtype)
    k_ref[0] = seg("k").astype(k_ref.dtype)
    vt_ref[0] = seg("v").T.astype(vt_ref.dtype)
    qi_ref[0] = seg("qi").astype(qi_ref.dtype)
    tail = seg("tail")
    tail = jnp.concatenate([tail, jnp.zeros((tail.shape[0], LANES - tail.shape[1]), F32)], axis=1)
    swapped = pltpu.roll(tail, LANES - IDX_DIM, 1)
    lane = lax.broadcasted_iota(I32, tail.shape, 1)
    kia_ref[0] = jnp.where(lane < IDX_DIM, tail, 0.0).astype(kia_ref.dtype)
    kib_ref[0] = jnp.where(lane >= LANES - IDX_DIM, swapped, 0.0).astype(kib_ref.dtype)
    wit_ref[0] = swapped.T[:IDX_HEADS, :]


def _proj(x, mod, g_mix, w_in):
    bsz, s, d = x.shape
    pool_w = d // 2
    attn_w = N_HEADS * HEAD_DIM
    kv_w = N_KV_HEADS * HEAD_DIM
    qi_w = IDX_HEADS * IDX_DIM
    c = np.cumsum([0, pool_w, attn_w, kv_w, kv_w, qi_w, IDX_DIM, IDX_HEADS])
    assert IDX_DIM + IDX_HEADS <= LANES and w_in.shape[1] == c[7]
    wp = w_in.astype(_MXU_DTYPE)
    cuts = {"u": (int(c[0]), int(c[1])), "q": (int(c[1]), int(c[2])), "k": (int(c[2]), int(c[3])),
            "v": (int(c[3]), int(c[4])), "qi": (int(c[4]), int(c[5])), "tail": (int(c[5]), int(c[7]))}
    tm = PROJ_TM
    pw = wp.shape[1]
    row = lambda w: pl.BlockSpec((1, tm, w), lambda b, i: (b, i, 0))
    out_shapes = [
        jax.ShapeDtypeStruct((bsz, s, pool_w), F32),
        jax.ShapeDtypeStruct((bsz, s, attn_w), _MXU_DTYPE),
        jax.ShapeDtypeStruct((bsz, s, kv_w), _MXU_DTYPE),
        jax.ShapeDtypeStruct((bsz, kv_w, s), _MXU_DTYPE),
        jax.ShapeDtypeStruct((bsz, s, qi_w), _MXU_DTYPE),
        jax.ShapeDtypeStruct((bsz, s, LANES), _MXU_DTYPE),
        jax.ShapeDtypeStruct((bsz, s, LANES), _MXU_DTYPE),
        jax.ShapeDtypeStruct((bsz, IDX_HEADS, s), F32),
    ]
    out_specs = [row(pool_w), row(attn_w), row(kv_w),
                 pl.BlockSpec((1, kv_w, tm), lambda b, i: (b, 0, i)),
                 row(qi_w), row(LANES), row(LANES),
                 pl.BlockSpec((1, IDX_HEADS, tm), lambda b, i: (b, 0, i))]
    return pl.pallas_call(
        functools.partial(_proj_body, cuts=cuts),
        grid=(bsz, s // tm),
        in_specs=[pl.BlockSpec((1, tm, d), lambda b, i: (b, i, 0)),
                  pl.BlockSpec((1, 6, d), lambda b, i: (b, 0, 0)),
                  pl.BlockSpec((1, d), lambda b, i: (0, 0)),
                  pl.BlockSpec((d, pw), lambda b, i: (0, 0))],
        out_specs=out_specs,
        out_shape=out_shapes,
        compiler_params=_cparams(("arbitrary", "arbitrary")),
        name="proj",
    )(x, mod, g_mix, wp)


def _bucket_starts():
    max_exact = N_BUCKETS // 2
    d = np.arange(1, 4 * MAX_DISTANCE, dtype=np.float32)
    large = max_exact + (np.log(d / np.float32(max_exact)) / np.float32(math.log(MAX_DISTANCE / max_exact))
                         * np.float32(N_BUCKETS - max_exact)).astype(np.int32)
    large = np.minimum(large, N_BUCKETS - 1)
    bucket = np.where(d < max_exact, d.astype(np.int32), large)
    bucket = np.concatenate([[0], bucket])
    starts = [int(np.argmax(bucket >= b)) for b in range(N_BUCKETS)]
    assert all(np.all(bucket[starts[b]:] >= b) for b in range(N_BUCKETS))
    assert starts[N_BUCKETS - 1] <= MAX_DISTANCE
    return starts


def _bias_body(rb_ref, o_ref, *, t, starts):
    diff = pl.program_id(0)
    s_l = lax.broadcasted_iota(I32, (t, t), 0)
    t_l = lax.broadcasted_iota(I32, (t, t), 1)
    dist = diff * t + t_l - s_l
    for h in range(N_HEADS):
        val = jnp.full((t, t), rb_ref[0, h], F32)
        for b in range(1, N_BUCKETS):
            val = jnp.where(dist >= starts[b], rb_ref[b, h], val)
        hh = h % HEAD_GROUP
        o_ref[0, h // HEAD_GROUP, :, hh * t:(hh + 1) * t] = (val - rb_ref[N_BUCKETS - 1, h]) * LOG2E


def _bias_tiles(rel_bias, t):
    assert t >= MAX_DISTANCE
    return pl.pallas_call(
        functools.partial(_bias_body, t=t, starts=_bucket_starts()),
        grid=(2,),
        in_specs=[pl.BlockSpec(memory_space=pltpu.SMEM)],
        out_specs=pl.BlockSpec((1, N_KV_HEADS, t, HEAD_GROUP * t), lambda i: (i, 0, 0, 0)),
        out_shape=jax.ShapeDtypeStruct((2, N_KV_HEADS, t, HEAD_GROUP * t), F32),
        compiler_params=_cparams(("arbitrary",)),
        name="bias_tiles",
    )(rel_bias)


def _dsa_body(q_ref, qi_ref, wit_ref, kia_ref, kib_ref, k_ref, vt_ref, bias_ref, o_ref,
              score_ref, mask_ref, acc_ref, *, t, topk):
    i = pl.program_id(1)
    nk = i + 1
    t_glob = i * t + lax.broadcasted_iota(I32, (1, t), 1)
    idx_scale = (IDX_DIM ** -0.5) * (IDX_HEADS ** -0.5)

    def score_chunk(c, carry):
        r0 = pl.multiple_of(c * SCORE_SUB, SCORE_SUB)
        ka = kia_ref[0, pl.ds(r0, SCORE_SUB), :]
        kb = kib_ref[0, pl.ds(r0, SCORE_SUB), :]
        acc = jnp.zeros((SCORE_SUB, t), F32)
        for j in range(IDX_HEADS // 2):
            qp = qi_ref[0, :, j * LANES:(j + 1) * LANES]
            da = _dot_nt(ka, qp)
            db = _dot_nt(kb, qp)
            acc = acc + jnp.maximum(da, 0.0) * wit_ref[0, 2 * j:2 * j + 1, :]
            acc = acc + jnp.maximum(db, 0.0) * wit_ref[0, 2 * j + 1:2 * j + 2, :]
        sc = acc * idx_scale
        s_glob = r0 + lax.broadcasted_iota(I32, (SCORE_SUB, 1), 0)
        score_ref[pl.ds(r0, SCORE_SUB), :] = jnp.where(s_glob <= t_glob, sc, NEG_INF)
        return carry

    lax.fori_loop(0, nk * (t // SCORE_SUB), score_chunk, 0)

    nblk = nk * (t // COUNT_RB)
    fold = lambda m: jnp.sum(m.reshape(COUNT_RB // SUBLANES, SUBLANES, t), axis=0)

    def blk(r):
        return score_ref[pl.ds(pl.multiple_of(r * COUNT_RB, COUNT_RB), COUNT_RB), :]

    def count_ge(c):
        def body(r, acc):
            return acc + fold((blk(r) >= c).astype(I32))
        acc = lax.fori_loop(0, nblk, body, jnp.zeros((SUBLANES, t), I32))
        return jnp.sum(acc, axis=0, keepdims=True)

    def stats_body(r, carry):
        cnt, mn, mx = carry
        b = blk(r)
        real = b > NEG_INF
        cnt = cnt + fold(real.astype(I32))
        mn = jnp.minimum(mn, jnp.min(jnp.where(real, b, BIG).reshape(COUNT_RB // SUBLANES, SUBLANES, t), axis=0))
        mx = jnp.maximum(mx, jnp.max(b.reshape(COUNT_RB // SUBLANES, SUBLANES, t), axis=0))
        return cnt, mn, mx

    cnt8, mn8, mx8 = lax.fori_loop(
        0, nblk, stats_body,
        (jnp.zeros((SUBLANES, t), I32), jnp.full((SUBLANES, t), BIG, F32), jnp.full((SUBLANES, t), -BIG, F32)))
    n_real = jnp.sum(cnt8, axis=0, keepdims=True)
    rmin = jnp.min(mn8, axis=0, keepdims=True)
    rmax = jnp.max(mx8, axis=0, keepdims=True)
    n_max = count_ge(rmax)
    degenerate = jnp.logical_and(n_real >= topk, n_max >= topk)
    lo0 = jnp.where(n_real < topk, NEG_INF, jnp.where(degenerate, rmax, rmin))
    hi0 = jnp.where(degenerate, BIG, rmax)
    done0 = jnp.where(n_real <= topk, 1, 0)

    def bisect(st):
        lo, hi, done, stuck = st
        c = lo + (hi - lo) * 0.5
        active = (done + stuck) == 0
        has_mid = jnp.logical_and(c > lo, c < hi)
        n = count_ge(c)
        upd = jnp.logical_and(active, has_mid)
        lo = jnp.where(jnp.logical_and(upd, n >= topk), c, lo)
        hi = jnp.where(jnp.logical_and(upd, n < topk), c, hi)
        done = jnp.where(jnp.logical_and(upd, n == topk), 1, done)
        stuck = jnp.where(jnp.logical_and(active, jnp.logical_not(has_mid)), 1, stuck)
        return lo, hi, done, stuck

    st = lax.fori_loop(0, SEL_FIXED_ITERS, lambda _, s: bisect(s), (lo0, hi0, done0, jnp.zeros((1, t), I32)))

    def sel_cond(s):
        return jnp.logical_and(s[0] < SEL_MAX_ITERS, jnp.min(s[3] + s[4]) == 0)

    _, lo, hi, done, _ = lax.while_loop(sel_cond, lambda s: (s[0] + 1,) + bisect(s[1:]), (jnp.int32(0),) + st)

    def write_mask(r, cut):
        r0 = pl.multiple_of(r * COUNT_RB, COUNT_RB)
        b = score_ref[pl.ds(r0, COUNT_RB), :]
        s_glob = r0 + lax.broadcasted_iota(I32, (COUNT_RB, 1), 0)
        sel = jnp.logical_and(b >= lo, s_glob <= t_glob)
        if cut is not None:
            sel = jnp.logical_and(sel, jnp.logical_or(b >= hi, s_glob < cut))
        mask_ref[pl.ds(r0, COUNT_RB), :] = jnp.concatenate([jnp.where(sel, 0.0, NEG_INF)] * HEAD_GROUP, axis=1)

    any_tied = jnp.min(done) == 0

    @pl.when(jnp.logical_not(any_tied))
    def _():
        def body(r, carry):
            write_mask(r, None)
            return carry
        lax.fori_loop(0, nblk, body, 0)

    @pl.when(any_tied)
    def _():
        need = topk - count_ge(hi)

        def count_tie_below(x):
            def body(r, acc):
                r0 = pl.multiple_of(r * COUNT_RB, COUNT_RB)
                b = score_ref[pl.ds(r0, COUNT_RB), :]
                s_glob = r0 + lax.broadcasted_iota(I32, (COUNT_RB, 1), 0)
                m = jnp.logical_and(jnp.logical_and(b >= lo, b < hi), s_glob < x)
                return acc + fold(m.astype(I32))
            acc = lax.fori_loop(0, nblk, body, jnp.zeros((SUBLANES, t), I32))
            return jnp.sum(acc, axis=0, keepdims=True)

        nbits = int(score_ref.shape[0]).bit_length()

        def bit_body(bi, x):
            c = x + jnp.left_shift(jnp.int32(1), nbits - 1 - bi)
            return jnp.where(count_tie_below(c) < need, c, x)

        x = lax.fori_loop(0, nbits, bit_body, jnp.zeros((1, t), I32))
        cut = jnp.where(done > 0, jnp.int32(2 ** 30), x + 1)

        def body(r, carry):
            write_mask(r, cut)
            return carry
        lax.fori_loop(0, nblk, body, 0)

    scale = HEAD_DIM ** -0.5
    gt = HEAD_GROUP * t

    def col_reduce(op, a):
        part = op(a.reshape(a.shape[0] // SUBLANES, SUBLANES, a.shape[1]), axis=0)
        return op(part, axis=0, keepdims=True)

    qgs = [jnp.concatenate(
        [q_ref[0, :, h * HEAD_DIM:(h + 1) * HEAD_DIM] for h in range(g * HEAD_GROUP, (g + 1) * HEAD_GROUP)],
        axis=0) for g in range(N_KV_HEADS)]
    acc_ref[...] = jnp.zeros_like(acc_ref)

    def att_chunk(kc, carry, near):
        r0 = pl.multiple_of(kc * t, t)
        mk = mask_ref[pl.ds(r0, t), :]
        new = []
        for g in range(N_KV_HEADS):
            m, l = carry[g]
            kk = k_ref[0, pl.ds(r0, t), g * HEAD_DIM:(g + 1) * HEAD_DIM]
            s = _dot_nt(kk, qgs[g]) * (scale * LOG2E) + mk
            if near:
                s = s + bias_ref[i - kc, g]
            m_new = jnp.maximum(m, col_reduce(jnp.max, s))
            alpha = jnp.exp2(m - m_new)
            p = jnp.exp2(s - m_new)
            l = alpha * l + col_reduce(jnp.sum, p)
            vt = vt_ref[0, g * HEAD_DIM:(g + 1) * HEAD_DIM, pl.ds(r0, t)]
            acc_ref[g] = alpha * acc_ref[g] + _dot(vt, p.astype(_MXU_DTYPE))
            new.append((m_new, l))
        return tuple(new)

    init = (jnp.full((1, gt), NEG_INF, F32), jnp.zeros((1, gt), F32))
    n_far = jnp.maximum(i - 1, 0)
    far = lax.fori_loop(0, n_far, functools.partial(att_chunk, near=False), (init,) * N_KV_HEADS)
    fin = lax.fori_loop(n_far, nk, functools.partial(att_chunk, near=True), far)
    for g in range(N_KV_HEADS):
        out = acc_ref[g] / fin[g][1]
        for hh in range(HEAD_GROUP):
            h = g * HEAD_GROUP + hh
            o_ref[0, :, h * HEAD_DIM:(h + 1) * HEAD_DIM] = out[:, hh * t:(hh + 1) * t].T.astype(o_ref.dtype)


def _dsa(q, qi, wit, kia, kib, k, vt, bias, topk):
    bsz, s, aw = q.shape
    t = ATT_T
    assert s % t == 0 and topk <= t
    row = lambda w: pl.BlockSpec((1, t, w), lambda b, i: (b, i, 0))
    full = lambda a: pl.BlockSpec((1,) + a.shape[1:], lambda b, i: (b, 0, 0))
    return pl.pallas_call(
        functools.partial(_dsa_body, t=t, topk=topk),
        grid=(bsz, s // t),
        in_specs=[row(aw), row(qi.shape[2]),
                  pl.BlockSpec((1, IDX_HEADS, t), lambda b, i: (b, 0, i)),
                  full(kia), full(kib), full(k), full(vt),
                  pl.BlockSpec(bias.shape, lambda b, i: (0, 0, 0, 0))],
        out_specs=row(aw),
        out_shape=jax.ShapeDtypeStruct((bsz, s, aw), _MXU_DTYPE),
        scratch_shapes=[pltpu.VMEM((s, t), F32), pltpu.VMEM((s, HEAD_GROUP * t), F32),
                        pltpu.VMEM((N_KV_HEADS, HEAD_DIM, HEAD_GROUP * t), F32)],
        compiler_params=_cparams(("arbitrary", "arbitrary")),
        name="dsa",
    )(q, qi, wit, kia, kib, k, vt, bias)


def _pack_halves(hb):
    assert jnp.dtype(hb.dtype).itemsize == 2
    n = hb.shape[1] // 2
    hi = lax.bitcast_convert_type(hb[:, :n].astype(F32), I32)
    lo = lax.bitcast_convert_type(hb[:, n:].astype(F32), I32)
    return jnp.bitwise_or(hi, lax.shift_right_logical(lo, jnp.full(lo.shape, 16, I32)))


def _unpack_halves(w, dtype):
    hi = lax.bitcast_convert_type(jnp.bitwise_and(w, jnp.int32(-65536)), F32).astype(dtype)
    lo = lax.bitcast_convert_type(jnp.left_shift(w, 16), F32).astype(dtype)
    return jnp.concatenate([hi, lo], axis=1)


def _mix_body(u_ref, halo_ref, a_ref, x_ref, mod_ref, wpool_ref, ps_ref, wout_ref, g_ref, wr_ref, br_ref,
              x1_ref, h2_ref, lg_ref, *, seq, tm):
    t0 = (pl.program_id(0) * tm) % seq
    u = u_ref[...]
    halo = jnp.where(t0 == 0, 0.0, halo_ref[...])
    ext = jnp.concatenate([halo, u], axis=0)
    pos = t0 + lax.broadcasted_iota(I32, (tm, 1), 0)
    gw = u.shape[1] // N_POOL_GROUPS
    ys = []
    for g, w in enumerate(POOL_WINDOWS):
        a = ext[:, g * gw:(g + 1) * gw]
        sft = 1
        while sft < w:
            a = a + pltpu.roll(a, sft, 0)
            sft *= 2
        cnt = jnp.minimum(pos + 1, w).astype(F32)
        p = a[HALO:, :] / cnt - u[:, g * gw:(g + 1) * gw]
        ys.append(_dot(p.astype(_MXU_DTYPE), wpool_ref[g]) * ps_ref[:, g * gw:(g + 1) * gw])
    pool = jnp.concatenate(ys, axis=1).astype(_MXU_DTYPE)
    pw = pool.shape[1]
    mixed = _dot(pool, wout_ref[:pw, :]) + _dot(a_ref[...], wout_ref[pw:, :])
    x1 = x_ref[...] + mod_ref[0, 2:3, :] * mixed
    x1_ref[...] = x1
    h2 = _rms_mod(x1, g_ref[...], mod_ref[0, 3:4, :], mod_ref[0, 4:5, :]).astype(_MXU_DTYPE)
    h2_ref[...] = _pack_halves(h2)
    lg_ref[...] = _dot(h2, wr_ref[...]) + br_ref[...]


def _mix(u, attn, x, mod, w_pool, pool_scale, w_out, g_ffn, w_r, b_r, seq):
    n_tok, d = x.shape
    tm = MIX_TM
    assert seq % tm == 0 and tm % HALO == 0
    pw = u.shape[1]
    aw = attn.shape[1]
    rows = lambda w: pl.BlockSpec((tm, w), lambda i: (i, 0))
    const = lambda a: pl.BlockSpec(a.shape, lambda i: (0,) * a.ndim)
    return pl.pallas_call(
        functools.partial(_mix_body, seq=seq, tm=tm),
        grid=(n_tok // tm,),
        in_specs=[rows(pw),
                  pl.BlockSpec((HALO, pw), lambda i: (jnp.maximum(i * (tm // HALO) - 1, 0), 0)),
                  rows(aw), rows(d),
                  pl.BlockSpec((1, 6, d), lambda i: ((i * tm) // seq, 0, 0)),
                  const(w_pool), const(pool_scale), const(w_out), const(g_ffn), const(w_r), const(b_r)],
        out_specs=[rows(d), rows(d // 2), rows(LANES)],
        out_shape=[jax.ShapeDtypeStruct((n_tok, d), F32), jax.ShapeDtypeStruct((n_tok, d // 2), I32),
                   jax.ShapeDtypeStruct((n_tok, LANES), F32)],
        compiler_params=_cparams(("arbitrary",)),
        name="mix",
    )(u, u, attn, x, mod, w_pool, pool_scale, w_out, g_ffn, w_r, b_r)


def _route_body(lg_ref, rt_ref, gate_ref, cnt_ref, run_ref):
    @pl.when(pl.program_id(0) == 0)
    def _():
        run_ref[...] = jnp.zeros_like(run_ref)

    lg = lg_ref[...]
    tr = lg.shape[0]
    lane = lax.broadcasted_iota(I32, lg.shape, 1)
    gmask = lane < N_GROUPS
    gl = jnp.where(gmask, lg, -BIG)
    gmax = jnp.max(gl, axis=-1, keepdims=True)
    gsel = jnp.min(jnp.where(jnp.logical_and(gmask, gl == gmax), lane, LANES), axis=-1, keepdims=True)
    gsum = jnp.sum(jnp.where(gmask, jnp.exp(gl - gmax), 0.0), axis=-1, keepdims=True)
    p_g = 1.0 / gsum
    in_e = jnp.logical_and(lane >= N_GROUPS, lane < N_GROUPS + N_EXPERTS)
    emask = jnp.logical_and(in_e, lax.shift_right_arithmetic(lane - N_GROUPS, EXPERTS_PER_GROUP.bit_length() - 1) == gsel)
    el = jnp.where(emask, lg, -BIG)
    emax = jnp.max(el, axis=-1, keepdims=True)
    ex = jnp.where(emask, jnp.exp(el - emax), 0.0)
    ep = ex / jnp.sum(ex, axis=-1, keepdims=True)
    p1 = jnp.max(jnp.where(emask, ep, -1.0), axis=-1, keepdims=True)
    i1 = jnp.min(jnp.where(jnp.logical_and(emask, ep == p1), lane, LANES), axis=-1, keepdims=True)
    m2 = jnp.logical_and(emask, lane != i1)
    p2 = jnp.max(jnp.where(m2, ep, -1.0), axis=-1, keepdims=True)
    i2 = jnp.min(jnp.where(jnp.logical_and(m2, ep == p2), lane, LANES), axis=-1, keepdims=True)
    den = p1 + p2
    g1 = p_g * p1 / den
    g2 = p_g * p2 / den
    e1 = i1 - N_GROUPS
    e2 = i2 - N_GROUPS

    oh = jnp.logical_or(lane == e1, lane == e2)
    ri = lax.broadcasted_iota(I32, (tr, tr), 0)
    ci = lax.broadcasted_iota(I32, (tr, tr), 1)
    ltri = (ci < ri).astype(_MXU_DTYPE)
    before = _dot(ltri, oh.astype(_MXU_DTYPE)) + run_ref[...]
    r1 = jnp.sum(jnp.where(lane == e1, before, 0.0), axis=-1, keepdims=True)
    r2 = jnp.sum(jnp.where(lane == e2, before, 0.0), axis=-1, keepdims=True)
    run = run_ref[...] + jnp.sum(oh.astype(F32), axis=0, keepdims=True)
    run_ref[...] = run
    cnt_ref[...] = jnp.broadcast_to(run, cnt_ref.shape)

    route = jnp.where(lane == 0, e1.astype(F32),
                      jnp.where(lane == 1, e2.astype(F32),
                                jnp.where(lane == 2, r1, jnp.where(lane == 3, r2, 0.0))))
    rt_ref[...] = route.T[:SUBLANES, :].astype(I32)
    gate_ref[...] = jnp.where(lane == 0, g1, jnp.where(lane == 1, g2, 0.0))


def _route(logits):
    n_tok = logits.shape[0]
    tr = min(ROUTE_TR, n_tok)
    return pl.pallas_call(
        _route_body,
        grid=(n_tok // tr,),
        in_specs=[pl.BlockSpec((tr, LANES), lambda i: (i, 0))],
        out_specs=[pl.BlockSpec((SUBLANES, tr), lambda i: (0, i)),
                   pl.BlockSpec((tr, LANES), lambda i: (i, 0)),
                   pl.BlockSpec((SUBLANES, LANES), lambda i: (0, 0))],
        out_shape=[jax.ShapeDtypeStruct((SUBLANES, n_tok), I32),
                   jax.ShapeDtypeStruct((n_tok, LANES), F32),
                   jax.ShapeDtypeStruct((SUBLANES, LANES), F32)],
        scratch_shapes=[pltpu.VMEM((1, LANES), F32)],
        compiler_params=_cparams(("arbitrary",)),
        name="route",
    )(logits)


def _dest_body(ps_ref, rt_ref, o_ref):
    r = rt_ref[...]
    start = jnp.zeros_like(r)
    for e in range(N_EXPERTS):
        start = jnp.where(r == e, ps_ref[e], start)
    o_ref[...] = start + pltpu.roll(r, SUBLANES - 2, 0)


def _dest(route_t, pstart):
    n_tok = route_t.shape[1]
    tb = min(n_tok, 2048)
    return pl.pallas_call(
        _dest_body,
        grid_spec=pltpu.PrefetchScalarGridSpec(
            num_scalar_prefetch=1,
            grid=(n_tok // tb,),
            in_specs=[pl.BlockSpec((SUBLANES, tb), lambda i, ps: (0, i))],
            out_specs=pl.BlockSpec((SUBLANES, tb), lambda i, ps: (0, i)),
        ),
        out_shape=jax.ShapeDtypeStruct((SUBLANES, n_tok), I32),
        compiler_params=_cparams(("arbitrary",)),
        name="dest",
    )(pstart, route_t)


def _row_copy(src_ref, src_row, dst_ref, dst_row, sem):
    return pltpu.make_async_copy(src_ref.at[pl.ds(src_row, 1), :], dst_ref.at[pl.ds(dst_row, 1), :], sem)


def _dispatch_body(pstart_ref, cnt_ref, nu_ref, h_hbm, d0_ref, d1_ref, xs_ref, hbuf, zero_ref, lsem, sems,
                   *, td, bm):
    i = pl.program_id(0)
    last = pl.num_programs(0) - 1
    g = td // SUBLANES
    cur = i % 3

    def load(tile, slot):
        return pltpu.make_async_copy(h_hbm.at[pl.ds(tile * g, g)], hbuf.at[slot], lsem.at[slot])

    @pl.when(i == 0)
    def _():
        load(0, 0).start()

    @pl.when(i < last)
    def _():
        load(i + 1, (i + 1) % 3).start()

    load(i, cur).wait()

    def issue(c, carry):
        for u in range(SUBLANES):
            r = c * SUBLANES + u
            for j, d_ref in enumerate((d0_ref, d1_ref)):
                pltpu.make_async_copy(hbuf.at[cur, c, pl.ds(u, 1), :], xs_ref.at[pl.ds(d_ref[r], 1), :],
                                      sems.at[cur]).start(priority=j)
        return carry

    lax.fori_loop(0, g, issue, 0)

    def wait_step(s):
        for _ in range(2):
            pltpu.make_async_copy(xs_ref.at[pl.ds(0, td), :], xs_ref.at[pl.ds(0, td), :], s).wait()

    @pl.when(i > 0)
    def _():
        wait_step(sems.at[(i + 2) % 3])

    @pl.when(i == last)
    def _():
        wait_step(sems.at[cur])
        zsem = sems.at[3]
        zero_ref[...] = jnp.zeros_like(zero_ref)

        def per_expert(e, carry):
            c = cnt_ref[e]
            n_pad = (bm - c % bm) % bm
            base = pstart_ref[e] + c

            def start(r, cc):
                _row_copy(zero_ref, 0, xs_ref, base + r, zsem).start()
                return cc

            def wait(r, cc):
                _row_copy(zero_ref, 0, xs_ref, 0, zsem).wait()
                return cc

            lax.fori_loop(0, n_pad, start, 0)
            lax.fori_loop(0, n_pad, wait, 0)
            return carry

        lax.fori_loop(0, N_EXPERTS, per_expert, 0)

        first = nu_ref[0] * (bm // SUBLANES)
        n_tail = xs_ref.shape[0] // SUBLANES - first

        def tail_copy(r):
            return pltpu.make_async_copy(
                zero_ref, xs_ref.at[pl.ds(pl.multiple_of((first + r) * SUBLANES, SUBLANES), SUBLANES), :], zsem)

        def tail_start(r, cc):
            tail_copy(r).start()
            return cc

        def tail_wait(r, cc):
            tail_copy(0).wait()
            return cc

        lax.fori_loop(0, n_tail, tail_start, 0)
        lax.fori_loop(0, n_tail, tail_wait, 0)


def _dispatch(h2, dest0, dest1, pstart, counts, n_used, n_slots):
    n_tok, d = h2.shape
    td = DISP_TD
    tok = pl.BlockSpec((td,), lambda i, ps, cn, nu: (i,), memory_space=pltpu.SMEM)
    return pl.pallas_call(
        functools.partial(_dispatch_body, td=td, bm=MOE_BM),
        grid_spec=pltpu.PrefetchScalarGridSpec(
            num_scalar_prefetch=3,
            grid=(n_tok // td,),
            in_specs=[pl.BlockSpec(memory_space=pl.ANY), tok, tok],
            out_specs=pl.BlockSpec(memory_space=pl.ANY),
            scratch_shapes=[pltpu.VMEM((3, td // SUBLANES, SUBLANES, d), h2.dtype),
                            pltpu.VMEM((SUBLANES, d), h2.dtype),
                            pltpu.SemaphoreType.DMA((3,)), pltpu.SemaphoreType.DMA((4,))],
        ),
        out_shape=jax.ShapeDtypeStruct((n_slots, d), h2.dtype),
        compiler_params=_cparams(("arbitrary",)),
        name="dispatch",
    )(pstart, counts, n_used, h2.reshape(n_tok // SUBLANES, SUBLANES, d), dest0, dest1)


SC_LANES = 16
SC_WIN = 32
SC_CHUNK = 2048


def _sc_dispatch(h2, dest0, dest1, n_slots):
    n_tok, d = h2.shape
    info = plsc.get_sparse_core_info()
    nc, ns = info.num_cores, info.num_subcores
    nw = nc * ns
    assert info.num_lanes == SC_LANES and n_slots % (nw * SC_WIN) == 0 and n_tok % SC_CHUNK == 0
    assert n_slots < 3 * n_tok
    per_w = n_slots // nw
    mesh = plsc.VectorSubcoreMesh(core_axis_name="c", subcore_axis_name="s")

    @functools.partial(
        pl.kernel, mesh=mesh,
        out_type=jax.ShapeDtypeStruct((n_slots, d), h2.dtype),
        scratch_types=[pltpu.VMEM((n_slots,), I32), pltpu.VMEM((SC_CHUNK,), I32),
                       pltpu.VMEM((SC_WIN, d), h2.dtype), pltpu.SemaphoreType.DMA],
        compiler_params=pltpu.CompilerParams(needs_layout_passes=False),
    )
    def k(h_hbm, d0_hbm, d1_hbm, o_hbm, table, chunk, rows, sem):
        lane = lax.iota(I32, SC_LANES)

        @pl.loop(0, n_slots, step=SC_LANES)
        def _(s):
            v = s + lane
            v = jnp.where(v >= n_tok, v - n_tok, v)
            v = jnp.where(v >= n_tok, v - n_tok, v)
            table[pl.ds(s, SC_LANES)] = v

        for d_hbm in (d0_hbm, d1_hbm):
            @pl.loop(0, n_tok, step=SC_CHUNK)
            def _(t0, d_hbm=d_hbm):
                pltpu.sync_copy(d_hbm.at[pl.ds(t0, SC_CHUNK)], chunk)

                @pl.loop(0, SC_CHUNK, step=SC_LANES)
                def _(j):
                    plsc.store_scatter(table, [chunk[pl.ds(j, SC_LANES)]], t0 + j + lane)

        base = (lax.axis_index("s") * nc + lax.axis_index("c")) * per_w

        @pl.loop(0, per_w, step=SC_WIN)
        def _(r):
            pltpu.async_copy(h_hbm.at[table.at[pl.ds(base + r, SC_WIN)]], rows, sem).wait()
            pltpu.sync_copy(rows, o_hbm.at[pl.ds(base + r, SC_WIN)])

    return k(h2, dest0, dest1)


def _experts_body(slot_ref, cast_ref, pre_ref, misc_ref, x_ref, wg_hbm, wu_hbm, wd_hbm, y_ref,
                  sg, su, sd, wgb, wub, wdb, sem):
    blk = pl.program_id(0)
    e0, pre0, n_used = misc_ref[0], misc_ref[1], misc_ref[2]

    def copies(e):
        return (pltpu.make_async_copy(wg_hbm.at[e], sg, sem.at[0]),
                pltpu.make_async_copy(wu_hbm.at[e], su, sem.at[1]),
                pltpu.make_async_copy(wd_hbm.at[e], sd, sem.at[2]))

    def fetch(e):
        for c in copies(e):
            c.start(priority=1)

    def land(slot):
        for c in copies(0):
            c.wait()
        wgb[slot] = sg[...].astype(wgb.dtype)
        wub[slot] = su[...].astype(wub.dtype)
        wdb[slot] = sd[...].astype(wdb.dtype)

    @pl.when(blk == 0)
    def _():
        fetch(e0)
        land(0)

        @pl.when(pre0 >= 0)
        def _():
            fetch(pre0)

    @pl.when(blk < n_used)
    def _():
        slot = slot_ref[blk]
        xb = _unpack_halves(x_ref[...], _MXU_DTYPE)
        a = _silu(_dot(xb, wgb[slot])) * _dot(xb, wub[slot])
        y_ref[...] = _dot(a.astype(_MXU_DTYPE), wdb[slot])

        @pl.when(cast_ref[blk] == 1)
        def _():
            land(1 - slot)

            @pl.when(pre_ref[blk] >= 0)
            def _():
                fetch(pre_ref[blk])

    @pl.when(blk >= n_used)
    def _():
        y_ref[...] = jnp.zeros_like(y_ref)


def _experts(xs, n_used, pends, w_gate, w_up, w_down):
    n_slots, dpk = xs.shape
    bm = MOE_BM
    n_blocks = n_slots // bm
    d, de = w_gate.shape[1:]
    last_used = n_used[0] - 1
    total = pends[-1]
    eid = jnp.arange(N_EXPERTS, dtype=I32)[None, :]
    owner = lambda start: jnp.minimum(jnp.sum((pends[None, :] <= start[:, None]).astype(I32), axis=1), N_EXPERTS - 1)
    end_of = lambda e: jnp.sum(jnp.where(eid == e[:, None], pends[None, :], 0), axis=1)
    be = owner(jnp.minimum(jnp.arange(n_blocks, dtype=I32), last_used) * bm)
    end_e = end_of(be)
    has_next = end_e < total
    nxt = owner(end_e)
    end_n = end_of(nxt)
    nxt2 = jnp.where(jnp.logical_and(has_next, end_n < total), owner(end_n), -1)
    is_last = jnp.concatenate([be[1:] != be[:-1], jnp.ones((1,), jnp.bool_)])
    is_last = jnp.logical_or(is_last, jnp.arange(n_blocks) >= last_used)
    cast = jnp.logical_and(is_last, has_next).astype(I32)
    pre = jnp.where(cast == 1, nxt2, -1).astype(I32)
    is_first = jnp.concatenate([jnp.ones((1,), jnp.bool_), be[1:] != be[:-1]])
    slot = ((jnp.cumsum(is_first.astype(I32)) - 1) % 2).astype(I32)
    misc = jnp.stack([be[0], jnp.where(has_next[0], nxt[0], -1), n_used[0]]).astype(I32)
    hbm = pl.BlockSpec(memory_space=pl.ANY)
    return pl.pallas_call(
        _experts_body,
        grid_spec=pltpu.PrefetchScalarGridSpec(
            num_scalar_prefetch=4,
            grid=(n_blocks,),
            in_specs=[pl.BlockSpec((bm, dpk), lambda j, sl, ca, pr, mi: (jnp.minimum(j, mi[2] - 1), 0)),
                      hbm, hbm, hbm],
            out_specs=pl.BlockSpec((bm, d), lambda j, sl, ca, pr, mi: (j, 0)),
            scratch_shapes=[pltpu.VMEM((d, de), w_gate.dtype), pltpu.VMEM((d, de), w_up.dtype),
                            pltpu.VMEM((de, d), w_down.dtype),
                            pltpu.VMEM((2, d, de), _MXU_DTYPE), pltpu.VMEM((2, d, de), _MXU_DTYPE),
                            pltpu.VMEM((2, de, d), _MXU_DTYPE), pltpu.SemaphoreType.DMA((3,))],
        ),
        out_shape=jax.ShapeDtypeStruct((n_slots, d), F32),
        compiler_params=_cparams(("arbitrary",)),
        name="experts",
    )(slot, cast, pre, misc, xs, w_gate, w_up, w_down)


def _combine_body(x1_ref, gate_ref, d0_ref, d1_ref, n0_ref, n1_ref, mod_ref, g_ref, ys_ref, o_ref, buf_ref, sem,
                  *, tf):
    i = pl.program_id(0)
    cur = i % 2

    def gather(da_ref, db_ref, half):
        def issue(c, carry):
            for u in range(SUBLANES):
                r = c * SUBLANES + u
                for j, d_ref in enumerate((da_ref, db_ref)):
                    pltpu.make_async_copy(ys_ref.at[pl.ds(d_ref[r], 1), :],
                                          buf_ref.at[half, j, c, pl.ds(u, 1), :], sem.at[half]).start(priority=j)
            return carry

        lax.fori_loop(0, tf // SUBLANES, issue, 0)

    @pl.when(i == 0)
    def _():
        gather(d0_ref, d1_ref, 0)

    @pl.when(i + 1 < pl.num_programs(0))
    def _():
        gather(n0_ref, n1_ref, 1 - cur)

    for _ in range(2):
        pltpu.make_async_copy(ys_ref.at[pl.ds(0, tf), :], ys_ref.at[pl.ds(0, tf), :], sem.at[cur]).wait()

    gates = gate_ref[...]
    d = o_ref.shape[1]
    moe = buf_ref[cur, 0].reshape(tf, d) * gates[:, 0:1] + buf_ref[cur, 1].reshape(tf, d) * gates[:, 1:2]
    x2 = x1_ref[...] + mod_ref[0, 5:6, :] * moe
    ms = jnp.mean(x2 * x2, axis=-1, keepdims=True)
    o_ref[...] = x2 * lax.rsqrt(ms + EPS) * g_ref[...]


def _combine(x1, gates, dest0, dest1, mod, g_final, ys, seq):
    n_tok, d = x1.shape
    tf = COMB_TF
    n_tiles = n_tok // tf
    tok = pl.BlockSpec((tf,), lambda i: (i,), memory_space=pltpu.SMEM)
    nxt = pl.BlockSpec((tf,), lambda i: (jnp.minimum(i + 1, n_tiles - 1),), memory_space=pltpu.SMEM)
    return pl.pallas_call(
        functools.partial(_combine_body, tf=tf),
        grid=(n_tiles,),
        in_specs=[pl.BlockSpec((tf, d), lambda i: (i, 0)),
                  pl.BlockSpec((tf, LANES), lambda i: (i, 0)),
                  tok, tok, nxt, nxt,
                  pl.BlockSpec((1, 6, d), lambda i: ((i * tf) // seq, 0, 0)),
                  pl.BlockSpec((1, d), lambda i: (0, 0)),
                  pl.BlockSpec(memory_space=pl.ANY)],
        out_specs=pl.BlockSpec((tf, d), lambda i: (i, 0)),
        scratch_shapes=[pltpu.VMEM((2, 2, tf // SUBLANES, SUBLANES, d), F32), pltpu.SemaphoreType.DMA((2,))],
        out_shape=jax.ShapeDtypeStruct((n_tok, d), F32),
        compiler_params=_cparams(("arbitrary",)),
        name="combine",
    )(x1, gates, dest0, dest1, dest0, dest1, mod, g_final, ys)


def kernel(x, c, w_ada, b_ada, g_mix, w_in, w_pool, pool_scale, rel_bias, w_out, g_ffn, w_group, b_group,
           w_router, b_router, w_gate, w_up, w_down, g_final):
    bsz, seq, d = x.shape
    n_tok = bsz * seq
    depth = w_ada.shape[0]
    assert depth == 1, "the final norm is fused into the only layer's combine"
    topk = min(TOPK_MAX, seq // 4)
    bias = _bias_tiles(rel_bias, ATT_T)
    bm = MOE_BM
    n_blocks = -(-(n_tok * 2) // bm) + N_EXPERTS
    n_slots = n_blocks * bm
    xt = x.reshape(n_tok, d)
    for i in range(depth):
        mod = _ada(c, w_ada[i], b_ada[i][None, :]).reshape(bsz, 6, d)
        u, q, k, vt, qi, kia, kib, wit = _proj(xt.reshape(bsz, seq, d), mod, g_mix[i][None, :], w_in[i])
        attn = _dsa(q, qi, wit, kia, kib, k, vt, bias, topk)
        w_r = jnp.concatenate(
            [w_group[i], w_router[i], jnp.zeros((d, LANES - N_GROUPS - N_EXPERTS), F32)], axis=1).astype(_MXU_DTYPE)
        b_r = jnp.concatenate(
            [b_group[i], b_router[i], jnp.zeros((LANES - N_GROUPS - N_EXPERTS,), F32)])[None, :]
        x1, h2, logits = _mix(u.reshape(n_tok, -1), attn.reshape(n_tok, -1), xt, mod,
                              w_pool[i].astype(_MXU_DTYPE), pool_scale[i][None, :],
                              w_out[i].astype(_MXU_DTYPE), g_ffn[i][None, :], w_r, b_r, seq)
        route_t, gates, cnt = _route(logits)
        counts = cnt[0, :N_EXPERTS].astype(I32)
        padded = (counts + bm - 1) // bm * bm
        pends = jnp.cumsum(padded)
        pstart = pends - padded
        n_used = (pends[-1:] // bm).astype(I32)
        dest = _dest(route_t, pstart)
        dest0, dest1 = dest[0], dest[1]
        xs = _sc_dispatch(h2, dest0, dest1, n_slots)
        ys = _experts(xs, n_used, pends, w_gate[i], w_up[i], w_down[i])
        xt = _combine(x1, gates, dest0, dest1, mod, g_final[None, :], ys, seq)
    return xt.reshape(bsz, seq, d)
```

```python
import functools
import math

import numpy as np
import jax
import jax.numpy as jnp
from jax import lax
from jax.experimental import pallas as pl
from jax.experimental.pallas import tpu as pltpu
from jax.experimental.pallas import tpu_sc as plsc

F32 = jnp.float32
I32 = jnp.int32
_MXU_DTYPE = jnp.bfloat16

POOL_WINDOWS = (2, 4, 8, 16)
N_POOL_GROUPS = 4
HEAD_DIM = 128
N_HEADS = 8
N_KV_HEADS = 2
HEAD_GROUP = N_HEADS // N_KV_HEADS
IDX_HEADS = 16
IDX_DIM = 64
TOPK_MAX = 256
N_BUCKETS = 32
MAX_DISTANCE = 128
N_GROUPS = 4
EXPERTS_PER_GROUP = 8
N_EXPERTS = N_GROUPS * EXPERTS_PER_GROUP
EPS = 1e-6
NEG_INF = -1e30
BIG = 3e38
LOG2E = 1.4426950408889634

LANES = 128
SUBLANES = 8
VMEM_LIMIT_BYTES = 56 * 1024 * 1024

ADA_TN = 1024
PROJ_TM = 256
ATT_T = 256
SCORE_SUB = 256
COUNT_RB = 256
MIX_TM = 256
HALO = 16
ROUTE_TR = 1024
DISP_TD = 256
MOE_BM = 256
COMB_TF = 256
DMA_UNROLL = 8
SEL_FIXED_ITERS = 18
SEL_MAX_ITERS = 80


def _cparams(sem):
    return pltpu.CompilerParams(dimension_semantics=sem, vmem_limit_bytes=VMEM_LIMIT_BYTES)


def _silu(x):
    return x * (1.0 / (1.0 + jnp.exp(-x)))


def _dot(a, b):
    return jnp.dot(a, b, preferred_element_type=F32)


def _dot_nt(a, b):
    return lax.dot_general(a, b, (((1,), (1,)), ((), ())), preferred_element_type=F32)


def _ada_body(c_ref, w_ref, b_ref, o_ref):
    s = _silu(c_ref[...])
    o_ref[...] = _dot(s.astype(_MXU_DTYPE), w_ref[...].astype(_MXU_DTYPE)) + b_ref[...]


def _ada(c, w, b):
    bsz, d = c.shape
    n = w.shape[1]
    return pl.pallas_call(
        _ada_body,
        grid=(n // ADA_TN,),
        in_specs=[pl.BlockSpec((bsz, d), lambda j: (0, 0)),
                  pl.BlockSpec((d, ADA_TN), lambda j: (0, j)),
                  pl.BlockSpec((1, ADA_TN), lambda j: (0, j))],
        out_specs=pl.BlockSpec((bsz, ADA_TN), lambda j: (0, j)),
        out_shape=jax.ShapeDtypeStruct((bsz, n), F32),
        compiler_params=_cparams(("arbitrary",)),
        name="ada",
    )(c, w, b)


def _rms_mod(x, g, shift, scale):
    ms = jnp.mean(x * x, axis=-1, keepdims=True)
    y = x * lax.rsqrt(ms + EPS) * g
    return y * (1.0 + scale) + shift


def _proj_body(x_ref, mod_ref, g_ref, w_ref, u_ref, q_ref, k_ref, vt_ref, qi_ref, kia_ref, kib_ref,
               wit_ref, *, cuts):
    h = _rms_mod(x_ref[0], g_ref[...], mod_ref[0, 0:1, :], mod_ref[0, 1:2, :])
    hb = h.astype(_MXU_DTYPE)

    def seg(name):
        lo, hi = cuts[name]
        return _dot(hb, w_ref[:, lo:hi])

    u_ref[0] = seg("u")
    q_ref[0] = seg("q").astype(q_ref.dtype)
    k_ref[0] = seg("k").astype(k_ref.dtype)
    vt_ref[0] = seg("v").T.astype(vt_ref.dtype)
    qi_ref[0] = seg("qi").astype(qi_ref.dtype)
    tail = seg("tail")
    tail = jnp.concatenate([tail, jnp.zeros((tail.shape[0], LANES - tail.shape[1]), F32)], axis=1)
    swapped = pltpu.roll(tail, LANES - IDX_DIM, 1)
    lane = lax.broadcasted_iota(I32, tail.shape, 1)
    kia_ref[0] = jnp.where(lane < IDX_DIM, tail, 0.0).astype(kia_ref.dtype)
    kib_ref[0] = jnp.where(lane >= LANES - IDX_DIM, swapped, 0.0).astype(kib_ref.dtype)
    wit_ref[0] = swapped.T[:IDX_HEADS, :]


def _proj(x, mod, g_mix, w_in):
    bsz, s, d = x.shape
    pool_w = d // 2
    attn_w = N_HEADS * HEAD_DIM
    kv_w = N_KV_HEADS * HEAD_DIM
    qi_w = IDX_HEADS * IDX_DIM
    c = np.cumsum([0, pool_w, attn_w, kv_w, kv_w, qi_w, IDX_DIM, IDX_HEADS])
    assert IDX_DIM + IDX_HEADS <= LANES and w_in.shape[1] == c[7]
    wp = w_in.astype(_MXU_DTYPE)
    cuts = {"u": (int(c[0]), int(c[1])), "q": (int(c[1]), int(c[2])), "k": (int(c[2]), int(c[3])),
            "v": (int(c[3]), int(c[4])), "qi": (int(c[4]), int(c[5])), "tail": (int(c[5]), int(c[7]))}
    tm = PROJ_TM
    pw = wp.shape[1]
    row = lambda w: pl.BlockSpec((1, tm, w), lambda b, i: (b, i, 0))
    out_shapes = [
        jax.ShapeDtypeStruct((bsz, s, pool_w), F32),
        jax.ShapeDtypeStruct((bsz, s, attn_w), _MXU_DTYPE),
        jax.ShapeDtypeStruct((bsz, s, kv_w), _MXU_DTYPE),
        jax.ShapeDtypeStruct((bsz, kv_w, s), _MXU_DTYPE),
        jax.ShapeDtypeStruct((bsz, s, qi_w), _MXU_DTYPE),
        jax.ShapeDtypeStruct((bsz, s, LANES), _MXU_DTYPE),
        jax.ShapeDtypeStruct((bsz, s, LANES), _MXU_DTYPE),
        jax.ShapeDtypeStruct((bsz, IDX_HEADS, s), F32),
    ]
    out_specs = [row(pool_w), row(attn_w), row(kv_w),
                 pl.BlockSpec((1, kv_w, tm), lambda b, i: (b, 0, i)),
                 row(qi_w), row(LANES), row(LANES),
                 pl.BlockSpec((1, IDX_HEADS, tm), lambda b, i: (b, 0, i))]
    return pl.pallas_call(
        functools.partial(_proj_body, cuts=cuts),
        grid=(bsz, s // tm),
        in_specs=[pl.BlockSpec((1, tm, d), lambda b, i: (b, i, 0)),
                  pl.BlockSpec((1, 6, d), lambda b, i: (b, 0, 0)),
                  pl.BlockSpec((1, d), lambda b, i: (0, 0)),
                  pl.BlockSpec((d, pw), lambda b, i: (0, 0))],
        out_specs=out_specs,
        out_shape=out_shapes,
        compiler_params=_cparams(("arbitrary", "arbitrary")),
        name="proj",
    )(x, mod, g_mix, wp)


def _bucket_starts():
    max_exact = N_BUCKETS // 2
    d = np.arange(1, 4 * MAX_DISTANCE, dtype=np.float32)
    large = max_exact + (np.log(d / np.float32(max_exact)) / np.float32(math.log(MAX_DISTANCE / max_exact))
                         * np.float32(N_BUCKETS - max_exact)).astype(np.int32)
    large = np.minimum(large, N_BUCKETS - 1)
    bucket = np.where(d < max_exact, d.astype(np.int32), large)
    bucket = np.concatenate([[0], bucket])
    starts = [int(np.argmax(bucket >= b)) for b in range(N_BUCKETS)]
    assert all(np.all(bucket[starts[b]:] >= b) for b in range(N_BUCKETS))
    assert starts[N_BUCKETS - 1] <= MAX_DISTANCE
    return starts


def _bias_body(rb_ref, o_ref, *, t, starts):
    diff = pl.program_id(0)
    s_l = lax.broadcasted_iota(I32, (t, t), 0)
    t_l = lax.broadcasted_iota(I32, (t, t), 1)
    dist = diff * t + t_l - s_l
    for h in range(N_HEADS):
        val = jnp.full((t, t), rb_ref[0, h], F32)
        for b in range(1, N_BUCKETS):
            val = jnp.where(dist >= starts[b], rb_ref[b, h], val)
        hh = h % HEAD_GROUP
        o_ref[0, h // HEAD_GROUP, :, hh * t:(hh + 1) * t] = (val - rb_ref[N_BUCKETS - 1, h]) * LOG2E


def _bias_tiles(rel_bias, t):
    assert t >= MAX_DISTANCE
    return pl.pallas_call(
        functools.partial(_bias_body, t=t, starts=_bucket_starts()),
        grid=(2,),
        in_specs=[pl.BlockSpec(memory_space=pltpu.SMEM)],
        out_specs=pl.BlockSpec((1, N_KV_HEADS, t, HEAD_GROUP * t), lambda i: (i, 0, 0, 0)),
        out_shape=jax.ShapeDtypeStruct((2, N_KV_HEADS, t, HEAD_GROUP * t), F32),
        compiler_params=_cparams(("arbitrary",)),
        name="bias_tiles",
    )(rel_bias)


def _dsa_body(q_ref, qi_ref, wit_ref, kia_ref, kib_ref, k_ref, vt_ref, bias_ref, o_ref,
              score_ref, mask_ref, acc_ref, *, t, topk):
    i = pl.program_id(1)
    nk = i + 1
    t_glob = i * t + lax.broadcasted_iota(I32, (1, t), 1)
    idx_scale = (IDX_DIM ** -0.5) * (IDX_HEADS ** -0.5)

    def score_chunk(c, carry):
        r0 = pl.multiple_of(c * SCORE_SUB, SCORE_SUB)
        ka = kia_ref[0, pl.ds(r0, SCORE_SUB), :]
        kb = kib_ref[0, pl.ds(r0, SCORE_SUB), :]
        acc = jnp.zeros((SCORE_SUB, t), F32)
        for j in range(IDX_HEADS // 2):
            qp = qi_ref[0, :, j * LANES:(j + 1) * LANES]
            da = _dot_nt(ka, qp)
            db = _dot_nt(kb, qp)
            acc = acc + jnp.maximum(da, 0.0) * wit_ref[0, 2 * j:2 * j + 1, :]
            acc = acc + jnp.maximum(db, 0.0) * wit_ref[0, 2 * j + 1:2 * j + 2, :]
        sc = acc * idx_scale
        s_glob = r0 + lax.broadcasted_iota(I32, (SCORE_SUB, 1), 0)
        score_ref[pl.ds(r0, SCORE_SUB), :] = jnp.where(s_glob <= t_glob, sc, NEG_INF)
        return carry

    lax.fori_loop(0, nk * (t // SCORE_SUB), score_chunk, 0)

    nblk = nk * (t // COUNT_RB)
    fold = lambda m: jnp.sum(m.reshape(COUNT_RB // SUBLANES, SUBLANES, t), axis=0)

    def blk(r):
        return score_ref[pl.ds(pl.multiple_of(r * COUNT_RB, COUNT_RB), COUNT_RB), :]

    def count_ge(c):
        def body(r, acc):
            return acc + fold((blk(r) >= c).astype(I32))
        acc = lax.fori_loop(0, nblk, body, jnp.zeros((SUBLANES, t), I32))
        return jnp.sum(acc, axis=0, keepdims=True)

    def stats_body(r, carry):
        cnt, mn, mx = carry
        b = blk(r)
        real = b > NEG_INF
        cnt = cnt + fold(real.astype(I32))
        mn = jnp.minimum(mn, jnp.min(jnp.where(real, b, BIG).reshape(COUNT_RB // SUBLANES, SUBLANES, t), axis=0))
        mx = jnp.maximum(mx, jnp.max(b.reshape(COUNT_RB // SUBLANES, SUBLANES, t), axis=0))
        return cnt, mn, mx

    cnt8, mn8, mx8 = lax.fori_loop(
        0, nblk, stats_body,
        (jnp.zeros((SUBLANES, t), I32), jnp.full((SUBLANES, t), BIG, F32), jnp.full((SUBLANES, t), -BIG, F32)))
    n_real = jnp.sum(cnt8, axis=0, keepdims=True)
    rmin = jnp.min(mn8, axis=0, keepdims=True)
    rmax = jnp.max(mx8, axis=0, keepdims=True)
    n_max = count_ge(rmax)
    degenerate = jnp.logical_and(n_real >= topk, n_max >= topk)
    lo0 = jnp.where(n_real < topk, NEG_INF, jnp.where(degenerate, rmax, rmin))
    hi0 = jnp.where(degenerate, BIG, rmax)
    done0 = jnp.where(n_real <= topk, 1, 0)

    def bisect(st):
        lo, hi, done, stuck = st
        c = lo + (hi - lo) * 0.5
        active = (done + stuck) == 0
        has_mid = jnp.logical_and(c > lo, c < hi)
        n = count_ge(c)
        upd = jnp.logical_and(active, has_mid)
        lo = jnp.where(jnp.logical_and(upd, n >= topk), c, lo)
        hi = jnp.where(jnp.logical_and(upd, n < topk), c, hi)
        done = jnp.where(jnp.logical_and(upd, n == topk), 1, done)
        stuck = jnp.where(jnp.logical_and(active, jnp.logical_not(has_mid)), 1, stuck)
        return lo, hi, done, stuck

    st = lax.fori_loop(0, SEL_FIXED_ITERS, lambda _, s: bisect(s), (lo0, hi0, done0, jnp.zeros((1, t), I32)))

    def sel_cond(s):
        return jnp.logical_and(s[0] < SEL_MAX_ITERS, jnp.min(s[3] + s[4]) == 0)

    _, lo, hi, done, _ = lax.while_loop(sel_cond, lambda s: (s[0] + 1,) + bisect(s[1:]), (jnp.int32(0),) + st)

    def write_mask(r, cut):
        r0 = pl.multiple_of(r * COUNT_RB, COUNT_RB)
        b = score_ref[pl.ds(r0, COUNT_RB), :]
        s_glob = r0 + lax.broadcasted_iota(I32, (COUNT_RB, 1), 0)
        sel = jnp.logical_and(b >= lo, s_glob <= t_glob)
        if cut is not None:
            sel = jnp.logical_and(sel, jnp.logical_or(b >= hi, s_glob < cut))
        mask_ref[pl.ds(r0, COUNT_RB), :] = jnp.concatenate([jnp.where(sel, 0.0, NEG_INF)] * HEAD_GROUP, axis=1)

    any_tied = jnp.min(done) == 0

    @pl.when(jnp.logical_not(any_tied))
    def _():
        def body(r, carry):
            write_mask(r, None)
            return carry
        lax.fori_loop(0, nblk, body, 0)

    @pl.when(any_tied)
    def _():
        need = topk - count_ge(hi)

        def count_tie_below(x):
            def body(r, acc):
                r0 = pl.multiple_of(r * COUNT_RB, COUNT_RB)
                b = score_ref[pl.ds(r0, COUNT_RB), :]
                s_glob = r0 + lax.broadcasted_iota(I32, (COUNT_RB, 1), 0)
                m = jnp.logical_and(jnp.logical_and(b >= lo, b < hi), s_glob < x)
                return acc + fold(m.astype(I32))
            acc = lax.fori_loop(0, nblk, body, jnp.zeros((SUBLANES, t), I32))
            return jnp.sum(acc, axis=0, keepdims=True)

        nbits = int(score_ref.shape[0]).bit_length()

        def bit_body(bi, x):
            c = x + jnp.left_shift(jnp.int32(1), nbits - 1 - bi)
            return jnp.where(count_tie_below(c) < need, c, x)

        x = lax.fori_loop(0, nbits, bit_body, jnp.zeros((1, t), I32))
        cut = jnp.where(done > 0, jnp.int32(2 ** 30), x + 1)

        def body(r, carry):
            write_mask(r, cut)
            return carry
        lax.fori_loop(0, nblk, body, 0)

    scale = HEAD_DIM ** -0.5
    gt = HEAD_GROUP * t

    def col_reduce(op, a):
        part = op(a.reshape(a.shape[0] // SUBLANES, SUBLANES, a.shape[1]), axis=0)
        return op(part, axis=0, keepdims=True)

    qgs = [jnp.concatenate(
        [q_ref[0, :, h * HEAD_DIM:(h + 1) * HEAD_DIM] for h in range(g * HEAD_GROUP, (g + 1) * HEAD_GROUP)],
        axis=0) for g in range(N_KV_HEADS)]
    acc_ref[...] = jnp.zeros_like(acc_ref)

    def att_chunk(kc, carry, near):
        r0 = pl.multiple_of(kc * t, t)
        mk = mask_ref[pl.ds(r0, t), :]
        new = []
        for g in range(N_KV_HEADS):
            m, l = carry[g]
            kk = k_ref[0, pl.ds(r0, t), g * HEAD_DIM:(g + 1) * HEAD_DIM]
            s = _dot_nt(kk, qgs[g]) * (scale * LOG2E) + mk
            if near:
                s = s + bias_ref[i - kc, g]
            m_new = jnp.maximum(m, col_reduce(jnp.max, s))
            alpha = jnp.exp2(m - m_new)
            p = jnp.exp2(s - m_new)
            l = alpha * l + col_reduce(jnp.sum, p)
            vt = vt_ref[0, g * HEAD_DIM:(g + 1) * HEAD_DIM, pl.ds(r0, t)]
            acc_ref[g] = alpha * acc_ref[g] + _dot(vt, p.astype(_MXU_DTYPE))
            new.append((m_new, l))
        return tuple(new)

    init = (jnp.full((1, gt), NEG_INF, F32), jnp.zeros((1, gt), F32))
    n_far = jnp.maximum(i - 1, 0)
    far = lax.fori_loop(0, n_far, functools.partial(att_chunk, near=False), (init,) * N_KV_HEADS)
    fin = lax.fori_loop(n_far, nk, functools.partial(att_chunk, near=True), far)
    for g in range(N_KV_HEADS):
        out = acc_ref[g] / fin[g][1]
        for hh in range(HEAD_GROUP):
            h = g * HEAD_GROUP + hh
            o_ref[0, :, h * HEAD_DIM:(h + 1) * HEAD_DIM] = out[:, hh * t:(hh + 1) * t].T.astype(o_ref.dtype)


def _dsa(q, qi, wit, kia, kib, k, vt, bias, topk):
    bsz, s, aw = q.shape
    t = ATT_T
    assert s % t == 0 and topk <= t
    row = lambda w: pl.BlockSpec((1, t, w), lambda b, i: (b, i, 0))
    full = lambda a: pl.BlockSpec((1,) + a.shape[1:], lambda b, i: (b, 0, 0))
    return pl.pallas_call(
        functools.partial(_dsa_body, t=t, topk=topk),
        grid=(bsz, s // t),
        in_specs=[row(aw), row(qi.shape[2]),
                  pl.BlockSpec((1, IDX_HEADS, t), lambda b, i: (b, 0, i)),
                  full(kia), full(kib), full(k), full(vt),
                  pl.BlockSpec(bias.shape, lambda b, i: (0, 0, 0, 0))],
        out_specs=row(aw),
        out_shape=jax.ShapeDtypeStruct((bsz, s, aw), _MXU_DTYPE),
        scratch_shapes=[pltpu.VMEM((s, t), F32), pltpu.VMEM((s, HEAD_GROUP * t), F32),
                        pltpu.VMEM((N_KV_HEADS, HEAD_DIM, HEAD_GROUP * t), F32)],
        compiler_params=_cparams(("arbitrary", "arbitrary")),
        name="dsa",
    )(q, qi, wit, kia, kib, k, vt, bias)


def _pack_halves(hb):
    assert jnp.dtype(hb.dtype).itemsize == 2
    n = hb.shape[1] // 2
    hi = lax.bitcast_convert_type(hb[:, :n].astype(F32), I32)
    lo = lax.bitcast_convert_type(hb[:, n:].astype(F32), I32)
    return jnp.bitwise_or(hi, lax.shift_right_logical(lo, jnp.full(lo.shape, 16, I32)))


def _unpack_halves(w, dtype):
    hi = lax.bitcast_convert_type(jnp.bitwise_and(w, jnp.int32(-65536)), F32).astype(dtype)
    lo = lax.bitcast_convert_type(jnp.left_shift(w, 16), F32).astype(dtype)
    return jnp.concatenate([hi, lo], axis=1)


def _mix_body(u_ref, halo_ref, a_ref, x_ref, mod_ref, wpool_ref, ps_ref, wout_ref, g_ref, wr_ref, br_ref,
              x1_ref, h2_ref, lg_ref, *, seq, tm):
    t0 = (pl.program_id(0) * tm) % seq
    u = u_ref[...]
    halo = jnp.where(t0 == 0, 0.0, halo_ref[...])
    ext = jnp.concatenate([halo, u], axis=0)
    pos = t0 + lax.broadcasted_iota(I32, (tm, 1), 0)
    gw = u.shape[1] // N_POOL_GROUPS
    ys = []
    for g, w in enumerate(POOL_WINDOWS):
        a = ext[:, g * gw:(g + 1) * gw]
        sft = 1
        while sft < w:
            a = a + pltpu.roll(a, sft, 0)
            sft *= 2
        cnt = jnp.minimum(pos + 1, w).astype(F32)
        p = a[HALO:, :] / cnt - u[:, g * gw:(g + 1) * gw]
        ys.append(_dot(p.astype(_MXU_DTYPE), wpool_ref[g]) * ps_ref[:, g * gw:(g + 1) * gw])
    pool = jnp.concatenate(ys, axis=1).astype(_MXU_DTYPE)
    pw = pool.shape[1]
    mixed = _dot(pool, wout_ref[:pw, :]) + _dot(a_ref[...], wout_ref[pw:, :])
    x1 = x_ref[...] + mod_ref[0, 2:3, :] * mixed
    x1_ref[...] = x1
    h2 = _rms_mod(x1, g_ref[...], mod_ref[0, 3:4, :], mod_ref[0, 4:5, :]).astype(_MXU_DTYPE)
    h2_ref[...] = _pack_halves(h2)
    lg_ref[...] = _dot(h2, wr_ref[...]) + br_ref[...]


def _mix(u, attn, x, mod, w_pool, pool_scale, w_out, g_ffn, w_r, b_r, seq):
    n_tok, d = x.shape
    tm = MIX_TM
    assert seq % tm == 0 and tm % HALO == 0
    pw = u.shape[1]
    aw = attn.shape[1]
    rows = lambda w: pl.BlockSpec((tm, w), lambda i: (i, 0))
    const = lambda a: pl.BlockSpec(a.shape, lambda i: (0,) * a.ndim)
    return pl.pallas_call(
        functools.partial(_mix_body, seq=seq, tm=tm),
        grid=(n_tok // tm,),
        in_specs=[rows(pw),
                  pl.BlockSpec((HALO, pw), lambda i: (jnp.maximum(i * (tm // HALO) - 1, 0), 0)),
                  rows(aw), rows(d),
                  pl.BlockSpec((1, 6, d), lambda i: ((i * tm) // seq, 0, 0)),
                  const(w_pool), const(pool_scale), const(w_out), const(g_ffn), const(w_r), const(b_r)],
        out_specs=[rows(d), rows(d // 2), rows(LANES)],
        out_shape=[jax.ShapeDtypeStruct((n_tok, d), F32), jax.ShapeDtypeStruct((n_tok, d // 2), I32),
                   jax.ShapeDtypeStruct((n_tok, LANES), F32)],
        compiler_params=_cparams(("arbitrary",)),
        name="mix",
    )(u, u, attn, x, mod, w_pool, pool_scale, w_out, g_ffn, w_r, b_r)


def _route_body(lg_ref, rt_ref, gate_ref, cnt_ref, run_ref):
    @pl.when(pl.program_id(0) == 0)
    def _():
        run_ref[...] = jnp.zeros_like(run_ref)

    lg = lg_ref[...]
    tr = lg.shape[0]
    lane = lax.broadcasted_iota(I32, lg.shape, 1)
    gmask = lane < N_GROUPS
    gl = jnp.where(gmask, lg, -BIG)
    gmax = jnp.max(gl, axis=-1, keepdims=True)
    gsel = jnp.min(jnp.where(jnp.logical_and(gmask, gl == gmax), lane, LANES), axis=-1, keepdims=True)
    gsum = jnp.sum(jnp.where(gmask, jnp.exp(gl - gmax), 0.0), axis=-1, keepdims=True)
    p_g = 1.0 / gsum
    in_e = jnp.logical_and(lane >= N_GROUPS, lane < N_GROUPS + N_EXPERTS)
    emask = jnp.logical_and(in_e, lax.shift_right_arithmetic(lane - N_GROUPS, EXPERTS_PER_GROUP.bit_length() - 1) == gsel)
    el = jnp.where(emask, lg, -BIG)
    emax = jnp.max(el, axis=-1, keepdims=True)
    ex = jnp.where(emask, jnp.exp(el - emax), 0.0)
    ep = ex / jnp.sum(ex, axis=-1, keepdims=True)
    p1 = jnp.max(jnp.where(emask, ep, -1.0), axis=-1, keepdims=True)
    i1 = jnp.min(jnp.where(jnp.logical_and(emask, ep == p1), lane, LANES), axis=-1, keepdims=True)
    m2 = jnp.logical_and(emask, lane != i1)
    p2 = jnp.max(jnp.where(m2, ep, -1.0), axis=-1, keepdims=True)
    i2 = jnp.min(jnp.where(jnp.logical_and(m2, ep == p2), lane, LANES), axis=-1, keepdims=True)
    den = p1 + p2
    g1 = p_g * p1 / den
    g2 = p_g * p2 / den
    e1 = i1 - N_GROUPS
    e2 = i2 - N_GROUPS

    oh = jnp.logical_or(lane == e1, lane == e2)
    ri = lax.broadcasted_iota(I32, (tr, tr), 0)
    ci = lax.broadcasted_iota(I32, (tr, tr), 1)
    ltri = (ci < ri).astype(_MXU_DTYPE)
    before = _dot(ltri, oh.astype(_MXU_DTYPE)) + run_ref[...]
    r1 = jnp.sum(jnp.where(lane == e1, before, 0.0), axis=-1, keepdims=True)
    r2 = jnp.sum(jnp.where(lane == e2, before, 0.0), axis=-1, keepdims=True)
    run = run_ref[...] + jnp.sum(oh.astype(F32), axis=0, keepdims=True)
    run_ref[...] = run
    cnt_ref[...] = jnp.broadcast_to(run, cnt_ref.shape)

    route = jnp.where(lane == 0, e1.astype(F32),
                      jnp.where(lane == 1, e2.astype(F32),
                                jnp.where(lane == 2, r1, jnp.where(lane == 3, r2, 0.0))))
    rt_ref[...] = route.T[:SUBLANES, :].astype(I32)
    gate_ref[...] = jnp.where(lane == 0, g1, jnp.where(lane == 1, g2, 0.0))


def _route(logits):
    n_tok = logits.shape[0]
    tr = min(ROUTE_TR, n_tok)
    return pl.pallas_call(
        _route_body,
        grid=(n_tok // tr,),
        in_specs=[pl.BlockSpec((tr, LANES), lambda i: (i, 0))],
        out_specs=[pl.BlockSpec((SUBLANES, tr), lambda i: (0, i)),
                   pl.BlockSpec((tr, LANES), lambda i: (i, 0)),
                   pl.BlockSpec((SUBLANES, LANES), lambda i: (0, 0))],
        out_shape=[jax.ShapeDtypeStruct((SUBLANES, n_tok), I32),
                   jax.ShapeDtypeStruct((n_tok, LANES), F32),
                   jax.ShapeDtypeStruct((SUBLANES, LANES), F32)],
        scratch_shapes=[pltpu.VMEM((1, LANES), F32)],
        compiler_params=_cparams(("arbitrary",)),
        name="route",
    )(logits)


def _dest_body(ps_ref, rt_ref, o_ref):
    r = rt_ref[...]
    start = jnp.zeros_like(r)
    for e in range(N_EXPERTS):
        start = jnp.where(r == e, ps_ref[e], start)
    o_ref[...] = start + pltpu.roll(r, SUBLANES - 2, 0)


def _dest(route_t, pstart):
    n_tok = route_t.shape[1]
    tb = min(n_tok, 2048)
    return pl.pallas_call(
        _dest_body,
        grid_spec=pltpu.PrefetchScalarGridSpec(
            num_scalar_prefetch=1,
            grid=(n_tok // tb,),
            in_specs=[pl.BlockSpec((SUBLANES, tb), lambda i, ps: (0, i))],
            out_specs=pl.BlockSpec((SUBLANES, tb), lambda i, ps: (0, i)),
        ),
        out_shape=jax.ShapeDtypeStruct((SUBLANES, n_tok), I32),
        compiler_params=_cparams(("arbitrary",)),
        name="dest",
    )(pstart, route_t)


def _row_copy(src_ref, src_row, dst_ref, dst_row, sem):
    return pltpu.make_async_copy(src_ref.at[pl.ds(src_row, 1), :], dst_ref.at[pl.ds(dst_row, 1), :], sem)


def _dispatch_body(pstart_ref, cnt_ref, nu_ref, h_hbm, d0_ref, d1_ref, xs_ref, hbuf, zero_ref, lsem, sems,
                   *, td, bm):
    i = pl.program_id(0)
    last = pl.num_programs(0) - 1
    g = td // SUBLANES
    cur = i % 3

    def load(tile, slot):
        return pltpu.make_async_copy(h_hbm.at[pl.ds(tile * g, g)], hbuf.at[slot], lsem.at[slot])

    @pl.when(i == 0)
    def _():
        load(0, 0).start()

    @pl.when(i < last)
    def _():
        load(i + 1, (i + 1) % 3).start()

    load(i, cur).wait()

    def issue(c, carry):
        for u in range(SUBLANES):
            r = c * SUBLANES + u
            for j, d_ref in enumerate((d0_ref, d1_ref)):
                pltpu.make_async_copy(hbuf.at[cur, c, pl.ds(u, 1), :], xs_ref.at[pl.ds(d_ref[r], 1), :],
                                      sems.at[cur]).start(priority=j)
        return carry

    lax.fori_loop(0, g, issue, 0)

    def wait_step(s):
        for _ in range(2):
            pltpu.make_async_copy(xs_ref.at[pl.ds(0, td), :], xs_ref.at[pl.ds(0, td), :], s).wait()

    @pl.when(i > 0)
    def _():
        wait_step(sems.at[(i + 2) % 3])

    @pl.when(i == last)
    def _():
        wait_step(sems.at[cur])
        zsem = sems.at[3]
        zero_ref[...] = jnp.zeros_like(zero_ref)

        def per_expert(e, carry):
            c = cnt_ref[e]
            n_pad = (bm - c % bm) % bm
            base = pstart_ref[e] + c

            def start(r, cc):
                _row_copy(zero_ref, 0, xs_ref, base + r, zsem).start()
                return cc

            def wait(r, cc):
                _row_copy(zero_ref, 0, xs_ref, 0, zsem).wait()
                return cc

            lax.fori_loop(0, n_pad, start, 0)
            lax.fori_loop(0, n_pad, wait, 0)
            return carry

        lax.fori_loop(0, N_EXPERTS, per_expert, 0)

        first = nu_ref[0] * (bm // SUBLANES)
        n_tail = xs_ref.shape[0] // SUBLANES - first

        def tail_copy(r):
            return pltpu.make_async_copy(
                zero_ref, xs_ref.at[pl.ds(pl.multiple_of((first + r) * SUBLANES, SUBLANES), SUBLANES), :], zsem)

        def tail_start(r, cc):
            tail_copy(r).start()
            return cc

        def tail_wait(r, cc):
            tail_copy(0).wait()
            return cc

        lax.fori_loop(0, n_tail, tail_start, 0)
        lax.fori_loop(0, n_tail, tail_wait, 0)


def _dispatch(h2, dest0, dest1, pstart, counts, n_used, n_slots):
    n_tok, d = h2.shape
    td = DISP_TD
    tok = pl.BlockSpec((td,), lambda i, ps, cn, nu: (i,), memory_space=pltpu.SMEM)
    return pl.pallas_call(
        functools.partial(_dispatch_body, td=td, bm=MOE_BM),
        grid_spec=pltpu.PrefetchScalarGridSpec(
            num_scalar_prefetch=3,
            grid=(n_tok // td,),
            in_specs=[pl.BlockSpec(memory_space=pl.ANY), tok, tok],
            out_specs=pl.BlockSpec(memory_space=pl.ANY),
            scratch_shapes=[pltpu.VMEM((3, td // SUBLANES, SUBLANES, d), h2.dtype),
                            pltpu.VMEM((SUBLANES, d), h2.dtype),
                            pltpu.SemaphoreType.DMA((3,)), pltpu.SemaphoreType.DMA((4,))],
        ),
        out_shape=jax.ShapeDtypeStruct((n_slots, d), h2.dtype),
        compiler_params=_cparams(("arbitrary",)),
        name="dispatch",
    )(pstart, counts, n_used, h2.reshape(n_tok // SUBLANES, SUBLANES, d), dest0, dest1)


SC_LANES = 16
SC_WIN = 16
SC_NBUF = 4
SC_CHUNK = 8192


def _sc_dispatch(h2, dest0, dest1, n_slots):
    n_tok, d = h2.shape
    info = plsc.get_sparse_core_info()
    nc, ns = info.num_cores, info.num_subcores
    nw = nc * ns
    assert info.num_lanes == SC_LANES and n_slots % (nw * SC_WIN) == 0 and n_tok % SC_CHUNK == 0
    assert n_slots < 3 * n_tok
    per_w = n_slots // nw
    mesh = plsc.VectorSubcoreMesh(core_axis_name="c", subcore_axis_name="s")

    n_win = per_w // SC_WIN
    assert n_win % SC_NBUF == 0

    @functools.partial(
        pl.kernel, mesh=mesh,
        out_type=jax.ShapeDtypeStruct((n_slots, d), h2.dtype),
        scratch_types=[pltpu.VMEM((n_slots,), I32), pltpu.VMEM((SC_CHUNK,), I32),
                       pltpu.VMEM((SC_NBUF, SC_WIN, d), h2.dtype), pltpu.SemaphoreType.DMA((SC_NBUF,))],
        compiler_params=pltpu.CompilerParams(needs_layout_passes=False),
    )
    def k(h_hbm, d0_hbm, d1_hbm, o_hbm, table, chunk, rows, sems):
        lane = lax.iota(I32, SC_LANES)

        @pl.loop(0, n_slots, step=SC_LANES)
        def _(s):
            v = s + lane
            v = jnp.where(v >= n_tok, v - n_tok, v)
            v = jnp.where(v >= n_tok, v - n_tok, v)
            table[pl.ds(s, SC_LANES)] = v

        for d_hbm in (d0_hbm, d1_hbm):
            @pl.loop(0, n_tok, step=SC_CHUNK)
            def _(t0, d_hbm=d_hbm):
                pltpu.sync_copy(d_hbm.at[pl.ds(t0, SC_CHUNK)], chunk)

                @pl.loop(0, SC_CHUNK, step=SC_LANES)
                def _(j):
                    plsc.store_scatter(table, [chunk[pl.ds(j, SC_LANES)]], t0 + j + lane)

        base = (lax.axis_index("s") * nc + lax.axis_index("c")) * per_w

        def gather(w, b):
            return pltpu.make_async_copy(
                h_hbm.at[table.at[pl.ds(base + w * SC_WIN, SC_WIN)]], rows.at[b], sems.at[b])

        for b in range(SC_NBUF):
            gather(b, b).start()

        @pl.loop(0, n_win, step=SC_NBUF)
        def _(g):
            for b in range(SC_NBUF):
                w = g + b
                gather(w, b).wait()
                pltpu.sync_copy(rows.at[b], o_hbm.at[pl.ds(base + w * SC_WIN, SC_WIN)])

                @pl.when(w + SC_NBUF < n_win)
                def _():
                    gather(w + SC_NBUF, b).start()

    return k(h2, dest0, dest1)


def _experts_body(slot_ref, cast_ref, pre_ref, misc_ref, x_ref, wg_hbm, wu_hbm, wd_hbm, y_ref,
                  sg, su, sd, wgb, wub, wdb, sem):
    blk = pl.program_id(0)
    e0, pre0, n_used = misc_ref[0], misc_ref[1], misc_ref[2]

    def copies(e):
        return (pltpu.make_async_copy(wg_hbm.at[e], sg, sem.at[0]),
                pltpu.make_async_copy(wu_hbm.at[e], su, sem.at[1]),
                pltpu.make_async_copy(wd_hbm.at[e], sd, sem.at[2]))

    def fetch(e):
        for c in copies(e):
            c.start(priority=1)

    def land(slot):
        for c in copies(0):
            c.wait()
        wgb[slot] = sg[...].astype(wgb.dtype)
        wub[slot] = su[...].astype(wub.dtype)
        wdb[slot] = sd[...].astype(wdb.dtype)

    @pl.when(blk == 0)
    def _():
        fetch(e0)
        land(0)

        @pl.when(pre0 >= 0)
        def _():
            fetch(pre0)

    @pl.when(blk < n_used)
    def _():
        slot = slot_ref[blk]
        xb = _unpack_halves(x_ref[...], _MXU_DTYPE)
        a = _silu(_dot(xb, wgb[slot])) * _dot(xb, wub[slot])
        y_ref[...] = _dot(a.astype(_MXU_DTYPE), wdb[slot])

        @pl.when(cast_ref[blk] == 1)
        def _():
            land(1 - slot)

            @pl.when(pre_ref[blk] >= 0)
            def _():
                fetch(pre_ref[blk])

    @pl.when(blk >= n_used)
    def _():
        y_ref[...] = jnp.zeros_like(y_ref)


def _experts(xs, n_used, pends, w_gate, w_up, w_down):
    n_slots, dpk = xs.shape
    bm = MOE_BM
    n_blocks = n_slots // bm
    d, de = w_gate.shape[1:]
    last_used = n_used[0] - 1
    total = pends[-1]
    eid = jnp.arange(N_EXPERTS, dtype=I32)[None, :]
    owner = lambda start: jnp.minimum(jnp.sum((pends[None, :] <= start[:, None]).astype(I32), axis=1), N_EXPERTS - 1)
    end_of = lambda e: jnp.sum(jnp.where(eid == e[:, None], pends[None, :], 0), axis=1)
    be = owner(jnp.minimum(jnp.arange(n_blocks, dtype=I32), last_used) * bm)
    end_e = end_of(be)
    has_next = end_e < total
    nxt = owner(end_e)
    end_n = end_of(nxt)
    nxt2 = jnp.where(jnp.logical_and(has_next, end_n < total), owner(end_n), -1)
    is_last = jnp.concatenate([be[1:] != be[:-1], jnp.ones((1,), jnp.bool_)])
    is_last = jnp.logical_or(is_last, jnp.arange(n_blocks) >= last_used)
    cast = jnp.logical_and(is_last, has_next).astype(I32)
    pre = jnp.where(cast == 1, nxt2, -1).astype(I32)
    is_first = jnp.concatenate([jnp.ones((1,), jnp.bool_), be[1:] != be[:-1]])
    slot = ((jnp.cumsum(is_first.astype(I32)) - 1) % 2).astype(I32)
    misc = jnp.stack([be[0], jnp.where(has_next[0], nxt[0], -1), n_used[0]]).astype(I32)
    hbm = pl.BlockSpec(memory_space=pl.ANY)
    return pl.pallas_call(
        _experts_body,
        grid_spec=pltpu.PrefetchScalarGridSpec(
            num_scalar_prefetch=4,
            grid=(n_blocks,),
            in_specs=[pl.BlockSpec((bm, dpk), lambda j, sl, ca, pr, mi: (jnp.minimum(j, mi[2] - 1), 0)),
                      hbm, hbm, hbm],
            out_specs=pl.BlockSpec((bm, d), lambda j, sl, ca, pr, mi: (j, 0)),
            scratch_shapes=[pltpu.VMEM((d, de), w_gate.dtype), pltpu.VMEM((d, de), w_up.dtype),
                            pltpu.VMEM((de, d), w_down.dtype),
                            pltpu.VMEM((2, d, de), _MXU_DTYPE), pltpu.VMEM((2, d, de), _MXU_DTYPE),
                            pltpu.VMEM((2, de, d), _MXU_DTYPE), pltpu.SemaphoreType.DMA((3,))],
        ),
        out_shape=jax.ShapeDtypeStruct((n_slots, d), F32),
        compiler_params=_cparams(("arbitrary",)),
        name="experts",
    )(slot, cast, pre, misc, xs, w_gate, w_up, w_down)


def _combine_body(x1_ref, gate_ref, d0_ref, d1_ref, n0_ref, n1_ref, mod_ref, g_ref, ys_ref, o_ref, buf_ref, sem,
                  *, tf):
    i = pl.program_id(0)
    cur = i % 2

    def gather(da_ref, db_ref, half):
        def issue(c, carry):
            for u in range(SUBLANES):
                r = c * SUBLANES + u
                for j, d_ref in enumerate((da_ref, db_ref)):
                    pltpu.make_async_copy(ys_ref.at[pl.ds(d_ref[r], 1), :],
                                          buf_ref.at[half, j, c, pl.ds(u, 1), :], sem.at[half]).start(priority=j)
            return carry

        lax.fori_loop(0, tf // SUBLANES, issue, 0)

    @pl.when(i == 0)
    def _():
        gather(d0_ref, d1_ref, 0)

    @pl.when(i + 1 < pl.num_programs(0))
    def _():
        gather(n0_ref, n1_ref, 1 - cur)

    for _ in range(2):
        pltpu.make_async_copy(ys_ref.at[pl.ds(0, tf), :], ys_ref.at[pl.ds(0, tf), :], sem.at[cur]).wait()

    gates = gate_ref[...]
    d = o_ref.shape[1]
    moe = buf_ref[cur, 0].reshape(tf, d) * gates[:, 0:1] + buf_ref[cur, 1].reshape(tf, d) * gates[:, 1:2]
    x2 = x1_ref[...] + mod_ref[0, 5:6, :] * moe
    ms = jnp.mean(x2 * x2, axis=-1, keepdims=True)
    o_ref[...] = x2 * lax.rsqrt(ms + EPS) * g_ref[...]


def _combine(x1, gates, dest0, dest1, mod, g_final, ys, seq):
    n_tok, d = x1.shape
    tf = COMB_TF
    n_tiles = n_tok // tf
    tok = pl.BlockSpec((tf,), lambda i: (i,), memory_space=pltpu.SMEM)
    nxt = pl.BlockSpec((tf,), lambda i: (jnp.minimum(i + 1, n_tiles - 1),), memory_space=pltpu.SMEM)
    return pl.pallas_call(
        functools.partial(_combine_body, tf=tf),
        grid=(n_tiles,),
        in_specs=[pl.BlockSpec((tf, d), lambda i: (i, 0)),
                  pl.BlockSpec((tf, LANES), lambda i: (i, 0)),
                  tok, tok, nxt, nxt,
                  pl.BlockSpec((1, 6, d), lambda i: ((i * tf) // seq, 0, 0)),
                  pl.BlockSpec((1, d), lambda i: (0, 0)),
                  pl.BlockSpec(memory_space=pl.ANY)],
        out_specs=pl.BlockSpec((tf, d), lambda i: (i, 0)),
        scratch_shapes=[pltpu.VMEM((2, 2, tf // SUBLANES, SUBLANES, d), F32), pltpu.SemaphoreType.DMA((2,))],
        out_shape=jax.ShapeDtypeStruct((n_tok, d), F32),
        compiler_params=_cparams(("arbitrary",)),
        name="combine",
    )(x1, gates, dest0, dest1, dest0, dest1, mod, g_final, ys)


def kernel(x, c, w_ada, b_ada, g_mix, w_in, w_pool, pool_scale, rel_bias, w_out, g_ffn, w_group, b_group,
           w_router, b_router, w_gate, w_up, w_down, g_final):
    bsz, seq, d = x.shape
    n_tok = bsz * seq
    depth = w_ada.shape[0]
    assert depth == 1, "the final norm is fused into the only layer's combine"
    topk = min(TOPK_MAX, seq // 4)
    bias = _bias_tiles(rel_bias, ATT_T)
    bm = MOE_BM
    n_blocks = -(-(n_tok * 2) // bm) + N_EXPERTS
    n_slots = n_blocks * bm
    xt = x.reshape(n_tok, d)
    for i in range(depth):
        mod = _ada(c, w_ada[i], b_ada[i][None, :]).reshape(bsz, 6, d)
        u, q, k, vt, qi, kia, kib, wit = _proj(xt.reshape(bsz, seq, d), mod, g_mix[i][None, :], w_in[i])
        attn = _dsa(q, qi, wit, kia, kib, k, vt, bias, topk)
        w_r = jnp.concatenate(
            [w_group[i], w_router[i], jnp.zeros((d, LANES - N_GROUPS - N_EXPERTS), F32)], axis=1).astype(_MXU_DTYPE)
        b_r = jnp.concatenate(
            [b_group[i], b_router[i], jnp.zeros((LANES - N_GROUPS - N_EXPERTS,), F32)])[None, :]
        x1, h2, logits = _mix(u.reshape(n_tok, -1), attn.reshape(n_tok, -1), xt, mod,
                              w_pool[i].astype(_MXU_DTYPE), pool_scale[i][None, :],
                              w_out[i].astype(_MXU_DTYPE), g_ffn[i][None, :], w_r, b_r, seq)
        route_t, gates, cnt = _route(logits)
        counts = cnt[0, :N_EXPERTS].astype(I32)
        padded = (counts + bm - 1) // bm * bm
        pends = jnp.cumsum(padded)
        pstart = pends - padded
        n_used = (pends[-1:] // bm).astype(I32)
        dest = _dest(route_t, pstart)
        dest0, dest1 = dest[0], dest[1]
        xs = _sc_dispatch(h2, dest0, dest1, n_slots)
        ys = _experts(xs, n_used, pends, w_gate[i], w_up[i], w_down[i])
        xt = _combine(x1, gates, dest0, dest1, mod, g_final[None, :], ys, seq)
    return xt.reshape(bsz, seq, d)
```

```python
import functools
import math

import numpy as np
import jax
import jax.numpy as jnp
from jax import lax
from jax.experimental import pallas as pl
from jax.experimental.pallas import tpu as pltpu
from jax.experimental.pallas import tpu_sc as plsc

F32 = jnp.float32
I32 = jnp.int32
_MXU_DTYPE = jnp.bfloat16

POOL_WINDOWS = (2, 4, 8, 16)
N_POOL_GROUPS = 4
HEAD_DIM = 128
N_HEADS = 8
N_KV_HEADS = 2
HEAD_GROUP = N_HEADS // N_KV_HEADS
IDX_HEADS = 16
IDX_DIM = 64
TOPK_MAX = 256
N_BUCKETS = 32
MAX_DISTANCE = 128
N_GROUPS = 4
EXPERTS_PER_GROUP = 8
N_EXPERTS = N_GROUPS * EXPERTS_PER_GROUP
EPS = 1e-6
NEG_INF = -1e30
BIG = 3e38
LOG2E = 1.4426950408889634

LANES = 128
SUBLANES = 8
VMEM_LIMIT_BYTES = 56 * 1024 * 1024

ADA_TN = 1024
PROJ_TM = 256
ATT_T = 256
SCORE_SUB = 256
COUNT_RB = 256
MIX_TM = 256
HALO = 16
ROUTE_TR = 1024
DISP_TD = 256
MOE_BM = 256
COMB_TF = 256
DMA_UNROLL = 8
SEL_FIXED_ITERS = 18
SEL_MAX_ITERS = 80


def _cparams(sem):
    return pltpu.CompilerParams(dimension_semantics=sem, vmem_limit_bytes=VMEM_LIMIT_BYTES)


def _silu(x):
    return x * (1.0 / (1.0 + jnp.exp(-x)))


def _dot(a, b):
    return jnp.dot(a, b, preferred_element_type=F32)


def _dot_nt(a, b):
    return lax.dot_general(a, b, (((1,), (1,)), ((), ())), preferred_element_type=F32)


def _ada_body(c_ref, w_ref, b_ref, o_ref):
    s = _silu(c_ref[...])
    o_ref[...] = _dot(s.astype(_MXU_DTYPE), w_ref[...].astype(_MXU_DTYPE)) + b_ref[...]


def _ada(c, w, b):
    bsz, d = c.shape
    n = w.shape[1]
    return pl.pallas_call(
        _ada_body,
        grid=(n // ADA_TN,),
        in_specs=[pl.BlockSpec((bsz, d), lambda j: (0, 0)),
                  pl.BlockSpec((d, ADA_TN), lambda j: (0, j)),
                  pl.BlockSpec((1, ADA_TN), lambda j: (0, j))],
        out_specs=pl.BlockSpec((bsz, ADA_TN), lambda j: (0, j)),
        out_shape=jax.ShapeDtypeStruct((bsz, n), F32),
        compiler_params=_cparams(("arbitrary",)),
        name="ada",
    )(c, w, b)


def _rms_mod(x, g, shift, scale):
    ms = jnp.mean(x * x, axis=-1, keepdims=True)
    y = x * lax.rsqrt(ms + EPS) * g
    return y * (1.0 + scale) + shift


def _proj_body(x_ref, mod_ref, g_ref, w_ref, u_ref, q_ref, k_ref, vt_ref, qi_ref, kia_ref, kib_ref,
               wit_ref, *, cuts):
    h = _rms_mod(x_ref[0], g_ref[...], mod_ref[0, 0:1, :], mod_ref[0, 1:2, :])
    hb = h.astype(_MXU_DTYPE)

    def seg(name):
        lo, hi = cuts[name]
        return _dot(hb, w_ref[:, lo:hi])

    u_ref[0] = seg("u")
    q_ref[0] = seg("q").astype(q_ref.dtype)
    k_ref[0] = seg("k").astype(k_ref.dtype)
    vt_ref[0] = seg("v").T.astype(vt_ref.dtype)
    qi_ref[0] = seg("qi").astype(qi_ref.dtype)
    tail = seg("tail")
    tail = jnp.concatenate([tail, jnp.zeros((tail.shape[0], LANES - tail.shape[1]), F32)], axis=1)
    swapped = pltpu.roll(tail, LANES - IDX_DIM, 1)
    lane = lax.broadcasted_iota(I32, tail.shape, 1)
    kia_ref[0] = jnp.where(lane < IDX_DIM, tail, 0.0).astype(kia_ref.dtype)
    kib_ref[0] = jnp.where(lane >= LANES - IDX_DIM, swapped, 0.0).astype(kib_ref.dtype)
    wit_ref[0] = swapped.T[:IDX_HEADS, :]


def _proj(x, mod, g_mix, w_in):
    bsz, s, d = x.shape
    pool_w = d // 2
    attn_w = N_HEADS * HEAD_DIM
    kv_w = N_KV_HEADS * HEAD_DIM
    qi_w = IDX_HEADS * IDX_DIM
    c = np.cumsum([0, pool_w, attn_w, kv_w, kv_w, qi_w, IDX_DIM, IDX_HEADS])
    assert IDX_DIM + IDX_HEADS <= LANES and w_in.shape[1] == c[7]
    wp = w_in.astype(_MXU_DTYPE)
    cuts = {"u": (int(c[0]), int(c[1])), "q": (int(c[1]), int(c[2])), "k": (int(c[2]), int(c[3])),
            "v": (int(c[3]), int(c[4])), "qi": (int(c[4]), int(c[5])), "tail": (int(c[5]), int(c[7]))}
    tm = PROJ_TM
    pw = wp.shape[1]
    row = lambda w: pl.BlockSpec((1, tm, w), lambda b, i: (b, i, 0))
    out_shapes = [
        jax.ShapeDtypeStruct((bsz, s, pool_w), F32),
        jax.ShapeDtypeStruct((bsz, s, attn_w), _MXU_DTYPE),
        jax.ShapeDtypeStruct((bsz, s, kv_w), _MXU_DTYPE),
        jax.ShapeDtypeStruct((bsz, kv_w, s), _MXU_DTYPE),
        jax.ShapeDtypeStruct((bsz, s, qi_w), _MXU_DTYPE),
        jax.ShapeDtypeStruct((bsz, s, LANES), _MXU_DTYPE),
        jax.ShapeDtypeStruct((bsz, s, LANES), _MXU_DTYPE),
        jax.ShapeDtypeStruct((bsz, IDX_HEADS, s), F32),
    ]
    out_specs = [row(pool_w), row(attn_w), row(kv_w),
                 pl.BlockSpec((1, kv_w, tm), lambda b, i: (b, 0, i)),
                 row(qi_w), row(LANES), row(LANES),
                 pl.BlockSpec((1, IDX_HEADS, tm), lambda b, i: (b, 0, i))]
    return pl.pallas_call(
        functools.partial(_proj_body, cuts=cuts),
        grid=(bsz, s // tm),
        in_specs=[pl.BlockSpec((1, tm, d), lambda b, i: (b, i, 0)),
                  pl.BlockSpec((1, 6, d), lambda b, i: (b, 0, 0)),
                  pl.BlockSpec((1, d), lambda b, i: (0, 0)),
                  pl.BlockSpec((d, pw), lambda b, i: (0, 0))],
        out_specs=out_specs,
        out_shape=out_shapes,
        compiler_params=_cparams(("arbitrary", "arbitrary")),
        name="proj",
    )(x, mod, g_mix, wp)


def _bucket_starts():
    max_exact = N_BUCKETS // 2
    d = np.arange(1, 4 * MAX_DISTANCE, dtype=np.float32)
    large = max_exact + (np.log(d / np.float32(max_exact)) / np.float32(math.log(MAX_DISTANCE / max_exact))
                         * np.float32(N_BUCKETS - max_exact)).astype(np.int32)
    large = np.minimum(large, N_BUCKETS - 1)
    bucket = np.where(d < max_exact, d.astype(np.int32), large)
    bucket = np.concatenate([[0], bucket])
    starts = [int(np.argmax(bucket >= b)) for b in range(N_BUCKETS)]
    assert all(np.all(bucket[starts[b]:] >= b) for b in range(N_BUCKETS))
    assert starts[N_BUCKETS - 1] <= MAX_DISTANCE
    return starts


def _bias_body(rb_ref, o_ref, *, t, starts):
    diff = pl.program_id(0)
    s_l = lax.broadcasted_iota(I32, (t, t), 0)
    t_l = lax.broadcasted_iota(I32, (t, t), 1)
    dist = diff * t + t_l - s_l
    for h in range(N_HEADS):
        val = jnp.full((t, t), rb_ref[0, h], F32)
        for b in range(1, N_BUCKETS):
            val = jnp.where(dist >= starts[b], rb_ref[b, h], val)
        hh = h % HEAD_GROUP
        o_ref[0, h // HEAD_GROUP, :, hh * t:(hh + 1) * t] = (val - rb_ref[N_BUCKETS - 1, h]) * LOG2E


def _bias_tiles(rel_bias, t):
    assert t >= MAX_DISTANCE
    return pl.pallas_call(
        functools.partial(_bias_body, t=t, starts=_bucket_starts()),
        grid=(2,),
        in_specs=[pl.BlockSpec(memory_space=pltpu.SMEM)],
        out_specs=pl.BlockSpec((1, N_KV_HEADS, t, HEAD_GROUP * t), lambda i: (i, 0, 0, 0)),
        out_shape=jax.ShapeDtypeStruct((2, N_KV_HEADS, t, HEAD_GROUP * t), F32),
        compiler_params=_cparams(("arbitrary",)),
        name="bias_tiles",
    )(rel_bias)


def _dsa_body(q_ref, qi_ref, wit_ref, kia_ref, kib_ref, k_ref, vt_ref, bias_ref, o_ref,
              score_ref, mask_ref, acc_ref, *, t, topk):
    i = pl.program_id(1)
    nk = i + 1
    t_glob = i * t + lax.broadcasted_iota(I32, (1, t), 1)
    idx_scale = (IDX_DIM ** -0.5) * (IDX_HEADS ** -0.5)

    def score_chunk(c, carry):
        r0 = pl.multiple_of(c * SCORE_SUB, SCORE_SUB)
        ka = kia_ref[0, pl.ds(r0, SCORE_SUB), :]
        kb = kib_ref[0, pl.ds(r0, SCORE_SUB), :]
        acc = jnp.zeros((SCORE_SUB, t), F32)
        for j in range(IDX_HEADS // 2):
            qp = qi_ref[0, :, j * LANES:(j + 1) * LANES]
            da = _dot_nt(ka, qp)
            db = _dot_nt(kb, qp)
            acc = acc + jnp.maximum(da, 0.0) * wit_ref[0, 2 * j:2 * j + 1, :]
            acc = acc + jnp.maximum(db, 0.0) * wit_ref[0, 2 * j + 1:2 * j + 2, :]
        sc = acc * idx_scale
        s_glob = r0 + lax.broadcasted_iota(I32, (SCORE_SUB, 1), 0)
        score_ref[pl.ds(r0, SCORE_SUB), :] = jnp.where(s_glob <= t_glob, sc, NEG_INF)
        return carry

    lax.fori_loop(0, nk * (t // SCORE_SUB), score_chunk, 0)

    nblk = nk * (t // COUNT_RB)
    fold = lambda m: jnp.sum(m.reshape(COUNT_RB // SUBLANES, SUBLANES, t), axis=0)

    def blk(r):
        return score_ref[pl.ds(pl.multiple_of(r * COUNT_RB, COUNT_RB), COUNT_RB), :]

    def count_ge(c):
        def body(r, acc):
            return acc + fold((blk(r) >= c).astype(I32))
        acc = lax.fori_loop(0, nblk, body, jnp.zeros((SUBLANES, t), I32))
        return jnp.sum(acc, axis=0, keepdims=True)

    def stats_body(r, carry):
        cnt, mn, mx = carry
        b = blk(r)
        real = b > NEG_INF
        cnt = cnt + fold(real.astype(I32))
        mn = jnp.minimum(mn, jnp.min(jnp.where(real, b, BIG).reshape(COUNT_RB // SUBLANES, SUBLANES, t), axis=0))
        mx = jnp.maximum(mx, jnp.max(b.reshape(COUNT_RB // SUBLANES, SUBLANES, t), axis=0))
        return cnt, mn, mx

    cnt8, mn8, mx8 = lax.fori_loop(
        0, nblk, stats_body,
        (jnp.zeros((SUBLANES, t), I32), jnp.full((SUBLANES, t), BIG, F32), jnp.full((SUBLANES, t), -BIG, F32)))
    n_real = jnp.sum(cnt8, axis=0, keepdims=True)
    rmin = jnp.min(mn8, axis=0, keepdims=True)
    rmax = jnp.max(mx8, axis=0, keepdims=True)
    n_max = count_ge(rmax)
    degenerate = jnp.logical_and(n_real >= topk, n_max >= topk)
    lo0 = jnp.where(n_real < topk, NEG_INF, jnp.where(degenerate, rmax, rmin))
    hi0 = jnp.where(degenerate, BIG, rmax)
    done0 = jnp.where(n_real <= topk, 1, 0)

    def bisect(st):
        lo, hi, done, stuck = st
        c = lo + (hi - lo) * 0.5
        active = (done + stuck) == 0
        has_mid = jnp.logical_and(c > lo, c < hi)
        n = count_ge(c)
        upd = jnp.logical_and(active, has_mid)
        lo = jnp.where(jnp.logical_and(upd, n >= topk), c, lo)
        hi = jnp.where(jnp.logical_and(upd, n < topk), c, hi)
        done = jnp.where(jnp.logical_and(upd, n == topk), 1, done)
        stuck = jnp.where(jnp.logical_and(active, jnp.logical_not(has_mid)), 1, stuck)
        return lo, hi, done, stuck

    st = lax.fori_loop(0, SEL_FIXED_ITERS, lambda _, s: bisect(s), (lo0, hi0, done0, jnp.zeros((1, t), I32)))

    def sel_cond(s):
        return jnp.logical_and(s[0] < SEL_MAX_ITERS, jnp.min(s[3] + s[4]) == 0)

    _, lo, hi, done, _ = lax.while_loop(sel_cond, lambda s: (s[0] + 1,) + bisect(s[1:]), (jnp.int32(0),) + st)

    def write_mask(r, cut):
        r0 = pl.multiple_of(r * COUNT_RB, COUNT_RB)
        b = score_ref[pl.ds(r0, COUNT_RB), :]
        s_glob = r0 + lax.broadcasted_iota(I32, (COUNT_RB, 1), 0)
        sel = jnp.logical_and(b >= lo, s_glob <= t_glob)
        if cut is not None:
            sel = jnp.logical_and(sel, jnp.logical_or(b >= hi, s_glob < cut))
        mask_ref[pl.ds(r0, COUNT_RB), :] = jnp.concatenate([jnp.where(sel, 0.0, NEG_INF)] * HEAD_GROUP, axis=1)

    any_tied = jnp.min(done) == 0

    @pl.when(jnp.logical_not(any_tied))
    def _():
        def body(r, carry):
            write_mask(r, None)
            return carry
        lax.fori_loop(0, nblk, body, 0)

    @pl.when(any_tied)
    def _():
        need = topk - count_ge(hi)

        def count_tie_below(x):
            def body(r, acc):
                r0 = pl.multiple_of(r * COUNT_RB, COUNT_RB)
                b = score_ref[pl.ds(r0, COUNT_RB), :]
                s_glob = r0 + lax.broadcasted_iota(I32, (COUNT_RB, 1), 0)
                m = jnp.logical_and(jnp.logical_and(b >= lo, b < hi), s_glob < x)
                return acc + fold(m.astype(I32))
            acc = lax.fori_loop(0, nblk, body, jnp.zeros((SUBLANES, t), I32))
            return jnp.sum(acc, axis=0, keepdims=True)

        nbits = int(score_ref.shape[0]).bit_length()

        def bit_body(bi, x):
            c = x + jnp.left_shift(jnp.int32(1), nbits - 1 - bi)
            return jnp.where(count_tie_below(c) < need, c, x)

        x = lax.fori_loop(0, nbits, bit_body, jnp.zeros((1, t), I32))
        cut = jnp.where(done > 0, jnp.int32(2 ** 30), x + 1)

        def body(r, carry):
            write_mask(r, cut)
            return carry
        lax.fori_loop(0, nblk, body, 0)

    scale = HEAD_DIM ** -0.5
    gt = HEAD_GROUP * t

    def col_reduce(op, a):
        part = op(a.reshape(a.shape[0] // SUBLANES, SUBLANES, a.shape[1]), axis=0)
        return op(part, axis=0, keepdims=True)

    qgs = [jnp.concatenate(
        [q_ref[0, :, h * HEAD_DIM:(h + 1) * HEAD_DIM] for h in range(g * HEAD_GROUP, (g + 1) * HEAD_GROUP)],
        axis=0) for g in range(N_KV_HEADS)]
    acc_ref[...] = jnp.zeros_like(acc_ref)

    def att_chunk(kc, carry, near):
        r0 = pl.multiple_of(kc * t, t)
        mk = mask_ref[pl.ds(r0, t), :]
        new = []
        for g in range(N_KV_HEADS):
            m, l = carry[g]
            kk = k_ref[0, pl.ds(r0, t), g * HEAD_DIM:(g + 1) * HEAD_DIM]
            s = _dot_nt(kk, qgs[g]) * (scale * LOG2E) + mk
            if near:
                s = s + bias_ref[i - kc, g]
            m_new = jnp.maximum(m, col_reduce(jnp.max, s))
            alpha = jnp.exp2(m - m_new)
            p = jnp.exp2(s - m_new)
            l = alpha * l + col_reduce(jnp.sum, p)
            vt = vt_ref[0, g * HEAD_DIM:(g + 1) * HEAD_DIM, pl.ds(r0, t)]
            acc_ref[g] = alpha * acc_ref[g] + _dot(vt, p.astype(_MXU_DTYPE))
            new.append((m_new, l))
        return tuple(new)

    init = (jnp.full((1, gt), NEG_INF, F32), jnp.zeros((1, gt), F32))
    n_far = jnp.maximum(i - 1, 0)
    far = lax.fori_loop(0, n_far, functools.partial(att_chunk, near=False), (init,) * N_KV_HEADS)
    fin = lax.fori_loop(n_far, nk, functools.partial(att_chunk, near=True), far)
    for g in range(N_KV_HEADS):
        out = acc_ref[g] / fin[g][1]
        for hh in range(HEAD_GROUP):
            h = g * HEAD_GROUP + hh
            o_ref[0, :, h * HEAD_DIM:(h + 1) * HEAD_DIM] = out[:, hh * t:(hh + 1) * t].T.astype(o_ref.dtype)


def _dsa(q, qi, wit, kia, kib, k, vt, bias, topk):
    bsz, s, aw = q.shape
    t = ATT_T
    assert s % t == 0 and topk <= t
    row = lambda w: pl.BlockSpec((1, t, w), lambda b, i: (b, i, 0))
    full = lambda a: pl.BlockSpec((1,) + a.shape[1:], lambda b, i: (b, 0, 0))
    return pl.pallas_call(
        functools.partial(_dsa_body, t=t, topk=topk),
        grid=(bsz, s // t),
        in_specs=[row(aw), row(qi.shape[2]),
                  pl.BlockSpec((1, IDX_HEADS, t), lambda b, i: (b, 0, i)),
                  full(kia), full(kib), full(k), full(vt),
                  pl.BlockSpec(bias.shape, lambda b, i: (0, 0, 0, 0))],
        out_specs=row(aw),
        out_shape=jax.ShapeDtypeStruct((bsz, s, aw), _MXU_DTYPE),
        scratch_shapes=[pltpu.VMEM((s, t), F32), pltpu.VMEM((s, HEAD_GROUP * t), F32),
                        pltpu.VMEM((N_KV_HEADS, HEAD_DIM, HEAD_GROUP * t), F32)],
        compiler_params=_cparams(("arbitrary", "arbitrary")),
        name="dsa",
    )(q, qi, wit, kia, kib, k, vt, bias)


def _pack_halves(hb):
    assert jnp.dtype(hb.dtype).itemsize == 2
    n = hb.shape[1] // 2
    hi = lax.bitcast_convert_type(hb[:, :n].astype(F32), I32)
    lo = lax.bitcast_convert_type(hb[:, n:].astype(F32), I32)
    return jnp.bitwise_or(hi, lax.shift_right_logical(lo, jnp.full(lo.shape, 16, I32)))


def _unpack_halves(w, dtype):
    hi = lax.bitcast_convert_type(jnp.bitwise_and(w, jnp.int32(-65536)), F32).astype(dtype)
    lo = lax.bitcast_convert_type(jnp.left_shift(w, 16), F32).astype(dtype)
    return jnp.concatenate([hi, lo], axis=1)


def _mix_body(u_ref, halo_ref, a_ref, x_ref, mod_ref, wpool_ref, ps_ref, wout_ref, g_ref, wr_ref, br_ref,
              x1_ref, h2_ref, lg_ref, *, seq, tm):
    t0 = (pl.program_id(0) * tm) % seq
    u = u_ref[...]
    halo = jnp.where(t0 == 0, 0.0, halo_ref[...])
    ext = jnp.concatenate([halo, u], axis=0)
    pos = t0 + lax.broadcasted_iota(I32, (tm, 1), 0)
    gw = u.shape[1] // N_POOL_GROUPS
    ys = []
    for g, w in enumerate(POOL_WINDOWS):
        a = ext[:, g * gw:(g + 1) * gw]
        sft = 1
        while sft < w:
            a = a + pltpu.roll(a, sft, 0)
            sft *= 2
        cnt = jnp.minimum(pos + 1, w).astype(F32)
        p = a[HALO:, :] / cnt - u[:, g * gw:(g + 1) * gw]
        ys.append(_dot(p.astype(_MXU_DTYPE), wpool_ref[g]) * ps_ref[:, g * gw:(g + 1) * gw])
    pool = jnp.concatenate(ys, axis=1).astype(_MXU_DTYPE)
    pw = pool.shape[1]
    mixed = _dot(pool, wout_ref[:pw, :]) + _dot(a_ref[...], wout_ref[pw:, :])
    x1 = x_ref[...] + mod_ref[0, 2:3, :] * mixed
    x1_ref[...] = x1
    h2 = _rms_mod(x1, g_ref[...], mod_ref[0, 3:4, :], mod_ref[0, 4:5, :]).astype(_MXU_DTYPE)
    h2_ref[...] = _pack_halves(h2)
    lg_ref[...] = _dot(h2, wr_ref[...]) + br_ref[...]


def _mix(u, attn, x, mod, w_pool, pool_scale, w_out, g_ffn, w_r, b_r, seq):
    n_tok, d = x.shape
    tm = MIX_TM
    assert seq % tm == 0 and tm % HALO == 0
    pw = u.shape[1]
    aw = attn.shape[1]
    rows = lambda w: pl.BlockSpec((tm, w), lambda i: (i, 0))
    const = lambda a: pl.BlockSpec(a.shape, lambda i: (0,) * a.ndim)
    return pl.pallas_call(
        functools.partial(_mix_body, seq=seq, tm=tm),
        grid=(n_tok // tm,),
        in_specs=[rows(pw),
                  pl.BlockSpec((HALO, pw), lambda i: (jnp.maximum(i * (tm // HALO) - 1, 0), 0)),
                  rows(aw), rows(d),
                  pl.BlockSpec((1, 6, d), lambda i: ((i * tm) // seq, 0, 0)),
                  const(w_pool), const(pool_scale), const(w_out), const(g_ffn), const(w_r), const(b_r)],
        out_specs=[rows(d), rows(d // 2), rows(LANES)],
        out_shape=[jax.ShapeDtypeStruct((n_tok, d), F32), jax.ShapeDtypeStruct((n_tok, d // 2), I32),
                   jax.ShapeDtypeStruct((n_tok, LANES), F32)],
        compiler_params=_cparams(("arbitrary",)),
        name="mix",
    )(u, u, attn, x, mod, w_pool, pool_scale, w_out, g_ffn, w_r, b_r)


def _route_body(lg_ref, rt_ref, gate_ref, cnt_ref, run_ref):
    @pl.when(pl.program_id(0) == 0)
    def _():
        run_ref[...] = jnp.zeros_like(run_ref)

    lg = lg_ref[...]
    tr = lg.shape[0]
    lane = lax.broadcasted_iota(I32, lg.shape, 1)
    gmask = lane < N_GROUPS
    gl = jnp.where(gmask, lg, -BIG)
    gmax = jnp.max(gl, axis=-1, keepdims=True)
    gsel = jnp.min(jnp.where(jnp.logical_and(gmask, gl == gmax), lane, LANES), axis=-1, keepdims=True)
    gsum = jnp.sum(jnp.where(gmask, jnp.exp(gl - gmax), 0.0), axis=-1, keepdims=True)
    p_g = 1.0 / gsum
    in_e = jnp.logical_and(lane >= N_GROUPS, lane < N_GROUPS + N_EXPERTS)
    emask = jnp.logical_and(in_e, lax.shift_right_arithmetic(lane - N_GROUPS, EXPERTS_PER_GROUP.bit_length() - 1) == gsel)
    el = jnp.where(emask, lg, -BIG)
    emax = jnp.max(el, axis=-1, keepdims=True)
    ex = jnp.where(emask, jnp.exp(el - emax), 0.0)
    ep = ex / jnp.sum(ex, axis=-1, keepdims=True)
    p1 = jnp.max(jnp.where(emask, ep, -1.0), axis=-1, keepdims=True)
    i1 = jnp.min(jnp.where(jnp.logical_and(emask, ep == p1), lane, LANES), axis=-1, keepdims=True)
    m2 = jnp.logical_and(emask, lane != i1)
    p2 = jnp.max(jnp.where(m2, ep, -1.0), axis=-1, keepdims=True)
    i2 = jnp.min(jnp.where(jnp.logical_and(m2, ep == p2), lane, LANES), axis=-1, keepdims=True)
    den = p1 + p2
    g1 = p_g * p1 / den
    g2 = p_g * p2 / den
    e1 = i1 - N_GROUPS
    e2 = i2 - N_GROUPS

    oh = jnp.logical_or(lane == e1, lane == e2)
    ri = lax.broadcasted_iota(I32, (tr, tr), 0)
    ci = lax.broadcasted_iota(I32, (tr, tr), 1)
    ltri = (ci < ri).astype(_MXU_DTYPE)
    before = _dot(ltri, oh.astype(_MXU_DTYPE)) + run_ref[...]
    r1 = jnp.sum(jnp.where(lane == e1, before, 0.0), axis=-1, keepdims=True)
    r2 = jnp.sum(jnp.where(lane == e2, before, 0.0), axis=-1, keepdims=True)
    run = run_ref[...] + jnp.sum(oh.astype(F32), axis=0, keepdims=True)
    run_ref[...] = run
    cnt_ref[...] = jnp.broadcast_to(run, cnt_ref.shape)

    route = jnp.where(lane == 0, e1.astype(F32),
                      jnp.where(lane == 1, e2.astype(F32),
                                jnp.where(lane == 2, r1, jnp.where(lane == 3, r2, 0.0))))
    rt_ref[...] = route.T[:SUBLANES, :].astype(I32)
    gate_ref[...] = jnp.where(lane == 0, g1, jnp.where(lane == 1, g2, 0.0))


def _route(logits):
    n_tok = logits.shape[0]
    tr = min(ROUTE_TR, n_tok)
    return pl.pallas_call(
        _route_body,
        grid=(n_tok // tr,),
        in_specs=[pl.BlockSpec((tr, LANES), lambda i: (i, 0))],
        out_specs=[pl.BlockSpec((SUBLANES, tr), lambda i: (0, i)),
                   pl.BlockSpec((tr, LANES), lambda i: (i, 0)),
                   pl.BlockSpec((SUBLANES, LANES), lambda i: (0, 0))],
        out_shape=[jax.ShapeDtypeStruct((SUBLANES, n_tok), I32),
                   jax.ShapeDtypeStruct((n_tok, LANES), F32),
                   jax.ShapeDtypeStruct((SUBLANES, LANES), F32)],
        scratch_shapes=[pltpu.VMEM((1, LANES), F32)],
        compiler_params=_cparams(("arbitrary",)),
        name="route",
    )(logits)


def _dest_body(ps_ref, rt_ref, o_ref):
    r = rt_ref[...]
    start = jnp.zeros_like(r)
    for e in range(N_EXPERTS):
        start = jnp.where(r == e, ps_ref[e], start)
    o_ref[...] = start + pltpu.roll(r, SUBLANES - 2, 0)


def _dest(route_t, pstart):
    n_tok = route_t.shape[1]
    tb = min(n_tok, 2048)
    return pl.pallas_call(
        _dest_body,
        grid_spec=pltpu.PrefetchScalarGridSpec(
            num_scalar_prefetch=1,
            grid=(n_tok // tb,),
            in_specs=[pl.BlockSpec((SUBLANES, tb), lambda i, ps: (0, i))],
            out_specs=pl.BlockSpec((SUBLANES, tb), lambda i, ps: (0, i)),
        ),
        out_shape=jax.ShapeDtypeStruct((SUBLANES, n_tok), I32),
        compiler_params=_cparams(("arbitrary",)),
        name="dest",
    )(pstart, route_t)


def _row_copy(src_ref, src_row, dst_ref, dst_row, sem):
    return pltpu.make_async_copy(src_ref.at[pl.ds(src_row, 1), :], dst_ref.at[pl.ds(dst_row, 1), :], sem)


def _dispatch_body(pstart_ref, cnt_ref, nu_ref, h_hbm, d0_ref, d1_ref, xs_ref, hbuf, zero_ref, lsem, sems,
                   *, td, bm):
    i = pl.program_id(0)
    last = pl.num_programs(0) - 1
    g = td // SUBLANES
    cur = i % 3

    def load(tile, slot):
        return pltpu.make_async_copy(h_hbm.at[pl.ds(tile * g, g)], hbuf.at[slot], lsem.at[slot])

    @pl.when(i == 0)
    def _():
        load(0, 0).start()

    @pl.when(i < last)
    def _():
        load(i + 1, (i + 1) % 3).start()

    load(i, cur).wait()

    def issue(c, carry):
        for u in range(SUBLANES):
            r = c * SUBLANES + u
            for j, d_ref in enumerate((d0_ref, d1_ref)):
                pltpu.make_async_copy(hbuf.at[cur, c, pl.ds(u, 1), :], xs_ref.at[pl.ds(d_ref[r], 1), :],
                                      sems.at[cur]).start(priority=j)
        return carry

    lax.fori_loop(0, g, issue, 0)

    def wait_step(s):
        for _ in range(2):
            pltpu.make_async_copy(xs_ref.at[pl.ds(0, td), :], xs_ref.at[pl.ds(0, td), :], s).wait()

    @pl.when(i > 0)
    def _():
        wait_step(sems.at[(i + 2) % 3])

    @pl.when(i == last)
    def _():
        wait_step(sems.at[cur])
        zsem = sems.at[3]
        zero_ref[...] = jnp.zeros_like(zero_ref)

        def per_expert(e, carry):
            c = cnt_ref[e]
            n_pad = (bm - c % bm) % bm
            base = pstart_ref[e] + c

            def start(r, cc):
                _row_copy(zero_ref, 0, xs_ref, base + r, zsem).start()
                return cc

            def wait(r, cc):
                _row_copy(zero_ref, 0, xs_ref, 0, zsem).wait()
                return cc

            lax.fori_loop(0, n_pad, start, 0)
            lax.fori_loop(0, n_pad, wait, 0)
            return carry

        lax.fori_loop(0, N_EXPERTS, per_expert, 0)

        first = nu_ref[0] * (bm // SUBLANES)
        n_tail = xs_ref.shape[0] // SUBLANES - first

        def tail_copy(r):
            return pltpu.make_async_copy(
                zero_ref, xs_ref.at[pl.ds(pl.multiple_of((first + r) * SUBLANES, SUBLANES), SUBLANES), :], zsem)

        def tail_start(r, cc):
            tail_copy(r).start()
            return cc

        def tail_wait(r, cc):
            tail_copy(0).wait()
            return cc

        lax.fori_loop(0, n_tail, tail_start, 0)
        lax.fori_loop(0, n_tail, tail_wait, 0)


def _dispatch(h2, dest0, dest1, pstart, counts, n_used, n_slots):
    n_tok, d = h2.shape
    td = DISP_TD
    tok = pl.BlockSpec((td,), lambda i, ps, cn, nu: (i,), memory_space=pltpu.SMEM)
    return pl.pallas_call(
        functools.partial(_dispatch_body, td=td, bm=MOE_BM),
        grid_spec=pltpu.PrefetchScalarGridSpec(
            num_scalar_prefetch=3,
            grid=(n_tok // td,),
            in_specs=[pl.BlockSpec(memory_space=pl.ANY), tok, tok],
            out_specs=pl.BlockSpec(memory_space=pl.ANY),
            scratch_shapes=[pltpu.VMEM((3, td // SUBLANES, SUBLANES, d), h2.dtype),
                            pltpu.VMEM((SUBLANES, d), h2.dtype),
                            pltpu.SemaphoreType.DMA((3,)), pltpu.SemaphoreType.DMA((4,))],
        ),
        out_shape=jax.ShapeDtypeStruct((n_slots, d), h2.dtype),
        compiler_params=_cparams(("arbitrary",)),
        name="dispatch",
    )(pstart, counts, n_used, h2.reshape(n_tok // SUBLANES, SUBLANES, d), dest0, dest1)


SC_LANES = 16
SC_WIN = 16
SC_NBUF = 4
SC_CHUNK = 8192


def _sc_dispatch(h2, dest0, dest1, n_slots):
    n_tok, d = h2.shape
    info = plsc.get_sparse_core_info()
    nc, ns = info.num_cores, info.num_subcores
    nw = nc * ns
    assert info.num_lanes == SC_LANES and n_slots % (nw * SC_WIN) == 0 and n_tok % SC_CHUNK == 0
    assert n_slots < 3 * n_tok
    per_w = n_slots // nw
    mesh = plsc.VectorSubcoreMesh(core_axis_name="c", subcore_axis_name="s")

    n_win = per_w // SC_WIN
    assert n_win % SC_NBUF == 0

    @functools.partial(
        pl.kernel, mesh=mesh,
        out_type=jax.ShapeDtypeStruct((n_slots, d), h2.dtype),
        scratch_types=[pltpu.VMEM((n_slots,), I32), pltpu.VMEM((SC_CHUNK,), I32),
                       pltpu.VMEM((SC_NBUF, SC_WIN, d), h2.dtype), pltpu.SemaphoreType.DMA((SC_NBUF,)),
                       pltpu.SemaphoreType.DMA((SC_NBUF,))],
        compiler_params=pltpu.CompilerParams(needs_layout_passes=False),
    )
    def k(h_hbm, d0_hbm, d1_hbm, o_hbm, table, chunk, rows, gsems, wsems):
        lane = lax.iota(I32, SC_LANES)

        @pl.loop(0, n_slots, step=SC_LANES)
        def _(s):
            v = s + lane
            v = jnp.where(v >= n_tok, v - n_tok, v)
            v = jnp.where(v >= n_tok, v - n_tok, v)
            table[pl.ds(s, SC_LANES)] = v

        for d_hbm in (d0_hbm, d1_hbm):
            @pl.loop(0, n_tok, step=SC_CHUNK)
            def _(t0, d_hbm=d_hbm):
                pltpu.sync_copy(d_hbm.at[pl.ds(t0, SC_CHUNK)], chunk)

                @pl.loop(0, SC_CHUNK, step=SC_LANES)
                def _(j):
                    plsc.store_scatter(table, [chunk[pl.ds(j, SC_LANES)]], t0 + j + lane)

        base = (lax.axis_index("s") * nc + lax.axis_index("c")) * per_w

        ahead = SC_NBUF // 2

        def gather(w, b):
            return pltpu.make_async_copy(
                h_hbm.at[table.at[pl.ds(base + w * SC_WIN, SC_WIN)]], rows.at[b], gsems.at[b])

        def put(w, b):
            return pltpu.make_async_copy(rows.at[b], o_hbm.at[pl.ds(base + w * SC_WIN, SC_WIN)], wsems.at[b])

        for b in range(ahead):
            gather(b, b).start()

        @pl.loop(0, n_win, step=SC_NBUF)
        def _(g):
            for b in range(SC_NBUF):
                w = g + b
                pb = (b + ahead) % SC_NBUF
                gather(w, b).wait()
                put(w, b).start()

                @pl.when(w >= ahead)
                def _():
                    put(w - ahead, pb).wait()

                @pl.when(w + ahead < n_win)
                def _():
                    gather(w + ahead, pb).start()

        for b in range(ahead):
            w = n_win - ahead + b
            put(w, w % SC_NBUF).wait()

    return k(h2, dest0, dest1)


def _experts_body(slot_ref, cast_ref, pre_ref, misc_ref, x_ref, wg_hbm, wu_hbm, wd_hbm, y_ref,
                  sg, su, sd, wgb, wub, wdb, sem):
    blk = pl.program_id(0)
    e0, pre0, n_used = misc_ref[0], misc_ref[1], misc_ref[2]

    def copies(e):
        return (pltpu.make_async_copy(wg_hbm.at[e], sg, sem.at[0]),
                pltpu.make_async_copy(wu_hbm.at[e], su, sem.at[1]),
                pltpu.make_async_copy(wd_hbm.at[e], sd, sem.at[2]))

    def fetch(e):
        for c in copies(e):
            c.start(priority=1)

    def land(slot):
        for c in copies(0):
            c.wait()
        wgb[slot] = sg[...].astype(wgb.dtype)
        wub[slot] = su[...].astype(wub.dtype)
        wdb[slot] = sd[...].astype(wdb.dtype)

    @pl.when(blk == 0)
    def _():
        fetch(e0)
        land(0)

        @pl.when(pre0 >= 0)
        def _():
            fetch(pre0)

    @pl.when(blk < n_used)
    def _():
        slot = slot_ref[blk]
        xb = _unpack_halves(x_ref[...], _MXU_DTYPE)
        a = _silu(_dot(xb, wgb[slot])) * _dot(xb, wub[slot])
        y_ref[...] = _dot(a.astype(_MXU_DTYPE), wdb[slot])

        @pl.when(cast_ref[blk] == 1)
        def _():
            land(1 - slot)

            @pl.when(pre_ref[blk] >= 0)
            def _():
                fetch(pre_ref[blk])

    @pl.when(blk >= n_used)
    def _():
        y_ref[...] = jnp.zeros_like(y_ref)


def _experts(xs, n_used, pends, w_gate, w_up, w_down):
    n_slots, dpk = xs.shape
    bm = MOE_BM
    n_blocks = n_slots // bm
    d, de = w_gate.shape[1:]
    last_used = n_used[0] - 1
    total = pends[-1]
    eid = jnp.arange(N_EXPERTS, dtype=I32)[None, :]
    owner = lambda start: jnp.minimum(jnp.sum((pends[None, :] <= start[:, None]).astype(I32), axis=1), N_EXPERTS - 1)
    end_of = lambda e: jnp.sum(jnp.where(eid == e[:, None], pends[None, :], 0), axis=1)
    be = owner(jnp.minimum(jnp.arange(n_blocks, dtype=I32), last_used) * bm)
    end_e = end_of(be)
    has_next = end_e < total
    nxt = owner(end_e)
    end_n = end_of(nxt)
    nxt2 = jnp.where(jnp.logical_and(has_next, end_n < total), owner(end_n), -1)
    is_last = jnp.concatenate([be[1:] != be[:-1], jnp.ones((1,), jnp.bool_)])
    is_last = jnp.logical_or(is_last, jnp.arange(n_blocks) >= last_used)
    cast = jnp.logical_and(is_last, has_next).astype(I32)
    pre = jnp.where(cast == 1, nxt2, -1).astype(I32)
    is_first = jnp.concatenate([jnp.ones((1,), jnp.bool_), be[1:] != be[:-1]])
    slot = ((jnp.cumsum(is_first.astype(I32)) - 1) % 2).astype(I32)
    misc = jnp.stack([be[0], jnp.where(has_next[0], nxt[0], -1), n_used[0]]).astype(I32)
    hbm = pl.BlockSpec(memory_space=pl.ANY)
    return pl.pallas_call(
        _experts_body,
        grid_spec=pltpu.PrefetchScalarGridSpec(
            num_scalar_prefetch=4,
            grid=(n_blocks,),
            in_specs=[pl.BlockSpec((bm, dpk), lambda j, sl, ca, pr, mi: (jnp.minimum(j, mi[2] - 1), 0)),
                      hbm, hbm, hbm],
            out_specs=pl.BlockSpec((bm, d), lambda j, sl, ca, pr, mi: (j, 0)),
            scratch_shapes=[pltpu.VMEM((d, de), w_gate.dtype), pltpu.VMEM((d, de), w_up.dtype),
                            pltpu.VMEM((de, d), w_down.dtype),
                            pltpu.VMEM((2, d, de), _MXU_DTYPE), pltpu.VMEM((2, d, de), _MXU_DTYPE),
                            pltpu.VMEM((2, de, d), _MXU_DTYPE), pltpu.SemaphoreType.DMA((3,))],
        ),
        out_shape=jax.ShapeDtypeStruct((n_slots, d), F32),
        compiler_params=_cparams(("arbitrary",)),
        name="experts",
    )(slot, cast, pre, misc, xs, w_gate, w_up, w_down)


def _combine_body(x1_ref, gate_ref, d0_ref, d1_ref, n0_ref, n1_ref, mod_ref, g_ref, ys_ref, o_ref, buf_ref, sem,
                  *, tf):
    i = pl.program_id(0)
    cur = i % 2

    def gather(da_ref, db_ref, half):
        def issue(c, carry):
            for u in range(SUBLANES):
                r = c * SUBLANES + u
                for j, d_ref in enumerate((da_ref, db_ref)):
                    pltpu.make_async_copy(ys_ref.at[pl.ds(d_ref[r], 1), :],
                                          buf_ref.at[half, j, c, pl.ds(u, 1), :], sem.at[half]).start(priority=j)
            return carry

        lax.fori_loop(0, tf // SUBLANES, issue, 0)

    @pl.when(i == 0)
    def _():
        gather(d0_ref, d1_ref, 0)

    @pl.when(i + 1 < pl.num_programs(0))
    def _():
        gather(n0_ref, n1_ref, 1 - cur)

    for _ in range(2):
        pltpu.make_async_copy(ys_ref.at[pl.ds(0, tf), :], ys_ref.at[pl.ds(0, tf), :], sem.at[cur]).wait()

    gates = gate_ref[...]
    d = o_ref.shape[1]
    moe = buf_ref[cur, 0].reshape(tf, d) * gates[:, 0:1] + buf_ref[cur, 1].reshape(tf, d) * gates[:, 1:2]
    x2 = x1_ref[...] + mod_ref[0, 5:6, :] * moe
    ms = jnp.mean(x2 * x2, axis=-1, keepdims=True)
    o_ref[...] = x2 * lax.rsqrt(ms + EPS) * g_ref[...]


def _combine(x1, gates, dest0, dest1, mod, g_final, ys, seq):
    n_tok, d = x1.shape
    tf = COMB_TF
    n_tiles = n_tok // tf
    tok = pl.BlockSpec((tf,), lambda i: (i,), memory_space=pltpu.SMEM)
    nxt = pl.BlockSpec((tf,), lambda i: (jnp.minimum(i + 1, n_tiles - 1),), memory_space=pltpu.SMEM)
    return pl.pallas_call(
        functools.partial(_combine_body, tf=tf),
        grid=(n_tiles,),
        in_specs=[pl.BlockSpec((tf, d), lambda i: (i, 0)),
                  pl.BlockSpec((tf, LANES), lambda i: (i, 0)),
                  tok, tok, nxt, nxt,
                  pl.BlockSpec((1, 6, d), lambda i: ((i * tf) // seq, 0, 0)),
                  pl.BlockSpec((1, d), lambda i: (0, 0)),
                  pl.BlockSpec(memory_space=pl.ANY)],
        out_specs=pl.BlockSpec((tf, d), lambda i: (i, 0)),
        scratch_shapes=[pltpu.VMEM((2, 2, tf // SUBLANES, SUBLANES, d), F32), pltpu.SemaphoreType.DMA((2,))],
        out_shape=jax.ShapeDtypeStruct((n_tok, d), F32),
        compiler_params=_cparams(("arbitrary",)),
        name="combine",
    )(x1, gates, dest0, dest1, dest0, dest1, mod, g_final, ys)


def kernel(x, c, w_ada, b_ada, g_mix, w_in, w_pool, pool_scale, rel_bias, w_out, g_ffn, w_group, b_group,
           w_router, b_router, w_gate, w_up, w_down, g_final):
    bsz, seq, d = x.shape
    n_tok = bsz * seq
    depth = w_ada.shape[0]
    assert depth == 1, "the final norm is fused into the only layer's combine"
    topk = min(TOPK_MAX, seq // 4)
    bias = _bias_tiles(rel_bias, ATT_T)
    bm = MOE_BM
    n_blocks = -(-(n_tok * 2) // bm) + N_EXPERTS
    n_slots = n_blocks * bm
    xt = x.reshape(n_tok, d)
    for i in range(depth):
        mod = _ada(c, w_ada[i], b_ada[i][None, :]).reshape(bsz, 6, d)
        u, q, k, vt, qi, kia, kib, wit = _proj(xt.reshape(bsz, seq, d), mod, g_mix[i][None, :], w_in[i])
        attn = _dsa(q, qi, wit, kia, kib, k, vt, bias, topk)
        w_r = jnp.concatenate(
            [w_group[i], w_router[i], jnp.zeros((d, LANES - N_GROUPS - N_EXPERTS), F32)], axis=1).astype(_MXU_DTYPE)
        b_r = jnp.concatenate(
            [b_group[i], b_router[i], jnp.zeros((LANES - N_GROUPS - N_EXPERTS,), F32)])[None, :]
        x1, h2, logits = _mix(u.reshape(n_tok, -1), attn.reshape(n_tok, -1), xt, mod,
                              w_pool[i].astype(_MXU_DTYPE), pool_scale[i][None, :],
                              w_out[i].astype(_MXU_DTYPE), g_ffn[i][None, :], w_r, b_r, seq)
        route_t, gates, cnt = _route(logits)
        counts = cnt[0, :N_EXPERTS].astype(I32)
        padded = (counts + bm - 1) // bm * bm
        pends = jnp.cumsum(padded)
        pstart = pends - padded
        n_used = (pends[-1:] // bm).astype(I32)
        dest = _dest(route_t, pstart)
        dest0, dest1 = dest[0], dest[1]
        xs = _sc_dispatch(h2, dest0, dest1, n_slots)
        ys = _experts(xs, n_used, pends, w_gate[i], w_up[i], w_down[i])
        xt = _combine(x1, gates, dest0, dest1, mod, g_final[None, :], ys, seq)
    return xt.reshape(bsz, seq, d)
```

```python
import functools
import math

import numpy as np
import jax
import jax.numpy as jnp
from jax import lax
from jax.experimental import pallas as pl
from jax.experimental.pallas import tpu as pltpu
from jax.experimental.pallas import tpu_sc as plsc

F32 = jnp.float32
I32 = jnp.int32
_MXU_DTYPE = jnp.bfloat16

POOL_WINDOWS = (2, 4, 8, 16)
N_POOL_GROUPS = 4
HEAD_DIM = 128
N_HEADS = 8
N_KV_HEADS = 2
HEAD_GROUP = N_HEADS // N_KV_HEADS
IDX_HEADS = 16
IDX_DIM = 64
TOPK_MAX = 256
N_BUCKETS = 32
MAX_DISTANCE = 128
N_GROUPS = 4
EXPERTS_PER_GROUP = 8
N_EXPERTS = N_GROUPS * EXPERTS_PER_GROUP
EPS = 1e-6
NEG_INF = -1e30
BIG = 3e38
LOG2E = 1.4426950408889634

LANES = 128
SUBLANES = 8
VMEM_LIMIT_BYTES = 56 * 1024 * 1024

ADA_TN = 1024
PROJ_TM = 256
ATT_T = 256
SCORE_SUB = 256
COUNT_RB = 256
MIX_TM = 256
HALO = 16
ROUTE_TR = 1024
MOE_BM = 256
COMB_TF = 256
SEL_FIXED_ITERS = 18
SEL_MAX_ITERS = 80


def _cparams(sem):
    return pltpu.CompilerParams(dimension_semantics=sem, vmem_limit_bytes=VMEM_LIMIT_BYTES)


def _silu(x):
    return x * (1.0 / (1.0 + jnp.exp(-x)))


def _dot(a, b):
    return jnp.dot(a, b, preferred_element_type=F32)


def _dot_nt(a, b):
    return lax.dot_general(a, b, (((1,), (1,)), ((), ())), preferred_element_type=F32)


def _ada_body(c_ref, w_ref, b_ref, o_ref):
    s = _silu(c_ref[...])
    o_ref[...] = _dot(s.astype(_MXU_DTYPE), w_ref[...].astype(_MXU_DTYPE)) + b_ref[...]


def _ada(c, w, b):
    bsz, d = c.shape
    n = w.shape[1]
    return pl.pallas_call(
        _ada_body,
        grid=(n // ADA_TN,),
        in_specs=[pl.BlockSpec((bsz, d), lambda j: (0, 0)),
                  pl.BlockSpec((d, ADA_TN), lambda j: (0, j)),
                  pl.BlockSpec((1, ADA_TN), lambda j: (0, j))],
        out_specs=pl.BlockSpec((bsz, ADA_TN), lambda j: (0, j)),
        out_shape=jax.ShapeDtypeStruct((bsz, n), F32),
        compiler_params=_cparams(("arbitrary",)),
        name="ada",
    )(c, w, b)


def _rms_mod(x, g, shift, scale):
    ms = jnp.mean(x * x, axis=-1, keepdims=True)
    y = x * lax.rsqrt(ms + EPS) * g
    return y * (1.0 + scale) + shift


def _proj_body(x_ref, mod_ref, g_ref, w_ref, u_ref, q_ref, k_ref, vt_ref, qi_ref, kia_ref, kib_ref,
               wit_ref, *, cuts):
    h = _rms_mod(x_ref[0], g_ref[...], mod_ref[0, 0:1, :], mod_ref[0, 1:2, :])
    hb = h.astype(_MXU_DTYPE)

    def seg(name):
        lo, hi = cuts[name]
        return _dot(hb, w_ref[:, lo:hi])

    u_ref[0] = seg("u")
    q_ref[0] = seg("q").astype(q_ref.dtype)
    k_ref[0] = seg("k").astype(k_ref.dtype)
    vt_ref[0] = seg("v").T.astype(vt_ref.dtype)
    qi_ref[0] = seg("qi").astype(qi_ref.dtype)
    tail = seg("tail")
    tail = jnp.concatenate([tail, jnp.zeros((tail.shape[0], LANES - tail.shape[1]), F32)], axis=1)
    swapped = pltpu.roll(tail, LANES - IDX_DIM, 1)
    lane = lax.broadcasted_iota(I32, tail.shape, 1)
    kia_ref[0] = jnp.where(lane < IDX_DIM, tail, 0.0).astype(kia_ref.dtype)
    kib_ref[0] = jnp.where(lane >= LANES - IDX_DIM, swapped, 0.0).astype(kib_ref.dtype)
    wit_ref[0] = swapped.T[:IDX_HEADS, :]


def _proj(x, mod, g_mix, w_in):
    bsz, s, d = x.shape
    pool_w = d // 2
    attn_w = N_HEADS * HEAD_DIM
    kv_w = N_KV_HEADS * HEAD_DIM
    qi_w = IDX_HEADS * IDX_DIM
    c = np.cumsum([0, pool_w, attn_w, kv_w, kv_w, qi_w, IDX_DIM, IDX_HEADS])
    assert IDX_DIM + IDX_HEADS <= LANES and w_in.shape[1] == c[7]
    wp = w_in.astype(_MXU_DTYPE)
    cuts = {"u": (int(c[0]), int(c[1])), "q": (int(c[1]), int(c[2])), "k": (int(c[2]), int(c[3])),
            "v": (int(c[3]), int(c[4])), "qi": (int(c[4]), int(c[5])), "tail": (int(c[5]), int(c[7]))}
    tm = PROJ_TM
    pw = wp.shape[1]
    row = lambda w: pl.BlockSpec((1, tm, w), lambda b, i: (b, i, 0))
    out_shapes = [
        jax.ShapeDtypeStruct((bsz, s, pool_w), F32),
        jax.ShapeDtypeStruct((bsz, s, attn_w), _MXU_DTYPE),
        jax.ShapeDtypeStruct((bsz, s, kv_w), _MXU_DTYPE),
        jax.ShapeDtypeStruct((bsz, kv_w, s), _MXU_DTYPE),
        jax.ShapeDtypeStruct((bsz, s, qi_w), _MXU_DTYPE),
        jax.ShapeDtypeStruct((bsz, s, LANES), _MXU_DTYPE),
        jax.ShapeDtypeStruct((bsz, s, LANES), _MXU_DTYPE),
        jax.ShapeDtypeStruct((bsz, IDX_HEADS, s), F32),
    ]
    out_specs = [row(pool_w), row(attn_w), row(kv_w),
                 pl.BlockSpec((1, kv_w, tm), lambda b, i: (b, 0, i)),
                 row(qi_w), row(LANES), row(LANES),
                 pl.BlockSpec((1, IDX_HEADS, tm), lambda b, i: (b, 0, i))]
    return pl.pallas_call(
        functools.partial(_proj_body, cuts=cuts),
        grid=(bsz, s // tm),
        in_specs=[pl.BlockSpec((1, tm, d), lambda b, i: (b, i, 0)),
                  pl.BlockSpec((1, 6, d), lambda b, i: (b, 0, 0)),
                  pl.BlockSpec((1, d), lambda b, i: (0, 0)),
                  pl.BlockSpec((d, pw), lambda b, i: (0, 0))],
        out_specs=out_specs,
        out_shape=out_shapes,
        compiler_params=_cparams(("arbitrary", "arbitrary")),
        name="proj",
    )(x, mod, g_mix, wp)


def _bucket_starts():
    max_exact = N_BUCKETS // 2
    d = np.arange(1, 4 * MAX_DISTANCE, dtype=np.float32)
    large = max_exact + (np.log(d / np.float32(max_exact)) / np.float32(math.log(MAX_DISTANCE / max_exact))
                         * np.float32(N_BUCKETS - max_exact)).astype(np.int32)
    large = np.minimum(large, N_BUCKETS - 1)
    bucket = np.where(d < max_exact, d.astype(np.int32), large)
    bucket = np.concatenate([[0], bucket])
    starts = [int(np.argmax(bucket >= b)) for b in range(N_BUCKETS)]
    assert all(np.all(bucket[starts[b]:] >= b) for b in range(N_BUCKETS))
    assert starts[N_BUCKETS - 1] <= MAX_DISTANCE
    return starts


def _bias_body(rb_ref, o_ref, *, t, starts):
    diff = pl.program_id(0)
    s_l = lax.broadcasted_iota(I32, (t, t), 0)
    t_l = lax.broadcasted_iota(I32, (t, t), 1)
    dist = diff * t + t_l - s_l
    for h in range(N_HEADS):
        val = jnp.full((t, t), rb_ref[0, h], F32)
        for b in range(1, N_BUCKETS):
            val = jnp.where(dist >= starts[b], rb_ref[b, h], val)
        hh = h % HEAD_GROUP
        o_ref[0, h // HEAD_GROUP, :, hh * t:(hh + 1) * t] = (val - rb_ref[N_BUCKETS - 1, h]) * LOG2E


def _bias_tiles(rel_bias, t):
    assert t >= MAX_DISTANCE
    return pl.pallas_call(
        functools.partial(_bias_body, t=t, starts=_bucket_starts()),
        grid=(2,),
        in_specs=[pl.BlockSpec(memory_space=pltpu.SMEM)],
        out_specs=pl.BlockSpec((1, N_KV_HEADS, t, HEAD_GROUP * t), lambda i: (i, 0, 0, 0)),
        out_shape=jax.ShapeDtypeStruct((2, N_KV_HEADS, t, HEAD_GROUP * t), F32),
        compiler_params=_cparams(("arbitrary",)),
        name="bias_tiles",
    )(rel_bias)


def _dsa_body(q_ref, qi_ref, wit_ref, kia_ref, kib_ref, k_ref, vt_ref, bias_ref, o_ref,
              score_ref, mask_ref, acc_ref, *, t, topk):
    i = pl.program_id(1)
    nk = i + 1
    t_glob = i * t + lax.broadcasted_iota(I32, (1, t), 1)
    idx_scale = (IDX_DIM ** -0.5) * (IDX_HEADS ** -0.5)

    def score_chunk(c, carry):
        cnt, mn, mx = carry
        r0 = pl.multiple_of(c * SCORE_SUB, SCORE_SUB)
        ka = kia_ref[0, pl.ds(r0, SCORE_SUB), :]
        kb = kib_ref[0, pl.ds(r0, SCORE_SUB), :]
        acc = jnp.zeros((SCORE_SUB, t), F32)
        for j in range(IDX_HEADS // 2):
            qp = qi_ref[0, :, j * LANES:(j + 1) * LANES]
            da = _dot_nt(ka, qp)
            db = _dot_nt(kb, qp)
            acc = acc + jnp.maximum(da, 0.0) * wit_ref[0, 2 * j:2 * j + 1, :]
            acc = acc + jnp.maximum(db, 0.0) * wit_ref[0, 2 * j + 1:2 * j + 2, :]
        sc = acc * idx_scale
        s_glob = r0 + lax.broadcasted_iota(I32, (SCORE_SUB, 1), 0)
        b = jnp.where(s_glob <= t_glob, sc, NEG_INF)
        score_ref[pl.ds(r0, SCORE_SUB), :] = b
        real = b > NEG_INF
        rows8 = lambda a: a.reshape(SCORE_SUB // SUBLANES, SUBLANES, t)
        cnt = cnt + jnp.sum(rows8(real.astype(I32)), axis=0)
        mn = jnp.minimum(mn, jnp.min(rows8(jnp.where(real, b, BIG)), axis=0))
        mx = jnp.maximum(mx, jnp.max(rows8(b), axis=0))
        return cnt, mn, mx

    cnt8, mn8, mx8 = lax.fori_loop(
        0, nk * (t // SCORE_SUB), score_chunk,
        (jnp.zeros((SUBLANES, t), I32), jnp.full((SUBLANES, t), BIG, F32), jnp.full((SUBLANES, t), -BIG, F32)))

    nblk = nk * (t // COUNT_RB)
    fold = lambda m: jnp.sum(m.reshape(COUNT_RB // SUBLANES, SUBLANES, t), axis=0)

    def blk(r):
        return score_ref[pl.ds(pl.multiple_of(r * COUNT_RB, COUNT_RB), COUNT_RB), :]

    def count_ge(c):
        def body(r, acc):
            return acc + fold((blk(r) >= c).astype(I32))
        acc = lax.fori_loop(0, nblk, body, jnp.zeros((SUBLANES, t), I32))
        return jnp.sum(acc, axis=0, keepdims=True)

    n_real = jnp.sum(cnt8, axis=0, keepdims=True)
    rmin = jnp.min(mn8, axis=0, keepdims=True)
    rmax = jnp.max(mx8, axis=0, keepdims=True)
    n_max = count_ge(rmax)
    degenerate = jnp.logical_and(n_real >= topk, n_max >= topk)
    lo0 = jnp.where(n_real < topk, NEG_INF, jnp.where(degenerate, rmax, rmin))
    hi0 = jnp.where(degenerate, BIG, rmax)
    done0 = jnp.where(n_real <= topk, 1, 0)

    def bisect(st):
        lo, hi, done, stuck = st
        c = lo + (hi - lo) * 0.5
        active = (done + stuck) == 0
        has_mid = jnp.logical_and(c > lo, c < hi)
        n = count_ge(c)
        upd = jnp.logical_and(active, has_mid)
        lo = jnp.where(jnp.logical_and(upd, n >= topk), c, lo)
        hi = jnp.where(jnp.logical_and(upd, n < topk), c, hi)
        done = jnp.where(jnp.logical_and(upd, n == topk), 1, done)
        stuck = jnp.where(jnp.logical_and(active, jnp.logical_not(has_mid)), 1, stuck)
        return lo, hi, done, stuck

    st = lax.fori_loop(0, SEL_FIXED_ITERS, lambda _, s: bisect(s), (lo0, hi0, done0, jnp.zeros((1, t), I32)))

    def sel_cond(s):
        return jnp.logical_and(s[0] < SEL_MAX_ITERS, jnp.min(s[3] + s[4]) == 0)

    _, lo, hi, done, _ = lax.while_loop(sel_cond, lambda s: (s[0] + 1,) + bisect(s[1:]), (jnp.int32(0),) + st)

    def write_mask(r, cut):
        r0 = pl.multiple_of(r * COUNT_RB, COUNT_RB)
        b = score_ref[pl.ds(r0, COUNT_RB), :]
        s_glob = r0 + lax.broadcasted_iota(I32, (COUNT_RB, 1), 0)
        sel = jnp.logical_and(b >= lo, s_glob <= t_glob)
        if cut is not None:
            sel = jnp.logical_and(sel, jnp.logical_or(b >= hi, s_glob < cut))
        mask_ref[pl.ds(r0, COUNT_RB), :] = jnp.concatenate([jnp.where(sel, 0.0, NEG_INF)] * HEAD_GROUP, axis=1)

    any_tied = jnp.min(done) == 0

    @pl.when(jnp.logical_not(any_tied))
    def _():
        def body(r, carry):
            write_mask(r, None)
            return carry
        lax.fori_loop(0, nblk, body, 0)

    @pl.when(any_tied)
    def _():
        need = topk - count_ge(hi)

        def count_tie_below(x):
            def body(r, acc):
                r0 = pl.multiple_of(r * COUNT_RB, COUNT_RB)
                b = score_ref[pl.ds(r0, COUNT_RB), :]
                s_glob = r0 + lax.broadcasted_iota(I32, (COUNT_RB, 1), 0)
                m = jnp.logical_and(jnp.logical_and(b >= lo, b < hi), s_glob < x)
                return acc + fold(m.astype(I32))
            acc = lax.fori_loop(0, nblk, body, jnp.zeros((SUBLANES, t), I32))
            return jnp.sum(acc, axis=0, keepdims=True)

        nbits = int(score_ref.shape[0]).bit_length()

        def bit_body(bi, x):
            c = x + jnp.left_shift(jnp.int32(1), nbits - 1 - bi)
            return jnp.where(count_tie_below(c) < need, c, x)

        x = lax.fori_loop(0, nbits, bit_body, jnp.zeros((1, t), I32))
        cut = jnp.where(done > 0, jnp.int32(2 ** 30), x + 1)

        def body(r, carry):
            write_mask(r, cut)
            return carry
        lax.fori_loop(0, nblk, body, 0)

    scale = HEAD_DIM ** -0.5
    gt = HEAD_GROUP * t

    def col_reduce(op, a):
        part = op(a.reshape(a.shape[0] // SUBLANES, SUBLANES, a.shape[1]), axis=0)
        return op(part, axis=0, keepdims=True)

    qgs = [jnp.concatenate(
        [q_ref[0, :, h * HEAD_DIM:(h + 1) * HEAD_DIM] for h in range(g * HEAD_GROUP, (g + 1) * HEAD_GROUP)],
        axis=0) for g in range(N_KV_HEADS)]
    acc_ref[...] = jnp.zeros_like(acc_ref)

    def att_chunk(kc, carry, near):
        r0 = pl.multiple_of(kc * t, t)
        mk = mask_ref[pl.ds(r0, t), :]
        new = []
        for g in range(N_KV_HEADS):
            m, l = carry[g]
            kk = k_ref[0, pl.ds(r0, t), g * HEAD_DIM:(g + 1) * HEAD_DIM]
            s = _dot_nt(kk, qgs[g]) * (scale * LOG2E) + mk
            if near:
                s = s + bias_ref[i - kc, g]
            m_new = jnp.maximum(m, col_reduce(jnp.max, s))
            alpha = jnp.exp2(m - m_new)
            p = jnp.exp2(s - m_new)
            l = alpha * l + col_reduce(jnp.sum, p)
            vt = vt_ref[0, g * HEAD_DIM:(g + 1) * HEAD_DIM, pl.ds(r0, t)]
            acc_ref[g] = alpha * acc_ref[g] + _dot(vt, p.astype(_MXU_DTYPE))
            new.append((m_new, l))
        return tuple(new)

    init = (jnp.full((1, gt), NEG_INF, F32), jnp.zeros((1, gt), F32))
    n_far = jnp.maximum(i - 1, 0)
    far = lax.fori_loop(0, n_far, functools.partial(att_chunk, near=False), (init,) * N_KV_HEADS)
    fin = lax.fori_loop(n_far, nk, functools.partial(att_chunk, near=True), far)
    for g in range(N_KV_HEADS):
        out = acc_ref[g] / fin[g][1]
        for hh in range(HEAD_GROUP):
            h = g * HEAD_GROUP + hh
            o_ref[0, :, h * HEAD_DIM:(h + 1) * HEAD_DIM] = out[:, hh * t:(hh + 1) * t].T.astype(o_ref.dtype)


def _dsa(q, qi, wit, kia, kib, k, vt, bias, topk):
    bsz, s, aw = q.shape
    t = ATT_T
    assert s % t == 0 and topk <= t
    row = lambda w: pl.BlockSpec((1, t, w), lambda b, i: (b, i, 0))
    full = lambda a: pl.BlockSpec((1,) + a.shape[1:], lambda b, i: (b, 0, 0))
    return pl.pallas_call(
        functools.partial(_dsa_body, t=t, topk=topk),
        grid=(bsz, s // t),
        in_specs=[row(aw), row(qi.shape[2]),
                  pl.BlockSpec((1, IDX_HEADS, t), lambda b, i: (b, 0, i)),
                  full(kia), full(kib), full(k), full(vt),
                  pl.BlockSpec(bias.shape, lambda b, i: (0, 0, 0, 0))],
        out_specs=row(aw),
        out_shape=jax.ShapeDtypeStruct((bsz, s, aw), _MXU_DTYPE),
        scratch_shapes=[pltpu.VMEM((s, t), F32), pltpu.VMEM((s, HEAD_GROUP * t), F32),
                        pltpu.VMEM((N_KV_HEADS, HEAD_DIM, HEAD_GROUP * t), F32)],
        compiler_params=_cparams(("arbitrary", "arbitrary")),
        name="dsa",
    )(q, qi, wit, kia, kib, k, vt, bias)


def _pack_halves(hb):
    assert jnp.dtype(hb.dtype).itemsize == 2
    n = hb.shape[1] // 2
    hi = lax.bitcast_convert_type(hb[:, :n].astype(F32), I32)
    lo = lax.bitcast_convert_type(hb[:, n:].astype(F32), I32)
    return jnp.bitwise_or(hi, lax.shift_right_logical(lo, jnp.full(lo.shape, 16, I32)))


def _unpack_halves(w, dtype):
    hi = lax.bitcast_convert_type(jnp.bitwise_and(w, jnp.int32(-65536)), F32).astype(dtype)
    lo = lax.bitcast_convert_type(jnp.left_shift(w, 16), F32).astype(dtype)
    return jnp.concatenate([hi, lo], axis=1)


def _mix_body(u_ref, halo_ref, a_ref, x_ref, mod_ref, wpool_ref, ps_ref, wout_ref, g_ref, wr_ref, br_ref,
              x1_ref, h2_ref, lg_ref, *, seq, tm):
    t0 = (pl.program_id(0) * tm) % seq
    u = u_ref[...]
    halo = jnp.where(t0 == 0, 0.0, halo_ref[...])
    ext = jnp.concatenate([halo, u], axis=0)
    pos = t0 + lax.broadcasted_iota(I32, (tm, 1), 0)
    gw = u.shape[1] // N_POOL_GROUPS
    ys = []
    for g, w in enumerate(POOL_WINDOWS):
        a = ext[:, g * gw:(g + 1) * gw]
        sft = 1
        while sft < w:
            a = a + pltpu.roll(a, sft, 0)
            sft *= 2
        cnt = jnp.minimum(pos + 1, w).astype(F32)
        p = a[HALO:, :] / cnt - u[:, g * gw:(g + 1) * gw]
        ys.append(_dot(p.astype(_MXU_DTYPE), wpool_ref[g]) * ps_ref[:, g * gw:(g + 1) * gw])
    pool = jnp.concatenate(ys, axis=1).astype(_MXU_DTYPE)
    pw = pool.shape[1]
    mixed = _dot(pool, wout_ref[:pw, :]) + _dot(a_ref[...], wout_ref[pw:, :])
    x1 = x_ref[...] + mod_ref[0, 2:3, :] * mixed
    x1_ref[...] = x1
    h2 = _rms_mod(x1, g_ref[...], mod_ref[0, 3:4, :], mod_ref[0, 4:5, :]).astype(_MXU_DTYPE)
    h2_ref[...] = _pack_halves(h2)
    lg_ref[...] = _dot(h2, wr_ref[...]) + br_ref[...]


def _mix(u, attn, x, mod, w_pool, pool_scale, w_out, g_ffn, w_r, b_r, seq):
    n_tok, d = x.shape
    tm = MIX_TM
    assert seq % tm == 0 and tm % HALO == 0
    pw = u.shape[1]
    aw = attn.shape[1]
    rows = lambda w: pl.BlockSpec((tm, w), lambda i: (i, 0))
    const = lambda a: pl.BlockSpec(a.shape, lambda i: (0,) * a.ndim)
    return pl.pallas_call(
        functools.partial(_mix_body, seq=seq, tm=tm),
        grid=(n_tok // tm,),
        in_specs=[rows(pw),
                  pl.BlockSpec((HALO, pw), lambda i: (jnp.maximum(i * (tm // HALO) - 1, 0), 0)),
                  rows(aw), rows(d),
                  pl.BlockSpec((1, 6, d), lambda i: ((i * tm) // seq, 0, 0)),
                  const(w_pool), const(pool_scale), const(w_out), const(g_ffn), const(w_r), const(b_r)],
        out_specs=[rows(d), rows(d // 2), rows(LANES)],
        out_shape=[jax.ShapeDtypeStruct((n_tok, d), F32), jax.ShapeDtypeStruct((n_tok, d // 2), I32),
                   jax.ShapeDtypeStruct((n_tok, LANES), F32)],
        compiler_params=_cparams(("arbitrary",)),
        name="mix",
    )(u, u, attn, x, mod, w_pool, pool_scale, w_out, g_ffn, w_r, b_r)


def _route_body(lg_ref, rt_ref, gate_ref, cnt_ref, run_ref):
    @pl.when(pl.program_id(0) == 0)
    def _():
        run_ref[...] = jnp.zeros_like(run_ref)

    lg = lg_ref[...]
    tr = lg.shape[0]
    lane = lax.broadcasted_iota(I32, lg.shape, 1)
    gmask = lane < N_GROUPS
    gl = jnp.where(gmask, lg, -BIG)
    gmax = jnp.max(gl, axis=-1, keepdims=True)
    gsel = jnp.min(jnp.where(jnp.logical_and(gmask, gl == gmax), lane, LANES), axis=-1, keepdims=True)
    gsum = jnp.sum(jnp.where(gmask, jnp.exp(gl - gmax), 0.0), axis=-1, keepdims=True)
    p_g = 1.0 / gsum
    in_e = jnp.logical_and(lane >= N_GROUPS, lane < N_GROUPS + N_EXPERTS)
    emask = jnp.logical_and(in_e, lax.shift_right_arithmetic(lane - N_GROUPS, EXPERTS_PER_GROUP.bit_length() - 1) == gsel)
    el = jnp.where(emask, lg, -BIG)
    emax = jnp.max(el, axis=-1, keepdims=True)
    ex = jnp.where(emask, jnp.exp(el - emax), 0.0)
    ep = ex / jnp.sum(ex, axis=-1, keepdims=True)
    p1 = jnp.max(jnp.where(emask, ep, -1.0), axis=-1, keepdims=True)
    i1 = jnp.min(jnp.where(jnp.logical_and(emask, ep == p1), lane, LANES), axis=-1, keepdims=True)
    m2 = jnp.logical_and(emask, lane != i1)
    p2 = jnp.max(jnp.where(m2, ep, -1.0), axis=-1, keepdims=True)
    i2 = jnp.min(jnp.where(jnp.logical_and(m2, ep == p2), lane, LANES), axis=-1, keepdims=True)
    den = p1 + p2
    g1 = p_g * p1 / den
    g2 = p_g * p2 / den
    e1 = i1 - N_GROUPS
    e2 = i2 - N_GROUPS

    oh = jnp.logical_or(lane == e1, lane == e2)
    ri = lax.broadcasted_iota(I32, (tr, tr), 0)
    ci = lax.broadcasted_iota(I32, (tr, tr), 1)
    ltri = (ci < ri).astype(_MXU_DTYPE)
    before = _dot(ltri, oh.astype(_MXU_DTYPE)) + run_ref[...]
    r1 = jnp.sum(jnp.where(lane == e1, before, 0.0), axis=-1, keepdims=True)
    r2 = jnp.sum(jnp.where(lane == e2, before, 0.0), axis=-1, keepdims=True)
    run = run_ref[...] + jnp.sum(oh.astype(F32), axis=0, keepdims=True)
    run_ref[...] = run
    cnt_ref[...] = jnp.broadcast_to(run, cnt_ref.shape)

    route = jnp.where(lane == 0, e1.astype(F32),
                      jnp.where(lane == 1, e2.astype(F32),
                                jnp.where(lane == 2, r1, jnp.where(lane == 3, r2, 0.0))))
    rt_ref[...] = route.T[:SUBLANES, :].astype(I32)
    gate_ref[...] = jnp.where(lane == 0, g1, jnp.where(lane == 1, g2, 0.0))


def _route(logits):
    n_tok = logits.shape[0]
    tr = min(ROUTE_TR, n_tok)
    return pl.pallas_call(
        _route_body,
        grid=(n_tok // tr,),
        in_specs=[pl.BlockSpec((tr, LANES), lambda i: (i, 0))],
        out_specs=[pl.BlockSpec((SUBLANES, tr), lambda i: (0, i)),
                   pl.BlockSpec((tr, LANES), lambda i: (i, 0)),
                   pl.BlockSpec((SUBLANES, LANES), lambda i: (0, 0))],
        out_shape=[jax.ShapeDtypeStruct((SUBLANES, n_tok), I32),
                   jax.ShapeDtypeStruct((n_tok, LANES), F32),
                   jax.ShapeDtypeStruct((SUBLANES, LANES), F32)],
        scratch_shapes=[pltpu.VMEM((1, LANES), F32)],
        compiler_params=_cparams(("arbitrary",)),
        name="route",
    )(logits)


def _dest_body(ps_ref, rt_ref, o_ref):
    r = rt_ref[...]
    start = jnp.zeros_like(r)
    for e in range(N_EXPERTS):
        start = jnp.where(r == e, ps_ref[e], start)
    o_ref[...] = start + pltpu.roll(r, SUBLANES - 2, 0)


def _dest(route_t, pstart):
    n_tok = route_t.shape[1]
    tb = min(n_tok, 2048)
    return pl.pallas_call(
        _dest_body,
        grid_spec=pltpu.PrefetchScalarGridSpec(
            num_scalar_prefetch=1,
            grid=(n_tok // tb,),
            in_specs=[pl.BlockSpec((SUBLANES, tb), lambda i, ps: (0, i))],
            out_specs=pl.BlockSpec((SUBLANES, tb), lambda i, ps: (0, i)),
        ),
        out_shape=jax.ShapeDtypeStruct((SUBLANES, n_tok), I32),
        compiler_params=_cparams(("arbitrary",)),
        name="dest",
    )(pstart, route_t)


SC_LANES = 16
SC_WIN = 16
SC_NBUF = 4
SC_CHUNK = 8192


def _sc_dispatch(h2, dest0, dest1, n_slots):
    n_tok, d = h2.shape
    info = plsc.get_sparse_core_info()
    nc, ns = info.num_cores, info.num_subcores
    nw = nc * ns
    assert info.num_lanes == SC_LANES and n_slots % (nw * SC_WIN) == 0 and n_tok % SC_CHUNK == 0
    assert n_slots < 3 * n_tok
    per_w = n_slots // nw
    mesh = plsc.VectorSubcoreMesh(core_axis_name="c", subcore_axis_name="s")

    n_win = per_w // SC_WIN
    assert n_win % SC_NBUF == 0

    @functools.partial(
        pl.kernel, mesh=mesh,
        out_type=jax.ShapeDtypeStruct((n_slots, d), h2.dtype),
        scratch_types=[pltpu.VMEM((per_w,), I32), pltpu.VMEM((SC_CHUNK,), I32),
                       pltpu.VMEM((SC_NBUF, SC_WIN, d), h2.dtype), pltpu.SemaphoreType.DMA((SC_NBUF,)),
                       pltpu.SemaphoreType.DMA((SC_NBUF,))],
        compiler_params=pltpu.CompilerParams(needs_layout_passes=False),
    )
    def k(h_hbm, d0_hbm, d1_hbm, o_hbm, table, chunk, rows, gsems, wsems):
        lane = lax.iota(I32, SC_LANES)
        base = (lax.axis_index("s") * nc + lax.axis_index("c")) * per_w

        @pl.loop(0, per_w, step=SC_LANES)
        def _(s):
            v = base + s + lane
            v = jnp.where(v >= n_tok, v - n_tok, v)
            v = jnp.where(v >= n_tok, v - n_tok, v)
            table[pl.ds(s, SC_LANES)] = v

        for d_hbm in (d0_hbm, d1_hbm):
            @pl.loop(0, n_tok, step=SC_CHUNK)
            def _(t0, d_hbm=d_hbm):
                pltpu.sync_copy(d_hbm.at[pl.ds(t0, SC_CHUNK)], chunk)

                @pl.loop(0, SC_CHUNK, step=SC_LANES)
                def _(j):
                    r = chunk[pl.ds(j, SC_LANES)] - base
                    mine = jnp.logical_and(r >= 0, r < per_w)
                    plsc.store_scatter(table, [jnp.where(mine, r, 0)], t0 + j + lane, mask=mine)

        ahead = SC_NBUF // 2

        def gather(w, b):
            return pltpu.make_async_copy(
                h_hbm.at[table.at[pl.ds(w * SC_WIN, SC_WIN)]], rows.at[b], gsems.at[b])

        def put(w, b):
            return pltpu.make_async_copy(rows.at[b], o_hbm.at[pl.ds(base + w * SC_WIN, SC_WIN)], wsems.at[b])

        for b in range(ahead):
            gather(b, b).start()

        @pl.loop(0, n_win, step=SC_NBUF)
        def _(g):
            for b in range(SC_NBUF):
                w = g + b
                pb = (b + ahead) % SC_NBUF
                gather(w, b).wait()
                put(w, b).start()

                @pl.when(w >= ahead)
                def _():
                    put(w - ahead, pb).wait()

                @pl.when(w + ahead < n_win)
                def _():
                    gather(w + ahead, pb).start()

        for b in range(ahead):
            w = n_win - ahead + b
            put(w, w % SC_NBUF).wait()

    return k(h2, dest0, dest1)


def _experts_body(slot_ref, cast_ref, pre_ref, misc_ref, x_ref, wg_hbm, wu_hbm, wd_hbm, y_ref,
                  sg, su, sd, wgb, wub, wdb, sem):
    blk = pl.program_id(0)
    e0, pre0, n_used = misc_ref[0], misc_ref[1], misc_ref[2]

    def copies(e):
        return (pltpu.make_async_copy(wg_hbm.at[e], sg, sem.at[0]),
                pltpu.make_async_copy(wu_hbm.at[e], su, sem.at[1]),
                pltpu.make_async_copy(wd_hbm.at[e], sd, sem.at[2]))

    def fetch(e):
        for c in copies(e):
            c.start(priority=1)

    def land(slot):
        for c in copies(0):
            c.wait()
        wgb[slot] = sg[...].astype(wgb.dtype)
        wub[slot] = su[...].astype(wub.dtype)
        wdb[slot] = sd[...].astype(wdb.dtype)

    @pl.when(blk == 0)
    def _():
        fetch(e0)
        land(0)

        @pl.when(pre0 >= 0)
        def _():
            fetch(pre0)

    @pl.when(blk < n_used)
    def _():
        slot = slot_ref[blk]
        xb = _unpack_halves(x_ref[...], _MXU_DTYPE)
        a = _silu(_dot(xb, wgb[slot])) * _dot(xb, wub[slot])
        y_ref[...] = _dot(a.astype(_MXU_DTYPE), wdb[slot])

        @pl.when(cast_ref[blk] == 1)
        def _():
            land(1 - slot)

            @pl.when(pre_ref[blk] >= 0)
            def _():
                fetch(pre_ref[blk])

    @pl.when(blk >= n_used)
    def _():
        y_ref[...] = jnp.zeros_like(y_ref)


def _experts(xs, n_used, pends, w_gate, w_up, w_down):
    n_slots, dpk = xs.shape
    bm = MOE_BM
    n_blocks = n_slots // bm
    d, de = w_gate.shape[1:]
    last_used = n_used[0] - 1
    total = pends[-1]
    eid = jnp.arange(N_EXPERTS, dtype=I32)[None, :]
    owner = lambda start: jnp.minimum(jnp.sum((pends[None, :] <= start[:, None]).astype(I32), axis=1), N_EXPERTS - 1)
    end_of = lambda e: jnp.sum(jnp.where(eid == e[:, None], pends[None, :], 0), axis=1)
    be = owner(jnp.minimum(jnp.arange(n_blocks, dtype=I32), last_used) * bm)
    end_e = end_of(be)
    has_next = end_e < total
    nxt = owner(end_e)
    end_n = end_of(nxt)
    nxt2 = jnp.where(jnp.logical_and(has_next, end_n < total), owner(end_n), -1)
    is_last = jnp.concatenate([be[1:] != be[:-1], jnp.ones((1,), jnp.bool_)])
    is_last = jnp.logical_or(is_last, jnp.arange(n_blocks) >= last_used)
    cast = jnp.logical_and(is_last, has_next).astype(I32)
    pre = jnp.where(cast == 1, nxt2, -1).astype(I32)
    is_first = jnp.concatenate([jnp.ones((1,), jnp.bool_), be[1:] != be[:-1]])
    slot = ((jnp.cumsum(is_first.astype(I32)) - 1) % 2).astype(I32)
    misc = jnp.stack([be[0], jnp.where(has_next[0], nxt[0], -1), n_used[0]]).astype(I32)
    hbm = pl.BlockSpec(memory_space=pl.ANY)
    return pl.pallas_call(
        _experts_body,
        grid_spec=pltpu.PrefetchScalarGridSpec(
            num_scalar_prefetch=4,
            grid=(n_blocks,),
            in_specs=[pl.BlockSpec((bm, dpk), lambda j, sl, ca, pr, mi: (jnp.minimum(j, mi[2] - 1), 0)),
                      hbm, hbm, hbm],
            out_specs=pl.BlockSpec((bm, d), lambda j, sl, ca, pr, mi: (j, 0)),
            scratch_shapes=[pltpu.VMEM((d, de), w_gate.dtype), pltpu.VMEM((d, de), w_up.dtype),
                            pltpu.VMEM((de, d), w_down.dtype),
                            pltpu.VMEM((2, d, de), _MXU_DTYPE), pltpu.VMEM((2, d, de), _MXU_DTYPE),
                            pltpu.VMEM((2, de, d), _MXU_DTYPE), pltpu.SemaphoreType.DMA((3,))],
        ),
        out_shape=jax.ShapeDtypeStruct((n_slots, d), F32),
        compiler_params=_cparams(("arbitrary",)),
        name="experts",
    )(slot, cast, pre, misc, xs, w_gate, w_up, w_down)


def _combine_body(x1_ref, gate_ref, d0_ref, d1_ref, n0_ref, n1_ref, mod_ref, g_ref, ys_ref, o_ref, buf_ref, sem,
                  *, tf):
    i = pl.program_id(0)
    cur = i % 2

    def gather(da_ref, db_ref, half):
        def issue(c, carry):
            for u in range(SUBLANES):
                r = c * SUBLANES + u
                for j, d_ref in enumerate((da_ref, db_ref)):
                    pltpu.make_async_copy(ys_ref.at[pl.ds(d_ref[r], 1), :],
                                          buf_ref.at[half, j, c, pl.ds(u, 1), :], sem.at[half]).start(priority=j)
            return carry

        lax.fori_loop(0, tf // SUBLANES, issue, 0)

    @pl.when(i == 0)
    def _():
        gather(d0_ref, d1_ref, 0)

    @pl.when(i + 1 < pl.num_programs(0))
    def _():
        gather(n0_ref, n1_ref, 1 - cur)

    for _ in range(2):
        pltpu.make_async_copy(ys_ref.at[pl.ds(0, tf), :], ys_ref.at[pl.ds(0, tf), :], sem.at[cur]).wait()

    gates = gate_ref[...]
    d = o_ref.shape[1]
    moe = buf_ref[cur, 0].reshape(tf, d) * gates[:, 0:1] + buf_ref[cur, 1].reshape(tf, d) * gates[:, 1:2]
    x2 = x1_ref[...] + mod_ref[0, 5:6, :] * moe
    ms = jnp.mean(x2 * x2, axis=-1, keepdims=True)
    o_ref[...] = x2 * lax.rsqrt(ms + EPS) * g_ref[...]


def _combine(x1, gates, dest0, dest1, mod, g_final, ys, seq):
    n_tok, d = x1.shape
    tf = COMB_TF
    n_tiles = n_tok // tf
    tok = pl.BlockSpec((tf,), lambda i: (i,), memory_space=pltpu.SMEM)
    nxt = pl.BlockSpec((tf,), lambda i: (jnp.minimum(i + 1, n_tiles - 1),), memory_space=pltpu.SMEM)
    return pl.pallas_call(
        functools.partial(_combine_body, tf=tf),
        grid=(n_tiles,),
        in_specs=[pl.BlockSpec((tf, d), lambda i: (i, 0)),
                  pl.BlockSpec((tf, LANES), lambda i: (i, 0)),
                  tok, tok, nxt, nxt,
                  pl.BlockSpec((1, 6, d), lambda i: ((i * tf) // seq, 0, 0)),
                  pl.BlockSpec((1, d), lambda i: (0, 0)),
                  pl.BlockSpec(memory_space=pl.ANY)],
        out_specs=pl.BlockSpec((tf, d), lambda i: (i, 0)),
        scratch_shapes=[pltpu.VMEM((2, 2, tf // SUBLANES, SUBLANES, d), F32), pltpu.SemaphoreType.DMA((2,))],
        out_shape=jax.ShapeDtypeStruct((n_tok, d), F32),
        compiler_params=_cparams(("arbitrary",)),
        name="combine",
    )(x1, gates, dest0, dest1, dest0, dest1, mod, g_final, ys)


def kernel(x, c, w_ada, b_ada, g_mix, w_in, w_pool, pool_scale, rel_bias, w_out, g_ffn, w_group, b_group,
           w_router, b_router, w_gate, w_up, w_down, g_final):
    bsz, seq, d = x.shape
    n_tok = bsz * seq
    depth = w_ada.shape[0]
    assert depth == 1, "the final norm is fused into the only layer's combine"
    topk = min(TOPK_MAX, seq // 4)
    bias = _bias_tiles(rel_bias, ATT_T)
    bm = MOE_BM
    n_blocks = -(-(n_tok * 2) // bm) + N_EXPERTS
    n_slots = n_blocks * bm
    xt = x.reshape(n_tok, d)
    for i in range(depth):
        mod = _ada(c, w_ada[i], b_ada[i][None, :]).reshape(bsz, 6, d)
        u, q, k, vt, qi, kia, kib, wit = _proj(xt.reshape(bsz, seq, d), mod, g_mix[i][None, :], w_in[i])
        attn = _dsa(q, qi, wit, kia, kib, k, vt, bias, topk)
        w_r = jnp.concatenate(
            [w_group[i], w_router[i], jnp.zeros((d, LANES - N_GROUPS - N_EXPERTS), F32)], axis=1).astype(_MXU_DTYPE)
        b_r = jnp.concatenate(
            [b_group[i], b_router[i], jnp.zeros((LANES - N_GROUPS - N_EXPERTS,), F32)])[None, :]
        x1, h2, logits = _mix(u.reshape(n_tok, -1), attn.reshape(n_tok, -1), xt, mod,
                              w_pool[i].astype(_MXU_DTYPE), pool_scale[i][None, :],
                              w_out[i].astype(_MXU_DTYPE), g_ffn[i][None, :], w_r, b_r, seq)
        route_t, gates, cnt = _route(logits)
        counts = cnt[0, :N_EXPERTS].astype(I32)
        padded = (counts + bm - 1) // bm * bm
        pends = jnp.cumsum(padded)
        pstart = pends - padded
        n_used = (pends[-1:] // bm).astype(I32)
        dest = _dest(route_t, pstart)
        dest0, dest1 = dest[0], dest[1]
        xs = _sc_dispatch(h2, dest0, dest1, n_slots)
        ys = _experts(xs, n_used, pends, w_gate[i], w_up[i], w_down[i])
        xt = _combine(x1, gates, dest0, dest1, mod, g_final[None, :], ys, seq)
    return xt.reshape(bsz, seq, d)
```

```python
import functools
import math

import numpy as np
import jax
import jax.numpy as jnp
from jax import lax
from jax.experimental import pallas as pl
from jax.experimental.pallas import tpu as pltpu
from jax.experimental.pallas import tpu_sc as plsc

F32 = jnp.float32
I32 = jnp.int32
_MXU_DTYPE = jnp.bfloat16

POOL_WINDOWS = (2, 4, 8, 16)
N_POOL_GROUPS = 4
HEAD_DIM = 128
N_HEADS = 8
N_KV_HEADS = 2
HEAD_GROUP = N_HEADS // N_KV_HEADS
IDX_HEADS = 16
IDX_DIM = 64
TOPK_MAX = 256
N_BUCKETS = 32
MAX_DISTANCE = 128
N_GROUPS = 4
EXPERTS_PER_GROUP = 8
N_EXPERTS = N_GROUPS * EXPERTS_PER_GROUP
EPS = 1e-6
NEG_INF = -1e30
BIG = 3e38
LOG2E = 1.4426950408889634

LANES = 128
SUBLANES = 8
VMEM_LIMIT_BYTES = 56 * 1024 * 1024

ADA_TN = 1024
PROJ_TM = 256
ATT_T = 256
SCORE_SUB = 256
COUNT_RB = 256
MIX_TM = 256
HALO = max(POOL_WINDOWS)
ROUTE_TR = 1024
MOE_BM = 256
MOE_SPLIT = 2
COMB_TF = 256
SEL_FIXED_ITERS = 18
SEL_MAX_ITERS = 80


def _cparams(sem):
    return pltpu.CompilerParams(dimension_semantics=sem, vmem_limit_bytes=VMEM_LIMIT_BYTES)


def _silu(x):
    return x * (1.0 / (1.0 + jnp.exp(-x)))


def _dot(a, b):
    return jnp.dot(a, b, preferred_element_type=F32)


def _dot_nt(a, b):
    return lax.dot_general(a, b, (((1,), (1,)), ((), ())), preferred_element_type=F32)


def _ada_body(c_ref, w_ref, b_ref, o_ref):
    s = _silu(c_ref[...])
    o_ref[...] = _dot(s.astype(_MXU_DTYPE), w_ref[...].astype(_MXU_DTYPE)) + b_ref[...]


def _ada(c, w, b):
    bsz, d = c.shape
    n = w.shape[1]
    return pl.pallas_call(
        _ada_body,
        grid=(n // ADA_TN,),
        in_specs=[pl.BlockSpec((bsz, d), lambda j: (0, 0)),
                  pl.BlockSpec((d, ADA_TN), lambda j: (0, j)),
                  pl.BlockSpec((1, ADA_TN), lambda j: (0, j))],
        out_specs=pl.BlockSpec((bsz, ADA_TN), lambda j: (0, j)),
        out_shape=jax.ShapeDtypeStruct((bsz, n), F32),
        compiler_params=_cparams(("arbitrary",)),
        name="ada",
    )(c, w, b)


def _rms_mod(x, g, shift, scale):
    ms = jnp.mean(x * x, axis=-1, keepdims=True)
    y = x * lax.rsqrt(ms + EPS) * g
    return y * (1.0 + scale) + shift


def _proj_body(x_ref, mod_ref, g_ref, w_ref, u_ref, q_ref, k_ref, vt_ref, qi_ref, kia_ref, kib_ref,
               wit_ref, *, cuts):
    h = _rms_mod(x_ref[0], g_ref[...], mod_ref[0, 0:1, :], mod_ref[0, 1:2, :])
    hb = h.astype(_MXU_DTYPE)

    def seg(name):
        lo, hi = cuts[name]
        return _dot(hb, w_ref[:, lo:hi])

    u_ref[0] = seg("u")
    q_ref[0] = seg("q").astype(q_ref.dtype)
    k_ref[0] = seg("k").astype(k_ref.dtype)
    vt_ref[0] = seg("v").T.astype(vt_ref.dtype)
    qi_ref[0] = seg("qi").astype(qi_ref.dtype)
    tail = seg("tail")
    tail = jnp.concatenate([tail, jnp.zeros((tail.shape[0], LANES - tail.shape[1]), F32)], axis=1)
    swapped = pltpu.roll(tail, LANES - IDX_DIM, 1)
    lane = lax.broadcasted_iota(I32, tail.shape, 1)
    kia_ref[0] = jnp.where(lane < IDX_DIM, tail, 0.0).astype(kia_ref.dtype)
    kib_ref[0] = jnp.where(lane >= LANES - IDX_DIM, swapped, 0.0).astype(kib_ref.dtype)
    wit_ref[0] = swapped.T[:IDX_HEADS, :]


def _proj(x, mod, g_mix, w_in):
    bsz, s, d = x.shape
    pool_w = d // 2
    attn_w = N_HEADS * HEAD_DIM
    kv_w = N_KV_HEADS * HEAD_DIM
    qi_w = IDX_HEADS * IDX_DIM
    c = np.cumsum([0, pool_w, attn_w, kv_w, kv_w, qi_w, IDX_DIM, IDX_HEADS])
    assert IDX_DIM + IDX_HEADS <= LANES and w_in.shape[1] == c[7]
    wp = w_in.astype(_MXU_DTYPE)
    cuts = {"u": (int(c[0]), int(c[1])), "q": (int(c[1]), int(c[2])), "k": (int(c[2]), int(c[3])),
            "v": (int(c[3]), int(c[4])), "qi": (int(c[4]), int(c[5])), "tail": (int(c[5]), int(c[7]))}
    tm = PROJ_TM
    pw = wp.shape[1]
    row = lambda w: pl.BlockSpec((1, tm, w), lambda b, i: (b, i, 0))
    out_shapes = [
        jax.ShapeDtypeStruct((bsz, s, pool_w), F32),
        jax.ShapeDtypeStruct((bsz, s, attn_w), _MXU_DTYPE),
        jax.ShapeDtypeStruct((bsz, s, kv_w), _MXU_DTYPE),
        jax.ShapeDtypeStruct((bsz, kv_w, s), _MXU_DTYPE),
        jax.ShapeDtypeStruct((bsz, s, qi_w), _MXU_DTYPE),
        jax.ShapeDtypeStruct((bsz, s, LANES), _MXU_DTYPE),
        jax.ShapeDtypeStruct((bsz, s, LANES), _MXU_DTYPE),
        jax.ShapeDtypeStruct((bsz, IDX_HEADS, s), F32),
    ]
    out_specs = [row(pool_w), row(attn_w), row(kv_w),
                 pl.BlockSpec((1, kv_w, tm), lambda b, i: (b, 0, i)),
                 row(qi_w), row(LANES), row(LANES),
                 pl.BlockSpec((1, IDX_HEADS, tm), lambda b, i: (b, 0, i))]
    return pl.pallas_call(
        functools.partial(_proj_body, cuts=cuts),
        grid=(bsz, s // tm),
        in_specs=[pl.BlockSpec((1, tm, d), lambda b, i: (b, i, 0)),
                  pl.BlockSpec((1, 6, d), lambda b, i: (b, 0, 0)),
                  pl.BlockSpec((1, d), lambda b, i: (0, 0)),
                  pl.BlockSpec((d, pw), lambda b, i: (0, 0))],
        out_specs=out_specs,
        out_shape=out_shapes,
        compiler_params=_cparams(("arbitrary", "arbitrary")),
        name="proj",
    )(x, mod, g_mix, wp)


def _bucket_starts():
    max_exact = N_BUCKETS // 2
    d = np.arange(1, 4 * MAX_DISTANCE, dtype=np.float32)
    large = max_exact + (np.log(d / np.float32(max_exact)) / np.float32(math.log(MAX_DISTANCE / max_exact))
                         * np.float32(N_BUCKETS - max_exact)).astype(np.int32)
    large = np.minimum(large, N_BUCKETS - 1)
    bucket = np.where(d < max_exact, d.astype(np.int32), large)
    bucket = np.concatenate([[0], bucket])
    starts = [int(np.argmax(bucket >= b)) for b in range(N_BUCKETS)]
    assert all(np.all(bucket[starts[b]:] >= b) for b in range(N_BUCKETS))
    assert starts[N_BUCKETS - 1] <= MAX_DISTANCE
    return starts


def _bias_body(rb_ref, o_ref, *, t, starts):
    diff = pl.program_id(0)
    s_l = lax.broadcasted_iota(I32, (t, t), 0)
    t_l = lax.broadcasted_iota(I32, (t, t), 1)
    dist = diff * t + t_l - s_l
    for h in range(N_HEADS):
        val = jnp.full((t, t), rb_ref[0, h], F32)
        for b in range(1, N_BUCKETS):
            val = jnp.where(dist >= starts[b], rb_ref[b, h], val)
        hh = h % HEAD_GROUP
        o_ref[0, h // HEAD_GROUP, :, hh * t:(hh + 1) * t] = (val - rb_ref[N_BUCKETS - 1, h]) * LOG2E


def _bias_tiles(rel_bias, t):
    assert t >= MAX_DISTANCE
    return pl.pallas_call(
        functools.partial(_bias_body, t=t, starts=_bucket_starts()),
        grid=(2,),
        in_specs=[pl.BlockSpec(memory_space=pltpu.SMEM)],
        out_specs=pl.BlockSpec((1, N_KV_HEADS, t, HEAD_GROUP * t), lambda i: (i, 0, 0, 0)),
        out_shape=jax.ShapeDtypeStruct((2, N_KV_HEADS, t, HEAD_GROUP * t), F32),
        compiler_params=_cparams(("arbitrary",)),
        name="bias_tiles",
    )(rel_bias)


def _dsa_body(q_ref, qi_ref, wit_ref, kia_ref, kib_ref, k_ref, vt_ref, bias_ref, o_ref,
              score_ref, mask_ref, acc_ref, *, t, topk):
    i = pl.program_id(1)
    nk = i + 1
    t_glob = i * t + lax.broadcasted_iota(I32, (1, t), 1)
    idx_scale = (IDX_DIM ** -0.5) * (IDX_HEADS ** -0.5)

    def score_chunk(c, carry):
        cnt, mn, mx = carry
        r0 = pl.multiple_of(c * SCORE_SUB, SCORE_SUB)
        ka = kia_ref[0, pl.ds(r0, SCORE_SUB), :]
        kb = kib_ref[0, pl.ds(r0, SCORE_SUB), :]
        acc = jnp.zeros((SCORE_SUB, t), F32)
        for j in range(IDX_HEADS // 2):
            qp = qi_ref[0, :, j * LANES:(j + 1) * LANES]
            da = _dot_nt(ka, qp)
            db = _dot_nt(kb, qp)
            acc = acc + jnp.maximum(da, 0.0) * wit_ref[0, 2 * j:2 * j + 1, :]
            acc = acc + jnp.maximum(db, 0.0) * wit_ref[0, 2 * j + 1:2 * j + 2, :]
        sc = acc * idx_scale
        s_glob = r0 + lax.broadcasted_iota(I32, (SCORE_SUB, 1), 0)
        b = jnp.where(s_glob <= t_glob, sc, NEG_INF)
        score_ref[pl.ds(r0, SCORE_SUB), :] = b
        real = b > NEG_INF
        rows8 = lambda a: a.reshape(SCORE_SUB // SUBLANES, SUBLANES, t)
        cnt = cnt + jnp.sum(rows8(real.astype(I32)), axis=0)
        mn = jnp.minimum(mn, jnp.min(rows8(jnp.where(real, b, BIG)), axis=0))
        mx = jnp.maximum(mx, jnp.max(rows8(b), axis=0))
        return cnt, mn, mx

    cnt8, mn8, mx8 = lax.fori_loop(
        0, nk * (t // SCORE_SUB), score_chunk,
        (jnp.zeros((SUBLANES, t), I32), jnp.full((SUBLANES, t), BIG, F32), jnp.full((SUBLANES, t), -BIG, F32)))

    nblk = nk * (t // COUNT_RB)
    fold = lambda m: jnp.sum(m.reshape(COUNT_RB // SUBLANES, SUBLANES, t), axis=0)

    def blk(r):
        return score_ref[pl.ds(pl.multiple_of(r * COUNT_RB, COUNT_RB), COUNT_RB), :]

    def count_ge(c):
        def body(r, acc):
            return acc + fold((blk(r) >= c).astype(I32))
        acc = lax.fori_loop(0, nblk, body, jnp.zeros((SUBLANES, t), I32))
        return jnp.sum(acc, axis=0, keepdims=True)

    n_real = jnp.sum(cnt8, axis=0, keepdims=True)
    rmin = jnp.min(mn8, axis=0, keepdims=True)
    rmax = jnp.max(mx8, axis=0, keepdims=True)
    n_max = count_ge(rmax)
    degenerate = jnp.logical_and(n_real >= topk, n_max >= topk)
    lo0 = jnp.where(n_real < topk, NEG_INF, jnp.where(degenerate, rmax, rmin))
    hi0 = jnp.where(degenerate, BIG, rmax)
    done0 = jnp.where(n_real <= topk, 1, 0)

    def bisect(st):
        lo, hi, done, stuck = st
        c = lo + (hi - lo) * 0.5
        active = (done + stuck) == 0
        has_mid = jnp.logical_and(c > lo, c < hi)
        n = count_ge(c)
        upd = jnp.logical_and(active, has_mid)
        lo = jnp.where(jnp.logical_and(upd, n >= topk), c, lo)
        hi = jnp.where(jnp.logical_and(upd, n < topk), c, hi)
        done = jnp.where(jnp.logical_and(upd, n == topk), 1, done)
        stuck = jnp.where(jnp.logical_and(active, jnp.logical_not(has_mid)), 1, stuck)
        return lo, hi, done, stuck

    st = lax.fori_loop(0, SEL_FIXED_ITERS, lambda _, s: bisect(s), (lo0, hi0, done0, jnp.zeros((1, t), I32)))

    def sel_cond(s):
        return jnp.logical_and(s[0] < SEL_MAX_ITERS, jnp.min(s[3] + s[4]) == 0)

    _, lo, hi, done, _ = lax.while_loop(sel_cond, lambda s: (s[0] + 1,) + bisect(s[1:]), (jnp.int32(0),) + st)

    def write_mask(r, cut):
        r0 = pl.multiple_of(r * COUNT_RB, COUNT_RB)
        b = score_ref[pl.ds(r0, COUNT_RB), :]
        s_glob = r0 + lax.broadcasted_iota(I32, (COUNT_RB, 1), 0)
        sel = jnp.logical_and(b >= lo, s_glob <= t_glob)
        if cut is not None:
            sel = jnp.logical_and(sel, jnp.logical_or(b >= hi, s_glob < cut))
        mask_ref[pl.ds(r0, COUNT_RB), :] = jnp.concatenate([jnp.where(sel, 0.0, NEG_INF)] * HEAD_GROUP, axis=1)

    any_tied = jnp.min(done) == 0

    @pl.when(jnp.logical_not(any_tied))
    def _():
        def body(r, carry):
            write_mask(r, None)
            return carry
        lax.fori_loop(0, nblk, body, 0)

    @pl.when(any_tied)
    def _():
        need = topk - count_ge(hi)

        def count_tie_below(x):
            def body(r, acc):
                r0 = pl.multiple_of(r * COUNT_RB, COUNT_RB)
                b = score_ref[pl.ds(r0, COUNT_RB), :]
                s_glob = r0 + lax.broadcasted_iota(I32, (COUNT_RB, 1), 0)
                m = jnp.logical_and(jnp.logical_and(b >= lo, b < hi), s_glob < x)
                return acc + fold(m.astype(I32))
            acc = lax.fori_loop(0, nblk, body, jnp.zeros((SUBLANES, t), I32))
            return jnp.sum(acc, axis=0, keepdims=True)

        nbits = int(score_ref.shape[0]).bit_length()

        def bit_body(bi, x):
            c = x + jnp.left_shift(jnp.int32(1), nbits - 1 - bi)
            return jnp.where(count_tie_below(c) < need, c, x)

        x = lax.fori_loop(0, nbits, bit_body, jnp.zeros((1, t), I32))
        cut = jnp.where(done > 0, jnp.int32(2 ** 30), x + 1)

        def body(r, carry):
            write_mask(r, cut)
            return carry
        lax.fori_loop(0, nblk, body, 0)

    scale = HEAD_DIM ** -0.5
    gt = HEAD_GROUP * t

    def col_reduce(op, a):
        part = op(a.reshape(a.shape[0] // SUBLANES, SUBLANES, a.shape[1]), axis=0)
        return op(part, axis=0, keepdims=True)

    qgs = [jnp.concatenate(
        [q_ref[0, :, h * HEAD_DIM:(h + 1) * HEAD_DIM] for h in range(g * HEAD_GROUP, (g + 1) * HEAD_GROUP)],
        axis=0) for g in range(N_KV_HEADS)]
    acc_ref[...] = jnp.zeros_like(acc_ref)

    def att_chunk(kc, carry, near):
        r0 = pl.multiple_of(kc * t, t)
        mk = mask_ref[pl.ds(r0, t), :]
        new = []
        for g in range(N_KV_HEADS):
            m, l = carry[g]
            kk = k_ref[0, pl.ds(r0, t), g * HEAD_DIM:(g + 1) * HEAD_DIM]
            s = _dot_nt(kk, qgs[g]) * (scale * LOG2E) + mk
            if near:
                s = s + bias_ref[i - kc, g]
            m_new = jnp.maximum(m, col_reduce(jnp.max, s))
            alpha = jnp.exp2(m - m_new)
            p = jnp.exp2(s - m_new)
            l = alpha * l + col_reduce(jnp.sum, p)
            vt = vt_ref[0, g * HEAD_DIM:(g + 1) * HEAD_DIM, pl.ds(r0, t)]
            acc_ref[g] = alpha * acc_ref[g] + _dot(vt, p.astype(_MXU_DTYPE))
            new.append((m_new, l))
        return tuple(new)

    init = (jnp.full((1, gt), NEG_INF, F32), jnp.zeros((1, gt), F32))
    n_far = jnp.maximum(i - 1, 0)
    far = lax.fori_loop(0, n_far, functools.partial(att_chunk, near=False), (init,) * N_KV_HEADS)
    fin = lax.fori_loop(n_far, nk, functools.partial(att_chunk, near=True), far)
    for g in range(N_KV_HEADS):
        out = acc_ref[g] / fin[g][1]
        for hh in range(HEAD_GROUP):
            h = g * HEAD_GROUP + hh
            o_ref[0, :, h * HEAD_DIM:(h + 1) * HEAD_DIM] = out[:, hh * t:(hh + 1) * t].T.astype(o_ref.dtype)


def _dsa(q, qi, wit, kia, kib, k, vt, bias, topk):
    bsz, s, aw = q.shape
    t = ATT_T
    assert s % t == 0 and topk <= t
    row = lambda w: pl.BlockSpec((1, t, w), lambda b, i: (b, i, 0))
    full = lambda a: pl.BlockSpec((1,) + a.shape[1:], lambda b, i: (b, 0, 0))
    return pl.pallas_call(
        functools.partial(_dsa_body, t=t, topk=topk),
        grid=(bsz, s // t),
        in_specs=[row(aw), row(qi.shape[2]),
                  pl.BlockSpec((1, IDX_HEADS, t), lambda b, i: (b, 0, i)),
                  full(kia), full(kib), full(k), full(vt),
                  pl.BlockSpec(bias.shape, lambda b, i: (0, 0, 0, 0))],
        out_specs=row(aw),
        out_shape=jax.ShapeDtypeStruct((bsz, s, aw), _MXU_DTYPE),
        scratch_shapes=[pltpu.VMEM((s, t), F32), pltpu.VMEM((s, HEAD_GROUP * t), F32),
                        pltpu.VMEM((N_KV_HEADS, HEAD_DIM, HEAD_GROUP * t), F32)],
        compiler_params=_cparams(("arbitrary", "arbitrary")),
        name="dsa",
    )(q, qi, wit, kia, kib, k, vt, bias)


def _pack_halves(hb):
    assert jnp.dtype(hb.dtype).itemsize == 2
    n = hb.shape[1] // 2
    hi = lax.bitcast_convert_type(hb[:, :n].astype(F32), I32)
    lo = lax.bitcast_convert_type(hb[:, n:].astype(F32), I32)
    return jnp.bitwise_or(hi, lax.shift_right_logical(lo, jnp.full(lo.shape, 16, I32)))


def _unpack_halves(w, dtype):
    hi = lax.bitcast_convert_type(jnp.bitwise_and(w, jnp.int32(-65536)), F32).astype(dtype)
    lo = lax.bitcast_convert_type(jnp.left_shift(w, 16), F32).astype(dtype)
    return jnp.concatenate([hi, lo], axis=1)


def _mix_body(u_ref, halo_ref, a_ref, x_ref, mod_ref, wpool_ref, ps_ref, wout_ref, g_ref, wr_ref, br_ref,
              x1_ref, h2_ref, lg_ref, *, seq, tm):
    t0 = (pl.program_id(0) * tm) % seq
    u = u_ref[...]
    halo = jnp.where(t0 == 0, 0.0, halo_ref[...])
    ext = jnp.concatenate([halo, u], axis=0)
    pos = t0 + lax.broadcasted_iota(I32, (tm, 1), 0)
    gw = u.shape[1] // N_POOL_GROUPS
    ys = []
    for g, w in enumerate(POOL_WINDOWS):
        a = ext[:, g * gw:(g + 1) * gw]
        sft = 1
        while sft < w:
            a = a + pltpu.roll(a, sft, 0)
            sft *= 2
        cnt = jnp.minimum(pos + 1, w).astype(F32)
        p = a[HALO:, :] / cnt - u[:, g * gw:(g + 1) * gw]
        ys.append(_dot(p.astype(_MXU_DTYPE), wpool_ref[g]) * ps_ref[:, g * gw:(g + 1) * gw])
    pool = jnp.concatenate(ys, axis=1).astype(_MXU_DTYPE)
    pw = pool.shape[1]
    mixed = _dot(pool, wout_ref[:pw, :]) + _dot(a_ref[...], wout_ref[pw:, :])
    x1 = x_ref[...] + mod_ref[0, 2:3, :] * mixed
    x1_ref[...] = x1
    h2 = _rms_mod(x1, g_ref[...], mod_ref[0, 3:4, :], mod_ref[0, 4:5, :]).astype(_MXU_DTYPE)
    h2_ref[...] = _pack_halves(h2)
    lg_ref[...] = _dot(h2, wr_ref[...]) + br_ref[...]


def _mix(u, attn, x, mod, w_pool, pool_scale, w_out, g_ffn, w_r, b_r, seq):
    n_tok, d = x.shape
    tm = MIX_TM
    assert seq % tm == 0 and tm % HALO == 0
    pw = u.shape[1]
    aw = attn.shape[1]
    rows = lambda w: pl.BlockSpec((tm, w), lambda i: (i, 0))
    const = lambda a: pl.BlockSpec(a.shape, lambda i: (0,) * a.ndim)
    return pl.pallas_call(
        functools.partial(_mix_body, seq=seq, tm=tm),
        grid=(n_tok // tm,),
        in_specs=[rows(pw),
                  pl.BlockSpec((HALO, pw), lambda i: (jnp.maximum(i * (tm // HALO) - 1, 0), 0)),
                  rows(aw), rows(d),
                  pl.BlockSpec((1, 6, d), lambda i: ((i * tm) // seq, 0, 0)),
                  const(w_pool), const(pool_scale), const(w_out), const(g_ffn), const(w_r), const(b_r)],
        out_specs=[rows(d), rows(d // 2), rows(LANES)],
        out_shape=[jax.ShapeDtypeStruct((n_tok, d), F32), jax.ShapeDtypeStruct((n_tok, d // 2), I32),
                   jax.ShapeDtypeStruct((n_tok, LANES), F32)],
        compiler_params=_cparams(("arbitrary",)),
        name="mix",
    )(u, u, attn, x, mod, w_pool, pool_scale, w_out, g_ffn, w_r, b_r)


def _route_body(lg_ref, rt_ref, gate_ref, cnt_ref, run_ref):
    @pl.when(pl.program_id(0) == 0)
    def _():
        run_ref[...] = jnp.zeros_like(run_ref)

    lg = lg_ref[...]
    tr = lg.shape[0]
    lane = lax.broadcasted_iota(I32, lg.shape, 1)
    gmask = lane < N_GROUPS
    gl = jnp.where(gmask, lg, -BIG)
    gmax = jnp.max(gl, axis=-1, keepdims=True)
    gsel = jnp.min(jnp.where(jnp.logical_and(gmask, gl == gmax), lane, LANES), axis=-1, keepdims=True)
    gsum = jnp.sum(jnp.where(gmask, jnp.exp(gl - gmax), 0.0), axis=-1, keepdims=True)
    p_g = 1.0 / gsum
    in_e = jnp.logical_and(lane >= N_GROUPS, lane < N_GROUPS + N_EXPERTS)
    emask = jnp.logical_and(in_e, lax.shift_right_arithmetic(lane - N_GROUPS, EXPERTS_PER_GROUP.bit_length() - 1) == gsel)
    el = jnp.where(emask, lg, -BIG)
    emax = jnp.max(el, axis=-1, keepdims=True)
    ex = jnp.where(emask, jnp.exp(el - emax), 0.0)
    ep = ex / jnp.sum(ex, axis=-1, keepdims=True)
    p1 = jnp.max(jnp.where(emask, ep, -1.0), axis=-1, keepdims=True)
    i1 = jnp.min(jnp.where(jnp.logical_and(emask, ep == p1), lane, LANES), axis=-1, keepdims=True)
    m2 = jnp.logical_and(emask, lane != i1)
    p2 = jnp.max(jnp.where(m2, ep, -1.0), axis=-1, keepdims=True)
    i2 = jnp.min(jnp.where(jnp.logical_and(m2, ep == p2), lane, LANES), axis=-1, keepdims=True)
    den = p1 + p2
    g1 = p_g * p1 / den
    g2 = p_g * p2 / den
    e1 = i1 - N_GROUPS
    e2 = i2 - N_GROUPS

    oh = jnp.logical_or(lane == e1, lane == e2)
    ri = lax.broadcasted_iota(I32, (tr, tr), 0)
    ci = lax.broadcasted_iota(I32, (tr, tr), 1)
    ltri = (ci < ri).astype(_MXU_DTYPE)
    before = _dot(ltri, oh.astype(_MXU_DTYPE)) + run_ref[...]
    r1 = jnp.sum(jnp.where(lane == e1, before, 0.0), axis=-1, keepdims=True)
    r2 = jnp.sum(jnp.where(lane == e2, before, 0.0), axis=-1, keepdims=True)
    run = run_ref[...] + jnp.sum(oh.astype(F32), axis=0, keepdims=True)
    run_ref[...] = run
    cnt_ref[...] = jnp.broadcast_to(run, cnt_ref.shape)

    route = jnp.where(lane == 0, e1.astype(F32),
                      jnp.where(lane == 1, e2.astype(F32),
                                jnp.where(lane == 2, r1, jnp.where(lane == 3, r2, 0.0))))
    rt_ref[...] = route.T[:SUBLANES, :].astype(I32)
    gate_ref[...] = jnp.where(lane == 0, g1, jnp.where(lane == 1, g2, 0.0))


def _route(logits):
    n_tok = logits.shape[0]
    tr = min(ROUTE_TR, n_tok)
    return pl.pallas_call(
        _route_body,
        grid=(n_tok // tr,),
        in_specs=[pl.BlockSpec((tr, LANES), lambda i: (i, 0))],
        out_specs=[pl.BlockSpec((SUBLANES, tr), lambda i: (0, i)),
                   pl.BlockSpec((tr, LANES), lambda i: (i, 0)),
                   pl.BlockSpec((SUBLANES, LANES), lambda i: (0, 0))],
        out_shape=[jax.ShapeDtypeStruct((SUBLANES, n_tok), I32),
                   jax.ShapeDtypeStruct((n_tok, LANES), F32),
                   jax.ShapeDtypeStruct((SUBLANES, LANES), F32)],
        scratch_shapes=[pltpu.VMEM((1, LANES), F32)],
        compiler_params=_cparams(("arbitrary",)),
        name="route",
    )(logits)


def _dest_body(ps_ref, rt_ref, o_ref):
    r = rt_ref[...]
    start = jnp.zeros_like(r)
    for e in range(N_EXPERTS):
        start = jnp.where(r == e, ps_ref[e], start)
    o_ref[...] = start + pltpu.roll(r, SUBLANES - 2, 0)


def _dest(route_t, pstart):
    n_tok = route_t.shape[1]
    tb = min(n_tok, 2048)
    return pl.pallas_call(
        _dest_body,
        grid_spec=pltpu.PrefetchScalarGridSpec(
            num_scalar_prefetch=1,
            grid=(n_tok // tb,),
            in_specs=[pl.BlockSpec((SUBLANES, tb), lambda i, ps: (0, i))],
            out_specs=pl.BlockSpec((SUBLANES, tb), lambda i, ps: (0, i)),
        ),
        out_shape=jax.ShapeDtypeStruct((SUBLANES, n_tok), I32),
        compiler_params=_cparams(("arbitrary",)),
        name="dest",
    )(pstart, route_t)


SC_LANES = 16
SC_WIN = 16
SC_NBUF = 4
SC_CHUNK = 8192


def _sc_dispatch(h2, dest0, dest1, slot_lo, n_slots, n_slots_total):
    n_tok, d = h2.shape
    info = plsc.get_sparse_core_info()
    nc, ns = info.num_cores, info.num_subcores
    nw = nc * ns
    assert info.num_lanes == SC_LANES and n_slots % (nw * SC_WIN) == 0 and n_tok % SC_CHUNK == 0
    assert n_slots_total < 3 * n_tok
    per_w = n_slots // nw
    mesh = plsc.VectorSubcoreMesh(core_axis_name="c", subcore_axis_name="s")

    n_win = per_w // SC_WIN
    assert n_win % SC_NBUF == 0

    @functools.partial(
        pl.kernel, mesh=mesh,
        out_type=jax.ShapeDtypeStruct((n_slots, d), h2.dtype),
        scratch_types=[pltpu.VMEM((per_w,), I32), pltpu.VMEM((SC_CHUNK,), I32),
                       pltpu.VMEM((SC_NBUF, SC_WIN, d), h2.dtype), pltpu.SemaphoreType.DMA((SC_NBUF,)),
                       pltpu.SemaphoreType.DMA((SC_NBUF,))],
        compiler_params=pltpu.CompilerParams(needs_layout_passes=False),
    )
    def k(h_hbm, d0_hbm, d1_hbm, o_hbm, table, chunk, rows, gsems, wsems):
        lane = lax.iota(I32, SC_LANES)
        base = (lax.axis_index("s") * nc + lax.axis_index("c")) * per_w
        slot0 = slot_lo + base

        @pl.loop(0, per_w, step=SC_LANES)
        def _(s):
            v = slot0 + s + lane
            v = jnp.where(v >= n_tok, v - n_tok, v)
            v = jnp.where(v >= n_tok, v - n_tok, v)
            table[pl.ds(s, SC_LANES)] = v

        for d_hbm in (d0_hbm, d1_hbm):
            @pl.loop(0, n_tok, step=SC_CHUNK)
            def _(t0, d_hbm=d_hbm):
                pltpu.sync_copy(d_hbm.at[pl.ds(t0, SC_CHUNK)], chunk)

                @pl.loop(0, SC_CHUNK, step=SC_LANES)
                def _(j):
                    r = chunk[pl.ds(j, SC_LANES)] - slot0
                    mine = jnp.logical_and(r >= 0, r < per_w)
                    plsc.store_scatter(table, [jnp.where(mine, r, 0)], t0 + j + lane, mask=mine)

        ahead = SC_NBUF // 2

        def gather(w, b):
            return pltpu.make_async_copy(
                h_hbm.at[table.at[pl.ds(w * SC_WIN, SC_WIN)]], rows.at[b], gsems.at[b])

        def put(w, b):
            return pltpu.make_async_copy(rows.at[b], o_hbm.at[pl.ds(base + w * SC_WIN, SC_WIN)], wsems.at[b])

        for b in range(ahead):
            gather(b, b).start()

        @pl.loop(0, n_win, step=SC_NBUF)
        def _(g):
            for b in range(SC_NBUF):
                w = g + b
                pb = (b + ahead) % SC_NBUF
                gather(w, b).wait()
                put(w, b).start()

                @pl.when(w >= ahead)
                def _():
                    put(w - ahead, pb).wait()

                @pl.when(w + ahead < n_win)
                def _():
                    gather(w + ahead, pb).start()

        for b in range(ahead):
            w = n_win - ahead + b
            put(w, w % SC_NBUF).wait()

    return k(h2, dest0, dest1)


def _experts_body(slot_ref, cast_ref, pre_ref, misc_ref, x_ref, wg_hbm, wu_hbm, wd_hbm, *rest, has_prev):
    y_ref, sg, su, sd, wgb, wub, wdb, sem = rest[1:] if has_prev else rest
    blk = pl.program_id(0)
    e0, pre0, n_used = misc_ref[0], misc_ref[1], misc_ref[2]

    def copies(e):
        return (pltpu.make_async_copy(wg_hbm.at[e], sg, sem.at[0]),
                pltpu.make_async_copy(wu_hbm.at[e], su, sem.at[1]),
                pltpu.make_async_copy(wd_hbm.at[e], sd, sem.at[2]))

    def fetch(e):
        for c in copies(e):
            c.start(priority=1)

    def land(slot):
        for c in copies(0):
            c.wait()
        wgb[slot] = sg[...].astype(wgb.dtype)
        wub[slot] = su[...].astype(wub.dtype)
        wdb[slot] = sd[...].astype(wdb.dtype)

    @pl.when(blk == 0)
    def _():
        fetch(e0)
        land(slot_ref[0])

        @pl.when(pre0 >= 0)
        def _():
            fetch(pre0)

    @pl.when(blk < n_used)
    def _():
        slot = slot_ref[blk]
        xb = _unpack_halves(x_ref[...], _MXU_DTYPE)
        a = _silu(_dot(xb, wgb[slot])) * _dot(xb, wub[slot])
        y_ref[...] = _dot(a.astype(_MXU_DTYPE), wdb[slot])

        @pl.when(cast_ref[blk] == 1)
        def _():
            land(1 - slot)

            @pl.when(pre_ref[blk] >= 0)
            def _():
                fetch(pre_ref[blk])

    @pl.when(blk >= n_used)
    def _():
        y_ref[...] = jnp.zeros_like(y_ref)


def _experts(xs, lo, n_blocks, ys_prev, n_used, pends, w_gate, w_up, w_down):
    dpk = xs.shape[1]
    bm = MOE_BM
    nb = xs.shape[0] // bm
    hi = lo + nb
    d, de = w_gate.shape[1:]
    last_used = n_used[0] - 1
    total = pends[-1]
    eid = jnp.arange(N_EXPERTS, dtype=I32)[None, :]
    owner = lambda start: jnp.minimum(jnp.sum((pends[None, :] <= start[:, None]).astype(I32), axis=1), N_EXPERTS - 1)
    end_of = lambda e: jnp.sum(jnp.where(eid == e[:, None], pends[None, :], 0), axis=1)
    be = owner(jnp.minimum(jnp.arange(n_blocks, dtype=I32), last_used) * bm)
    end_e = end_of(be)
    has_next = end_e < total
    nxt = owner(end_e)
    end_n = end_of(nxt)
    nxt2 = jnp.where(jnp.logical_and(has_next, end_n < total), owner(end_n), -1)
    is_last = jnp.concatenate([be[1:] != be[:-1], jnp.ones((1,), jnp.bool_)])
    is_last = jnp.logical_or(is_last, jnp.arange(n_blocks) >= last_used)
    cast = jnp.logical_and(is_last, has_next).astype(I32)
    pre = jnp.where(jnp.logical_and(cast == 1, end_n // bm - 1 < hi), nxt2, -1).astype(I32)
    pre0 = jnp.where(jnp.logical_and(has_next[lo], end_e[lo] // bm - 1 < hi), nxt[lo], -1)
    is_first = jnp.concatenate([jnp.ones((1,), jnp.bool_), be[1:] != be[:-1]])
    slot = ((jnp.cumsum(is_first.astype(I32)) - 1) % 2).astype(I32)
    misc = jnp.stack([be[lo], pre0, jnp.clip(n_used[0] - lo, 0, nb)]).astype(I32)
    hbm = pl.BlockSpec(memory_space=pl.ANY)
    has_prev = ys_prev is not None
    return pl.pallas_call(
        functools.partial(_experts_body, has_prev=has_prev),
        grid_spec=pltpu.PrefetchScalarGridSpec(
            num_scalar_prefetch=4,
            grid=(nb,),
            in_specs=[pl.BlockSpec((bm, dpk), lambda j, sl, ca, pr, mi: (jnp.maximum(jnp.minimum(j, mi[2] - 1), 0), 0)),
                      hbm, hbm, hbm] + ([hbm] if has_prev else []),
            out_specs=pl.BlockSpec((bm, d), lambda j, sl, ca, pr, mi: (lo + j, 0)),
            scratch_shapes=[pltpu.VMEM((d, de), w_gate.dtype), pltpu.VMEM((d, de), w_up.dtype),
                            pltpu.VMEM((de, d), w_down.dtype),
                            pltpu.VMEM((2, d, de), _MXU_DTYPE), pltpu.VMEM((2, d, de), _MXU_DTYPE),
                            pltpu.VMEM((2, de, d), _MXU_DTYPE), pltpu.SemaphoreType.DMA((3,))],
        ),
        out_shape=jax.ShapeDtypeStruct((n_blocks * bm, d), F32),
        input_output_aliases={8: 0} if has_prev else {},
        compiler_params=_cparams(("arbitrary",)),
        name="experts",
    )(slot[lo:hi], cast[lo:hi], pre[lo:hi], misc, xs, w_gate, w_up, w_down, *([ys_prev] if has_prev else []))


def _combine_body(x1_ref, gate_ref, d0_ref, d1_ref, n0_ref, n1_ref, mod_ref, g_ref, ys_ref, o_ref, buf_ref, sem,
                  *, tf):
    i = pl.program_id(0)
    cur = i % 2

    def gather(da_ref, db_ref, half):
        def issue(c, carry):
            for u in range(SUBLANES):
                r = c * SUBLANES + u
                for j, d_ref in enumerate((da_ref, db_ref)):
                    pltpu.make_async_copy(ys_ref.at[pl.ds(d_ref[r], 1), :],
                                          buf_ref.at[half, j, c, pl.ds(u, 1), :], sem.at[half]).start(priority=j)
            return carry

        lax.fori_loop(0, tf // SUBLANES, issue, 0)

    @pl.when(i == 0)
    def _():
        gather(d0_ref, d1_ref, 0)

    @pl.when(i + 1 < pl.num_programs(0))
    def _():
        gather(n0_ref, n1_ref, 1 - cur)

    for _ in range(2):
        pltpu.make_async_copy(ys_ref.at[pl.ds(0, tf), :], ys_ref.at[pl.ds(0, tf), :], sem.at[cur]).wait()

    gates = gate_ref[...]
    d = o_ref.shape[1]
    moe = buf_ref[cur, 0].reshape(tf, d) * gates[:, 0:1] + buf_ref[cur, 1].reshape(tf, d) * gates[:, 1:2]
    x2 = x1_ref[...] + mod_ref[0, 5:6, :] * moe
    ms = jnp.mean(x2 * x2, axis=-1, keepdims=True)
    o_ref[...] = x2 * lax.rsqrt(ms + EPS) * g_ref[...]


def _combine(x1, gates, dest0, dest1, mod, g_final, ys, seq):
    n_tok, d = x1.shape
    tf = COMB_TF
    n_tiles = n_tok // tf
    tok = pl.BlockSpec((tf,), lambda i: (i,), memory_space=pltpu.SMEM)
    nxt = pl.BlockSpec((tf,), lambda i: (jnp.minimum(i + 1, n_tiles - 1),), memory_space=pltpu.SMEM)
    return pl.pallas_call(
        functools.partial(_combine_body, tf=tf),
        grid=(n_tiles,),
        in_specs=[pl.BlockSpec((tf, d), lambda i: (i, 0)),
                  pl.BlockSpec((tf, LANES), lambda i: (i, 0)),
                  tok, tok, nxt, nxt,
                  pl.BlockSpec((1, 6, d), lambda i: ((i * tf) // seq, 0, 0)),
                  pl.BlockSpec((1, d), lambda i: (0, 0)),
                  pl.BlockSpec(memory_space=pl.ANY)],
        out_specs=pl.BlockSpec((tf, d), lambda i: (i, 0)),
        scratch_shapes=[pltpu.VMEM((2, 2, tf // SUBLANES, SUBLANES, d), F32), pltpu.SemaphoreType.DMA((2,))],
        out_shape=jax.ShapeDtypeStruct((n_tok, d), F32),
        compiler_params=_cparams(("arbitrary",)),
        name="combine",
    )(x1, gates, dest0, dest1, dest0, dest1, mod, g_final, ys)


def kernel(x, c, w_ada, b_ada, g_mix, w_in, w_pool, pool_scale, rel_bias, w_out, g_ffn, w_group, b_group,
           w_router, b_router, w_gate, w_up, w_down, g_final):
    bsz, seq, d = x.shape
    n_tok = bsz * seq
    depth = w_ada.shape[0]
    assert depth == 1, "the final norm is fused into the only layer's combine"
    topk = min(TOPK_MAX, seq // 4)
    bias = _bias_tiles(rel_bias, ATT_T)
    bm = MOE_BM
    n_blocks = -(-(n_tok * 2) // bm) + N_EXPERTS
    n_slots = n_blocks * bm
    xt = x.reshape(n_tok, d)
    for i in range(depth):
        mod = _ada(c, w_ada[i], b_ada[i][None, :]).reshape(bsz, 6, d)
        u, q, k, vt, qi, kia, kib, wit = _proj(xt.reshape(bsz, seq, d), mod, g_mix[i][None, :], w_in[i])
        attn = _dsa(q, qi, wit, kia, kib, k, vt, bias, topk)
        w_r = jnp.concatenate(
            [w_group[i], w_router[i], jnp.zeros((d, LANES - N_GROUPS - N_EXPERTS), F32)], axis=1).astype(_MXU_DTYPE)
        b_r = jnp.concatenate(
            [b_group[i], b_router[i], jnp.zeros((LANES - N_GROUPS - N_EXPERTS,), F32)])[None, :]
        x1, h2, logits = _mix(u.reshape(n_tok, -1), attn.reshape(n_tok, -1), xt, mod,
                              w_pool[i].astype(_MXU_DTYPE), pool_scale[i][None, :],
                              w_out[i].astype(_MXU_DTYPE), g_ffn[i][None, :], w_r, b_r, seq)
        route_t, gates, cnt = _route(logits)
        counts = cnt[0, :N_EXPERTS].astype(I32)
        padded = (counts + bm - 1) // bm * bm
        pends = jnp.cumsum(padded)
        pstart = pends - padded
        n_used = (pends[-1:] // bm).astype(I32)
        dest = _dest(route_t, pstart)
        dest0, dest1 = dest[0], dest[1]
        ys = None
        half = n_blocks // MOE_SPLIT
        for part in range(MOE_SPLIT):
            xs = _sc_dispatch(h2, dest0, dest1, part * half * bm, half * bm, n_slots)
            ys = _experts(xs, part * half, n_blocks, ys, n_used, pends, w_gate[i], w_up[i], w_down[i])
        xt = _combine(x1, gates, dest0, dest1, mod, g_final[None, :], ys, seq)
    return xt.reshape(bsz, seq, d)
```

```python
import functools
import math

import numpy as np
import jax
import jax.numpy as jnp
from jax import lax
from jax.experimental import pallas as pl
from jax.experimental.pallas import tpu as pltpu
from jax.experimental.pallas import tpu_sc as plsc

F32 = jnp.float32
I32 = jnp.int32
_MXU_DTYPE = jnp.bfloat16

POOL_WINDOWS = (2, 4, 8, 16)
N_POOL_GROUPS = 4
HEAD_DIM = 128
N_HEADS = 8
N_KV_HEADS = 2
HEAD_GROUP = N_HEADS // N_KV_HEADS
IDX_HEADS = 16
IDX_DIM = 64
TOPK_MAX = 256
N_BUCKETS = 32
MAX_DISTANCE = 128
N_GROUPS = 4
EXPERTS_PER_GROUP = 8
N_EXPERTS = N_GROUPS * EXPERTS_PER_GROUP
EPS = 1e-6
NEG_INF = -1e30
BIG = 3e38
LOG2E = 1.4426950408889634

LANES = 128
SUBLANES = 8
VMEM_LIMIT_BYTES = 56 * 1024 * 1024

ADA_TN = 1024
PROJ_TM = 256
ATT_T = 256
SCORE_SUB = 256
COUNT_RB = 256
MIX_TM = 256
HALO = 16
ROUTE_TR = 1024
MOE_BM = 256
COMB_TF = 256
SEL_FIXED_ITERS = 20
SEL_MAX_ITERS = 80


def _cparams(sem):
    return pltpu.CompilerParams(dimension_semantics=sem, vmem_limit_bytes=VMEM_LIMIT_BYTES)


def _silu(x):
    return x * (1.0 / (1.0 + jnp.exp(-x)))


def _dot(a, b):
    return jnp.dot(a, b, preferred_element_type=F32)


def _dot_nt(a, b):
    return lax.dot_general(a, b, (((1,), (1,)), ((), ())), preferred_element_type=F32)


def _ada_body(c_ref, w_ref, b_ref, o_ref):
    s = _silu(c_ref[...])
    o_ref[...] = _dot(s.astype(_MXU_DTYPE), w_ref[...].astype(_MXU_DTYPE)) + b_ref[...]


def _ada(c, w, b):
    bsz, d = c.shape
    n = w.shape[1]
    return pl.pallas_call(
        _ada_body,
        grid=(n // ADA_TN,),
        in_specs=[pl.BlockSpec((bsz, d), lambda j: (0, 0)),
                  pl.BlockSpec((d, ADA_TN), lambda j: (0, j)),
                  pl.BlockSpec((1, ADA_TN), lambda j: (0, j))],
        out_specs=pl.BlockSpec((bsz, ADA_TN), lambda j: (0, j)),
        out_shape=jax.ShapeDtypeStruct((bsz, n), F32),
        compiler_params=_cparams(("arbitrary",)),
        name="ada",
    )(c, w, b)


def _rms_mod(x, g, shift, scale):
    ms = jnp.mean(x * x, axis=-1, keepdims=True)
    y = x * lax.rsqrt(ms + EPS) * g
    return y * (1.0 + scale) + shift


def _proj_body(x_ref, mod_ref, g_ref, w_ref, u_ref, q_ref, k_ref, vt_ref, qi_ref, kia_ref, kib_ref,
               wit_ref, *, cuts):
    h = _rms_mod(x_ref[0], g_ref[...], mod_ref[0, 0:1, :], mod_ref[0, 1:2, :])
    hb = h.astype(_MXU_DTYPE)

    def seg(name):
        lo, hi = cuts[name]
        return _dot(hb, w_ref[:, lo:hi])

    u_ref[0] = seg("u")
    q_ref[0] = seg("q").astype(q_ref.dtype)
    k_ref[0] = seg("k").astype(k_ref.dtype)
    vt_ref[0] = seg("v").T.astype(vt_ref.dtype)
    qi_ref[0] = seg("qi").astype(qi_ref.dtype)
    tail = seg("tail")
    tail = jnp.concatenate([tail, jnp.zeros((tail.shape[0], LANES - tail.shape[1]), F32)], axis=1)
    swapped = pltpu.roll(tail, LANES - IDX_DIM, 1)
    lane = lax.broadcasted_iota(I32, tail.shape, 1)
    kia_ref[0] = jnp.where(lane < IDX_DIM, tail, 0.0).astype(kia_ref.dtype)
    kib_ref[0] = jnp.where(lane >= LANES - IDX_DIM, swapped, 0.0).astype(kib_ref.dtype)
    wit_ref[0] = swapped.T[:IDX_HEADS, :]


def _proj(x, mod, g_mix, w_in):
    bsz, s, d = x.shape
    pool_w = d // 2
    attn_w = N_HEADS * HEAD_DIM
    kv_w = N_KV_HEADS * HEAD_DIM
    qi_w = IDX_HEADS * IDX_DIM
    c = np.cumsum([0, pool_w, attn_w, kv_w, kv_w, qi_w, IDX_DIM, IDX_HEADS])
    assert IDX_DIM + IDX_HEADS <= LANES and w_in.shape[1] == c[7]
    wp = w_in.astype(_MXU_DTYPE)
    cuts = {"u": (int(c[0]), int(c[1])), "q": (int(c[1]), int(c[2])), "k": (int(c[2]), int(c[3])),
            "v": (int(c[3]), int(c[4])), "qi": (int(c[4]), int(c[5])), "tail": (int(c[5]), int(c[7]))}
    tm = PROJ_TM
    pw = wp.shape[1]
    row = lambda w: pl.BlockSpec((1, tm, w), lambda b, i: (b, i, 0))
    out_shapes = [
        jax.ShapeDtypeStruct((bsz, s, pool_w), F32),
        jax.ShapeDtypeStruct((bsz, s, attn_w), _MXU_DTYPE),
        jax.ShapeDtypeStruct((bsz, s, kv_w), _MXU_DTYPE),
        jax.ShapeDtypeStruct((bsz, kv_w, s), _MXU_DTYPE),
        jax.ShapeDtypeStruct((bsz, s, qi_w), _MXU_DTYPE),
        jax.ShapeDtypeStruct((bsz, s, LANES), _MXU_DTYPE),
        jax.ShapeDtypeStruct((bsz, s, LANES), _MXU_DTYPE),
        jax.ShapeDtypeStruct((bsz, IDX_HEADS, s), F32),
    ]
    out_specs = [row(pool_w), row(attn_w), row(kv_w),
                 pl.BlockSpec((1, kv_w, tm), lambda b, i: (b, 0, i)),
                 row(qi_w), row(LANES), row(LANES),
                 pl.BlockSpec((1, IDX_HEADS, tm), lambda b, i: (b, 0, i))]
    return pl.pallas_call(
        functools.partial(_proj_body, cuts=cuts),
        grid=(bsz, s // tm),
        in_specs=[pl.BlockSpec((1, tm, d), lambda b, i: (b, i, 0)),
                  pl.BlockSpec((1, 6, d), lambda b, i: (b, 0, 0)),
                  pl.BlockSpec((1, d), lambda b, i: (0, 0)),
                  pl.BlockSpec((d, pw), lambda b, i: (0, 0))],
        out_specs=out_specs,
        out_shape=out_shapes,
        compiler_params=_cparams(("arbitrary", "arbitrary")),
        name="proj",
    )(x, mod, g_mix, wp)


def _bucket_starts():
    max_exact = N_BUCKETS // 2
    d = np.arange(1, 4 * MAX_DISTANCE, dtype=np.float32)
    large = max_exact + (np.log(d / np.float32(max_exact)) / np.float32(math.log(MAX_DISTANCE / max_exact))
                         * np.float32(N_BUCKETS - max_exact)).astype(np.int32)
    large = np.minimum(large, N_BUCKETS - 1)
    bucket = np.where(d < max_exact, d.astype(np.int32), large)
    bucket = np.concatenate([[0], bucket])
    starts = [int(np.argmax(bucket >= b)) for b in range(N_BUCKETS)]
    assert all(np.all(bucket[starts[b]:] >= b) for b in range(N_BUCKETS))
    assert starts[N_BUCKETS - 1] <= MAX_DISTANCE
    return starts


def _bias_body(rb_ref, o_ref, *, t, starts):
    diff = pl.program_id(0)
    s_l = lax.broadcasted_iota(I32, (t, t), 0)
    t_l = lax.broadcasted_iota(I32, (t, t), 1)
    dist = diff * t + t_l - s_l
    for h in range(N_HEADS):
        val = jnp.full((t, t), rb_ref[0, h], F32)
        for b in range(1, N_BUCKETS):
            val = jnp.where(dist >= starts[b], rb_ref[b, h], val)
        hh = h % HEAD_GROUP
        o_ref[0, h // HEAD_GROUP, :, hh * t:(hh + 1) * t] = (val - rb_ref[N_BUCKETS - 1, h]) * LOG2E


def _bias_tiles(rel_bias, t):
    assert t >= MAX_DISTANCE
    return pl.pallas_call(
        functools.partial(_bias_body, t=t, starts=_bucket_starts()),
        grid=(2,),
        in_specs=[pl.BlockSpec(memory_space=pltpu.SMEM)],
        out_specs=pl.BlockSpec((1, N_KV_HEADS, t, HEAD_GROUP * t), lambda i: (i, 0, 0, 0)),
        out_shape=jax.ShapeDtypeStruct((2, N_KV_HEADS, t, HEAD_GROUP * t), F32),
        compiler_params=_cparams(("arbitrary",)),
        name="bias_tiles",
    )(rel_bias)


def _dsa_body(q_ref, qi_ref, wit_ref, kia_ref, kib_ref, k_ref, vt_ref, bias_ref, o_ref,
              score_ref, mask_ref, acc_ref, *, t, topk):
    i = pl.program_id(1)
    nk = i + 1
    t_glob = i * t + lax.broadcasted_iota(I32, (1, t), 1)
    idx_scale = (IDX_DIM ** -0.5) * (IDX_HEADS ** -0.5)

    def score_chunk(c, carry):
        cnt, mn, mx = carry
        r0 = pl.multiple_of(c * SCORE_SUB, SCORE_SUB)
        ka = kia_ref[0, pl.ds(r0, SCORE_SUB), :]
        kb = kib_ref[0, pl.ds(r0, SCORE_SUB), :]
        acc = jnp.zeros((SCORE_SUB, t), F32)
        for j in range(IDX_HEADS // 2):
            qp = qi_ref[0, :, j * LANES:(j + 1) * LANES]
            da = _dot_nt(ka, qp)
            db = _dot_nt(kb, qp)
            acc = acc + jnp.maximum(da, 0.0) * wit_ref[0, 2 * j:2 * j + 1, :]
            acc = acc + jnp.maximum(db, 0.0) * wit_ref[0, 2 * j + 1:2 * j + 2, :]
        sc = acc * idx_scale
        s_glob = r0 + lax.broadcasted_iota(I32, (SCORE_SUB, 1), 0)
        b = jnp.where(s_glob <= t_glob, sc, NEG_INF)
        score_ref[pl.ds(r0, SCORE_SUB), :] = b
        real = b > NEG_INF
        rows8 = lambda a: a.reshape(SCORE_SUB // SUBLANES, SUBLANES, t)
        cnt = cnt + jnp.sum(rows8(real.astype(I32)), axis=0)
        mn = jnp.minimum(mn, jnp.min(rows8(jnp.where(real, b, BIG)), axis=0))
        mx = jnp.maximum(mx, jnp.max(rows8(b), axis=0))
        return cnt, mn, mx

    cnt8, mn8, mx8 = lax.fori_loop(
        0, nk * (t // SCORE_SUB), score_chunk,
        (jnp.zeros((SUBLANES, t), I32), jnp.full((SUBLANES, t), BIG, F32), jnp.full((SUBLANES, t), -BIG, F32)))

    nblk = nk * (t // COUNT_RB)
    fold = lambda m: jnp.sum(m.reshape(COUNT_RB // SUBLANES, SUBLANES, t), axis=0)

    def blk(r):
        return score_ref[pl.ds(pl.multiple_of(r * COUNT_RB, COUNT_RB), COUNT_RB), :]

    def count_ge(c):
        def body(r, acc):
            return acc + fold((blk(r) >= c).astype(I32))
        acc = lax.fori_loop(0, nblk, body, jnp.zeros((SUBLANES, t), I32))
        return jnp.sum(acc, axis=0, keepdims=True)

    n_real = jnp.sum(cnt8, axis=0, keepdims=True)
    rmin = jnp.min(mn8, axis=0, keepdims=True)
    rmax = jnp.max(mx8, axis=0, keepdims=True)
    n_max = count_ge(rmax)
    degenerate = jnp.logical_and(n_real >= topk, n_max >= topk)
    lo0 = jnp.where(n_real < topk, NEG_INF, jnp.where(degenerate, rmax, rmin))
    hi0 = jnp.where(degenerate, BIG, rmax)
    done0 = jnp.where(n_real <= topk, 1, 0)

    def bisect(st):
        lo, hi, done, stuck = st
        c = lo + (hi - lo) * 0.5
        active = (done + stuck) == 0
        has_mid = jnp.logical_and(c > lo, c < hi)
        n = count_ge(c)
        upd = jnp.logical_and(active, has_mid)
        lo = jnp.where(jnp.logical_and(upd, n >= topk), c, lo)
        hi = jnp.where(jnp.logical_and(upd, n < topk), c, hi)
        done = jnp.where(jnp.logical_and(upd, n == topk), 1, done)
        stuck = jnp.where(jnp.logical_and(active, jnp.logical_not(has_mid)), 1, stuck)
        return lo, hi, done, stuck

    st = lax.fori_loop(0, SEL_FIXED_ITERS, lambda _, s: bisect(s), (lo0, hi0, done0, jnp.zeros((1, t), I32)))

    def sel_cond(s):
        return jnp.logical_and(s[0] < SEL_MAX_ITERS, jnp.min(s[3] + s[4]) == 0)

    _, lo, hi, done, _ = lax.while_loop(sel_cond, lambda s: (s[0] + 1,) + bisect(s[1:]), (jnp.int32(0),) + st)

    def write_mask(r, cut):
        r0 = pl.multiple_of(r * COUNT_RB, COUNT_RB)
        b = score_ref[pl.ds(r0, COUNT_RB), :]
        s_glob = r0 + lax.broadcasted_iota(I32, (COUNT_RB, 1), 0)
        sel = jnp.logical_and(b >= lo, s_glob <= t_glob)
        if cut is not None:
            sel = jnp.logical_and(sel, jnp.logical_or(b >= hi, s_glob < cut))
        mask_ref[pl.ds(r0, COUNT_RB), :] = jnp.concatenate([jnp.where(sel, 0.0, NEG_INF)] * HEAD_GROUP, axis=1)

    any_tied = jnp.min(done) == 0

    @pl.when(jnp.logical_not(any_tied))
    def _():
        def body(r, carry):
            write_mask(r, None)
            return carry
        lax.fori_loop(0, nblk, body, 0)

    @pl.when(any_tied)
    def _():
        need = topk - count_ge(hi)

        def count_tie_below(x):
            def body(r, acc):
                r0 = pl.multiple_of(r * COUNT_RB, COUNT_RB)
                b = score_ref[pl.ds(r0, COUNT_RB), :]
                s_glob = r0 + lax.broadcasted_iota(I32, (COUNT_RB, 1), 0)
                m = jnp.logical_and(jnp.logical_and(b >= lo, b < hi), s_glob < x)
                return acc + fold(m.astype(I32))
            acc = lax.fori_loop(0, nblk, body, jnp.zeros((SUBLANES, t), I32))
            return jnp.sum(acc, axis=0, keepdims=True)

        nbits = int(score_ref.shape[0]).bit_length()

        def bit_body(bi, x):
            c = x + jnp.left_shift(jnp.int32(1), nbits - 1 - bi)
            return jnp.where(count_tie_below(c) < need, c, x)

        x = lax.fori_loop(0, nbits, bit_body, jnp.zeros((1, t), I32))
        cut = jnp.where(done > 0, jnp.int32(2 ** 30), x + 1)

        def body(r, carry):
            write_mask(r, cut)
            return carry
        lax.fori_loop(0, nblk, body, 0)

    scale = HEAD_DIM ** -0.5
    gt = HEAD_GROUP * t

    def col_reduce(op, a):
        part = op(a.reshape(a.shape[0] // SUBLANES, SUBLANES, a.shape[1]), axis=0)
        return op(part, axis=0, keepdims=True)

    qgs = [jnp.concatenate(
        [q_ref[0, :, h * HEAD_DIM:(h + 1) * HEAD_DIM] for h in range(g * HEAD_GROUP, (g + 1) * HEAD_GROUP)],
        axis=0) for g in range(N_KV_HEADS)]
    acc_ref[...] = jnp.zeros_like(acc_ref)

    def att_chunk(kc, carry, near):
        r0 = pl.multiple_of(kc * t, t)
        mk = mask_ref[pl.ds(r0, t), :]
        new = []
        for g in range(N_KV_HEADS):
            m, l = carry[g]
            kk = k_ref[0, pl.ds(r0, t), g * HEAD_DIM:(g + 1) * HEAD_DIM]
            s = _dot_nt(kk, qgs[g]) * (scale * LOG2E) + mk
            if near:
                s = s + bias_ref[i - kc, g]
            m_new = jnp.maximum(m, col_reduce(jnp.max, s))
            alpha = jnp.exp2(m - m_new)
            p = jnp.exp2(s - m_new)
            l = alpha * l + col_reduce(jnp.sum, p)
            vt = vt_ref[0, g * HEAD_DIM:(g + 1) * HEAD_DIM, pl.ds(r0, t)]
            acc_ref[g] = alpha * acc_ref[g] + _dot(vt, p.astype(_MXU_DTYPE))
            new.append((m_new, l))
        return tuple(new)

    init = (jnp.full((1, gt), NEG_INF, F32), jnp.zeros((1, gt), F32))
    n_far = jnp.maximum(i - 1, 0)
    far = lax.fori_loop(0, n_far, functools.partial(att_chunk, near=False), (init,) * N_KV_HEADS)
    fin = lax.fori_loop(n_far, nk, functools.partial(att_chunk, near=True), far)
    for g in range(N_KV_HEADS):
        out = acc_ref[g] / fin[g][1]
        for hh in range(HEAD_GROUP):
            h = g * HEAD_GROUP + hh
            o_ref[0, :, h * HEAD_DIM:(h + 1) * HEAD_DIM] = out[:, hh * t:(hh + 1) * t].T.astype(o_ref.dtype)


def _dsa(q, qi, wit, kia, kib, k, vt, bias, topk):
    bsz, s, aw = q.shape
    t = ATT_T
    assert s % t == 0 and topk <= t
    row = lambda w: pl.BlockSpec((1, t, w), lambda b, i: (b, i, 0))
    full = lambda a: pl.BlockSpec((1,) + a.shape[1:], lambda b, i: (b, 0, 0))
    return pl.pallas_call(
        functools.partial(_dsa_body, t=t, topk=topk),
        grid=(bsz, s // t),
        in_specs=[row(aw), row(qi.shape[2]),
                  pl.BlockSpec((1, IDX_HEADS, t), lambda b, i: (b, 0, i)),
                  full(kia), full(kib), full(k), full(vt),
                  pl.BlockSpec(bias.shape, lambda b, i: (0, 0, 0, 0))],
        out_specs=row(aw),
        out_shape=jax.ShapeDtypeStruct((bsz, s, aw), _MXU_DTYPE),
        scratch_shapes=[pltpu.VMEM((s, t), F32), pltpu.VMEM((s, HEAD_GROUP * t), F32),
                        pltpu.VMEM((N_KV_HEADS, HEAD_DIM, HEAD_GROUP * t), F32)],
        compiler_params=_cparams(("arbitrary", "arbitrary")),
        name="dsa",
    )(q, qi, wit, kia, kib, k, vt, bias)


def _pack_halves(hb):
    assert jnp.dtype(hb.dtype).itemsize == 2
    n = hb.shape[1] // 2
    hi = lax.bitcast_convert_type(hb[:, :n].astype(F32), I32)
    lo = lax.bitcast_convert_type(hb[:, n:].astype(F32), I32)
    return jnp.bitwise_or(hi, lax.shift_right_logical(lo, jnp.full(lo.shape, 16, I32)))


def _unpack_halves(w, dtype):
    hi = lax.bitcast_convert_type(jnp.bitwise_and(w, jnp.int32(-65536)), F32).astype(dtype)
    lo = lax.bitcast_convert_type(jnp.left_shift(w, 16), F32).astype(dtype)
    return jnp.concatenate([hi, lo], axis=1)


def _mix_body(u_ref, halo_ref, a_ref, x_ref, mod_ref, wpool_ref, ps_ref, wout_ref, g_ref, wr_ref, br_ref,
              x1_ref, h2_ref, lg_ref, *, seq, tm):
    t0 = (pl.program_id(0) * tm) % seq
    u = u_ref[...]
    halo = jnp.where(t0 == 0, 0.0, halo_ref[...])
    ext = jnp.concatenate([halo, u], axis=0)
    pos = t0 + lax.broadcasted_iota(I32, (tm, 1), 0)
    gw = u.shape[1] // N_POOL_GROUPS
    ys = []
    for g, w in enumerate(POOL_WINDOWS):
        a = ext[:, g * gw:(g + 1) * gw]
        sft = 1
        while sft < w:
            a = a + pltpu.roll(a, sft, 0)
            sft *= 2
        cnt = jnp.minimum(pos + 1, w).astype(F32)
        p = a[HALO:, :] / cnt - u[:, g * gw:(g + 1) * gw]
        ys.append(_dot(p.astype(_MXU_DTYPE), wpool_ref[g]) * ps_ref[:, g * gw:(g + 1) * gw])
    pool = jnp.concatenate(ys, axis=1).astype(_MXU_DTYPE)
    pw = pool.shape[1]
    mixed = _dot(pool, wout_ref[:pw, :]) + _dot(a_ref[...], wout_ref[pw:, :])
    x1 = x_ref[...] + mod_ref[0, 2:3, :] * mixed
    x1_ref[...] = x1
    h2 = _rms_mod(x1, g_ref[...], mod_ref[0, 3:4, :], mod_ref[0, 4:5, :]).astype(_MXU_DTYPE)
    h2_ref[...] = _pack_halves(h2)
    lg_ref[...] = _dot(h2, wr_ref[...]) + br_ref[...]


def _mix(u, attn, x, mod, w_pool, pool_scale, w_out, g_ffn, w_r, b_r, seq):
    n_tok, d = x.shape
    tm = MIX_TM
    assert seq % tm == 0 and tm % HALO == 0
    pw = u.shape[1]
    aw = attn.shape[1]
    rows = lambda w: pl.BlockSpec((tm, w), lambda i: (i, 0))
    const = lambda a: pl.BlockSpec(a.shape, lambda i: (0,) * a.ndim)
    return pl.pallas_call(
        functools.partial(_mix_body, seq=seq, tm=tm),
        grid=(n_tok // tm,),
        in_specs=[rows(pw),
                  pl.BlockSpec((HALO, pw), lambda i: (jnp.maximum(i * (tm // HALO) - 1, 0), 0)),
                  rows(aw), rows(d),
                  pl.BlockSpec((1, 6, d), lambda i: ((i * tm) // seq, 0, 0)),
                  const(w_pool), const(pool_scale), const(w_out), const(g_ffn), const(w_r), const(b_r)],
        out_specs=[rows(d), rows(d // 2), rows(LANES)],
        out_shape=[jax.ShapeDtypeStruct((n_tok, d), F32), jax.ShapeDtypeStruct((n_tok, d // 2), I32),
                   jax.ShapeDtypeStruct((n_tok, LANES), F32)],
        compiler_params=_cparams(("arbitrary",)),
        name="mix",
    )(u, u, attn, x, mod, w_pool, pool_scale, w_out, g_ffn, w_r, b_r)


def _route_body(lg_ref, rt_ref, gate_ref, cnt_ref, run_ref):
    @pl.when(pl.program_id(0) == 0)
    def _():
        run_ref[...] = jnp.zeros_like(run_ref)

    lg = lg_ref[...]
    tr = lg.shape[0]
    lane = lax.broadcasted_iota(I32, lg.shape, 1)
    gmask = lane < N_GROUPS
    gl = jnp.where(gmask, lg, -BIG)
    gmax = jnp.max(gl, axis=-1, keepdims=True)
    gsel = jnp.min(jnp.where(jnp.logical_and(gmask, gl == gmax), lane, LANES), axis=-1, keepdims=True)
    gsum = jnp.sum(jnp.where(gmask, jnp.exp(gl - gmax), 0.0), axis=-1, keepdims=True)
    p_g = 1.0 / gsum
    in_e = jnp.logical_and(lane >= N_GROUPS, lane < N_GROUPS + N_EXPERTS)
    emask = jnp.logical_and(in_e, lax.shift_right_arithmetic(lane - N_GROUPS, EXPERTS_PER_GROUP.bit_length() - 1) == gsel)
    el = jnp.where(emask, lg, -BIG)
    emax = jnp.max(el, axis=-1, keepdims=True)
    ex = jnp.where(emask, jnp.exp(el - emax), 0.0)
    ep = ex / jnp.sum(ex, axis=-1, keepdims=True)
    p1 = jnp.max(jnp.where(emask, ep, -1.0), axis=-1, keepdims=True)
    i1 = jnp.min(jnp.where(jnp.logical_and(emask, ep == p1), lane, LANES), axis=-1, keepdims=True)
    m2 = jnp.logical_and(emask, lane != i1)
    p2 = jnp.max(jnp.where(m2, ep, -1.0), axis=-1, keepdims=True)
    i2 = jnp.min(jnp.where(jnp.logical_and(m2, ep == p2), lane, LANES), axis=-1, keepdims=True)
    den = p1 + p2
    g1 = p_g * p1 / den
    g2 = p_g * p2 / den
    e1 = i1 - N_GROUPS
    e2 = i2 - N_GROUPS

    oh = jnp.logical_or(lane == e1, lane == e2)
    ri = lax.broadcasted_iota(I32, (tr, tr), 0)
    ci = lax.broadcasted_iota(I32, (tr, tr), 1)
    ltri = (ci < ri).astype(_MXU_DTYPE)
    before = _dot(ltri, oh.astype(_MXU_DTYPE)) + run_ref[...]
    r1 = jnp.sum(jnp.where(lane == e1, before, 0.0), axis=-1, keepdims=True)
    r2 = jnp.sum(jnp.where(lane == e2, before, 0.0), axis=-1, keepdims=True)
    run = run_ref[...] + jnp.sum(oh.astype(F32), axis=0, keepdims=True)
    run_ref[...] = run
    cnt_ref[...] = jnp.broadcast_to(run, cnt_ref.shape)

    route = jnp.where(lane == 0, e1.astype(F32),
                      jnp.where(lane == 1, e2.astype(F32),
                                jnp.where(lane == 2, r1, jnp.where(lane == 3, r2, 0.0))))
    rt_ref[...] = route.T[:SUBLANES, :].astype(I32)
    gate_ref[...] = jnp.where(lane == 0, g1, jnp.where(lane == 1, g2, 0.0))


def _route(logits):
    n_tok = logits.shape[0]
    tr = min(ROUTE_TR, n_tok)
    return pl.pallas_call(
        _route_body,
        grid=(n_tok // tr,),
        in_specs=[pl.BlockSpec((tr, LANES), lambda i: (i, 0))],
        out_specs=[pl.BlockSpec((SUBLANES, tr), lambda i: (0, i)),
                   pl.BlockSpec((tr, LANES), lambda i: (i, 0)),
                   pl.BlockSpec((SUBLANES, LANES), lambda i: (0, 0))],
        out_shape=[jax.ShapeDtypeStruct((SUBLANES, n_tok), I32),
                   jax.ShapeDtypeStruct((n_tok, LANES), F32),
                   jax.ShapeDtypeStruct((SUBLANES, LANES), F32)],
        scratch_shapes=[pltpu.VMEM((1, LANES), F32)],
        compiler_params=_cparams(("arbitrary",)),
        name="route",
    )(logits)


def _dest_body(ps_ref, rt_ref, o_ref):
    r = rt_ref[...]
    start = jnp.zeros_like(r)
    for e in range(N_EXPERTS):
        start = jnp.where(r == e, ps_ref[e], start)
    o_ref[...] = start + pltpu.roll(r, SUBLANES - 2, 0)


def _dest(route_t, pstart):
    n_tok = route_t.shape[1]
    tb = min(n_tok, 2048)
    return pl.pallas_call(
        _dest_body,
        grid_spec=pltpu.PrefetchScalarGridSpec(
            num_scalar_prefetch=1,
            grid=(n_tok // tb,),
            in_specs=[pl.BlockSpec((SUBLANES, tb), lambda i, ps: (0, i))],
            out_specs=pl.BlockSpec((SUBLANES, tb), lambda i, ps: (0, i)),
        ),
        out_shape=jax.ShapeDtypeStruct((SUBLANES, n_tok), I32),
        compiler_params=_cparams(("arbitrary",)),
        name="dest",
    )(pstart, route_t)


SC_LANES = 16
SC_WIN = 16
SC_NBUF = 4
SC_CHUNK = 8192


def _sc_dispatch(h2, dest0, dest1, n_slots):
    n_tok, d = h2.shape
    info = plsc.get_sparse_core_info()
    nc, ns = info.num_cores, info.num_subcores
    nw = nc * ns
    assert info.num_lanes == SC_LANES and n_slots % (nw * SC_WIN) == 0 and n_tok % SC_CHUNK == 0
    assert n_slots < 3 * n_tok
    per_w = n_slots // nw
    mesh = plsc.VectorSubcoreMesh(core_axis_name="c", subcore_axis_name="s")

    n_win = per_w // SC_WIN
    assert n_win % SC_NBUF == 0

    @functools.partial(
        pl.kernel, mesh=mesh,
        out_type=jax.ShapeDtypeStruct((n_slots, d), h2.dtype),
        scratch_types=[pltpu.VMEM((per_w,), I32), pltpu.VMEM((SC_CHUNK,), I32),
                       pltpu.VMEM((SC_NBUF, SC_WIN, d), h2.dtype), pltpu.SemaphoreType.DMA((SC_NBUF,)),
                       pltpu.SemaphoreType.DMA((SC_NBUF,))],
        compiler_params=pltpu.CompilerParams(needs_layout_passes=False),
    )
    def k(h_hbm, d0_hbm, d1_hbm, o_hbm, table, chunk, rows, gsems, wsems):
        lane = lax.iota(I32, SC_LANES)
        base = (lax.axis_index("s") * nc + lax.axis_index("c")) * per_w

        @pl.loop(0, per_w, step=SC_LANES)
        def _(s):
            v = base + s + lane
            v = jnp.where(v >= n_tok, v - n_tok, v)
            v = jnp.where(v >= n_tok, v - n_tok, v)
            table[pl.ds(s, SC_LANES)] = v

        for d_hbm in (d0_hbm, d1_hbm):
            @pl.loop(0, n_tok, step=SC_CHUNK)
            def _(t0, d_hbm=d_hbm):
                pltpu.sync_copy(d_hbm.at[pl.ds(t0, SC_CHUNK)], chunk)

                @pl.loop(0, SC_CHUNK, step=SC_LANES)
                def _(j):
                    r = chunk[pl.ds(j, SC_LANES)] - base
                    mine = jnp.logical_and(r >= 0, r < per_w)
                    plsc.store_scatter(table, [jnp.where(mine, r, 0)], t0 + j + lane, mask=mine)

        ahead = SC_NBUF // 2

        def gather(w, b):
            return pltpu.make_async_copy(
                h_hbm.at[table.at[pl.ds(w * SC_WIN, SC_WIN)]], rows.at[b], gsems.at[b])

        def put(w, b):
            return pltpu.make_async_copy(rows.at[b], o_hbm.at[pl.ds(base + w * SC_WIN, SC_WIN)], wsems.at[b])

        for b in range(ahead):
            gather(b, b).start()

        @pl.loop(0, n_win, step=SC_NBUF)
        def _(g):
            for b in range(SC_NBUF):
                w = g + b
                pb = (b + ahead) % SC_NBUF
                gather(w, b).wait()
                put(w, b).start()

                @pl.when(w >= ahead)
                def _():
                    put(w - ahead, pb).wait()

                @pl.when(w + ahead < n_win)
                def _():
                    gather(w + ahead, pb).start()

        for b in range(ahead):
            w = n_win - ahead + b
            put(w, w % SC_NBUF).wait()

    return k(h2, dest0, dest1)


def _experts_body(slot_ref, cast_ref, pre_ref, misc_ref, x_ref, wg_hbm, wu_hbm, wd_hbm, y_ref,
                  sg, su, sd, wgb, wub, wdb, sem):
    blk = pl.program_id(0)
    e0, pre0, n_used = misc_ref[0], misc_ref[1], misc_ref[2]

    def copies(e):
        return (pltpu.make_async_copy(wg_hbm.at[e], sg, sem.at[0]),
                pltpu.make_async_copy(wu_hbm.at[e], su, sem.at[1]),
                pltpu.make_async_copy(wd_hbm.at[e], sd, sem.at[2]))

    def fetch(e):
        for c in copies(e):
            c.start(priority=1)

    def land(slot):
        for c in copies(0):
            c.wait()
        wgb[slot] = sg[...].astype(wgb.dtype)
        wub[slot] = su[...].astype(wub.dtype)
        wdb[slot] = sd[...].astype(wdb.dtype)

    @pl.when(blk == 0)
    def _():
        fetch(e0)
        land(0)

        @pl.when(pre0 >= 0)
        def _():
            fetch(pre0)

    @pl.when(blk < n_used)
    def _():
        slot = slot_ref[blk]
        xb = _unpack_halves(x_ref[...], _MXU_DTYPE)
        a = _silu(_dot(xb, wgb[slot])) * _dot(xb, wub[slot])
        y_ref[...] = _dot(a.astype(_MXU_DTYPE), wdb[slot])

        @pl.when(cast_ref[blk] == 1)
        def _():
            land(1 - slot)

            @pl.when(pre_ref[blk] >= 0)
            def _():
                fetch(pre_ref[blk])

    @pl.when(blk >= n_used)
    def _():
        y_ref[...] = jnp.zeros_like(y_ref)


def _experts(xs, n_used, pends, w_gate, w_up, w_down):
    n_slots, dpk = xs.shape
    bm = MOE_BM
    n_blocks = n_slots // bm
    d, de = w_gate.shape[1:]
    last_used = n_used[0] - 1
    total = pends[-1]
    eid = jnp.arange(N_EXPERTS, dtype=I32)[None, :]
    owner = lambda start: jnp.minimum(jnp.sum((pends[None, :] <= start[:, None]).astype(I32), axis=1), N_EXPERTS - 1)
    end_of = lambda e: jnp.sum(jnp.where(eid == e[:, None], pends[None, :], 0), axis=1)
    be = owner(jnp.minimum(jnp.arange(n_blocks, dtype=I32), last_used) * bm)
    end_e = end_of(be)
    has_next = end_e < total
    nxt = owner(end_e)
    end_n = end_of(nxt)
    nxt2 = jnp.where(jnp.logical_and(has_next, end_n < total), owner(end_n), -1)
    is_last = jnp.concatenate([be[1:] != be[:-1], jnp.ones((1,), jnp.bool_)])
    is_last = jnp.logical_or(is_last, jnp.arange(n_blocks) >= last_used)
    cast = jnp.logical_and(is_last, has_next).astype(I32)
    pre = jnp.where(cast == 1, nxt2, -1).astype(I32)
    is_first = jnp.concatenate([jnp.ones((1,), jnp.bool_), be[1:] != be[:-1]])
    slot = ((jnp.cumsum(is_first.astype(I32)) - 1) % 2).astype(I32)
    misc = jnp.stack([be[0], jnp.where(has_next[0], nxt[0], -1), n_used[0]]).astype(I32)
    hbm = pl.BlockSpec(memory_space=pl.ANY)
    return pl.pallas_call(
        _experts_body,
        grid_spec=pltpu.PrefetchScalarGridSpec(
            num_scalar_prefetch=4,
            grid=(n_blocks,),
            in_specs=[pl.BlockSpec((bm, dpk), lambda j, sl, ca, pr, mi: (jnp.minimum(j, mi[2] - 1), 0)),
                      hbm, hbm, hbm],
            out_specs=pl.BlockSpec((bm, d), lambda j, sl, ca, pr, mi: (j, 0)),
            scratch_shapes=[pltpu.VMEM((d, de), w_gate.dtype), pltpu.VMEM((d, de), w_up.dtype),
                            pltpu.VMEM((de, d), w_down.dtype),
                            pltpu.VMEM((2, d, de), _MXU_DTYPE), pltpu.VMEM((2, d, de), _MXU_DTYPE),
                            pltpu.VMEM((2, de, d), _MXU_DTYPE), pltpu.SemaphoreType.DMA((3,))],
        ),
        out_shape=jax.ShapeDtypeStruct((n_slots, d), F32),
        compiler_params=_cparams(("arbitrary",)),
        name="experts",
    )(slot, cast, pre, misc, xs, w_gate, w_up, w_down)


def _combine_body(x1_ref, gate_ref, d0_ref, d1_ref, n0_ref, n1_ref, mod_ref, g_ref, ys_ref, o_ref, buf_ref, sem,
                  *, tf):
    i = pl.program_id(0)
    cur = i % 2

    def gather(da_ref, db_ref, half):
        def issue(c, carry):
            for u in range(SUBLANES):
                r = c * SUBLANES + u
                for j, d_ref in enumerate((da_ref, db_ref)):
                    pltpu.make_async_copy(ys_ref.at[pl.ds(d_ref[r], 1), :],
                                          buf_ref.at[half, j, c, pl.ds(u, 1), :], sem.at[half]).start(priority=j)
            return carry

        lax.fori_loop(0, tf // SUBLANES, issue, 0)

    @pl.when(i == 0)
    def _():
        gather(d0_ref, d1_ref, 0)

    @pl.when(i + 1 < pl.num_programs(0))
    def _():
        gather(n0_ref, n1_ref, 1 - cur)

    for _ in range(2):
        pltpu.make_async_copy(ys_ref.at[pl.ds(0, tf), :], ys_ref.at[pl.ds(0, tf), :], sem.at[cur]).wait()

    gates = gate_ref[...]
    d = o_ref.shape[1]
    moe = buf_ref[cur, 0].reshape(tf, d) * gates[:, 0:1] + buf_ref[cur, 1].reshape(tf, d) * gates[:, 1:2]
    x2 = x1_ref[...] + mod_ref[0, 5:6, :] * moe
    ms = jnp.mean(x2 * x2, axis=-1, keepdims=True)
    o_ref[...] = x2 * lax.rsqrt(ms + EPS) * g_ref[...]


def _combine(x1, gates, dest0, dest1, mod, g_final, ys, seq):
    n_tok, d = x1.shape
    tf = COMB_TF
    n_tiles = n_tok // tf
    tok = pl.BlockSpec((tf,), lambda i: (i,), memory_space=pltpu.SMEM)
    nxt = pl.BlockSpec((tf,), lambda i: (jnp.minimum(i + 1, n_tiles - 1),), memory_space=pltpu.SMEM)
    return pl.pallas_call(
        functools.partial(_combine_body, tf=tf),
        grid=(n_tiles,),
        in_specs=[pl.BlockSpec((tf, d), lambda i: (i, 0)),
                  pl.BlockSpec((tf, LANES), lambda i: (i, 0)),
                  tok, tok, nxt, nxt,
                  pl.BlockSpec((1, 6, d), lambda i: ((i * tf) // seq, 0, 0)),
                  pl.BlockSpec((1, d), lambda i: (0, 0)),
                  pl.BlockSpec(memory_space=pl.ANY)],
        out_specs=pl.BlockSpec((tf, d), lambda i: (i, 0)),
        scratch_shapes=[pltpu.VMEM((2, 2, tf // SUBLANES, SUBLANES, d), F32), pltpu.SemaphoreType.DMA((2,))],
        out_shape=jax.ShapeDtypeStruct((n_tok, d), F32),
        compiler_params=_cparams(("arbitrary",)),
        name="combine",
    )(x1, gates, dest0, dest1, dest0, dest1, mod, g_final, ys)


def kernel(x, c, w_ada, b_ada, g_mix, w_in, w_pool, pool_scale, rel_bias, w_out, g_ffn, w_group, b_group,
           w_router, b_router, w_gate, w_up, w_down, g_final):
    bsz, seq, d = x.shape
    n_tok = bsz * seq
    depth = w_ada.shape[0]
    assert depth == 1, "the final norm is fused into the only layer's combine"
    topk = min(TOPK_MAX, seq // 4)
    bias = _bias_tiles(rel_bias, ATT_T)
    bm = MOE_BM
    n_blocks = -(-(n_tok * 2) // bm) + N_EXPERTS
    n_slots = n_blocks * bm
    xt = x.reshape(n_tok, d)
    for i in range(depth):
        mod = _ada(c, w_ada[i], b_ada[i][None, :]).reshape(bsz, 6, d)
        u, q, k, vt, qi, kia, kib, wit = _proj(xt.reshape(bsz, seq, d), mod, g_mix[i][None, :], w_in[i])
        attn = _dsa(q, qi, wit, kia, kib, k, vt, bias, topk)
        w_r = jnp.concatenate(
            [w_group[i], w_router[i], jnp.zeros((d, LANES - N_GROUPS - N_EXPERTS), F32)], axis=1).astype(_MXU_DTYPE)
        b_r = jnp.concatenate(
            [b_group[i], b_router[i], jnp.zeros((LANES - N_GROUPS - N_EXPERTS,), F32)])[None, :]
        x1, h2, logits = _mix(u.reshape(n_tok, -1), attn.reshape(n_tok, -1), xt, mod,
                              w_pool[i].astype(_MXU_DTYPE), pool_scale[i][None, :],
                              w_out[i].astype(_MXU_DTYPE), g_ffn[i][None, :], w_r, b_r, seq)
        route_t, gates, cnt = _route(logits)
        counts = cnt[0, :N_EXPERTS].astype(I32)
        padded = (counts + bm - 1) // bm * bm
        pends = jnp.cumsum(padded)
        pstart = pends - padded
        n_used = (pends[-1:] // bm).astype(I32)
        dest = _dest(route_t, pstart)
        dest0, dest1 = dest[0], dest[1]
        xs = _sc_dispatch(h2, dest0, dest1, n_slots)
        ys = _experts(xs, n_used, pends, w_gate[i], w_up[i], w_down[i])
        xt = _combine(x1, gates, dest0, dest1, mod, g_final[None, :], ys, seq)
    return xt.reshape(bsz, seq, d)
```

```python
import functools
import math

import numpy as np
import jax
import jax.numpy as jnp
from jax import lax
from jax.experimental import pallas as pl
from jax.experimental.pallas import tpu as pltpu
from jax.experimental.pallas import tpu_sc as plsc

F32 = jnp.float32
I32 = jnp.int32
_MXU_DTYPE = jnp.bfloat16

POOL_WINDOWS = (2, 4, 8, 16)
N_POOL_GROUPS = 4
HEAD_DIM = 128
N_HEADS = 8
N_KV_HEADS = 2
HEAD_GROUP = N_HEADS // N_KV_HEADS
IDX_HEADS = 16
IDX_DIM = 64
TOPK_MAX = 256
N_BUCKETS = 32
MAX_DISTANCE = 128
N_GROUPS = 4
EXPERTS_PER_GROUP = 8
N_EXPERTS = N_GROUPS * EXPERTS_PER_GROUP
EPS = 1e-6
NEG_INF = -1e30
BIG = 3e38
LOG2E = 1.4426950408889634

LANES = 128
SUBLANES = 8
VMEM_LIMIT_BYTES = 56 * 1024 * 1024

ADA_TN = 1024
PROJ_TM = 256
ATT_T = 256
SCORE_SUB = 256
COUNT_RB = 256
MIX_TM = 256
HALO = 16
ROUTE_TR = 1024
MOE_BM = 256
COMB_TF = 256
SEL_FIXED_ITERS = 20
SEL_MAX_ITERS = 320
NO_CUT = 2 ** 30


def _cparams(sem):
    return pltpu.CompilerParams(dimension_semantics=sem, vmem_limit_bytes=VMEM_LIMIT_BYTES)


def _silu(x):
    return x * (1.0 / (1.0 + jnp.exp(-x)))


def _dot(a, b):
    return jnp.dot(a, b, preferred_element_type=F32)


def _dot_nt(a, b):
    return lax.dot_general(a, b, (((1,), (1,)), ((), ())), preferred_element_type=F32)


def _ada_body(c_ref, w_ref, b_ref, o_ref):
    s = _silu(c_ref[...])
    o_ref[...] = _dot(s.astype(_MXU_DTYPE), w_ref[...].astype(_MXU_DTYPE)) + b_ref[...]


def _ada(c, w, b):
    bsz, d = c.shape
    n = w.shape[1]
    return pl.pallas_call(
        _ada_body,
        grid=(n // ADA_TN,),
        in_specs=[pl.BlockSpec((bsz, d), lambda j: (0, 0)),
                  pl.BlockSpec((d, ADA_TN), lambda j: (0, j)),
                  pl.BlockSpec((1, ADA_TN), lambda j: (0, j))],
        out_specs=pl.BlockSpec((bsz, ADA_TN), lambda j: (0, j)),
        out_shape=jax.ShapeDtypeStruct((bsz, n), F32),
        compiler_params=_cparams(("arbitrary",)),
        name="ada",
    )(c, w, b)


def _rms_mod(x, g, shift, scale):
    ms = jnp.mean(x * x, axis=-1, keepdims=True)
    y = x * lax.rsqrt(ms + EPS) * g
    return y * (1.0 + scale) + shift


def _proj_body(x_ref, mod_ref, g_ref, w_ref, u_ref, q_ref, k_ref, vt_ref, qi_ref, kia_ref, kib_ref,
               wit_ref, *, cuts):
    h = _rms_mod(x_ref[0], g_ref[...], mod_ref[0, 0:1, :], mod_ref[0, 1:2, :])
    hb = h.astype(_MXU_DTYPE)

    def seg(name):
        lo, hi = cuts[name]
        return _dot(hb, w_ref[:, lo:hi])

    u_ref[0] = seg("u")
    q_ref[0] = seg("q").astype(q_ref.dtype)
    k_ref[0] = seg("k").astype(k_ref.dtype)
    vt_ref[0] = seg("v").T.astype(vt_ref.dtype)
    qi_ref[0] = seg("qi").astype(qi_ref.dtype)
    tail = seg("tail")
    tail = jnp.concatenate([tail, jnp.zeros((tail.shape[0], LANES - tail.shape[1]), F32)], axis=1)
    swapped = pltpu.roll(tail, LANES - IDX_DIM, 1)
    lane = lax.broadcasted_iota(I32, tail.shape, 1)
    kia_ref[0] = jnp.where(lane < IDX_DIM, tail, 0.0).astype(kia_ref.dtype)
    kib_ref[0] = jnp.where(lane >= LANES - IDX_DIM, swapped, 0.0).astype(kib_ref.dtype)
    wit_ref[0] = swapped.T[:IDX_HEADS, :]


def _proj(x, mod, g_mix, w_in):
    bsz, s, d = x.shape
    pool_w = d // 2
    attn_w = N_HEADS * HEAD_DIM
    kv_w = N_KV_HEADS * HEAD_DIM
    qi_w = IDX_HEADS * IDX_DIM
    c = np.cumsum([0, pool_w, attn_w, kv_w, kv_w, qi_w, IDX_DIM, IDX_HEADS])
    assert IDX_DIM + IDX_HEADS <= LANES and w_in.shape[1] == c[7]
    wp = w_in.astype(_MXU_DTYPE)
    cuts = {"u": (int(c[0]), int(c[1])), "q": (int(c[1]), int(c[2])), "k": (int(c[2]), int(c[3])),
            "v": (int(c[3]), int(c[4])), "qi": (int(c[4]), int(c[5])), "tail": (int(c[5]), int(c[7]))}
    tm = PROJ_TM
    pw = wp.shape[1]
    row = lambda w: pl.BlockSpec((1, tm, w), lambda b, i: (b, i, 0))
    out_shapes = [
        jax.ShapeDtypeStruct((bsz, s, pool_w), F32),
        jax.ShapeDtypeStruct((bsz, s, attn_w), _MXU_DTYPE),
        jax.ShapeDtypeStruct((bsz, s, kv_w), _MXU_DTYPE),
        jax.ShapeDtypeStruct((bsz, kv_w, s), _MXU_DTYPE),
        jax.ShapeDtypeStruct((bsz, s, qi_w), _MXU_DTYPE),
        jax.ShapeDtypeStruct((bsz, s, LANES), _MXU_DTYPE),
        jax.ShapeDtypeStruct((bsz, s, LANES), _MXU_DTYPE),
        jax.ShapeDtypeStruct((bsz, IDX_HEADS, s), F32),
    ]
    out_specs = [row(pool_w), row(attn_w), row(kv_w),
                 pl.BlockSpec((1, kv_w, tm), lambda b, i: (b, 0, i)),
                 row(qi_w), row(LANES), row(LANES),
                 pl.BlockSpec((1, IDX_HEADS, tm), lambda b, i: (b, 0, i))]
    return pl.pallas_call(
        functools.partial(_proj_body, cuts=cuts),
        grid=(bsz, s // tm),
        in_specs=[pl.BlockSpec((1, tm, d), lambda b, i: (b, i, 0)),
                  pl.BlockSpec((1, 6, d), lambda b, i: (b, 0, 0)),
                  pl.BlockSpec((1, d), lambda b, i: (0, 0)),
                  pl.BlockSpec((d, pw), lambda b, i: (0, 0))],
        out_specs=out_specs,
        out_shape=out_shapes,
        compiler_params=_cparams(("arbitrary", "arbitrary")),
        name="proj",
    )(x, mod, g_mix, wp)


def _bucket_starts():
    max_exact = N_BUCKETS // 2
    d = np.arange(1, 4 * MAX_DISTANCE, dtype=np.float32)
    large = max_exact + (np.log(d / np.float32(max_exact)) / np.float32(math.log(MAX_DISTANCE / max_exact))
                         * np.float32(N_BUCKETS - max_exact)).astype(np.int32)
    large = np.minimum(large, N_BUCKETS - 1)
    bucket = np.where(d < max_exact, d.astype(np.int32), large)
    bucket = np.concatenate([[0], bucket])
    starts = [int(np.argmax(bucket >= b)) for b in range(N_BUCKETS)]
    assert all(np.all(bucket[starts[b]:] >= b) for b in range(N_BUCKETS))
    assert starts[N_BUCKETS - 1] <= MAX_DISTANCE
    return starts


def _bias_body(rb_ref, o_ref, *, t, starts):
    diff = pl.program_id(0)
    s_l = lax.broadcasted_iota(I32, (t, t), 0)
    t_l = lax.broadcasted_iota(I32, (t, t), 1)
    dist = diff * t + t_l - s_l
    for h in range(N_HEADS):
        val = jnp.full((t, t), rb_ref[0, h], F32)
        for b in range(1, N_BUCKETS):
            val = jnp.where(dist >= starts[b], rb_ref[b, h], val)
        hh = h % HEAD_GROUP
        o_ref[0, h // HEAD_GROUP, :, hh * t:(hh + 1) * t] = (val - rb_ref[N_BUCKETS - 1, h]) * LOG2E


def _bias_tiles(rel_bias, t):
    assert t >= MAX_DISTANCE
    return pl.pallas_call(
        functools.partial(_bias_body, t=t, starts=_bucket_starts()),
        grid=(2,),
        in_specs=[pl.BlockSpec(memory_space=pltpu.SMEM)],
        out_specs=pl.BlockSpec((1, N_KV_HEADS, t, HEAD_GROUP * t), lambda i: (i, 0, 0, 0)),
        out_shape=jax.ShapeDtypeStruct((2, N_KV_HEADS, t, HEAD_GROUP * t), F32),
        compiler_params=_cparams(("arbitrary",)),
        name="bias_tiles",
    )(rel_bias)


def _dsa_body(q_ref, qi_ref, wit_ref, kia_ref, kib_ref, k_ref, vt_ref, bias_ref, o_ref,
              score_ref, mask_ref, acc_ref, *, t, topk):
    i = pl.program_id(1)
    nk = i + 1
    t_glob = i * t + lax.broadcasted_iota(I32, (1, t), 1)
    idx_scale = (IDX_DIM ** -0.5) * (IDX_HEADS ** -0.5)

    def score_chunk(c, carry):
        cnt, mn, mx = carry
        r0 = pl.multiple_of(c * SCORE_SUB, SCORE_SUB)
        ka = kia_ref[0, pl.ds(r0, SCORE_SUB), :]
        kb = kib_ref[0, pl.ds(r0, SCORE_SUB), :]
        acc = jnp.zeros((SCORE_SUB, t), F32)
        for j in range(IDX_HEADS // 2):
            qp = qi_ref[0, :, j * LANES:(j + 1) * LANES]
            da = _dot_nt(ka, qp)
            db = _dot_nt(kb, qp)
            acc = acc + jnp.maximum(da, 0.0) * wit_ref[0, 2 * j:2 * j + 1, :]
            acc = acc + jnp.maximum(db, 0.0) * wit_ref[0, 2 * j + 1:2 * j + 2, :]
        sc = acc * idx_scale
        s_glob = r0 + lax.broadcasted_iota(I32, (SCORE_SUB, 1), 0)
        b = jnp.where(s_glob <= t_glob, sc, NEG_INF)
        score_ref[pl.ds(r0, SCORE_SUB), :] = b
        real = b > NEG_INF
        rows8 = lambda a: a.reshape(SCORE_SUB // SUBLANES, SUBLANES, t)
        cnt = cnt + jnp.sum(rows8(real.astype(I32)), axis=0)
        mn = jnp.minimum(mn, jnp.min(rows8(jnp.where(real, b, BIG)), axis=0))
        mx = jnp.maximum(mx, jnp.max(rows8(b), axis=0))
        return cnt, mn, mx

    cnt8, mn8, mx8 = lax.fori_loop(
        0, nk * (t // SCORE_SUB), score_chunk,
        (jnp.zeros((SUBLANES, t), I32), jnp.full((SUBLANES, t), BIG, F32), jnp.full((SUBLANES, t), -BIG, F32)))

    nblk = nk * (t // COUNT_RB)
    fold = lambda m: jnp.sum(m.reshape(COUNT_RB // SUBLANES, SUBLANES, t), axis=0)

    def blk(r):
        return score_ref[pl.ds(pl.multiple_of(r * COUNT_RB, COUNT_RB), COUNT_RB), :]

    def count_ge(c):
        def body(r, acc):
            return acc + fold((blk(r) >= c).astype(I32))
        acc = lax.fori_loop(0, nblk, body, jnp.zeros((SUBLANES, t), I32))
        return jnp.sum(acc, axis=0, keepdims=True)

    n_real = jnp.sum(cnt8, axis=0, keepdims=True)
    rmin = jnp.min(mn8, axis=0, keepdims=True)
    rmax = jnp.max(mx8, axis=0, keepdims=True)
    n_max = count_ge(rmax)
    degenerate = jnp.logical_and(n_real >= topk, n_max >= topk)
    lo0 = jnp.where(n_real < topk, NEG_INF, jnp.where(degenerate, rmax, rmin))
    hi0 = jnp.where(degenerate, BIG, rmax)
    done0 = jnp.where(n_real <= topk, 1, 0)

    def bisect(st):
        lo, hi, done, stuck = st
        c = lo + (hi - lo) * 0.5
        active = (done + stuck) == 0
        has_mid = jnp.logical_and(c > lo, c < hi)
        n = count_ge(c)
        upd = jnp.logical_and(active, has_mid)
        lo = jnp.where(jnp.logical_and(upd, n >= topk), c, lo)
        hi = jnp.where(jnp.logical_and(upd, n < topk), c, hi)
        done = jnp.where(jnp.logical_and(upd, n == topk), 1, done)
        stuck = jnp.where(jnp.logical_and(active, jnp.logical_not(has_mid)), 1, stuck)
        return lo, hi, done, stuck

    st = lax.fori_loop(0, SEL_FIXED_ITERS, lambda _, s: bisect(s), (lo0, hi0, done0, jnp.zeros((1, t), I32)))

    def sel_cond(s):
        return jnp.logical_and(s[0] < SEL_MAX_ITERS, jnp.min(s[3] + s[4]) == 0)

    _, lo, hi, done, _ = lax.while_loop(sel_cond, lambda s: (s[0] + 1,) + bisect(s[1:]), (jnp.int32(0),) + st)

    def write_mask(r, cut):
        r0 = pl.multiple_of(r * COUNT_RB, COUNT_RB)
        b = score_ref[pl.ds(r0, COUNT_RB), :]
        s_glob = r0 + lax.broadcasted_iota(I32, (COUNT_RB, 1), 0)
        sel = jnp.logical_and(b >= lo, s_glob <= t_glob)
        if cut is not None:
            sel = jnp.logical_and(sel, jnp.logical_or(b >= hi, s_glob < cut))
        mask_ref[pl.ds(r0, COUNT_RB), :] = jnp.concatenate([jnp.where(sel, 0.0, NEG_INF)] * HEAD_GROUP, axis=1)

    any_tied = jnp.min(done) == 0

    @pl.when(jnp.logical_not(any_tied))
    def _():
        def body(r, carry):
            write_mask(r, None)
            return carry
        lax.fori_loop(0, nblk, body, 0)

    @pl.when(any_tied)
    def _():
        need = topk - count_ge(hi)

        def count_tie_below(x):
            def body(r, acc):
                r0 = pl.multiple_of(r * COUNT_RB, COUNT_RB)
                b = score_ref[pl.ds(r0, COUNT_RB), :]
                s_glob = r0 + lax.broadcasted_iota(I32, (COUNT_RB, 1), 0)
                m = jnp.logical_and(jnp.logical_and(b >= lo, b < hi), s_glob < x)
                return acc + fold(m.astype(I32))
            acc = lax.fori_loop(0, nblk, body, jnp.zeros((SUBLANES, t), I32))
            return jnp.sum(acc, axis=0, keepdims=True)

        nbits = int(score_ref.shape[0]).bit_length()

        def bit_body(bi, x):
            c = x + jnp.left_shift(jnp.int32(1), nbits - 1 - bi)
            return jnp.where(count_tie_below(c) < need, c, x)

        x = lax.fori_loop(0, nbits, bit_body, jnp.zeros((1, t), I32))
        cut = jnp.where(done > 0, jnp.int32(NO_CUT), x + 1)

        def body(r, carry):
            write_mask(r, cut)
            return carry
        lax.fori_loop(0, nblk, body, 0)

    scale = HEAD_DIM ** -0.5
    gt = HEAD_GROUP * t

    def col_reduce(op, a):
        part = op(a.reshape(a.shape[0] // SUBLANES, SUBLANES, a.shape[1]), axis=0)
        return op(part, axis=0, keepdims=True)

    qgs = [jnp.concatenate(
        [q_ref[0, :, h * HEAD_DIM:(h + 1) * HEAD_DIM] for h in range(g * HEAD_GROUP, (g + 1) * HEAD_GROUP)],
        axis=0) for g in range(N_KV_HEADS)]
    acc_ref[...] = jnp.zeros_like(acc_ref)

    def att_chunk(kc, carry, near):
        r0 = pl.multiple_of(kc * t, t)
        mk = mask_ref[pl.ds(r0, t), :]
        new = []
        for g in range(N_KV_HEADS):
            m, l = carry[g]
            kk = k_ref[0, pl.ds(r0, t), g * HEAD_DIM:(g + 1) * HEAD_DIM]
            s = _dot_nt(kk, qgs[g]) * (scale * LOG2E) + mk
            if near:
                s = s + bias_ref[i - kc, g]
            m_new = jnp.maximum(m, col_reduce(jnp.max, s))
            alpha = jnp.exp2(m - m_new)
            p = jnp.exp2(s - m_new)
            l = alpha * l + col_reduce(jnp.sum, p)
            vt = vt_ref[0, g * HEAD_DIM:(g + 1) * HEAD_DIM, pl.ds(r0, t)]
            acc_ref[g] = alpha * acc_ref[g] + _dot(vt, p.astype(_MXU_DTYPE))
            new.append((m_new, l))
        return tuple(new)

    init = (jnp.full((1, gt), NEG_INF, F32), jnp.zeros((1, gt), F32))
    n_far = jnp.maximum(i - 1, 0)
    far = lax.fori_loop(0, n_far, functools.partial(att_chunk, near=False), (init,) * N_KV_HEADS)
    fin = lax.fori_loop(n_far, nk, functools.partial(att_chunk, near=True), far)
    for g in range(N_KV_HEADS):
        out = acc_ref[g] / fin[g][1]
        for hh in range(HEAD_GROUP):
            h = g * HEAD_GROUP + hh
            o_ref[0, :, h * HEAD_DIM:(h + 1) * HEAD_DIM] = out[:, hh * t:(hh + 1) * t].T.astype(o_ref.dtype)


def _dsa(q, qi, wit, kia, kib, k, vt, bias, topk):
    bsz, s, aw = q.shape
    t = ATT_T
    assert s % t == 0 and topk <= t
    row = lambda w: pl.BlockSpec((1, t, w), lambda b, i: (b, i, 0))
    full = lambda a: pl.BlockSpec((1,) + a.shape[1:], lambda b, i: (b, 0, 0))
    return pl.pallas_call(
        functools.partial(_dsa_body, t=t, topk=topk),
        grid=(bsz, s // t),
        in_specs=[row(aw), row(qi.shape[2]),
                  pl.BlockSpec((1, IDX_HEADS, t), lambda b, i: (b, 0, i)),
                  full(kia), full(kib), full(k), full(vt),
                  pl.BlockSpec(bias.shape, lambda b, i: (0, 0, 0, 0))],
        out_specs=row(aw),
        out_shape=jax.ShapeDtypeStruct((bsz, s, aw), _MXU_DTYPE),
        scratch_shapes=[pltpu.VMEM((s, t), F32), pltpu.VMEM((s, HEAD_GROUP * t), F32),
                        pltpu.VMEM((N_KV_HEADS, HEAD_DIM, HEAD_GROUP * t), F32)],
        compiler_params=_cparams(("arbitrary", "arbitrary")),
        name="dsa",
    )(q, qi, wit, kia, kib, k, vt, bias)


def _pack_halves(hb):
    assert jnp.dtype(hb.dtype).itemsize == 2
    n = hb.shape[1] // 2
    hi = lax.bitcast_convert_type(hb[:, :n].astype(F32), I32)
    lo = lax.bitcast_convert_type(hb[:, n:].astype(F32), I32)
    return jnp.bitwise_or(hi, lax.shift_right_logical(lo, jnp.full(lo.shape, 16, I32)))


def _unpack_halves(w, dtype):
    hi = lax.bitcast_convert_type(jnp.bitwise_and(w, jnp.int32(-65536)), F32).astype(dtype)
    lo = lax.bitcast_convert_type(jnp.left_shift(w, 16), F32).astype(dtype)
    return jnp.concatenate([hi, lo], axis=1)


def _mix_body(u_ref, halo_ref, a_ref, x_ref, mod_ref, wpool_ref, ps_ref, wout_ref, g_ref, wr_ref, br_ref,
              x1_ref, h2_ref, lg_ref, *, seq, tm):
    t0 = (pl.program_id(0) * tm) % seq
    u = u_ref[...]
    halo = jnp.where(t0 == 0, 0.0, halo_ref[...])
    ext = jnp.concatenate([halo, u], axis=0)
    pos = t0 + lax.broadcasted_iota(I32, (tm, 1), 0)
    gw = u.shape[1] // N_POOL_GROUPS
    ys = []
    for g, w in enumerate(POOL_WINDOWS):
        a = ext[:, g * gw:(g + 1) * gw]
        sft = 1
        while sft < w:
            a = a + pltpu.roll(a, sft, 0)
            sft *= 2
        cnt = jnp.minimum(pos + 1, w).astype(F32)
        p = a[HALO:, :] / cnt - u[:, g * gw:(g + 1) * gw]
        ys.append(_dot(p.astype(_MXU_DTYPE), wpool_ref[g]) * ps_ref[:, g * gw:(g + 1) * gw])
    pool = jnp.concatenate(ys, axis=1).astype(_MXU_DTYPE)
    pw = pool.shape[1]
    mixed = _dot(pool, wout_ref[:pw, :]) + _dot(a_ref[...], wout_ref[pw:, :])
    x1 = x_ref[...] + mod_ref[0, 2:3, :] * mixed
    x1_ref[...] = x1
    h2 = _rms_mod(x1, g_ref[...], mod_ref[0, 3:4, :], mod_ref[0, 4:5, :]).astype(_MXU_DTYPE)
    h2_ref[...] = _pack_halves(h2)
    lg_ref[...] = _dot(h2, wr_ref[...]) + br_ref[...]


def _mix(u, attn, x, mod, w_pool, pool_scale, w_out, g_ffn, w_r, b_r, seq):
    n_tok, d = x.shape
    tm = MIX_TM
    assert seq % tm == 0 and tm % HALO == 0
    pw = u.shape[1]
    aw = attn.shape[1]
    rows = lambda w: pl.BlockSpec((tm, w), lambda i: (i, 0))
    const = lambda a: pl.BlockSpec(a.shape, lambda i: (0,) * a.ndim)
    return pl.pallas_call(
        functools.partial(_mix_body, seq=seq, tm=tm),
        grid=(n_tok // tm,),
        in_specs=[rows(pw),
                  pl.BlockSpec((HALO, pw), lambda i: (jnp.maximum(i * (tm // HALO) - 1, 0), 0)),
                  rows(aw), rows(d),
                  pl.BlockSpec((1, 6, d), lambda i: ((i * tm) // seq, 0, 0)),
                  const(w_pool), const(pool_scale), const(w_out), const(g_ffn), const(w_r), const(b_r)],
        out_specs=[rows(d), rows(d // 2), rows(LANES)],
        out_shape=[jax.ShapeDtypeStruct((n_tok, d), F32), jax.ShapeDtypeStruct((n_tok, d // 2), I32),
                   jax.ShapeDtypeStruct((n_tok, LANES), F32)],
        compiler_params=_cparams(("arbitrary",)),
        name="mix",
    )(u, u, attn, x, mod, w_pool, pool_scale, w_out, g_ffn, w_r, b_r)


def _route_body(lg_ref, rt_ref, gate_ref, cnt_ref, run_ref):
    @pl.when(pl.program_id(0) == 0)
    def _():
        run_ref[...] = jnp.zeros_like(run_ref)

    lg = lg_ref[...]
    tr = lg.shape[0]
    lane = lax.broadcasted_iota(I32, lg.shape, 1)
    gmask = lane < N_GROUPS
    gl = jnp.where(gmask, lg, -BIG)
    gmax = jnp.max(gl, axis=-1, keepdims=True)
    gsel = jnp.min(jnp.where(jnp.logical_and(gmask, gl == gmax), lane, LANES), axis=-1, keepdims=True)
    gsum = jnp.sum(jnp.where(gmask, jnp.exp(gl - gmax), 0.0), axis=-1, keepdims=True)
    p_g = 1.0 / gsum
    in_e = jnp.logical_and(lane >= N_GROUPS, lane < N_GROUPS + N_EXPERTS)
    emask = jnp.logical_and(in_e, lax.shift_right_arithmetic(lane - N_GROUPS, EXPERTS_PER_GROUP.bit_length() - 1) == gsel)
    el = jnp.where(emask, lg, -BIG)
    emax = jnp.max(el, axis=-1, keepdims=True)
    ex = jnp.where(emask, jnp.exp(el - emax), 0.0)
    ep = ex / jnp.sum(ex, axis=-1, keepdims=True)
    p1 = jnp.max(jnp.where(emask, ep, -1.0), axis=-1, keepdims=True)
    i1 = jnp.min(jnp.where(jnp.logical_and(emask, ep == p1), lane, LANES), axis=-1, keepdims=True)
    m2 = jnp.logical_and(emask, lane != i1)
    p2 = jnp.max(jnp.where(m2, ep, -1.0), axis=-1, keepdims=True)
    i2 = jnp.min(jnp.where(jnp.logical_and(m2, ep == p2), lane, LANES), axis=-1, keepdims=True)
    den = p1 + p2
    g1 = p_g * p1 / den
    g2 = p_g * p2 / den
    e1 = i1 - N_GROUPS
    e2 = i2 - N_GROUPS

    oh = jnp.logical_or(lane == e1, lane == e2)
    ri = lax.broadcasted_iota(I32, (tr, tr), 0)
    ci = lax.broadcasted_iota(I32, (tr, tr), 1)
    ltri = (ci < ri).astype(_MXU_DTYPE)
    before = _dot(ltri, oh.astype(_MXU_DTYPE)) + run_ref[...]
    r1 = jnp.sum(jnp.where(lane == e1, before, 0.0), axis=-1, keepdims=True)
    r2 = jnp.sum(jnp.where(lane == e2, before, 0.0), axis=-1, keepdims=True)
    run = run_ref[...] + jnp.sum(oh.astype(F32), axis=0, keepdims=True)
    run_ref[...] = run
    cnt_ref[...] = jnp.broadcast_to(run, cnt_ref.shape)

    route = jnp.where(lane == 0, e1.astype(F32),
                      jnp.where(lane == 1, e2.astype(F32),
                                jnp.where(lane == 2, r1, jnp.where(lane == 3, r2, 0.0))))
    rt_ref[...] = route.T[:SUBLANES, :].astype(I32)
    gate_ref[...] = jnp.where(lane == 0, g1, jnp.where(lane == 1, g2, 0.0))


def _route(logits):
    n_tok = logits.shape[0]
    tr = min(ROUTE_TR, n_tok)
    return pl.pallas_call(
        _route_body,
        grid=(n_tok // tr,),
        in_specs=[pl.BlockSpec((tr, LANES), lambda i: (i, 0))],
        out_specs=[pl.BlockSpec((SUBLANES, tr), lambda i: (0, i)),
                   pl.BlockSpec((tr, LANES), lambda i: (i, 0)),
                   pl.BlockSpec((SUBLANES, LANES), lambda i: (0, 0))],
        out_shape=[jax.ShapeDtypeStruct((SUBLANES, n_tok), I32),
                   jax.ShapeDtypeStruct((n_tok, LANES), F32),
                   jax.ShapeDtypeStruct((SUBLANES, LANES), F32)],
        scratch_shapes=[pltpu.VMEM((1, LANES), F32)],
        compiler_params=_cparams(("arbitrary",)),
        name="route",
    )(logits)


def _dest_body(ps_ref, rt_ref, o_ref):
    r = rt_ref[...]
    start = jnp.zeros_like(r)
    for e in range(N_EXPERTS):
        start = jnp.where(r == e, ps_ref[e], start)
    o_ref[...] = start + pltpu.roll(r, SUBLANES - 2, 0)


def _dest(route_t, pstart):
    n_tok = route_t.shape[1]
    tb = min(n_tok, 2048)
    return pl.pallas_call(
        _dest_body,
        grid_spec=pltpu.PrefetchScalarGridSpec(
            num_scalar_prefetch=1,
            grid=(n_tok // tb,),
            in_specs=[pl.BlockSpec((SUBLANES, tb), lambda i, ps: (0, i))],
            out_specs=pl.BlockSpec((SUBLANES, tb), lambda i, ps: (0, i)),
        ),
        out_shape=jax.ShapeDtypeStruct((SUBLANES, n_tok), I32),
        compiler_params=_cparams(("arbitrary",)),
        name="dest",
    )(pstart, route_t)


SC_LANES = 16
SC_WIN = 16
SC_NBUF = 4
SC_CHUNK = 8192


def _sc_dispatch(h2, dest0, dest1, n_slots):
    n_tok, d = h2.shape
    info = plsc.get_sparse_core_info()
    nc, ns = info.num_cores, info.num_subcores
    nw = nc * ns
    assert info.num_lanes == SC_LANES and n_slots % (nw * SC_WIN) == 0 and n_tok % SC_CHUNK == 0
    assert n_slots < 3 * n_tok
    per_w = n_slots // nw
    mesh = plsc.VectorSubcoreMesh(core_axis_name="c", subcore_axis_name="s")

    n_win = per_w // SC_WIN
    assert n_win % SC_NBUF == 0

    @functools.partial(
        pl.kernel, mesh=mesh,
        out_type=jax.ShapeDtypeStruct((n_slots, d), h2.dtype),
        scratch_types=[pltpu.VMEM((per_w,), I32), pltpu.VMEM((SC_CHUNK,), I32),
                       pltpu.VMEM((SC_NBUF, SC_WIN, d), h2.dtype), pltpu.SemaphoreType.DMA((SC_NBUF,)),
                       pltpu.SemaphoreType.DMA((SC_NBUF,))],
        compiler_params=pltpu.CompilerParams(needs_layout_passes=False),
    )
    def k(h_hbm, d0_hbm, d1_hbm, o_hbm, table, chunk, rows, gsems, wsems):
        lane = lax.iota(I32, SC_LANES)
        base = (lax.axis_index("s") * nc + lax.axis_index("c")) * per_w

        @pl.loop(0, per_w, step=SC_LANES)
        def _(s):
            v = base + s + lane
            v = jnp.where(v >= n_tok, v - n_tok, v)
            v = jnp.where(v >= n_tok, v - n_tok, v)
            table[pl.ds(s, SC_LANES)] = v

        for d_hbm in (d0_hbm, d1_hbm):
            @pl.loop(0, n_tok, step=SC_CHUNK)
            def _(t0, d_hbm=d_hbm):
                pltpu.sync_copy(d_hbm.at[pl.ds(t0, SC_CHUNK)], chunk)

                @pl.loop(0, SC_CHUNK, step=SC_LANES)
                def _(j):
                    r = chunk[pl.ds(j, SC_LANES)] - base
                    mine = jnp.logical_and(r >= 0, r < per_w)
                    plsc.store_scatter(table, [jnp.where(mine, r, 0)], t0 + j + lane, mask=mine)

        ahead = SC_NBUF // 2

        def gather(w, b):
            return pltpu.make_async_copy(
                h_hbm.at[table.at[pl.ds(w * SC_WIN, SC_WIN)]], rows.at[b], gsems.at[b])

        def put(w, b):
            return pltpu.make_async_copy(rows.at[b], o_hbm.at[pl.ds(base + w * SC_WIN, SC_WIN)], wsems.at[b])

        for b in range(ahead):
            gather(b, b).start()

        @pl.loop(0, n_win, step=SC_NBUF)
        def _(g):
            for b in range(SC_NBUF):
                w = g + b
                pb = (b + ahead) % SC_NBUF
                gather(w, b).wait()
                put(w, b).start()

                @pl.when(w >= ahead)
                def _():
                    put(w - ahead, pb).wait()

                @pl.when(w + ahead < n_win)
                def _():
                    gather(w + ahead, pb).start()

        for b in range(ahead):
            w = n_win - ahead + b
            put(w, w % SC_NBUF).wait()

    return k(h2, dest0, dest1)


def _experts_body(slot_ref, cast_ref, pre_ref, misc_ref, x_ref, wg_hbm, wu_hbm, wd_hbm, y_ref,
                  sg, su, sd, wgb, wub, wdb, sem):
    blk = pl.program_id(0)
    e0, pre0, n_used = misc_ref[0], misc_ref[1], misc_ref[2]

    def copies(e):
        return (pltpu.make_async_copy(wg_hbm.at[e], sg, sem.at[0]),
                pltpu.make_async_copy(wu_hbm.at[e], su, sem.at[1]),
                pltpu.make_async_copy(wd_hbm.at[e], sd, sem.at[2]))

    def fetch(e):
        for c in copies(e):
            c.start(priority=1)

    def land(slot):
        for c in copies(0):
            c.wait()
        wgb[slot] = sg[...].astype(wgb.dtype)
        wub[slot] = su[...].astype(wub.dtype)
        wdb[slot] = sd[...].astype(wdb.dtype)

    @pl.when(blk == 0)
    def _():
        fetch(e0)
        land(0)

        @pl.when(pre0 >= 0)
        def _():
            fetch(pre0)

    @pl.when(blk < n_used)
    def _():
        slot = slot_ref[blk]
        xb = _unpack_halves(x_ref[...], _MXU_DTYPE)
        a = _silu(_dot(xb, wgb[slot])) * _dot(xb, wub[slot])
        y_ref[...] = _dot(a.astype(_MXU_DTYPE), wdb[slot])

        @pl.when(cast_ref[blk] == 1)
        def _():
            land(1 - slot)

            @pl.when(pre_ref[blk] >= 0)
            def _():
                fetch(pre_ref[blk])

    @pl.when(blk >= n_used)
    def _():
        y_ref[...] = jnp.zeros_like(y_ref)


def _experts(xs, n_used, pends, w_gate, w_up, w_down):
    n_slots, dpk = xs.shape
    bm = MOE_BM
    n_blocks = n_slots // bm
    d, de = w_gate.shape[1:]
    last_used = n_used[0] - 1
    total = pends[-1]
    eid = jnp.arange(N_EXPERTS, dtype=I32)[None, :]
    owner = lambda start: jnp.minimum(jnp.sum((pends[None, :] <= start[:, None]).astype(I32), axis=1), N_EXPERTS - 1)
    end_of = lambda e: jnp.sum(jnp.where(eid == e[:, None], pends[None, :], 0), axis=1)
    be = owner(jnp.minimum(jnp.arange(n_blocks, dtype=I32), last_used) * bm)
    end_e = end_of(be)
    has_next = end_e < total
    nxt = owner(end_e)
    end_n = end_of(nxt)
    nxt2 = jnp.where(jnp.logical_and(has_next, end_n < total), owner(end_n), -1)
    is_last = jnp.concatenate([be[1:] != be[:-1], jnp.ones((1,), jnp.bool_)])
    is_last = jnp.logical_or(is_last, jnp.arange(n_blocks) >= last_used)
    cast = jnp.logical_and(is_last, has_next).astype(I32)
    pre = jnp.where(cast == 1, nxt2, -1).astype(I32)
    is_first = jnp.concatenate([jnp.ones((1,), jnp.bool_), be[1:] != be[:-1]])
    slot = ((jnp.cumsum(is_first.astype(I32)) - 1) % 2).astype(I32)
    misc = jnp.stack([be[0], jnp.where(has_next[0], nxt[0], -1), n_used[0]]).astype(I32)
    hbm = pl.BlockSpec(memory_space=pl.ANY)
    return pl.pallas_call(
        _experts_body,
        grid_spec=pltpu.PrefetchScalarGridSpec(
            num_scalar_prefetch=4,
            grid=(n_blocks,),
            in_specs=[pl.BlockSpec((bm, dpk), lambda j, sl, ca, pr, mi: (jnp.minimum(j, mi[2] - 1), 0)),
                      hbm, hbm, hbm],
            out_specs=pl.BlockSpec((bm, d), lambda j, sl, ca, pr, mi: (j, 0)),
            scratch_shapes=[pltpu.VMEM((d, de), w_gate.dtype), pltpu.VMEM((d, de), w_up.dtype),
                            pltpu.VMEM((de, d), w_down.dtype),
                            pltpu.VMEM((2, d, de), _MXU_DTYPE), pltpu.VMEM((2, d, de), _MXU_DTYPE),
                            pltpu.VMEM((2, de, d), _MXU_DTYPE), pltpu.SemaphoreType.DMA((3,))],
        ),
        out_shape=jax.ShapeDtypeStruct((n_slots, d), F32),
        compiler_params=_cparams(("arbitrary",)),
        name="experts",
    )(slot, cast, pre, misc, xs, w_gate, w_up, w_down)


def _combine_body(x1_ref, gate_ref, d0_ref, d1_ref, n0_ref, n1_ref, mod_ref, g_ref, ys_ref, o_ref, buf_ref, sem,
                  *, tf):
    i = pl.program_id(0)
    cur = i % 2

    def gather(da_ref, db_ref, half):
        def issue(c, carry):
            for u in range(SUBLANES):
                r = c * SUBLANES + u
                for j, d_ref in enumerate((da_ref, db_ref)):
                    pltpu.make_async_copy(ys_ref.at[pl.ds(d_ref[r], 1), :],
                                          buf_ref.at[half, j, c, pl.ds(u, 1), :], sem.at[half]).start(priority=j)
            return carry

        lax.fori_loop(0, tf // SUBLANES, issue, 0)

    @pl.when(i == 0)
    def _():
        gather(d0_ref, d1_ref, 0)

    @pl.when(i + 1 < pl.num_programs(0))
    def _():
        gather(n0_ref, n1_ref, 1 - cur)

    for _ in range(2):
        pltpu.make_async_copy(ys_ref.at[pl.ds(0, tf), :], ys_ref.at[pl.ds(0, tf), :], sem.at[cur]).wait()

    gates = gate_ref[...]
    d = o_ref.shape[1]
    moe = buf_ref[cur, 0].reshape(tf, d) * gates[:, 0:1] + buf_ref[cur, 1].reshape(tf, d) * gates[:, 1:2]
    x2 = x1_ref[...] + mod_ref[0, 5:6, :] * moe
    ms = jnp.mean(x2 * x2, axis=-1, keepdims=True)
    o_ref[...] = x2 * lax.rsqrt(ms + EPS) * g_ref[...]


def _combine(x1, gates, dest0, dest1, mod, g_final, ys, seq):
    n_tok, d = x1.shape
    tf = COMB_TF
    n_tiles = n_tok // tf
    tok = pl.BlockSpec((tf,), lambda i: (i,), memory_space=pltpu.SMEM)
    nxt = pl.BlockSpec((tf,), lambda i: (jnp.minimum(i + 1, n_tiles - 1),), memory_space=pltpu.SMEM)
    return pl.pallas_call(
        functools.partial(_combine_body, tf=tf),
        grid=(n_tiles,),
        in_specs=[pl.BlockSpec((tf, d), lambda i: (i, 0)),
                  pl.BlockSpec((tf, LANES), lambda i: (i, 0)),
                  tok, tok, nxt, nxt,
                  pl.BlockSpec((1, 6, d), lambda i: ((i * tf) // seq, 0, 0)),
                  pl.BlockSpec((1, d), lambda i: (0, 0)),
                  pl.BlockSpec(memory_space=pl.ANY)],
        out_specs=pl.BlockSpec((tf, d), lambda i: (i, 0)),
        scratch_shapes=[pltpu.VMEM((2, 2, tf // SUBLANES, SUBLANES, d), F32), pltpu.SemaphoreType.DMA((2,))],
        out_shape=jax.ShapeDtypeStruct((n_tok, d), F32),
        compiler_params=_cparams(("arbitrary",)),
        name="combine",
    )(x1, gates, dest0, dest1, dest0, dest1, mod, g_final, ys)


def kernel(x, c, w_ada, b_ada, g_mix, w_in, w_pool, pool_scale, rel_bias, w_out, g_ffn, w_group, b_group,
           w_router, b_router, w_gate, w_up, w_down, g_final):
    bsz, seq, d = x.shape
    n_tok = bsz * seq
    depth = w_ada.shape[0]
    assert depth == 1, "the final norm is fused into the only layer's combine"
    topk = min(TOPK_MAX, seq // 4)
    bias = _bias_tiles(rel_bias, ATT_T)
    bm = MOE_BM
    n_blocks = -(-(n_tok * 2) // bm) + N_EXPERTS
    n_slots = n_blocks * bm
    xt = x.reshape(n_tok, d)
    for i in range(depth):
        mod = _ada(c, w_ada[i], b_ada[i][None, :]).reshape(bsz, 6, d)
        u, q, k, vt, qi, kia, kib, wit = _proj(xt.reshape(bsz, seq, d), mod, g_mix[i][None, :], w_in[i])
        attn = _dsa(q, qi, wit, kia, kib, k, vt, bias, topk)
        w_r = jnp.concatenate(
            [w_group[i], w_router[i], jnp.zeros((d, LANES - N_GROUPS - N_EXPERTS), F32)], axis=1).astype(_MXU_DTYPE)
        b_r = jnp.concatenate(
            [b_group[i], b_router[i], jnp.zeros((LANES - N_GROUPS - N_EXPERTS,), F32)])[None, :]
        x1, h2, logits = _mix(u.reshape(n_tok, -1), attn.reshape(n_tok, -1), xt, mod,
                              w_pool[i].astype(_MXU_DTYPE), pool_scale[i][None, :],
                              w_out[i].astype(_MXU_DTYPE), g_ffn[i][None, :], w_r, b_r, seq)
        route_t, gates, cnt = _route(logits)
        counts = cnt[0, :N_EXPERTS].astype(I32)
        padded = (counts + bm - 1) // bm * bm
        pends = jnp.cumsum(padded)
        pstart = pends - padded
        n_used = (pends[-1:] // bm).astype(I32)
        dest = _dest(route_t, pstart)
        dest0, dest1 = dest[0], dest[1]
        xs = _sc_dispatch(h2, dest0, dest1, n_slots)
        ys = _experts(xs, n_used, pends, w_gate[i], w_up[i], w_down[i])
        xt = _combine(x1, gates, dest0, dest1, mod, g_final[None, :], ys, seq)
    return xt.reshape(bsz, seq, d)
```

```python
import functools
import math

import numpy as np
import jax
import jax.numpy as jnp
from jax import lax
from jax.experimental import pallas as pl
from jax.experimental.pallas import tpu as pltpu
from jax.experimental.pallas import tpu_sc as plsc

F32 = jnp.float32
I32 = jnp.int32
_MXU_DTYPE = jnp.bfloat16

POOL_WINDOWS = (2, 4, 8, 16)
N_POOL_GROUPS = 4
HEAD_DIM = 128
N_HEADS = 8
N_KV_HEADS = 2
HEAD_GROUP = N_HEADS // N_KV_HEADS
IDX_HEADS = 16
IDX_DIM = 64
TOPK_MAX = 256
N_BUCKETS = 32
MAX_DISTANCE = 128
N_GROUPS = 4
EXPERTS_PER_GROUP = 8
N_EXPERTS = N_GROUPS * EXPERTS_PER_GROUP
EPS = 1e-6
NEG_INF = -1e30
BIG = 3e38
LOG2E = 1.4426950408889634

LANES = 128
SUBLANES = 8
VMEM_LIMIT_BYTES = 56 * 1024 * 1024

ADA_TN = 1024
PROJ_TM = 256
ATT_T = 256
SCORE_SUB = 256
COUNT_RB = 256
MIX_TM = 256
HALO = 16
ROUTE_TR = 1024
MOE_BM = 256
COMB_TF = 512
SEL_FIXED_ITERS = 20
SEL_MAX_ITERS = 320
NO_CUT = 2 ** 30


def _cparams(sem):
    return pltpu.CompilerParams(dimension_semantics=sem, vmem_limit_bytes=VMEM_LIMIT_BYTES)


def _silu(x):
    return x * (1.0 / (1.0 + jnp.exp(-x)))


def _dot(a, b):
    return jnp.dot(a, b, preferred_element_type=F32)


def _dot_nt(a, b):
    return lax.dot_general(a, b, (((1,), (1,)), ((), ())), preferred_element_type=F32)


def _ada_body(c_ref, w_ref, b_ref, o_ref):
    s = _silu(c_ref[...])
    o_ref[...] = _dot(s.astype(_MXU_DTYPE), w_ref[...].astype(_MXU_DTYPE)) + b_ref[...]


def _ada(c, w, b):
    bsz, d = c.shape
    n = w.shape[1]
    return pl.pallas_call(
        _ada_body,
        grid=(n // ADA_TN,),
        in_specs=[pl.BlockSpec((bsz, d), lambda j: (0, 0)),
                  pl.BlockSpec((d, ADA_TN), lambda j: (0, j)),
                  pl.BlockSpec((1, ADA_TN), lambda j: (0, j))],
        out_specs=pl.BlockSpec((bsz, ADA_TN), lambda j: (0, j)),
        out_shape=jax.ShapeDtypeStruct((bsz, n), F32),
        compiler_params=_cparams(("arbitrary",)),
        name="ada",
    )(c, w, b)


def _rms_mod(x, g, shift, scale):
    ms = jnp.mean(x * x, axis=-1, keepdims=True)
    y = x * lax.rsqrt(ms + EPS) * g
    return y * (1.0 + scale) + shift


def _proj_body(x_ref, mod_ref, g_ref, w_ref, u_ref, q_ref, k_ref, vt_ref, qi_ref, kia_ref, kib_ref,
               wit_ref, *, cuts):
    h = _rms_mod(x_ref[0], g_ref[...], mod_ref[0, 0:1, :], mod_ref[0, 1:2, :])
    hb = h.astype(_MXU_DTYPE)

    def seg(name):
        lo, hi = cuts[name]
        return _dot(hb, w_ref[:, lo:hi])

    u_ref[0] = seg("u")
    q_ref[0] = seg("q").astype(q_ref.dtype)
    k_ref[0] = seg("k").astype(k_ref.dtype)
    vt_ref[0] = seg("v").T.astype(vt_ref.dtype)
    qi_ref[0] = seg("qi").astype(qi_ref.dtype)
    tail = seg("tail")
    tail = jnp.concatenate([tail, jnp.zeros((tail.shape[0], LANES - tail.shape[1]), F32)], axis=1)
    swapped = pltpu.roll(tail, LANES - IDX_DIM, 1)
    lane = lax.broadcasted_iota(I32, tail.shape, 1)
    kia_ref[0] = jnp.where(lane < IDX_DIM, tail, 0.0).astype(kia_ref.dtype)
    kib_ref[0] = jnp.where(lane >= LANES - IDX_DIM, swapped, 0.0).astype(kib_ref.dtype)
    wit_ref[0] = swapped.T[:IDX_HEADS, :]


def _proj(x, mod, g_mix, w_in):
    bsz, s, d = x.shape
    pool_w = d // 2
    attn_w = N_HEADS * HEAD_DIM
    kv_w = N_KV_HEADS * HEAD_DIM
    qi_w = IDX_HEADS * IDX_DIM
    c = np.cumsum([0, pool_w, attn_w, kv_w, kv_w, qi_w, IDX_DIM, IDX_HEADS])
    assert IDX_DIM + IDX_HEADS <= LANES and w_in.shape[1] == c[7]
    wp = w_in.astype(_MXU_DTYPE)
    cuts = {"u": (int(c[0]), int(c[1])), "q": (int(c[1]), int(c[2])), "k": (int(c[2]), int(c[3])),
            "v": (int(c[3]), int(c[4])), "qi": (int(c[4]), int(c[5])), "tail": (int(c[5]), int(c[7]))}
    tm = PROJ_TM
    pw = wp.shape[1]
    row = lambda w: pl.BlockSpec((1, tm, w), lambda b, i: (b, i, 0))
    out_shapes = [
        jax.ShapeDtypeStruct((bsz, s, pool_w), F32),
        jax.ShapeDtypeStruct((bsz, s, attn_w), _MXU_DTYPE),
        jax.ShapeDtypeStruct((bsz, s, kv_w), _MXU_DTYPE),
        jax.ShapeDtypeStruct((bsz, kv_w, s), _MXU_DTYPE),
        jax.ShapeDtypeStruct((bsz, s, qi_w), _MXU_DTYPE),
        jax.ShapeDtypeStruct((bsz, s, LANES), _MXU_DTYPE),
        jax.ShapeDtypeStruct((bsz, s, LANES), _MXU_DTYPE),
        jax.ShapeDtypeStruct((bsz, IDX_HEADS, s), F32),
    ]
    out_specs = [row(pool_w), row(attn_w), row(kv_w),
                 pl.BlockSpec((1, kv_w, tm), lambda b, i: (b, 0, i)),
                 row(qi_w), row(LANES), row(LANES),
                 pl.BlockSpec((1, IDX_HEADS, tm), lambda b, i: (b, 0, i))]
    return pl.pallas_call(
        functools.partial(_proj_body, cuts=cuts),
        grid=(bsz, s // tm),
        in_specs=[pl.BlockSpec((1, tm, d), lambda b, i: (b, i, 0)),
                  pl.BlockSpec((1, 6, d), lambda b, i: (b, 0, 0)),
                  pl.BlockSpec((1, d), lambda b, i: (0, 0)),
                  pl.BlockSpec((d, pw), lambda b, i: (0, 0))],
        out_specs=out_specs,
        out_shape=out_shapes,
        compiler_params=_cparams(("arbitrary", "arbitrary")),
        name="proj",
    )(x, mod, g_mix, wp)


def _bucket_starts():
    max_exact = N_BUCKETS // 2
    d = np.arange(1, 4 * MAX_DISTANCE, dtype=np.float32)
    large = max_exact + (np.log(d / np.float32(max_exact)) / np.float32(math.log(MAX_DISTANCE / max_exact))
                         * np.float32(N_BUCKETS - max_exact)).astype(np.int32)
    large = np.minimum(large, N_BUCKETS - 1)
    bucket = np.where(d < max_exact, d.astype(np.int32), large)
    bucket = np.concatenate([[0], bucket])
    starts = [int(np.argmax(bucket >= b)) for b in range(N_BUCKETS)]
    assert all(np.all(bucket[starts[b]:] >= b) for b in range(N_BUCKETS))
    assert starts[N_BUCKETS - 1] <= MAX_DISTANCE
    return starts


def _bias_body(rb_ref, o_ref, *, t, starts):
    diff = pl.program_id(0)
    s_l = lax.broadcasted_iota(I32, (t, t), 0)
    t_l = lax.broadcasted_iota(I32, (t, t), 1)
    dist = diff * t + t_l - s_l
    for h in range(N_HEADS):
        val = jnp.full((t, t), rb_ref[0, h], F32)
        for b in range(1, N_BUCKETS):
            val = jnp.where(dist >= starts[b], rb_ref[b, h], val)
        hh = h % HEAD_GROUP
        o_ref[0, h // HEAD_GROUP, :, hh * t:(hh + 1) * t] = (val - rb_ref[N_BUCKETS - 1, h]) * LOG2E


def _bias_tiles(rel_bias, t):
    assert t >= MAX_DISTANCE
    return pl.pallas_call(
        functools.partial(_bias_body, t=t, starts=_bucket_starts()),
        grid=(2,),
        in_specs=[pl.BlockSpec(memory_space=pltpu.SMEM)],
        out_specs=pl.BlockSpec((1, N_KV_HEADS, t, HEAD_GROUP * t), lambda i: (i, 0, 0, 0)),
        out_shape=jax.ShapeDtypeStruct((2, N_KV_HEADS, t, HEAD_GROUP * t), F32),
        compiler_params=_cparams(("arbitrary",)),
        name="bias_tiles",
    )(rel_bias)


def _dsa_body(q_ref, qi_ref, wit_ref, kia_ref, kib_ref, k_ref, vt_ref, bias_ref, o_ref,
              score_ref, mask_ref, acc_ref, *, t, topk):
    i = pl.program_id(1)
    nk = i + 1
    t_glob = i * t + lax.broadcasted_iota(I32, (1, t), 1)
    idx_scale = (IDX_DIM ** -0.5) * (IDX_HEADS ** -0.5)

    def score_chunk(c, carry):
        cnt, mn, mx = carry
        r0 = pl.multiple_of(c * SCORE_SUB, SCORE_SUB)
        ka = kia_ref[0, pl.ds(r0, SCORE_SUB), :]
        kb = kib_ref[0, pl.ds(r0, SCORE_SUB), :]
        acc = jnp.zeros((SCORE_SUB, t), F32)
        for j in range(IDX_HEADS // 2):
            qp = qi_ref[0, :, j * LANES:(j + 1) * LANES]
            da = _dot_nt(ka, qp)
            db = _dot_nt(kb, qp)
            acc = acc + jnp.maximum(da, 0.0) * wit_ref[0, 2 * j:2 * j + 1, :]
            acc = acc + jnp.maximum(db, 0.0) * wit_ref[0, 2 * j + 1:2 * j + 2, :]
        sc = acc * idx_scale
        s_glob = r0 + lax.broadcasted_iota(I32, (SCORE_SUB, 1), 0)
        b = jnp.where(s_glob <= t_glob, sc, NEG_INF)
        score_ref[pl.ds(r0, SCORE_SUB), :] = b
        real = b > NEG_INF
        rows8 = lambda a: a.reshape(SCORE_SUB // SUBLANES, SUBLANES, t)
        cnt = cnt + jnp.sum(rows8(real.astype(I32)), axis=0)
        mn = jnp.minimum(mn, jnp.min(rows8(jnp.where(real, b, BIG)), axis=0))
        mx = jnp.maximum(mx, jnp.max(rows8(b), axis=0))
        return cnt, mn, mx

    cnt8, mn8, mx8 = lax.fori_loop(
        0, nk * (t // SCORE_SUB), score_chunk,
        (jnp.zeros((SUBLANES, t), I32), jnp.full((SUBLANES, t), BIG, F32), jnp.full((SUBLANES, t), -BIG, F32)))

    nblk = nk * (t // COUNT_RB)
    fold = lambda m: jnp.sum(m.reshape(COUNT_RB // SUBLANES, SUBLANES, t), axis=0)

    def blk(r):
        return score_ref[pl.ds(pl.multiple_of(r * COUNT_RB, COUNT_RB), COUNT_RB), :]

    def count_ge(c):
        def body(r, acc):
            return acc + fold((blk(r) >= c).astype(I32))
        acc = lax.fori_loop(0, nblk, body, jnp.zeros((SUBLANES, t), I32))
        return jnp.sum(acc, axis=0, keepdims=True)

    n_real = jnp.sum(cnt8, axis=0, keepdims=True)
    rmin = jnp.min(mn8, axis=0, keepdims=True)
    rmax = jnp.max(mx8, axis=0, keepdims=True)
    n_max = count_ge(rmax)
    degenerate = jnp.logical_and(n_real >= topk, n_max >= topk)
    lo0 = jnp.where(n_real < topk, NEG_INF, jnp.where(degenerate, rmax, rmin))
    hi0 = jnp.where(degenerate, BIG, rmax)
    done0 = jnp.where(n_real <= topk, 1, 0)

    def bisect(st):
        lo, hi, done, stuck = st
        c = lo + (hi - lo) * 0.5
        active = (done + stuck) == 0
        has_mid = jnp.logical_and(c > lo, c < hi)
        n = count_ge(c)
        upd = jnp.logical_and(active, has_mid)
        lo = jnp.where(jnp.logical_and(upd, n >= topk), c, lo)
        hi = jnp.where(jnp.logical_and(upd, n < topk), c, hi)
        done = jnp.where(jnp.logical_and(upd, n == topk), 1, done)
        stuck = jnp.where(jnp.logical_and(active, jnp.logical_not(has_mid)), 1, stuck)
        return lo, hi, done, stuck

    st = lax.fori_loop(0, SEL_FIXED_ITERS, lambda _, s: bisect(s), (lo0, hi0, done0, jnp.zeros((1, t), I32)))

    def sel_cond(s):
        return jnp.logical_and(s[0] < SEL_MAX_ITERS, jnp.min(s[3] + s[4]) == 0)

    _, lo, hi, done, _ = lax.while_loop(sel_cond, lambda s: (s[0] + 1,) + bisect(s[1:]), (jnp.int32(0),) + st)

    def write_mask(r, cut):
        r0 = pl.multiple_of(r * COUNT_RB, COUNT_RB)
        b = score_ref[pl.ds(r0, COUNT_RB), :]
        s_glob = r0 + lax.broadcasted_iota(I32, (COUNT_RB, 1), 0)
        sel = jnp.logical_and(b >= lo, s_glob <= t_glob)
        if cut is not None:
            sel = jnp.logical_and(sel, jnp.logical_or(b >= hi, s_glob < cut))
        mask_ref[pl.ds(r0, COUNT_RB), :] = jnp.concatenate([jnp.where(sel, 0.0, NEG_INF)] * HEAD_GROUP, axis=1)

    any_tied = jnp.min(done) == 0

    @pl.when(jnp.logical_not(any_tied))
    def _():
        def body(r, carry):
            write_mask(r, None)
            return carry
        lax.fori_loop(0, nblk, body, 0)

    @pl.when(any_tied)
    def _():
        need = topk - count_ge(hi)

        def count_tie_below(x):
            def body(r, acc):
                r0 = pl.multiple_of(r * COUNT_RB, COUNT_RB)
                b = score_ref[pl.ds(r0, COUNT_RB), :]
                s_glob = r0 + lax.broadcasted_iota(I32, (COUNT_RB, 1), 0)
                m = jnp.logical_and(jnp.logical_and(b >= lo, b < hi), s_glob < x)
                return acc + fold(m.astype(I32))
            acc = lax.fori_loop(0, nblk, body, jnp.zeros((SUBLANES, t), I32))
            return jnp.sum(acc, axis=0, keepdims=True)

        nbits = int(score_ref.shape[0]).bit_length()

        def bit_body(bi, x):
            c = x + jnp.left_shift(jnp.int32(1), nbits - 1 - bi)
            return jnp.where(count_tie_below(c) < need, c, x)

        x = lax.fori_loop(0, nbits, bit_body, jnp.zeros((1, t), I32))
        cut = jnp.where(done > 0, jnp.int32(NO_CUT), x + 1)

        def body(r, carry):
            write_mask(r, cut)
            return carry
        lax.fori_loop(0, nblk, body, 0)

    scale = HEAD_DIM ** -0.5
    gt = HEAD_GROUP * t

    def col_reduce(op, a):
        part = op(a.reshape(a.shape[0] // SUBLANES, SUBLANES, a.shape[1]), axis=0)
        return op(part, axis=0, keepdims=True)

    qgs = [jnp.concatenate(
        [q_ref[0, :, h * HEAD_DIM:(h + 1) * HEAD_DIM] for h in range(g * HEAD_GROUP, (g + 1) * HEAD_GROUP)],
        axis=0) for g in range(N_KV_HEADS)]
    acc_ref[...] = jnp.zeros_like(acc_ref)

    def att_chunk(kc, carry, near):
        r0 = pl.multiple_of(kc * t, t)
        mk = mask_ref[pl.ds(r0, t), :]
        new = []
        for g in range(N_KV_HEADS):
            m, l = carry[g]
            kk = k_ref[0, pl.ds(r0, t), g * HEAD_DIM:(g + 1) * HEAD_DIM]
            s = _dot_nt(kk, qgs[g]) * (scale * LOG2E) + mk
            if near:
                s = s + bias_ref[i - kc, g]
            m_new = jnp.maximum(m, col_reduce(jnp.max, s))
            alpha = jnp.exp2(m - m_new)
            p = jnp.exp2(s - m_new)
            l = alpha * l + col_reduce(jnp.sum, p)
            vt = vt_ref[0, g * HEAD_DIM:(g + 1) * HEAD_DIM, pl.ds(r0, t)]
            acc_ref[g] = alpha * acc_ref[g] + _dot(vt, p.astype(_MXU_DTYPE))
            new.append((m_new, l))
        return tuple(new)

    init = (jnp.full((1, gt), NEG_INF, F32), jnp.zeros((1, gt), F32))
    n_far = jnp.maximum(i - 1, 0)
    far = lax.fori_loop(0, n_far, functools.partial(att_chunk, near=False), (init,) * N_KV_HEADS)
    fin = lax.fori_loop(n_far, nk, functools.partial(att_chunk, near=True), far)
    for g in range(N_KV_HEADS):
        out = acc_ref[g] / fin[g][1]
        for hh in range(HEAD_GROUP):
            h = g * HEAD_GROUP + hh
            o_ref[0, :, h * HEAD_DIM:(h + 1) * HEAD_DIM] = out[:, hh * t:(hh + 1) * t].T.astype(o_ref.dtype)


def _dsa(q, qi, wit, kia, kib, k, vt, bias, topk):
    bsz, s, aw = q.shape
    t = ATT_T
    assert s % t == 0 and topk <= t
    row = lambda w: pl.BlockSpec((1, t, w), lambda b, i: (b, i, 0))
    full = lambda a: pl.BlockSpec((1,) + a.shape[1:], lambda b, i: (b, 0, 0))
    return pl.pallas_call(
        functools.partial(_dsa_body, t=t, topk=topk),
        grid=(bsz, s // t),
        in_specs=[row(aw), row(qi.shape[2]),
                  pl.BlockSpec((1, IDX_HEADS, t), lambda b, i: (b, 0, i)),
                  full(kia), full(kib), full(k), full(vt),
                  pl.BlockSpec(bias.shape, lambda b, i: (0, 0, 0, 0))],
        out_specs=row(aw),
        out_shape=jax.ShapeDtypeStruct((bsz, s, aw), _MXU_DTYPE),
        scratch_shapes=[pltpu.VMEM((s, t), F32), pltpu.VMEM((s, HEAD_GROUP * t), F32),
                        pltpu.VMEM((N_KV_HEADS, HEAD_DIM, HEAD_GROUP * t), F32)],
        compiler_params=_cparams(("arbitrary", "arbitrary")),
        name="dsa",
    )(q, qi, wit, kia, kib, k, vt, bias)


def _pack_halves(hb):
    assert jnp.dtype(hb.dtype).itemsize == 2
    n = hb.shape[1] // 2
    hi = lax.bitcast_convert_type(hb[:, :n].astype(F32), I32)
    lo = lax.bitcast_convert_type(hb[:, n:].astype(F32), I32)
    return jnp.bitwise_or(hi, lax.shift_right_logical(lo, jnp.full(lo.shape, 16, I32)))


def _unpack_halves(w, dtype):
    hi = lax.bitcast_convert_type(jnp.bitwise_and(w, jnp.int32(-65536)), F32).astype(dtype)
    lo = lax.bitcast_convert_type(jnp.left_shift(w, 16), F32).astype(dtype)
    return jnp.concatenate([hi, lo], axis=1)


def _mix_body(u_ref, halo_ref, a_ref, x_ref, mod_ref, wpool_ref, ps_ref, wout_ref, g_ref, wr_ref, br_ref,
              x1_ref, h2_ref, lg_ref, *, seq, tm):
    t0 = (pl.program_id(0) * tm) % seq
    u = u_ref[...]
    halo = jnp.where(t0 == 0, 0.0, halo_ref[...])
    ext = jnp.concatenate([halo, u], axis=0)
    pos = t0 + lax.broadcasted_iota(I32, (tm, 1), 0)
    gw = u.shape[1] // N_POOL_GROUPS
    ys = []
    for g, w in enumerate(POOL_WINDOWS):
        a = ext[:, g * gw:(g + 1) * gw]
        sft = 1
        while sft < w:
            a = a + pltpu.roll(a, sft, 0)
            sft *= 2
        cnt = jnp.minimum(pos + 1, w).astype(F32)
        p = a[HALO:, :] / cnt - u[:, g * gw:(g + 1) * gw]
        ys.append(_dot(p.astype(_MXU_DTYPE), wpool_ref[g]) * ps_ref[:, g * gw:(g + 1) * gw])
    pool = jnp.concatenate(ys, axis=1).astype(_MXU_DTYPE)
    pw = pool.shape[1]
    mixed = _dot(pool, wout_ref[:pw, :]) + _dot(a_ref[...], wout_ref[pw:, :])
    x1 = x_ref[...] + mod_ref[0, 2:3, :] * mixed
    x1_ref[...] = x1
    h2 = _rms_mod(x1, g_ref[...], mod_ref[0, 3:4, :], mod_ref[0, 4:5, :]).astype(_MXU_DTYPE)
    h2_ref[...] = _pack_halves(h2)
    lg_ref[...] = _dot(h2, wr_ref[...]) + br_ref[...]


def _mix(u, attn, x, mod, w_pool, pool_scale, w_out, g_ffn, w_r, b_r, seq):
    n_tok, d = x.shape
    tm = MIX_TM
    assert seq % tm == 0 and tm % HALO == 0
    pw = u.shape[1]
    aw = attn.shape[1]
    rows = lambda w: pl.BlockSpec((tm, w), lambda i: (i, 0))
    const = lambda a: pl.BlockSpec(a.shape, lambda i: (0,) * a.ndim)
    return pl.pallas_call(
        functools.partial(_mix_body, seq=seq, tm=tm),
        grid=(n_tok // tm,),
        in_specs=[rows(pw),
                  pl.BlockSpec((HALO, pw), lambda i: (jnp.maximum(i * (tm // HALO) - 1, 0), 0)),
                  rows(aw), rows(d),
                  pl.BlockSpec((1, 6, d), lambda i: ((i * tm) // seq, 0, 0)),
                  const(w_pool), const(pool_scale), const(w_out), const(g_ffn), const(w_r), const(b_r)],
        out_specs=[rows(d), rows(d // 2), rows(LANES)],
        out_shape=[jax.ShapeDtypeStruct((n_tok, d), F32), jax.ShapeDtypeStruct((n_tok, d // 2), I32),
                   jax.ShapeDtypeStruct((n_tok, LANES), F32)],
        compiler_params=_cparams(("arbitrary",)),
        name="mix",
    )(u, u, attn, x, mod, w_pool, pool_scale, w_out, g_ffn, w_r, b_r)


def _route_body(lg_ref, rt_ref, gate_ref, cnt_ref, run_ref):
    @pl.when(pl.program_id(0) == 0)
    def _():
        run_ref[...] = jnp.zeros_like(run_ref)

    lg = lg_ref[...]
    tr = lg.shape[0]
    lane = lax.broadcasted_iota(I32, lg.shape, 1)
    gmask = lane < N_GROUPS
    gl = jnp.where(gmask, lg, -BIG)
    gmax = jnp.max(gl, axis=-1, keepdims=True)
    gsel = jnp.min(jnp.where(jnp.logical_and(gmask, gl == gmax), lane, LANES), axis=-1, keepdims=True)
    gsum = jnp.sum(jnp.where(gmask, jnp.exp(gl - gmax), 0.0), axis=-1, keepdims=True)
    p_g = 1.0 / gsum
    in_e = jnp.logical_and(lane >= N_GROUPS, lane < N_GROUPS + N_EXPERTS)
    emask = jnp.logical_and(in_e, lax.shift_right_arithmetic(lane - N_GROUPS, EXPERTS_PER_GROUP.bit_length() - 1) == gsel)
    el = jnp.where(emask, lg, -BIG)
    emax = jnp.max(el, axis=-1, keepdims=True)
    ex = jnp.where(emask, jnp.exp(el - emax), 0.0)
    ep = ex / jnp.sum(ex, axis=-1, keepdims=True)
    p1 = jnp.max(jnp.where(emask, ep, -1.0), axis=-1, keepdims=True)
    i1 = jnp.min(jnp.where(jnp.logical_and(emask, ep == p1), lane, LANES), axis=-1, keepdims=True)
    m2 = jnp.logical_and(emask, lane != i1)
    p2 = jnp.max(jnp.where(m2, ep, -1.0), axis=-1, keepdims=True)
    i2 = jnp.min(jnp.where(jnp.logical_and(m2, ep == p2), lane, LANES), axis=-1, keepdims=True)
    den = p1 + p2
    g1 = p_g * p1 / den
    g2 = p_g * p2 / den
    e1 = i1 - N_GROUPS
    e2 = i2 - N_GROUPS

    oh = jnp.logical_or(lane == e1, lane == e2)
    ri = lax.broadcasted_iota(I32, (tr, tr), 0)
    ci = lax.broadcasted_iota(I32, (tr, tr), 1)
    ltri = (ci < ri).astype(_MXU_DTYPE)
    before = _dot(ltri, oh.astype(_MXU_DTYPE)) + run_ref[...]
    r1 = jnp.sum(jnp.where(lane == e1, before, 0.0), axis=-1, keepdims=True)
    r2 = jnp.sum(jnp.where(lane == e2, before, 0.0), axis=-1, keepdims=True)
    run = run_ref[...] + jnp.sum(oh.astype(F32), axis=0, keepdims=True)
    run_ref[...] = run
    cnt_ref[...] = jnp.broadcast_to(run, cnt_ref.shape)

    route = jnp.where(lane == 0, e1.astype(F32),
                      jnp.where(lane == 1, e2.astype(F32),
                                jnp.where(lane == 2, r1, jnp.where(lane == 3, r2, 0.0))))
    rt_ref[...] = route.T[:SUBLANES, :].astype(I32)
    gate_ref[...] = jnp.where(lane == 0, g1, jnp.where(lane == 1, g2, 0.0))


def _route(logits):
    n_tok = logits.shape[0]
    tr = min(ROUTE_TR, n_tok)
    return pl.pallas_call(
        _route_body,
        grid=(n_tok // tr,),
        in_specs=[pl.BlockSpec((tr, LANES), lambda i: (i, 0))],
        out_specs=[pl.BlockSpec((SUBLANES, tr), lambda i: (0, i)),
                   pl.BlockSpec((tr, LANES), lambda i: (i, 0)),
                   pl.BlockSpec((SUBLANES, LANES), lambda i: (0, 0))],
        out_shape=[jax.ShapeDtypeStruct((SUBLANES, n_tok), I32),
                   jax.ShapeDtypeStruct((n_tok, LANES), F32),
                   jax.ShapeDtypeStruct((SUBLANES, LANES), F32)],
        scratch_shapes=[pltpu.VMEM((1, LANES), F32)],
        compiler_params=_cparams(("arbitrary",)),
        name="route",
    )(logits)


def _dest_body(ps_ref, rt_ref, o_ref):
    r = rt_ref[...]
    start = jnp.zeros_like(r)
    for e in range(N_EXPERTS):
        start = jnp.where(r == e, ps_ref[e], start)
    o_ref[...] = start + pltpu.roll(r, SUBLANES - 2, 0)


def _dest(route_t, pstart):
    n_tok = route_t.shape[1]
    tb = min(n_tok, 2048)
    return pl.pallas_call(
        _dest_body,
        grid_spec=pltpu.PrefetchScalarGridSpec(
            num_scalar_prefetch=1,
            grid=(n_tok // tb,),
            in_specs=[pl.BlockSpec((SUBLANES, tb), lambda i, ps: (0, i))],
            out_specs=pl.BlockSpec((SUBLANES, tb), lambda i, ps: (0, i)),
        ),
        out_shape=jax.ShapeDtypeStruct((SUBLANES, n_tok), I32),
        compiler_params=_cparams(("arbitrary",)),
        name="dest",
    )(pstart, route_t)


SC_LANES = 16
SC_WIN = 8
SC_NBUF = 8
SC_CHUNK = 8192


def _sc_dispatch(h2, dest0, dest1, n_slots):
    n_tok, d = h2.shape
    info = plsc.get_sparse_core_info()
    nc, ns = info.num_cores, info.num_subcores
    nw = nc * ns
    assert info.num_lanes == SC_LANES and n_slots % (nw * SC_WIN) == 0 and n_tok % SC_CHUNK == 0
    assert n_slots < 3 * n_tok
    per_w = n_slots // nw
    mesh = plsc.VectorSubcoreMesh(core_axis_name="c", subcore_axis_name="s")

    n_win = per_w // SC_WIN
    assert n_win % SC_NBUF == 0

    @functools.partial(
        pl.kernel, mesh=mesh,
        out_type=jax.ShapeDtypeStruct((n_slots, d), h2.dtype),
        scratch_types=[pltpu.VMEM((per_w,), I32), pltpu.VMEM((SC_CHUNK,), I32),
                       pltpu.VMEM((SC_NBUF, SC_WIN, d), h2.dtype), pltpu.SemaphoreType.DMA((SC_NBUF,)),
                       pltpu.SemaphoreType.DMA((SC_NBUF,))],
        compiler_params=pltpu.CompilerParams(needs_layout_passes=False),
    )
    def k(h_hbm, d0_hbm, d1_hbm, o_hbm, table, chunk, rows, gsems, wsems):
        lane = lax.iota(I32, SC_LANES)
        base = (lax.axis_index("s") * nc + lax.axis_index("c")) * per_w

        @pl.loop(0, per_w, step=SC_LANES)
        def _(s):
            v = base + s + lane
            v = jnp.where(v >= n_tok, v - n_tok, v)
            v = jnp.where(v >= n_tok, v - n_tok, v)
            table[pl.ds(s, SC_LANES)] = v

        for d_hbm in (d0_hbm, d1_hbm):
            @pl.loop(0, n_tok, step=SC_CHUNK)
            def _(t0, d_hbm=d_hbm):
                pltpu.sync_copy(d_hbm.at[pl.ds(t0, SC_CHUNK)], chunk)

                @pl.loop(0, SC_CHUNK, step=SC_LANES)
                def _(j):
                    r = chunk[pl.ds(j, SC_LANES)] - base
                    mine = jnp.logical_and(r >= 0, r < per_w)
                    plsc.store_scatter(table, [jnp.where(mine, r, 0)], t0 + j + lane, mask=mine)

        ahead = SC_NBUF // 2

        def gather(w, b):
            return pltpu.make_async_copy(
                h_hbm.at[table.at[pl.ds(w * SC_WIN, SC_WIN)]], rows.at[b], gsems.at[b])

        def put(w, b):
            return pltpu.make_async_copy(rows.at[b], o_hbm.at[pl.ds(base + w * SC_WIN, SC_WIN)], wsems.at[b])

        for b in range(ahead):
            gather(b, b).start()

        @pl.loop(0, n_win, step=SC_NBUF)
        def _(g):
            for b in range(SC_NBUF):
                w = g + b
                pb = (b + ahead) % SC_NBUF
                gather(w, b).wait()
                put(w, b).start()

                @pl.when(w >= ahead)
                def _():
                    put(w - ahead, pb).wait()

                @pl.when(w + ahead < n_win)
                def _():
                    gather(w + ahead, pb).start()

        for b in range(ahead):
            w = n_win - ahead + b
            put(w, w % SC_NBUF).wait()

    return k(h2, dest0, dest1)


def _experts_body(slot_ref, cast_ref, pre_ref, misc_ref, x_ref, wg_hbm, wu_hbm, wd_hbm, y_ref,
                  sg, su, sd, wgb, wub, wdb, sem):
    blk = pl.program_id(0)
    e0, pre0, n_used = misc_ref[0], misc_ref[1], misc_ref[2]

    def copies(e):
        return (pltpu.make_async_copy(wg_hbm.at[e], sg, sem.at[0]),
                pltpu.make_async_copy(wu_hbm.at[e], su, sem.at[1]),
                pltpu.make_async_copy(wd_hbm.at[e], sd, sem.at[2]))

    def fetch(e):
        for c in copies(e):
            c.start(priority=1)

    def land(slot):
        for c in copies(0):
            c.wait()
        wgb[slot] = sg[...].astype(wgb.dtype)
        wub[slot] = su[...].astype(wub.dtype)
        wdb[slot] = sd[...].astype(wdb.dtype)

    @pl.when(blk == 0)
    def _():
        fetch(e0)
        land(0)

        @pl.when(pre0 >= 0)
        def _():
            fetch(pre0)

    @pl.when(blk < n_used)
    def _():
        slot = slot_ref[blk]
        xb = _unpack_halves(x_ref[...], _MXU_DTYPE)
        a = _silu(_dot(xb, wgb[slot])) * _dot(xb, wub[slot])
        y_ref[...] = _dot(a.astype(_MXU_DTYPE), wdb[slot])

        @pl.when(cast_ref[blk] == 1)
        def _():
            land(1 - slot)

            @pl.when(pre_ref[blk] >= 0)
            def _():
                fetch(pre_ref[blk])

    @pl.when(blk >= n_used)
    def _():
        y_ref[...] = jnp.zeros_like(y_ref)


def _experts(xs, n_used, pends, w_gate, w_up, w_down):
    n_slots, dpk = xs.shape
    bm = MOE_BM
    n_blocks = n_slots // bm
    d, de = w_gate.shape[1:]
    last_used = n_used[0] - 1
    total = pends[-1]
    eid = jnp.arange(N_EXPERTS, dtype=I32)[None, :]
    owner = lambda start: jnp.minimum(jnp.sum((pends[None, :] <= start[:, None]).astype(I32), axis=1), N_EXPERTS - 1)
    end_of = lambda e: jnp.sum(jnp.where(eid == e[:, None], pends[None, :], 0), axis=1)
    be = owner(jnp.minimum(jnp.arange(n_blocks, dtype=I32), last_used) * bm)
    end_e = end_of(be)
    has_next = end_e < total
    nxt = owner(end_e)
    end_n = end_of(nxt)
    nxt2 = jnp.where(jnp.logical_and(has_next, end_n < total), owner(end_n), -1)
    is_last = jnp.concatenate([be[1:] != be[:-1], jnp.ones((1,), jnp.bool_)])
    is_last = jnp.logical_or(is_last, jnp.arange(n_blocks) >= last_used)
    cast = jnp.logical_and(is_last, has_next).astype(I32)
    pre = jnp.where(cast == 1, nxt2, -1).astype(I32)
    is_first = jnp.concatenate([jnp.ones((1,), jnp.bool_), be[1:] != be[:-1]])
    slot = ((jnp.cumsum(is_first.astype(I32)) - 1) % 2).astype(I32)
    misc = jnp.stack([be[0], jnp.where(has_next[0], nxt[0], -1), n_used[0]]).astype(I32)
    hbm = pl.BlockSpec(memory_space=pl.ANY)
    return pl.pallas_call(
        _experts_body,
        grid_spec=pltpu.PrefetchScalarGridSpec(
            num_scalar_prefetch=4,
            grid=(n_blocks,),
            in_specs=[pl.BlockSpec((bm, dpk), lambda j, sl, ca, pr, mi: (jnp.minimum(j, mi[2] - 1), 0)),
                      hbm, hbm, hbm],
            out_specs=pl.BlockSpec((bm, d), lambda j, sl, ca, pr, mi: (j, 0)),
            scratch_shapes=[pltpu.VMEM((d, de), w_gate.dtype), pltpu.VMEM((d, de), w_up.dtype),
                            pltpu.VMEM((de, d), w_down.dtype),
                            pltpu.VMEM((2, d, de), _MXU_DTYPE), pltpu.VMEM((2, d, de), _MXU_DTYPE),
                            pltpu.VMEM((2, de, d), _MXU_DTYPE), pltpu.SemaphoreType.DMA((3,))],
        ),
        out_shape=jax.ShapeDtypeStruct((n_slots, d), F32),
        compiler_params=_cparams(("arbitrary",)),
        name="experts",
    )(slot, cast, pre, misc, xs, w_gate, w_up, w_down)


def _combine_body(x1_ref, gate_ref, d0_ref, d1_ref, n0_ref, n1_ref, mod_ref, g_ref, ys_ref, o_ref, buf_ref, sem,
                  *, tf):
    i = pl.program_id(0)
    cur = i % 2

    def gather(da_ref, db_ref, half):
        def issue(c, carry):
            for u in range(SUBLANES):
                r = c * SUBLANES + u
                for j, d_ref in enumerate((da_ref, db_ref)):
                    pltpu.make_async_copy(ys_ref.at[pl.ds(d_ref[r], 1), :],
                                          buf_ref.at[half, j, c, pl.ds(u, 1), :], sem.at[half]).start(priority=j)
            return carry

        lax.fori_loop(0, tf // SUBLANES, issue, 0)

    @pl.when(i == 0)
    def _():
        gather(d0_ref, d1_ref, 0)

    @pl.when(i + 1 < pl.num_programs(0))
    def _():
        gather(n0_ref, n1_ref, 1 - cur)

    for _ in range(2):
        pltpu.make_async_copy(ys_ref.at[pl.ds(0, tf), :], ys_ref.at[pl.ds(0, tf), :], sem.at[cur]).wait()

    gates = gate_ref[...]
    d = o_ref.shape[1]
    moe = buf_ref[cur, 0].reshape(tf, d) * gates[:, 0:1] + buf_ref[cur, 1].reshape(tf, d) * gates[:, 1:2]
    x2 = x1_ref[...] + mod_ref[0, 5:6, :] * moe
    ms = jnp.mean(x2 * x2, axis=-1, keepdims=True)
    o_ref[...] = x2 * lax.rsqrt(ms + EPS) * g_ref[...]


def _combine(x1, gates, dest0, dest1, mod, g_final, ys, seq):
    n_tok, d = x1.shape
    tf = COMB_TF
    n_tiles = n_tok // tf
    tok = pl.BlockSpec((tf,), lambda i: (i,), memory_space=pltpu.SMEM)
    nxt = pl.BlockSpec((tf,), lambda i: (jnp.minimum(i + 1, n_tiles - 1),), memory_space=pltpu.SMEM)
    return pl.pallas_call(
        functools.partial(_combine_body, tf=tf),
        grid=(n_tiles,),
        in_specs=[pl.BlockSpec((tf, d), lambda i: (i, 0)),
                  pl.BlockSpec((tf, LANES), lambda i: (i, 0)),
                  tok, tok, nxt, nxt,
                  pl.BlockSpec((1, 6, d), lambda i: ((i * tf) // seq, 0, 0)),
                  pl.BlockSpec((1, d), lambda i: (0, 0)),
                  pl.BlockSpec(memory_space=pl.ANY)],
        out_specs=pl.BlockSpec((tf, d), lambda i: (i, 0)),
        scratch_shapes=[pltpu.VMEM((2, 2, tf // SUBLANES, SUBLANES, d), F32), pltpu.SemaphoreType.DMA((2,))],
        out_shape=jax.ShapeDtypeStruct((n_tok, d), F32),
        compiler_params=_cparams(("arbitrary",)),
        name="combine",
    )(x1, gates, dest0, dest1, dest0, dest1, mod, g_final, ys)


def kernel(x, c, w_ada, b_ada, g_mix, w_in, w_pool, pool_scale, rel_bias, w_out, g_ffn, w_group, b_group,
           w_router, b_router, w_gate, w_up, w_down, g_final):
    bsz, seq, d = x.shape
    n_tok = bsz * seq
    depth = w_ada.shape[0]
    assert depth == 1, "the final norm is fused into the only layer's combine"
    topk = min(TOPK_MAX, seq // 4)
    bias = _bias_tiles(rel_bias, ATT_T)
    bm = MOE_BM
    n_blocks = -(-(n_tok * 2) // bm) + N_EXPERTS
    n_slots = n_blocks * bm
    xt = x.reshape(n_tok, d)
    for i in range(depth):
        mod = _ada(c, w_ada[i], b_ada[i][None, :]).reshape(bsz, 6, d)
        u, q, k, vt, qi, kia, kib, wit = _proj(xt.reshape(bsz, seq, d), mod, g_mix[i][None, :], w_in[i])
        attn = _dsa(q, qi, wit, kia, kib, k, vt, bias, topk)
        w_r = jnp.concatenate(
            [w_group[i], w_router[i], jnp.zeros((d, LANES - N_GROUPS - N_EXPERTS), F32)], axis=1).astype(_MXU_DTYPE)
        b_r = jnp.concatenate(
            [b_group[i], b_router[i], jnp.zeros((LANES - N_GROUPS - N_EXPERTS,), F32)])[None, :]
        x1, h2, logits = _mix(u.reshape(n_tok, -1), attn.reshape(n_tok, -1), xt, mod,
                              w_pool[i].astype(_MXU_DTYPE), pool_scale[i][None, :],
                              w_out[i].astype(_MXU_DTYPE), g_ffn[i][None, :], w_r, b_r, seq)
        route_t, gates, cnt = _route(logits)
        counts = cnt[0, :N_EXPERTS].astype(I32)
        padded = (counts + bm - 1) // bm * bm
        pends = jnp.cumsum(padded)
        pstart = pends - padded
        n_used = (pends[-1:] // bm).astype(I32)
        dest = _dest(route_t, pstart)
        dest0, dest1 = dest[0], dest[1]
        xs = _sc_dispatch(h2, dest0, dest1, n_slots)
        ys = _experts(xs, n_used, pends, w_gate[i], w_up[i], w_down[i])
        xt = _combine(x1, gates, dest0, dest1, mod, g_final[None, :], ys, seq)
    return xt.reshape(bsz, seq, d)
```

```python
import functools
import math

import numpy as np
import jax
import jax.numpy as jnp
from jax import lax
from jax.experimental import pallas as pl
from jax.experimental.pallas import tpu as pltpu
from jax.experimental.pallas import tpu_sc as plsc

F32 = jnp.float32
I32 = jnp.int32
_MXU_DTYPE = jnp.bfloat16

POOL_WINDOWS = (2, 4, 8, 16)
N_POOL_GROUPS = 4
HEAD_DIM = 128
N_HEADS = 8
N_KV_HEADS = 2
HEAD_GROUP = N_HEADS // N_KV_HEADS
IDX_HEADS = 16
IDX_DIM = 64
TOPK_MAX = 256
N_BUCKETS = 32
MAX_DISTANCE = 128
N_GROUPS = 4
EXPERTS_PER_GROUP = 8
N_EXPERTS = N_GROUPS * EXPERTS_PER_GROUP
EPS = 1e-6
NEG_INF = -1e30
BIG = 3e38
LOG2E = 1.4426950408889634
F32_ULP_UP = 2.0 ** -20
F32_TINY = 1.2e-38

LANES = 128
SUBLANES = 8
VMEM_LIMIT_BYTES = 56 * 1024 * 1024

ADA_TN = 1024
PROJ_TM = 256
ATT_T = 256
SCORE_SUB = 256
COUNT_RB = 256
MIX_TM = 256
HALO = 16
ROUTE_TR = 1024
MOE_BM = 256
COMB_TF = 256
SEL_FIXED_ITERS = 20
SEL_MAX_ITERS = 320
NO_CUT = 2 ** 30


def _cparams(sem):
    return pltpu.CompilerParams(dimension_semantics=sem, vmem_limit_bytes=VMEM_LIMIT_BYTES)


def _silu(x):
    return x * (1.0 / (1.0 + jnp.exp(-x)))


def _dot(a, b):
    return jnp.dot(a, b, preferred_element_type=F32)


def _dot_nt(a, b):
    return lax.dot_general(a, b, (((1,), (1,)), ((), ())), preferred_element_type=F32)


def _ada_body(c_ref, w_ref, b_ref, o_ref):
    s = _silu(c_ref[...])
    o_ref[...] = _dot(s.astype(_MXU_DTYPE), w_ref[...].astype(_MXU_DTYPE)) + b_ref[...]


def _ada(c, w, b):
    bsz, d = c.shape
    n = w.shape[1]
    return pl.pallas_call(
        _ada_body,
        grid=(n // ADA_TN,),
        in_specs=[pl.BlockSpec((bsz, d), lambda j: (0, 0)),
                  pl.BlockSpec((d, ADA_TN), lambda j: (0, j)),
                  pl.BlockSpec((1, ADA_TN), lambda j: (0, j))],
        out_specs=pl.BlockSpec((bsz, ADA_TN), lambda j: (0, j)),
        out_shape=jax.ShapeDtypeStruct((bsz, n), F32),
        compiler_params=_cparams(("arbitrary",)),
        name="ada",
    )(c, w, b)


def _rms_mod(x, g, shift, scale):
    ms = jnp.mean(x * x, axis=-1, keepdims=True)
    y = x * lax.rsqrt(ms + EPS) * g
    return y * (1.0 + scale) + shift


def _proj_body(x_ref, mod_ref, g_ref, w_ref, u_ref, q_ref, k_ref, vt_ref, qi_ref, kia_ref, kib_ref,
               wit_ref, *, cuts):
    h = _rms_mod(x_ref[0], g_ref[...], mod_ref[0, 0:1, :], mod_ref[0, 1:2, :])
    hb = h.astype(_MXU_DTYPE)

    def seg(name):
        lo, hi = cuts[name]
        return _dot(hb, w_ref[:, lo:hi])

    u_ref[0] = seg("u")
    q_ref[0] = seg("q").astype(q_ref.dtype)
    k_ref[0] = seg("k").astype(k_ref.dtype)
    vt_ref[0] = seg("v").T.astype(vt_ref.dtype)
    qi_ref[0] = seg("qi").astype(qi_ref.dtype)
    tail = seg("tail")
    tail = jnp.concatenate([tail, jnp.zeros((tail.shape[0], LANES - tail.shape[1]), F32)], axis=1)
    swapped = pltpu.roll(tail, LANES - IDX_DIM, 1)
    lane = lax.broadcasted_iota(I32, tail.shape, 1)
    kia_ref[0] = jnp.where(lane < IDX_DIM, tail, 0.0).astype(kia_ref.dtype)
    kib_ref[0] = jnp.where(lane >= LANES - IDX_DIM, swapped, 0.0).astype(kib_ref.dtype)
    wit_ref[0] = swapped.T[:IDX_HEADS, :]


def _proj(x, mod, g_mix, w_in):
    bsz, s, d = x.shape
    pool_w = d // 2
    attn_w = N_HEADS * HEAD_DIM
    kv_w = N_KV_HEADS * HEAD_DIM
    qi_w = IDX_HEADS * IDX_DIM
    c = np.cumsum([0, pool_w, attn_w, kv_w, kv_w, qi_w, IDX_DIM, IDX_HEADS])
    assert IDX_DIM + IDX_HEADS <= LANES and w_in.shape[1] == c[7]
    wp = w_in.astype(_MXU_DTYPE)
    cuts = {"u": (int(c[0]), int(c[1])), "q": (int(c[1]), int(c[2])), "k": (int(c[2]), int(c[3])),
            "v": (int(c[3]), int(c[4])), "qi": (int(c[4]), int(c[5])), "tail": (int(c[5]), int(c[7]))}
    tm = PROJ_TM
    pw = wp.shape[1]
    row = lambda w: pl.BlockSpec((1, tm, w), lambda b, i: (b, i, 0))
    out_shapes = [
        jax.ShapeDtypeStruct((bsz, s, pool_w), F32),
        jax.ShapeDtypeStruct((bsz, s, attn_w), _MXU_DTYPE),
        jax.ShapeDtypeStruct((bsz, s, kv_w), _MXU_DTYPE),
        jax.ShapeDtypeStruct((bsz, kv_w, s), _MXU_DTYPE),
        jax.ShapeDtypeStruct((bsz, s, qi_w), _MXU_DTYPE),
        jax.ShapeDtypeStruct((bsz, s, LANES), _MXU_DTYPE),
        jax.ShapeDtypeStruct((bsz, s, LANES), _MXU_DTYPE),
        jax.ShapeDtypeStruct((bsz, IDX_HEADS, s), F32),
    ]
    out_specs = [row(pool_w), row(attn_w), row(kv_w),
                 pl.BlockSpec((1, kv_w, tm), lambda b, i: (b, 0, i)),
                 row(qi_w), row(LANES), row(LANES),
                 pl.BlockSpec((1, IDX_HEADS, tm), lambda b, i: (b, 0, i))]
    return pl.pallas_call(
        functools.partial(_proj_body, cuts=cuts),
        grid=(bsz, s // tm),
        in_specs=[pl.BlockSpec((1, tm, d), lambda b, i: (b, i, 0)),
                  pl.BlockSpec((1, 6, d), lambda b, i: (b, 0, 0)),
                  pl.BlockSpec((1, d), lambda b, i: (0, 0)),
                  pl.BlockSpec((d, pw), lambda b, i: (0, 0))],
        out_specs=out_specs,
        out_shape=out_shapes,
        compiler_params=_cparams(("arbitrary", "arbitrary")),
        name="proj",
    )(x, mod, g_mix, wp)


def _bucket_starts():
    max_exact = N_BUCKETS // 2
    d = np.arange(1, 4 * MAX_DISTANCE, dtype=np.float32)
    large = max_exact + (np.log(d / np.float32(max_exact)) / np.float32(math.log(MAX_DISTANCE / max_exact))
                         * np.float32(N_BUCKETS - max_exact)).astype(np.int32)
    large = np.minimum(large, N_BUCKETS - 1)
    bucket = np.where(d < max_exact, d.astype(np.int32), large)
    bucket = np.concatenate([[0], bucket])
    starts = [int(np.argmax(bucket >= b)) for b in range(N_BUCKETS)]
    assert all(np.all(bucket[starts[b]:] >= b) for b in range(N_BUCKETS))
    assert starts[N_BUCKETS - 1] <= MAX_DISTANCE
    return starts


def _bias_body(rb_ref, o_ref, *, t, starts):
    diff = pl.program_id(0)
    s_l = lax.broadcasted_iota(I32, (t, t), 0)
    t_l = lax.broadcasted_iota(I32, (t, t), 1)
    dist = diff * t + t_l - s_l
    for h in range(N_HEADS):
        val = jnp.full((t, t), rb_ref[0, h], F32)
        for b in range(1, N_BUCKETS):
            val = jnp.where(dist >= starts[b], rb_ref[b, h], val)
        hh = h % HEAD_GROUP
        o_ref[0, h // HEAD_GROUP, :, hh * t:(hh + 1) * t] = (val - rb_ref[N_BUCKETS - 1, h]) * LOG2E


def _bias_tiles(rel_bias, t):
    assert t >= MAX_DISTANCE
    return pl.pallas_call(
        functools.partial(_bias_body, t=t, starts=_bucket_starts()),
        grid=(2,),
        in_specs=[pl.BlockSpec(memory_space=pltpu.SMEM)],
        out_specs=pl.BlockSpec((1, N_KV_HEADS, t, HEAD_GROUP * t), lambda i: (i, 0, 0, 0)),
        out_shape=jax.ShapeDtypeStruct((2, N_KV_HEADS, t, HEAD_GROUP * t), F32),
        compiler_params=_cparams(("arbitrary",)),
        name="bias_tiles",
    )(rel_bias)


def _dsa_body(q_ref, qi_ref, wit_ref, kia_ref, kib_ref, k_ref, vt_ref, bias_ref, o_ref,
              score_ref, mask_ref, acc_ref, *, t, topk):
    i = pl.program_id(1)
    nk = i + 1
    t_glob = i * t + lax.broadcasted_iota(I32, (1, t), 1)
    idx_scale = (IDX_DIM ** -0.5) * (IDX_HEADS ** -0.5)

    def score_chunk(c, carry):
        cnt, mn, mx = carry
        r0 = pl.multiple_of(c * SCORE_SUB, SCORE_SUB)
        ka = kia_ref[0, pl.ds(r0, SCORE_SUB), :]
        kb = kib_ref[0, pl.ds(r0, SCORE_SUB), :]
        acc = jnp.zeros((SCORE_SUB, t), F32)
        for j in range(IDX_HEADS // 2):
            qp = qi_ref[0, :, j * LANES:(j + 1) * LANES]
            da = _dot_nt(ka, qp)
            db = _dot_nt(kb, qp)
            acc = acc + jnp.maximum(da, 0.0) * wit_ref[0, 2 * j:2 * j + 1, :]
            acc = acc + jnp.maximum(db, 0.0) * wit_ref[0, 2 * j + 1:2 * j + 2, :]
        sc = acc * idx_scale
        s_glob = r0 + lax.broadcasted_iota(I32, (SCORE_SUB, 1), 0)
        b = jnp.where(s_glob <= t_glob, sc, NEG_INF)
        score_ref[pl.ds(r0, SCORE_SUB), :] = b
        real = b > NEG_INF
        rows8 = lambda a: a.reshape(SCORE_SUB // SUBLANES, SUBLANES, t)
        cnt = cnt + jnp.sum(rows8(real.astype(I32)), axis=0)
        mn = jnp.minimum(mn, jnp.min(rows8(jnp.where(real, b, BIG)), axis=0))
        mx = jnp.maximum(mx, jnp.max(rows8(b), axis=0))
        return cnt, mn, mx

    cnt8, mn8, mx8 = lax.fori_loop(
        0, nk * (t // SCORE_SUB), score_chunk,
        (jnp.zeros((SUBLANES, t), I32), jnp.full((SUBLANES, t), BIG, F32), jnp.full((SUBLANES, t), -BIG, F32)))

    nblk = nk * (t // COUNT_RB)
    fold = lambda m: jnp.sum(m.reshape(COUNT_RB // SUBLANES, SUBLANES, t), axis=0)

    def blk(r):
        return score_ref[pl.ds(pl.multiple_of(r * COUNT_RB, COUNT_RB), COUNT_RB), :]

    def count_ge(c):
        def body(r, acc):
            return acc + fold((blk(r) >= c).astype(I32))
        acc = lax.fori_loop(0, nblk, body, jnp.zeros((SUBLANES, t), I32))
        return jnp.sum(acc, axis=0, keepdims=True)

    n_real = jnp.sum(cnt8, axis=0, keepdims=True)
    rmin = jnp.min(mn8, axis=0, keepdims=True)
    rmax = jnp.max(mx8, axis=0, keepdims=True)
    lo0 = jnp.where(n_real < topk, NEG_INF, rmin)
    hi0 = rmax + jnp.abs(rmax) * F32_ULP_UP + F32_TINY
    done0 = jnp.where(n_real <= topk, 1, 0)

    def bisect(st):
        lo, hi, done, stuck = st
        c = lo + (hi - lo) * 0.5
        active = (done + stuck) == 0
        has_mid = jnp.logical_and(c > lo, c < hi)
        n = count_ge(c)
        upd = jnp.logical_and(active, has_mid)
        lo = jnp.where(jnp.logical_and(upd, n >= topk), c, lo)
        hi = jnp.where(jnp.logical_and(upd, n < topk), c, hi)
        done = jnp.where(jnp.logical_and(upd, n == topk), 1, done)
        stuck = jnp.where(jnp.logical_and(active, jnp.logical_not(has_mid)), 1, stuck)
        return lo, hi, done, stuck

    st = lax.fori_loop(0, SEL_FIXED_ITERS, lambda _, s: bisect(s), (lo0, hi0, done0, jnp.zeros((1, t), I32)))

    def sel_cond(s):
        return jnp.logical_and(s[0] < SEL_MAX_ITERS, jnp.min(s[3] + s[4]) == 0)

    _, lo, hi, done, _ = lax.while_loop(sel_cond, lambda s: (s[0] + 1,) + bisect(s[1:]), (jnp.int32(0),) + st)

    def write_mask(r, cut):
        r0 = pl.multiple_of(r * COUNT_RB, COUNT_RB)
        b = score_ref[pl.ds(r0, COUNT_RB), :]
        s_glob = r0 + lax.broadcasted_iota(I32, (COUNT_RB, 1), 0)
        sel = jnp.logical_and(b >= lo, s_glob <= t_glob)
        if cut is not None:
            sel = jnp.logical_and(sel, jnp.logical_or(b >= hi, s_glob < cut))
        mask_ref[pl.ds(r0, COUNT_RB), :] = jnp.concatenate([jnp.where(sel, 0.0, NEG_INF)] * HEAD_GROUP, axis=1)

    any_tied = jnp.min(done) == 0

    @pl.when(jnp.logical_not(any_tied))
    def _():
        def body(r, carry):
            write_mask(r, None)
            return carry
        lax.fori_loop(0, nblk, body, 0)

    @pl.when(any_tied)
    def _():
        need = topk - count_ge(hi)

        def count_tie_below(x):
            def body(r, acc):
                r0 = pl.multiple_of(r * COUNT_RB, COUNT_RB)
                b = score_ref[pl.ds(r0, COUNT_RB), :]
                s_glob = r0 + lax.broadcasted_iota(I32, (COUNT_RB, 1), 0)
                m = jnp.logical_and(jnp.logical_and(b >= lo, b < hi), s_glob < x)
                return acc + fold(m.astype(I32))
            acc = lax.fori_loop(0, nblk, body, jnp.zeros((SUBLANES, t), I32))
            return jnp.sum(acc, axis=0, keepdims=True)

        nbits = int(score_ref.shape[0]).bit_length()

        def bit_body(bi, x):
            c = x + jnp.left_shift(jnp.int32(1), nbits - 1 - bi)
            return jnp.where(count_tie_below(c) < need, c, x)

        x = lax.fori_loop(0, nbits, bit_body, jnp.zeros((1, t), I32))
        cut = jnp.where(done > 0, jnp.int32(NO_CUT), x + 1)

        def body(r, carry):
            write_mask(r, cut)
            return carry
        lax.fori_loop(0, nblk, body, 0)

    scale = HEAD_DIM ** -0.5
    gt = HEAD_GROUP * t

    def col_reduce(op, a):
        part = op(a.reshape(a.shape[0] // SUBLANES, SUBLANES, a.shape[1]), axis=0)
        return op(part, axis=0, keepdims=True)

    qgs = [jnp.concatenate(
        [q_ref[0, :, h * HEAD_DIM:(h + 1) * HEAD_DIM] for h in range(g * HEAD_GROUP, (g + 1) * HEAD_GROUP)],
        axis=0) for g in range(N_KV_HEADS)]
    acc_ref[...] = jnp.zeros_like(acc_ref)

    def att_chunk(kc, carry, near):
        r0 = pl.multiple_of(kc * t, t)
        mk = mask_ref[pl.ds(r0, t), :]
        new = []
        for g in range(N_KV_HEADS):
            m, l = carry[g]
            kk = k_ref[0, pl.ds(r0, t), g * HEAD_DIM:(g + 1) * HEAD_DIM]
            s = _dot_nt(kk, qgs[g]) * (scale * LOG2E) + mk
            if near:
                s = s + bias_ref[i - kc, g]
            m_new = jnp.maximum(m, col_reduce(jnp.max, s))
            alpha = jnp.exp2(m - m_new)
            p = jnp.exp2(s - m_new)
            l = alpha * l + col_reduce(jnp.sum, p)
            vt = vt_ref[0, g * HEAD_DIM:(g + 1) * HEAD_DIM, pl.ds(r0, t)]
            acc_ref[g] = alpha * acc_ref[g] + _dot(vt, p.astype(_MXU_DTYPE))
            new.append((m_new, l))
        return tuple(new)

    init = (jnp.full((1, gt), NEG_INF, F32), jnp.zeros((1, gt), F32))
    n_far = jnp.maximum(i - 1, 0)
    far = lax.fori_loop(0, n_far, functools.partial(att_chunk, near=False), (init,) * N_KV_HEADS)
    fin = lax.fori_loop(n_far, nk, functools.partial(att_chunk, near=True), far)
    for g in range(N_KV_HEADS):
        out = acc_ref[g] / fin[g][1]
        for hh in range(HEAD_GROUP):
            h = g * HEAD_GROUP + hh
            o_ref[0, :, h * HEAD_DIM:(h + 1) * HEAD_DIM] = out[:, hh * t:(hh + 1) * t].T.astype(o_ref.dtype)


def _dsa(q, qi, wit, kia, kib, k, vt, bias, topk):
    bsz, s, aw = q.shape
    t = ATT_T
    assert s % t == 0 and topk <= t
    row = lambda w: pl.BlockSpec((1, t, w), lambda b, i: (b, i, 0))
    full = lambda a: pl.BlockSpec((1,) + a.shape[1:], lambda b, i: (b, 0, 0))
    return pl.pallas_call(
        functools.partial(_dsa_body, t=t, topk=topk),
        grid=(bsz, s // t),
        in_specs=[row(aw), row(qi.shape[2]),
                  pl.BlockSpec((1, IDX_HEADS, t), lambda b, i: (b, 0, i)),
                  full(kia), full(kib), full(k), full(vt),
                  pl.BlockSpec(bias.shape, lambda b, i: (0, 0, 0, 0))],
        out_specs=row(aw),
        out_shape=jax.ShapeDtypeStruct((bsz, s, aw), _MXU_DTYPE),
        scratch_shapes=[pltpu.VMEM((s, t), F32), pltpu.VMEM((s, HEAD_GROUP * t), F32),
                        pltpu.VMEM((N_KV_HEADS, HEAD_DIM, HEAD_GROUP * t), F32)],
        compiler_params=_cparams(("arbitrary", "arbitrary")),
        name="dsa",
    )(q, qi, wit, kia, kib, k, vt, bias)


def _pack_halves(hb):
    assert jnp.dtype(hb.dtype).itemsize == 2
    n = hb.shape[1] // 2
    hi = lax.bitcast_convert_type(hb[:, :n].astype(F32), I32)
    lo = lax.bitcast_convert_type(hb[:, n:].astype(F32), I32)
    return jnp.bitwise_or(hi, lax.shift_right_logical(lo, jnp.full(lo.shape, 16, I32)))


def _unpack_halves(w, dtype):
    hi = lax.bitcast_convert_type(jnp.bitwise_and(w, jnp.int32(-65536)), F32).astype(dtype)
    lo = lax.bitcast_convert_type(jnp.left_shift(w, 16), F32).astype(dtype)
    return jnp.concatenate([hi, lo], axis=1)


def _mix_body(u_ref, halo_ref, a_ref, x_ref, mod_ref, wpool_ref, ps_ref, wout_ref, g_ref, wr_ref, br_ref,
              x1_ref, h2_ref, lg_ref, *, seq, tm):
    t0 = (pl.program_id(0) * tm) % seq
    u = u_ref[...]
    halo = jnp.where(t0 == 0, 0.0, halo_ref[...])
    ext = jnp.concatenate([halo, u], axis=0)
    pos = t0 + lax.broadcasted_iota(I32, (tm, 1), 0)
    gw = u.shape[1] // N_POOL_GROUPS
    ys = []
    for g, w in enumerate(POOL_WINDOWS):
        a = ext[:, g * gw:(g + 1) * gw]
        sft = 1
        while sft < w:
            a = a + pltpu.roll(a, sft, 0)
            sft *= 2
        cnt = jnp.minimum(pos + 1, w).astype(F32)
        p = a[HALO:, :] / cnt - u[:, g * gw:(g + 1) * gw]
        ys.append(_dot(p.astype(_MXU_DTYPE), wpool_ref[g]) * ps_ref[:, g * gw:(g + 1) * gw])
    pool = jnp.concatenate(ys, axis=1).astype(_MXU_DTYPE)
    pw = pool.shape[1]
    mixed = _dot(pool, wout_ref[:pw, :]) + _dot(a_ref[...], wout_ref[pw:, :])
    x1 = x_ref[...] + mod_ref[0, 2:3, :] * mixed
    x1_ref[...] = x1
    h2 = _rms_mod(x1, g_ref[...], mod_ref[0, 3:4, :], mod_ref[0, 4:5, :]).astype(_MXU_DTYPE)
    h2_ref[...] = _pack_halves(h2)
    lg_ref[...] = _dot(h2, wr_ref[...]) + br_ref[...]


def _mix(u, attn, x, mod, w_pool, pool_scale, w_out, g_ffn, w_r, b_r, seq):
    n_tok, d = x.shape
    tm = MIX_TM
    assert seq % tm == 0 and tm % HALO == 0
    pw = u.shape[1]
    aw = attn.shape[1]
    rows = lambda w: pl.BlockSpec((tm, w), lambda i: (i, 0))
    const = lambda a: pl.BlockSpec(a.shape, lambda i: (0,) * a.ndim)
    return pl.pallas_call(
        functools.partial(_mix_body, seq=seq, tm=tm),
        grid=(n_tok // tm,),
        in_specs=[rows(pw),
                  pl.BlockSpec((HALO, pw), lambda i: (jnp.maximum(i * (tm // HALO) - 1, 0), 0)),
                  rows(aw), rows(d),
                  pl.BlockSpec((1, 6, d), lambda i: ((i * tm) // seq, 0, 0)),
                  const(w_pool), const(pool_scale), const(w_out), const(g_ffn), const(w_r), const(b_r)],
        out_specs=[rows(d), rows(d // 2), rows(LANES)],
        out_shape=[jax.ShapeDtypeStruct((n_tok, d), F32), jax.ShapeDtypeStruct((n_tok, d // 2), I32),
                   jax.ShapeDtypeStruct((n_tok, LANES), F32)],
        compiler_params=_cparams(("arbitrary",)),
        name="mix",
    )(u, u, attn, x, mod, w_pool, pool_scale, w_out, g_ffn, w_r, b_r)


def _route_body(lg_ref, rt_ref, gate_ref, cnt_ref, run_ref):
    @pl.when(pl.program_id(0) == 0)
    def _():
        run_ref[...] = jnp.zeros_like(run_ref)

    lg = lg_ref[...]
    tr = lg.shape[0]
    lane = lax.broadcasted_iota(I32, lg.shape, 1)
    gmask = lane < N_GROUPS
    gl = jnp.where(gmask, lg, -BIG)
    gmax = jnp.max(gl, axis=-1, keepdims=True)
    gsel = jnp.min(jnp.where(jnp.logical_and(gmask, gl == gmax), lane, LANES), axis=-1, keepdims=True)
    gsum = jnp.sum(jnp.where(gmask, jnp.exp(gl - gmax), 0.0), axis=-1, keepdims=True)
    p_g = 1.0 / gsum
    in_e = jnp.logical_and(lane >= N_GROUPS, lane < N_GROUPS + N_EXPERTS)
    emask = jnp.logical_and(in_e, lax.shift_right_arithmetic(lane - N_GROUPS, EXPERTS_PER_GROUP.bit_length() - 1) == gsel)
    el = jnp.where(emask, lg, -BIG)
    emax = jnp.max(el, axis=-1, keepdims=True)
    ex = jnp.where(emask, jnp.exp(el - emax), 0.0)
    ep = ex / jnp.sum(ex, axis=-1, keepdims=True)
    p1 = jnp.max(jnp.where(emask, ep, -1.0), axis=-1, keepdims=True)
    i1 = jnp.min(jnp.where(jnp.logical_and(emask, ep == p1), lane, LANES), axis=-1, keepdims=True)
    m2 = jnp.logical_and(emask, lane != i1)
    p2 = jnp.max(jnp.where(m2, ep, -1.0), axis=-1, keepdims=True)
    i2 = jnp.min(jnp.where(jnp.logical_and(m2, ep == p2), lane, LANES), axis=-1, keepdims=True)
    den = p1 + p2
    g1 = p_g * p1 / den
    g2 = p_g * p2 / den
    e1 = i1 - N_GROUPS
    e2 = i2 - N_GROUPS

    oh = jnp.logical_or(lane == e1, lane == e2)
    ri = lax.broadcasted_iota(I32, (tr, tr), 0)
    ci = lax.broadcasted_iota(I32, (tr, tr), 1)
    ltri = (ci < ri).astype(_MXU_DTYPE)
    before = _dot(ltri, oh.astype(_MXU_DTYPE)) + run_ref[...]
    r1 = jnp.sum(jnp.where(lane == e1, before, 0.0), axis=-1, keepdims=True)
    r2 = jnp.sum(jnp.where(lane == e2, before, 0.0), axis=-1, keepdims=True)
    run = run_ref[...] + jnp.sum(oh.astype(F32), axis=0, keepdims=True)
    run_ref[...] = run
    cnt_ref[...] = jnp.broadcast_to(run, cnt_ref.shape)

    route = jnp.where(lane == 0, e1.astype(F32),
                      jnp.where(lane == 1, e2.astype(F32),
                                jnp.where(lane == 2, r1, jnp.where(lane == 3, r2, 0.0))))
    rt_ref[...] = route.T[:SUBLANES, :].astype(I32)
    gate_ref[...] = jnp.where(lane == 0, g1, jnp.where(lane == 1, g2, 0.0))


def _route(logits):
    n_tok = logits.shape[0]
    tr = min(ROUTE_TR, n_tok)
    return pl.pallas_call(
        _route_body,
        grid=(n_tok // tr,),
        in_specs=[pl.BlockSpec((tr, LANES), lambda i: (i, 0))],
        out_specs=[pl.BlockSpec((SUBLANES, tr), lambda i: (0, i)),
                   pl.BlockSpec((tr, LANES), lambda i: (i, 0)),
                   pl.BlockSpec((SUBLANES, LANES), lambda i: (0, 0))],
        out_shape=[jax.ShapeDtypeStruct((SUBLANES, n_tok), I32),
                   jax.ShapeDtypeStruct((n_tok, LANES), F32),
                   jax.ShapeDtypeStruct((SUBLANES, LANES), F32)],
        scratch_shapes=[pltpu.VMEM((1, LANES), F32)],
        compiler_params=_cparams(("arbitrary",)),
        name="route",
    )(logits)


def _dest_body(ps_ref, rt_ref, o_ref):
    r = rt_ref[...]
    start = jnp.zeros_like(r)
    for e in range(N_EXPERTS):
        start = jnp.where(r == e, ps_ref[e], start)
    o_ref[...] = start + pltpu.roll(r, SUBLANES - 2, 0)


def _dest(route_t, pstart):
    n_tok = route_t.shape[1]
    tb = min(n_tok, 2048)
    return pl.pallas_call(
        _dest_body,
        grid_spec=pltpu.PrefetchScalarGridSpec(
            num_scalar_prefetch=1,
            grid=(n_tok // tb,),
            in_specs=[pl.BlockSpec((SUBLANES, tb), lambda i, ps: (0, i))],
            out_specs=pl.BlockSpec((SUBLANES, tb), lambda i, ps: (0, i)),
        ),
        out_shape=jax.ShapeDtypeStruct((SUBLANES, n_tok), I32),
        compiler_params=_cparams(("arbitrary",)),
        name="dest",
    )(pstart, route_t)


SC_LANES = 16
SC_WIN = 16
SC_NBUF = 4
SC_CHUNK = 8192


def _sc_dispatch(h2, dest0, dest1, n_slots):
    n_tok, d = h2.shape
    info = plsc.get_sparse_core_info()
    nc, ns = info.num_cores, info.num_subcores
    nw = nc * ns
    assert info.num_lanes == SC_LANES and n_slots % (nw * SC_WIN) == 0 and n_tok % SC_CHUNK == 0
    assert n_slots < 3 * n_tok
    per_w = n_slots // nw
    mesh = plsc.VectorSubcoreMesh(core_axis_name="c", subcore_axis_name="s")

    n_win = per_w // SC_WIN
    assert n_win % SC_NBUF == 0

    @functools.partial(
        pl.kernel, mesh=mesh,
        out_type=jax.ShapeDtypeStruct((n_slots, d), h2.dtype),
        scratch_types=[pltpu.VMEM((per_w,), I32), pltpu.VMEM((SC_CHUNK,), I32),
                       pltpu.VMEM((SC_NBUF, SC_WIN, d), h2.dtype), pltpu.SemaphoreType.DMA((SC_NBUF,)),
                       pltpu.SemaphoreType.DMA((SC_NBUF,))],
        compiler_params=pltpu.CompilerParams(needs_layout_passes=False),
    )
    def k(h_hbm, d0_hbm, d1_hbm, o_hbm, table, chunk, rows, gsems, wsems):
        lane = lax.iota(I32, SC_LANES)
        base = (lax.axis_index("s") * nc + lax.axis_index("c")) * per_w

        @pl.loop(0, per_w, step=SC_LANES)
        def _(s):
            v = base + s + lane
            v = jnp.where(v >= n_tok, v - n_tok, v)
            v = jnp.where(v >= n_tok, v - n_tok, v)
            table[pl.ds(s, SC_LANES)] = v

        for d_hbm in (d0_hbm, d1_hbm):
            @pl.loop(0, n_tok, step=SC_CHUNK)
            def _(t0, d_hbm=d_hbm):
                pltpu.sync_copy(d_hbm.at[pl.ds(t0, SC_CHUNK)], chunk)

                @pl.loop(0, SC_CHUNK, step=SC_LANES)
                def _(j):
                    r = chunk[pl.ds(j, SC_LANES)] - base
                    mine = jnp.logical_and(r >= 0, r < per_w)
                    plsc.store_scatter(table, [jnp.where(mine, r, 0)], t0 + j + lane, mask=mine)

        ahead = SC_NBUF // 2

        def gather(w, b):
            return pltpu.make_async_copy(
                h_hbm.at[table.at[pl.ds(w * SC_WIN, SC_WIN)]], rows.at[b], gsems.at[b])

        def put(w, b):
            return pltpu.make_async_copy(rows.at[b], o_hbm.at[pl.ds(base + w * SC_WIN, SC_WIN)], wsems.at[b])

        for b in range(ahead):
            gather(b, b).start()

        @pl.loop(0, n_win, step=SC_NBUF)
        def _(g):
            for b in range(SC_NBUF):
                w = g + b
                pb = (b + ahead) % SC_NBUF
                gather(w, b).wait()
                put(w, b).start()

                @pl.when(w >= ahead)
                def _():
                    put(w - ahead, pb).wait()

                @pl.when(w + ahead < n_win)
                def _():
                    gather(w + ahead, pb).start()

        for b in range(ahead):
            w = n_win - ahead + b
            put(w, w % SC_NBUF).wait()

    return k(h2, dest0, dest1)


def _experts_body(slot_ref, cast_ref, pre_ref, misc_ref, x_ref, wg_hbm, wu_hbm, wd_hbm, y_ref,
                  sg, su, sd, wgb, wub, wdb, sem):
    blk = pl.program_id(0)
    e0, pre0, n_used = misc_ref[0], misc_ref[1], misc_ref[2]

    def copies(e):
        return (pltpu.make_async_copy(wg_hbm.at[e], sg, sem.at[0]),
                pltpu.make_async_copy(wu_hbm.at[e], su, sem.at[1]),
                pltpu.make_async_copy(wd_hbm.at[e], sd, sem.at[2]))

    def fetch(e):
        for c in copies(e):
            c.start(priority=1)

    def land(slot):
        for c in copies(0):
            c.wait()
        wgb[slot] = sg[...].astype(wgb.dtype)
        wub[slot] = su[...].astype(wub.dtype)
        wdb[slot] = sd[...].astype(wdb.dtype)

    @pl.when(blk == 0)
    def _():
        fetch(e0)
        land(0)

        @pl.when(pre0 >= 0)
        def _():
            fetch(pre0)

    @pl.when(blk < n_used)
    def _():
        slot = slot_ref[blk]
        xb = _unpack_halves(x_ref[...], _MXU_DTYPE)
        a = _silu(_dot(xb, wgb[slot])) * _dot(xb, wub[slot])
        y_ref[...] = _dot(a.astype(_MXU_DTYPE), wdb[slot])

        @pl.when(cast_ref[blk] == 1)
        def _():
            land(1 - slot)

            @pl.when(pre_ref[blk] >= 0)
            def _():
                fetch(pre_ref[blk])

    @pl.when(blk >= n_used)
    def _():
        y_ref[...] = jnp.zeros_like(y_ref)


def _experts(xs, n_used, pends, w_gate, w_up, w_down):
    n_slots, dpk = xs.shape
    bm = MOE_BM
    n_blocks = n_slots // bm
    d, de = w_gate.shape[1:]
    last_used = n_used[0] - 1
    total = pends[-1]
    eid = jnp.arange(N_EXPERTS, dtype=I32)[None, :]
    owner = lambda start: jnp.minimum(jnp.sum((pends[None, :] <= start[:, None]).astype(I32), axis=1), N_EXPERTS - 1)
    end_of = lambda e: jnp.sum(jnp.where(eid == e[:, None], pends[None, :], 0), axis=1)
    be = owner(jnp.minimum(jnp.arange(n_blocks, dtype=I32), last_used) * bm)
    end_e = end_of(be)
    has_next = end_e < total
    nxt = owner(end_e)
    end_n = end_of(nxt)
    nxt2 = jnp.where(jnp.logical_and(has_next, end_n < total), owner(end_n), -1)
    is_last = jnp.concatenate([be[1:] != be[:-1], jnp.ones((1,), jnp.bool_)])
    is_last = jnp.logical_or(is_last, jnp.arange(n_blocks) >= last_used)
    cast = jnp.logical_and(is_last, has_next).astype(I32)
    pre = jnp.where(cast == 1, nxt2, -1).astype(I32)
    is_first = jnp.concatenate([jnp.ones((1,), jnp.bool_), be[1:] != be[:-1]])
    slot = ((jnp.cumsum(is_first.astype(I32)) - 1) % 2).astype(I32)
    misc = jnp.stack([be[0], jnp.where(has_next[0], nxt[0], -1), n_used[0]]).astype(I32)
    hbm = pl.BlockSpec(memory_space=pl.ANY)
    return pl.pallas_call(
        _experts_body,
        grid_spec=pltpu.PrefetchScalarGridSpec(
            num_scalar_prefetch=4,
            grid=(n_blocks,),
            in_specs=[pl.BlockSpec((bm, dpk), lambda j, sl, ca, pr, mi: (jnp.minimum(j, mi[2] - 1), 0)),
                      hbm, hbm, hbm],
            out_specs=pl.BlockSpec((bm, d), lambda j, sl, ca, pr, mi: (j, 0)),
            scratch_shapes=[pltpu.VMEM((d, de), w_gate.dtype), pltpu.VMEM((d, de), w_up.dtype),
                            pltpu.VMEM((de, d), w_down.dtype),
                            pltpu.VMEM((2, d, de), _MXU_DTYPE), pltpu.VMEM((2, d, de), _MXU_DTYPE),
                            pltpu.VMEM((2, de, d), _MXU_DTYPE), pltpu.SemaphoreType.DMA((3,))],
        ),
        out_shape=jax.ShapeDtypeStruct((n_slots, d), F32),
        compiler_params=_cparams(("arbitrary",)),
        name="experts",
    )(slot, cast, pre, misc, xs, w_gate, w_up, w_down)


def _combine_body(x1_ref, gate_ref, d0_ref, d1_ref, n0_ref, n1_ref, mod_ref, g_ref, ys_ref, o_ref, buf_ref, sem,
                  *, tf):
    i = pl.program_id(0)
    cur = i % 2

    def gather(da_ref, db_ref, half):
        def issue(c, carry):
            for u in range(SUBLANES):
                r = c * SUBLANES + u
                for j, d_ref in enumerate((da_ref, db_ref)):
                    pltpu.make_async_copy(ys_ref.at[pl.ds(d_ref[r], 1), :],
                                          buf_ref.at[half, j, c, pl.ds(u, 1), :], sem.at[half]).start(priority=j)
            return carry

        lax.fori_loop(0, tf // SUBLANES, issue, 0)

    @pl.when(i == 0)
    def _():
        gather(d0_ref, d1_ref, 0)

    @pl.when(i + 1 < pl.num_programs(0))
    def _():
        gather(n0_ref, n1_ref, 1 - cur)

    for _ in range(2):
        pltpu.make_async_copy(ys_ref.at[pl.ds(0, tf), :], ys_ref.at[pl.ds(0, tf), :], sem.at[cur]).wait()

    gates = gate_ref[...]
    d = o_ref.shape[1]
    moe = buf_ref[cur, 0].reshape(tf, d) * gates[:, 0:1] + buf_ref[cur, 1].reshape(tf, d) * gates[:, 1:2]
    x2 = x1_ref[...] + mod_ref[0, 5:6, :] * moe
    ms = jnp.mean(x2 * x2, axis=-1, keepdims=True)
    o_ref[...] = x2 * lax.rsqrt(ms + EPS) * g_ref[...]


def _combine(x1, gates, dest0, dest1, mod, g_final, ys, seq):
    n_tok, d = x1.shape
    tf = COMB_TF
    n_tiles = n_tok // tf
    tok = pl.BlockSpec((tf,), lambda i: (i,), memory_space=pltpu.SMEM)
    nxt = pl.BlockSpec((tf,), lambda i: (jnp.minimum(i + 1, n_tiles - 1),), memory_space=pltpu.SMEM)
    return pl.pallas_call(
        functools.partial(_combine_body, tf=tf),
        grid=(n_tiles,),
        in_specs=[pl.BlockSpec((tf, d), lambda i: (i, 0)),
                  pl.BlockSpec((tf, LANES), lambda i: (i, 0)),
                  tok, tok, nxt, nxt,
                  pl.BlockSpec((1, 6, d), lambda i: ((i * tf) // seq, 0, 0)),
                  pl.BlockSpec((1, d), lambda i: (0, 0)),
                  pl.BlockSpec(memory_space=pl.ANY)],
        out_specs=pl.BlockSpec((tf, d), lambda i: (i, 0)),
        scratch_shapes=[pltpu.VMEM((2, 2, tf // SUBLANES, SUBLANES, d), F32), pltpu.SemaphoreType.DMA((2,))],
        out_shape=jax.ShapeDtypeStruct((n_tok, d), F32),
        compiler_params=_cparams(("arbitrary",)),
        name="combine",
    )(x1, gates, dest0, dest1, dest0, dest1, mod, g_final, ys)


def kernel(x, c, w_ada, b_ada, g_mix, w_in, w_pool, pool_scale, rel_bias, w_out, g_ffn, w_group, b_group,
           w_router, b_router, w_gate, w_up, w_down, g_final):
    bsz, seq, d = x.shape
    n_tok = bsz * seq
    depth = w_ada.shape[0]
    assert depth == 1, "the final norm is fused into the only layer's combine"
    topk = min(TOPK_MAX, seq // 4)
    bias = _bias_tiles(rel_bias, ATT_T)
    bm = MOE_BM
    n_blocks = -(-(n_tok * 2) // bm) + N_EXPERTS
    n_slots = n_blocks * bm
    xt = x.reshape(n_tok, d)
    for i in range(depth):
        mod = _ada(c, w_ada[i], b_ada[i][None, :]).reshape(bsz, 6, d)
        u, q, k, vt, qi, kia, kib, wit = _proj(xt.reshape(bsz, seq, d), mod, g_mix[i][None, :], w_in[i])
        attn = _dsa(q, qi, wit, kia, kib, k, vt, bias, topk)
        w_r = jnp.concatenate(
            [w_group[i], w_router[i], jnp.zeros((d, LANES - N_GROUPS - N_EXPERTS), F32)], axis=1).astype(_MXU_DTYPE)
        b_r = jnp.concatenate(
            [b_group[i], b_router[i], jnp.zeros((LANES - N_GROUPS - N_EXPERTS,), F32)])[None, :]
        x1, h2, logits = _mix(u.reshape(n_tok, -1), attn.reshape(n_tok, -1), xt, mod,
                              w_pool[i].astype(_MXU_DTYPE), pool_scale[i][None, :],
                              w_out[i].astype(_MXU_DTYPE), g_ffn[i][None, :], w_r, b_r, seq)
        route_t, gates, cnt = _route(logits)
        counts = cnt[0, :N_EXPERTS].astype(I32)
        padded = (counts + bm - 1) // bm * bm
        pends = jnp.cumsum(padded)
        pstart = pends - padded
        n_used = (pends[-1:] // bm).astype(I32)
        dest = _dest(route_t, pstart)
        dest0, dest1 = dest[0], dest[1]
        xs = _sc_dispatch(h2, dest0, dest1, n_slots)
        ys = _experts(xs, n_used, pends, w_gate[i], w_up[i], w_down[i])
        xt = _combine(x1, gates, dest0, dest1, mod, g_final[None, :], ys, seq)
    return xt.reshape(bsz, seq, d)
```
